```python
import jax, jax.numpy as jnp
from jax import lax
import numpy as np

D_MODEL = 1024
BATCH = 16
SEQ = 4096
DEPTH = 2
DEC_BATCH = 32
DEC_SEQ = 64
PAST_LEN = 4096

CHUNK = 64
NORM_EPS = 1e-6
LOG_FLOOR = 1e-30
HG_HEADS = 4
HG_DK = 128
HG_DV = 128
HG_KW = HG_HEADS * HG_DK
HG_W = HG_HEADS * HG_DV
GLA_HEADS = 4
GLA_DK = 64
GLA_DV = 128
GLA_KW = GLA_HEADS * GLA_DK
GLA_W = GLA_HEADS * GLA_DV
GLA_GATE_RANK = 16
GLA_GATE_NORM = 16.0
IN_SPLITS = (HG_KW, HG_KW, HG_W, HG_W, GLA_KW, GLA_KW, GLA_W, GLA_W, GLA_GATE_RANK, D_MODEL, D_MODEL)
IN_COLS = sum(IN_SPLITS)
N_GROUPS = 4
EXPERTS_PER_GROUP = 8
N_EXPERTS = N_GROUPS * EXPERTS_PER_GROUP
TOPK_IN_GROUP = 2
D_EXPERT = 512
MOE_BLOCK = 128

kernel_name = 'hybrid_hgrn2_gla_hmoe_stream_step'


def rms_norm(x, g):
    xf = x.astype(jnp.float32)
    y = xf * lax.rsqrt(jnp.mean(xf * xf, axis=-1, keepdims=True) + NORM_EPS)
    return (y * g.astype(jnp.float32)).astype(x.dtype)


def gated_head_norm(o, g, gate):
    o = o.astype(jnp.float32)
    o = o * lax.rsqrt(jnp.mean(o * o, axis=-1, keepdims=True) + NORM_EPS) * g.astype(jnp.float32)
    return o * jax.nn.silu(gate.astype(jnp.float32))


def chunked_gated_linear_attention(q, k, v, log_a, s0):
    B, L, H, dk = q.shape
    dv = v.shape[-1]
    C = min(CHUNK, L)
    n = L // C

    def to_chunks(t):
        return t.astype(jnp.float32).reshape(B, n, C, H, t.shape[-1]).transpose(1, 0, 3, 2, 4)

    causal = jnp.tril(jnp.ones((C, C), dtype=bool))[:, :, None]
    causal_f = causal.astype(jnp.float32)

    def step(S, inp):
        qc, kc, vc, gc = inp
        b = jnp.cumsum(gc, axis=2)
        o_inter = jnp.einsum('bhtk,bhkv->bhtv', qc * jnp.exp(b), S)
        diff = b[:, :, :, None, :] - b[:, :, None, :, :]
        decay = jnp.exp(jnp.where(causal, diff, 0.0)) * causal_f
        scores = jnp.einsum('bhtk,bhsk,bhtsk->bhts', qc, kc, decay)
        o = o_inter + jnp.einsum('bhts,bhsv->bhtv', scores, vc)
        b_last = b[:, :, -1, :]
        S = jnp.exp(b_last)[..., None] * S + jnp.einsum(
            'bhsk,bhsv->bhkv', kc * jnp.exp(b_last[:, :, None, :] - b), vc)
        return S, o

    S, o = lax.scan(step, s0.astype(jnp.float32),
                    (to_chunks(q), to_chunks(k), to_chunks(v), to_chunks(log_a)))
    o = o.transpose(1, 0, 3, 2, 4).reshape(B, L, H, dv)
    return o, S


def token_mixer(h, s_hg, s_gla, lb, w_in, hg_onorm, w_gk2, b_gk, gla_onorm, w_br_a, w_br_b, w_out):
    B, L, _ = h.shape
    split_at = np.cumsum(IN_SPLITS)[:-1].tolist()
    hq, hf, hi, hog, gq, gk, gv, gog, glr, ga, gb = jnp.split(h @ w_in, split_at, axis=-1)

    def heads(t, n_h):
        return t.reshape(B, L, n_h, -1)

    z = heads(hf, HG_HEADS).astype(jnp.float32)
    lbh = lb.reshape(HG_HEADS, HG_DK)
    f = lbh + (1.0 - lbh) * jax.nn.sigmoid(z)
    log_f = jnp.log(jnp.maximum(f, LOG_FLOOR))
    k_hg = (1.0 - lbh) * jax.nn.sigmoid(-z)
    q_hg = jax.nn.silu(heads(hq, HG_HEADS).astype(jnp.float32)) * (HG_DK ** -0.5)
    o_hg, s_hg_new = chunked_gated_linear_attention(q_hg, k_hg, heads(hi, HG_HEADS), log_f, s_hg)
    o_hg = gated_head_norm(o_hg, hg_onorm, heads(hog, HG_HEADS)).reshape(B, L, HG_W).astype(h.dtype)

    log_g = jax.nn.log_sigmoid((glr @ w_gk2 + b_gk).astype(jnp.float32)) / GLA_GATE_NORM
    q_gla = heads(gq, GLA_HEADS).astype(jnp.float32) * (GLA_DK ** -0.5)
    o_gla, s_gla_new = chunked_gated_linear_attention(
        q_gla, heads(gk, GLA_HEADS), heads(gv, GLA_HEADS), heads(log_g, GLA_HEADS), s_gla)
    o_gla = gated_head_norm(o_gla, gla_onorm, heads(gog, GLA_HEADS)).reshape(B, L, GLA_W).astype(h.dtype)

    merged = jax.nn.sigmoid(ga) * (o_hg @ w_br_a) + jax.nn.sigmoid(gb) * (o_gla @ w_br_b)
    return merged @ w_out, s_hg_new.astype(s_hg.dtype), s_gla_new.astype(s_gla.dtype)


def routed_experts(xf, eid, wts, w_gate, w_up, w_down):
    T, D = xf.shape
    A = T * TOPK_IN_GROUP
    flat_e = eid.reshape(A).astype(jnp.int32)
    order = jnp.argsort(flat_e)
    sorted_e = flat_e[order]
    counts = jnp.zeros((N_EXPERTS,), jnp.int32).at[flat_e].add(1)
    starts = jnp.cumsum(counts) - counts
    padded = (counts + MOE_BLOCK - 1) // MOE_BLOCK * MOE_BLOCK
    pad_end = jnp.cumsum(padded)
    pad_start = pad_end - padded
    dest = pad_start[sorted_e] + jnp.arange(A, dtype=jnp.int32) - starts[sorted_e]
    n_blocks = -(-A // MOE_BLOCK) + N_EXPERTS
    buf = jnp.zeros((n_blocks * MOE_BLOCK, D), xf.dtype).at[dest].set(xf[order // TOPK_IN_GROUP])
    block_e = jnp.minimum(
        jnp.searchsorted(pad_end, jnp.arange(n_blocks, dtype=jnp.int32) * MOE_BLOCK, side='right'),
        N_EXPERTS - 1)

    def expert_block(args):
        xb, e = args
        return (jax.nn.silu(xb @ w_gate[e]) * (xb @ w_up[e])) @ w_down[e]

    yb = lax.map(expert_block, (buf.reshape(n_blocks, MOE_BLOCK, D), block_e)).reshape(-1, D)
    y_assign = jnp.zeros((A, D), yb.dtype).at[order].set(yb[dest])
    return jnp.einsum('tkd,tk->td', y_assign.reshape(T, TOPK_IN_GROUP, D), wts.astype(yb.dtype))


def hierarchical_moe(h, w_rg, b_rg, w_re, b_re, w_e_gate, w_e_up, w_e_down):
    B, L, D = h.shape
    xf = h.reshape(B * L, D)
    p_group = jax.nn.softmax((xf @ w_rg + b_rg).astype(jnp.float32), axis=-1)
    gp, gi = lax.top_k(p_group, 1)
    le = (xf @ w_re + b_re).astype(jnp.float32).reshape(-1, N_GROUPS, EXPERTS_PER_GROUP)
    le_sel = jnp.take_along_axis(le, gi[:, :, None], axis=1)[:, 0]
    ve, ie = lax.top_k(jax.nn.softmax(le_sel, axis=-1), TOPK_IN_GROUP)
    wts = gp * ve / jnp.sum(ve, axis=-1, keepdims=True)
    eid = gi * EXPERTS_PER_GROUP + ie
    y = routed_experts(xf, eid, wts, w_e_gate, w_e_up, w_e_down)
    return y.reshape(B, L, D).astype(h.dtype)


def run_trunk(x, c, s_hg_all, s_gla_all, lbs, w_ada, b_ada, norm_mix, norm_ffn, w_in, hg_onorm,
              w_gk2, b_gk, gla_onorm, w_br_a, w_br_b, w_out, w_rg, b_rg, w_re, b_re,
              w_e_gate, w_e_up, w_e_down, norm_final):
    new_hg, new_gla = [], []
    for l in range(DEPTH):
        mod = jax.nn.silu(c) @ w_ada[l] + b_ada[l]
        sh1, sc1, g1, sh2, sc2, g2 = jnp.split(mod[:, None, :], 6, axis=-1)
        h = rms_norm(x, norm_mix[l]) * (1.0 + sc1) + sh1
        m, s_hg, s_gla = token_mixer(h, s_hg_all[l], s_gla_all[l], lbs[l], w_in[l], hg_onorm[l],
                                     w_gk2[l], b_gk[l], gla_onorm[l], w_br_a[l], w_br_b[l], w_out[l])
        x = x + g1 * m
        h = rms_norm(x, norm_ffn[l]) * (1.0 + sc2) + sh2
        x = x + g2 * hierarchical_moe(h, w_rg[l], b_rg[l], w_re[l], b_re[l],
                                      w_e_gate[l], w_e_up[l], w_e_down[l])
        new_hg.append(s_hg)
        new_gla.append(s_gla)
    return rms_norm(x, norm_final), jnp.stack(new_hg), jnp.stack(new_gla)


def setup_inputs(seed: int = 0) -> dict:
    key = jax.random.key(seed)
    ks = jax.random.split(key, 32)
    f32 = jnp.float32

    def nrm(k, shape, scale):
        return jax.random.normal(k, shape, f32) * scale

    D = D_MODEL
    return {
        'x_prompt': nrm(ks[0], (BATCH, SEQ, D), 1.0),
        'x_sample': nrm(ks[1], (DEC_BATCH, DEC_SEQ, D), 1.0),
        'c_prompt': nrm(ks[2], (BATCH, D), 1.0),
        'c_sample': nrm(ks[3], (DEC_BATCH, D), 1.0),
        'state_hgrn': nrm(ks[4], (DEPTH, DEC_BATCH, HG_HEADS, HG_DK, HG_DV), 0.5),
        'state_gla': nrm(ks[5], (DEPTH, DEC_BATCH, GLA_HEADS, GLA_DK, GLA_DV), 2.0),
        'w_ada': nrm(ks[6], (DEPTH, D, 6 * D), 0.5 * D ** -0.5),
        'b_ada': nrm(ks[7], (DEPTH, 6 * D), 0.02),
        'norm_mix': 1.0 + nrm(ks[8], (DEPTH, D), 0.02),
        'norm_ffn': 1.0 + nrm(ks[9], (DEPTH, D), 0.02),
        'w_in': nrm(ks[10], (DEPTH, D, IN_COLS), D ** -0.5),
        'hg_lb': nrm(ks[11], (DEPTH, HG_KW), 0.5),
        'hg_onorm': 1.0 + nrm(ks[12], (DEPTH, HG_DV), 0.02),
        'w_gk2': nrm(ks[13], (DEPTH, GLA_GATE_RANK, GLA_KW), GLA_GATE_RANK ** -0.5),
        'b_gk': nrm(ks[14], (DEPTH, GLA_KW), 0.1),
        'gla_onorm': 1.0 + nrm(ks[15], (DEPTH, GLA_DV), 0.02),
        'w_br_a': nrm(ks[16], (DEPTH, HG_W, D), HG_W ** -0.5),
        'w_br_b': nrm(ks[17], (DEPTH, GLA_W, D), GLA_W ** -0.5),
        'w_out': nrm(ks[18], (DEPTH, D, D), D ** -0.5),
        'w_rg': nrm(ks[19], (DEPTH, D, N_GROUPS), D ** -0.5),
        'b_rg': nrm(ks[20], (DEPTH, N_GROUPS), 0.01),
        'w_re': nrm(ks[21], (DEPTH, D, N_EXPERTS), D ** -0.5),
        'b_re': nrm(ks[22], (DEPTH, N_EXPERTS), 0.01),
        'w_e_gate': nrm(ks[23], (DEPTH, N_EXPERTS, D, D_EXPERT), D ** -0.5),
        'w_e_up': nrm(ks[24], (DEPTH, N_EXPERTS, D, D_EXPERT), D ** -0.5),
        'w_e_down': nrm(ks[25], (DEPTH, N_EXPERTS, D_EXPERT, D), D_EXPERT ** -0.5),
        'norm_final': 1.0 + nrm(ks[26], (D,), 0.02),
    }


def reference(x_prompt, x_sample, c_prompt, c_sample, state_hgrn, state_gla, w_ada, b_ada, norm_mix,
              norm_ffn, w_in, hg_lb, hg_onorm, w_gk2, b_gk, gla_onorm, w_br_a, w_br_b, w_out, w_rg, b_rg,
              w_re, b_re, w_e_gate, w_e_up, w_e_down, norm_final):
    sm = jax.nn.softmax(hg_lb.astype(jnp.float32), axis=0)
    lbs = jnp.clip(jnp.cumsum(sm, axis=0) - sm[0], 0.0, 1.0)
    weights = (w_ada, b_ada, norm_mix, norm_ffn, w_in, hg_onorm, w_gk2, b_gk, gla_onorm, w_br_a, w_br_b,
               w_out, w_rg, b_rg, w_re, b_re, w_e_gate, w_e_up, w_e_down, norm_final)
    bp = x_prompt.shape[0]
    s_hg0 = jnp.zeros((DEPTH, bp, HG_HEADS, HG_DK, HG_DV), x_prompt.dtype)
    s_gla0 = jnp.zeros((DEPTH, bp, GLA_HEADS, GLA_DK, GLA_DV), x_prompt.dtype)
    y_prompt, hg_prompt, gla_prompt = run_trunk(x_prompt, c_prompt, s_hg0, s_gla0, lbs, *weights)
    y_sample, hg_sample, gla_sample = run_trunk(x_sample, c_sample, state_hgrn, state_gla, lbs, *weights)
    return (y_prompt, y_sample, hg_prompt, gla_prompt, hg_sample, gla_sample)
```

```python
import functools

import numpy as np
import jax
import jax.numpy as jnp
from jax import lax
from jax.experimental import pallas as pl
from jax.experimental.pallas import tpu as pltpu

F32 = jnp.float32
BF16 = jnp.bfloat16

D_MODEL = 1024
DEPTH = 2
CHUNK = 64
NORM_EPS = 1e-6
LOG_FLOOR = 1e-30
HG_HEADS = 4
HG_DK = 128
HEAD_DV = 128
HG_KW = HG_HEADS * HG_DK
HG_W = HG_HEADS * HEAD_DV
GLA_HEADS = 4
GLA_DK = 64
GLA_KW = GLA_HEADS * GLA_DK
GLA_W = GLA_HEADS * HEAD_DV
GLA_GATE_RANK = 16
GLA_GATE_NORM = 16.0
N_GROUPS = 4
EXPERTS_PER_GROUP = 8
N_EXPERTS = N_GROUPS * EXPERTS_PER_GROUP
TOPK = 2
D_EXPERT = 512

LANES = 128
VMEM_LIMIT = 56 * 1024 * 1024

C_HQ = 0
C_HF = C_HQ + HG_KW
C_HI = C_HF + HG_KW
C_HOG = C_HI + HG_W
C_GQ = C_HOG + HG_W
C_GK = C_GQ + GLA_KW
C_GV = C_GK + GLA_KW
C_GOG = C_GV + GLA_W
C_GA = C_GOG + GLA_W
C_GB = C_GA + D_MODEL
C_GLR = C_GB + D_MODEL
IN_COLS_PAD = C_GLR + LANES
PROJ_TILE = 640
assert IN_COLS_PAD % PROJ_TILE == 0

LEVELS = (32, 16, 8, 4, 2)
N_MASKS = len(LEVELS) + 2
CUM_ROWS = (2 * len(LEVELS) + 1) * CHUNK


def _segment_sum_matrix():
    t = np.arange(CHUNK)[:, None]
    r = np.arange(CHUNK)[None, :]
    rows = []
    for m in LEVELS:
        same = (t // m) == (r // m)
        rows.append(same & (r <= t))
        rows.append(same & (r > t))
    rows.append(r <= t)
    return np.concatenate(rows, axis=0).astype(np.float32)


def _level_masks():
    t = np.arange(CHUNK)[:, None]
    s = np.arange(CHUNK)[None, :]
    masks = [t == s]
    for m in LEVELS + (1,):
        masks.append(((t // (2 * m)) == (s // (2 * m))) & ((t // m) % 2 == 1) & ((s // m) % 2 == 0))
    return np.stack(masks).astype(np.float32)


def _dot(a, b):
    return jnp.dot(a, b, preferred_element_type=F32)


def _dot_nt(a, b):
    return lax.dot_general(a, b, (((1,), (1,)), ((), ())), preferred_element_type=F32)


def _sigmoid(x):
    return 1.0 / (1.0 + jnp.exp(-x))


def _silu(x):
    return x * _sigmoid(x)


def _rms_mod(x, gain, scale, shift):
    y = x * lax.rsqrt(jnp.mean(x * x, axis=-1, keepdims=True) + NORM_EPS)
    return y * gain * (1.0 + scale) + shift


def _ada_kernel(c_ref, w_ref, b_ref, o_ref):
    c = c_ref[...]
    o_ref[0] = jnp.dot(_silu(c), w_ref[0], preferred_element_type=F32,
                       precision=lax.Precision.HIGHEST) + b_ref[0]


def _ada_mod(c_all, w_ada, b_ada):
    nb = c_all.shape[0]
    tn = 512
    return pl.pallas_call(
        _ada_kernel,
        grid=(DEPTH, 6 * D_MODEL // tn),
        in_specs=[
            pl.BlockSpec((nb, D_MODEL), lambda l, j: (0, 0)),
            pl.BlockSpec((1, D_MODEL, tn), lambda l, j: (l, 0, j)),
            pl.BlockSpec((1, 1, tn), lambda l, j: (l, 0, j)),
        ],
        out_specs=pl.BlockSpec((1, nb, tn), lambda l, j: (l, 0, j)),
        out_shape=jax.ShapeDtypeStruct((DEPTH, nb, 6 * D_MODEL), F32),
        name="ada_mod",
    )(c_all, w_ada, b_ada.reshape(DEPTH, 1, 6 * D_MODEL))


def _chunk_attention(q, k, v, g, s_ref, mall_ref, mask_ref, heads_per_tile):
    w = q.shape[1]
    n_tiles = w // LANES
    g_hi = g.astype(BF16)
    r1 = g - g_hi.astype(F32)
    g_mid = r1.astype(BF16)
    g_lo = (r1 - g_mid.astype(F32)).astype(BF16)
    mall = mall_ref[...]
    cums = _dot(mall, g_hi) + _dot(mall, g_mid) + _dot(mall, g_lo)
    b = cums[CUM_ROWS - CHUNK:CUM_ROWS]
    qs = [q]
    ks = [k]
    for i in range(len(LEVELS)):
        qs.append(q * jnp.exp(cums[2 * i * CHUNK:(2 * i + 1) * CHUNK]))
        ks.append(k * jnp.exp(cums[(2 * i + 1) * CHUNK:(2 * i + 2) * CHUNK]))
    qs.append(q * jnp.exp(g))
    ks.append(k)
    b_last = b[CHUNK - 1:CHUNK]
    q_in = q * jnp.exp(b)
    k_out = k * jnp.exp(b_last - b)
    e_last = jnp.exp(b_last)

    dk = LANES // heads_per_tile
    lane = lax.broadcasted_iota(jnp.int32, (CHUNK, LANES), 1)
    row = lax.broadcasted_iota(jnp.int32, (LANES, HEAD_DV), 0)
    outs = []
    for ti in range(n_tiles):
        sl = slice(ti * LANES, (ti + 1) * LANES)
        ks_t = [kk[:, sl].astype(BF16) for kk in ks]
        k_out_t = k_out[:, sl].T.astype(BF16)
        e_col = jnp.broadcast_to(e_last[:, sl], (LANES, LANES)).T
        s_old = s_ref[ti]
        s_old_b = s_old.astype(BF16)
        upd = None
        for j in range(heads_per_tile):
            head = ti * heads_per_tile + j
            if heads_per_tile == 1:
                sel = lambda a: a
            else:
                in_head = (lane // dk) == j
                sel = lambda a, in_head=in_head: jnp.where(in_head, a, 0.0)
            sc = jnp.zeros((CHUNK, CHUNK), F32)
            for i in range(N_MASKS):
                sc = sc + _dot_nt(sel(qs[i][:, sl]).astype(BF16), ks_t[i]) * mask_ref[i]
            vh = v[:, head * HEAD_DV:(head + 1) * HEAD_DV].astype(BF16)
            o = _dot(sc.astype(BF16), vh) + _dot(sel(q_in[:, sl]).astype(BF16), s_old_b)
            outs.append(o)
            u = _dot(k_out_t, vh)
            upd = u if upd is None else jnp.where((row // dk) == j, u, upd)
        s_ref[ti] = e_col * s_old + upd
    return jnp.concatenate(outs, axis=1)


def _head_norm_gate(o, gain, gate):
    outs = []
    for h in range(o.shape[1] // HEAD_DV):
        sl = slice(h * HEAD_DV, (h + 1) * HEAD_DV)
        oh = o[:, sl]
        oh = oh * lax.rsqrt(jnp.mean(oh * oh, axis=-1, keepdims=True) + NORM_EPS) * gain
        outs.append(oh * _silu(gate[:, sl]))
    return jnp.concatenate(outs, axis=1)


def _mixer_kernel(x_ref, mod_ref, nrm_ref, win_ref, lb_ref, wgk2_ref, bgk_ref, hgn_ref, glan_ref,
                  wa_ref, wb_ref, wo_ref, shg0_ref, sgla0_ref, mall_ref, mask_ref,
                  xo_ref, shg_o_ref, sgla_o_ref,
                  p_scr, k_scr, lg_scr, shg_scr, sgla_scr, *, layer, tb):
    j = pl.program_id(1)

    @pl.when(j == 0)
    def _():
        shg_scr[...] = shg0_ref[0]
        sgla_scr[...] = sgla0_ref[0]

    x = x_ref[0]
    sh1 = mod_ref[0, 0:1, :]
    sc1 = mod_ref[0, 1:2, :]
    g1 = mod_ref[0, 2:3, :]
    hb = _rms_mod(x, nrm_ref[...], sc1, sh1).astype(BF16)
    for c in range(0, IN_COLS_PAD, PROJ_TILE):
        p_scr[:, c:c + PROJ_TILE] = _dot(hb, win_ref[:, c:c + PROJ_TILE])

    lb_all = lb_ref[...]
    lb_max = jnp.max(lb_all, axis=0, keepdims=True)
    lb_exp = jnp.exp(lb_all - lb_max)
    sm = lb_exp / jnp.sum(lb_exp, axis=0, keepdims=True)
    lbl = jnp.clip(jnp.sum(sm[0:layer + 1], axis=0, keepdims=True) - sm[0:1], 0.0, 1.0)

    p_scr[:, C_HQ:C_HQ + HG_KW] = _silu(p_scr[:, C_HQ:C_HQ + HG_KW]) * (HG_DK ** -0.5)
    z = p_scr[:, C_HF:C_HF + HG_KW]
    f = lbl + (1.0 - lbl) * _sigmoid(z)
    p_scr[:, C_HF:C_HF + HG_KW] = jnp.log(jnp.maximum(f, LOG_FLOOR))
    k_scr[...] = (1.0 - lbl) * _sigmoid(-z)
    glr = p_scr[:, C_GLR:C_GLR + LANES].astype(BF16)
    gate = _dot(glr, wgk2_ref[...]) + bgk_ref[...]
    lg_scr[...] = (jnp.minimum(gate, 0.0) - jnp.log1p(jnp.exp(-jnp.abs(gate)))) * (1.0 / GLA_GATE_NORM)
    p_scr[:, C_GQ:C_GQ + GLA_KW] = p_scr[:, C_GQ:C_GQ + GLA_KW] * (GLA_DK ** -0.5)

    def chunk_body(ci, carry):
        rows = pl.ds(pl.multiple_of(ci * CHUNK, CHUNK), CHUNK)
        o_hg = _chunk_attention(p_scr[rows, C_HQ:C_HQ + HG_KW], k_scr[rows, :],
                                p_scr[rows, C_HI:C_HI + HG_W], p_scr[rows, C_HF:C_HF + HG_KW],
                                shg_scr, mall_ref, mask_ref, 1)
        p_scr[rows, C_HI:C_HI + HG_W] = o_hg
        o_gla = _chunk_attention(p_scr[rows, C_GQ:C_GQ + GLA_KW], p_scr[rows, C_GK:C_GK + GLA_KW],
                                 p_scr[rows, C_GV:C_GV + GLA_W], lg_scr[rows, :],
                                 sgla_scr, mall_ref, mask_ref, 2)
        p_scr[rows, C_GV:C_GV + GLA_W] = o_gla
        return carry

    lax.fori_loop(0, tb // CHUNK, chunk_body, 0)

    o_hg = _head_norm_gate(p_scr[:, C_HI:C_HI + HG_W], hgn_ref[...], p_scr[:, C_HOG:C_HOG + HG_W])
    o_gla = _head_norm_gate(p_scr[:, C_GV:C_GV + GLA_W], glan_ref[...], p_scr[:, C_GOG:C_GOG + GLA_W])
    ya = _dot(o_hg.astype(BF16), wa_ref[...])
    yb = _dot(o_gla.astype(BF16), wb_ref[...])
    merged = (_sigmoid(p_scr[:, C_GA:C_GA + D_MODEL]) * ya
              + _sigmoid(p_scr[:, C_GB:C_GB + D_MODEL]) * yb)
    m = _dot(merged.astype(BF16), wo_ref[...])
    xo_ref[0] = x + g1 * m

    @pl.when(j == pl.num_programs(1) - 1)
    def _():
        shg_o_ref[0] = shg_scr[...]
        sgla_o_ref[0] = sgla_scr[...]


def _const_spec(shape):
    nd = len(shape)
    return pl.BlockSpec(shape, lambda b, j, nd=nd: (0,) * nd, pipeline_mode=pl.Buffered(1))


def _mixer(x, mod, nrm, win, hg_lb, wgk2, bgk, hgn, glan, wa, wb, wo, shg0, sgla0, mall, masks,
           *, layer, tb):
    bsz, seq, _ = x.shape
    kern = functools.partial(_mixer_kernel, layer=layer, tb=tb)
    n_gla_tiles = GLA_KW // LANES
    return pl.pallas_call(
        kern,
        grid=(bsz, seq // tb),
        in_specs=[
            pl.BlockSpec((1, tb, D_MODEL), lambda b, j: (b, j, 0)),
            pl.BlockSpec((1, 6, D_MODEL), lambda b, j: (b, 0, 0)),
            _const_spec((1, D_MODEL)),
            _const_spec((D_MODEL, IN_COLS_PAD)),
            _const_spec((DEPTH, HG_KW)),
            _const_spec((LANES, GLA_KW)),
            _const_spec((1, GLA_KW)),
            _const_spec((1, HEAD_DV)),
            _const_spec((1, HEAD_DV)),
            _const_spec((HG_W, D_MODEL)),
            _const_spec((GLA_W, D_MODEL)),
            _const_spec((D_MODEL, D_MODEL)),
            pl.BlockSpec((1, HG_HEADS, HG_DK, HEAD_DV), lambda b, j: (b, 0, 0, 0)),
            pl.BlockSpec((1, n_gla_tiles, LANES, HEAD_DV), lambda b, j: (b, 0, 0, 0)),
            _const_spec((CUM_ROWS, CHUNK)),
            _const_spec((N_MASKS, CHUNK, CHUNK)),
        ],
        out_specs=[
            pl.BlockSpec((1, tb, D_MODEL), lambda b, j: (b, j, 0)),
            pl.BlockSpec((1, HG_HEADS, HG_DK, HEAD_DV), lambda b, j: (b, 0, 0, 0)),
            pl.BlockSpec((1, n_gla_tiles, LANES, HEAD_DV), lambda b, j: (b, 0, 0, 0)),
        ],
        out_shape=[
            jax.ShapeDtypeStruct((bsz, seq, D_MODEL), F32),
            jax.ShapeDtypeStruct((bsz, HG_HEADS, HG_DK, HEAD_DV), F32),
            jax.ShapeDtypeStruct((bsz, n_gla_tiles, LANES, HEAD_DV), F32),
        ],
        scratch_shapes=[
            pltpu.VMEM((tb, IN_COLS_PAD), F32),
            pltpu.VMEM((tb, HG_KW), F32),
            pltpu.VMEM((tb, GLA_KW), F32),
            pltpu.VMEM((HG_HEADS, HG_DK, HEAD_DV), F32),
            pltpu.VMEM((n_gla_tiles, LANES, HEAD_DV), F32),
        ],
        compiler_params=pltpu.CompilerParams(
            dimension_semantics=("arbitrary", "arbitrary"), vmem_limit_bytes=VMEM_LIMIT),
        name=f"mixer_l{layer}",
    )(x, mod, nrm, win, hg_lb, wgk2, bgk, hgn, glan, wa, wb, wo, shg0, sgla0, mall, masks)


ROUTER_ROWS = 8 + N_EXPERTS
ROUTER_TILE = 512
EXPERT_BLOCK = 256
COMBINE_TILE = 256


def _first_argmax_rows(vals, n):
    ridx = lax.broadcasted_iota(jnp.int32, vals.shape, 0)
    vmax = jnp.max(vals, axis=0, keepdims=True)
    imax = jnp.min(jnp.where(vals == vmax, ridx, n), axis=0, keepdims=True)
    return vmax, imax


def _router_kernel(x_ref, mod_ref, nrm_ref, wr_ref, br_ref, h_ref, eid_ref, wts_ref):
    u, lt, _ = x_ref.shape
    x = x_ref[...]
    sh2 = mod_ref[:, 3:4, :]
    sc2 = mod_ref[:, 4:5, :]
    h = _rms_mod(x, nrm_ref[...].reshape(1, 1, D_MODEL), sc2, sh2).reshape(u * lt, D_MODEL)
    h_ref[...] = h
    logits = lax.dot_general(wr_ref[...], h, (((1,), (1,)), ((), ())), preferred_element_type=F32,
                             precision=lax.Precision.HIGHEST) + br_ref[...]
    gl = logits[0:N_GROUPS]
    gmax, gi = _first_argmax_rows(gl, N_GROUPS)
    gp = 1.0 / jnp.sum(jnp.exp(gl - gmax), axis=0, keepdims=True)
    le = logits[8:8 + EXPERTS_PER_GROUP]
    for g in range(1, N_GROUPS):
        le = jnp.where(gi == g, logits[8 + g * EXPERTS_PER_GROUP:8 + (g + 1) * EXPERTS_PER_GROUP], le)
    pe = jnp.exp(le - jnp.max(le, axis=0, keepdims=True))
    pe = pe / jnp.sum(pe, axis=0, keepdims=True)
    v1, i1 = _first_argmax_rows(pe, EXPERTS_PER_GROUP)
    ridx = lax.broadcasted_iota(jnp.int32, pe.shape, 0)
    v2, i2 = _first_argmax_rows(jnp.where(ridx == i1, -1.0, pe), EXPERTS_PER_GROUP)
    vsum = v1 + v2
    eid_ref[0:1, :] = gi * EXPERTS_PER_GROUP + i1
    eid_ref[1:2, :] = gi * EXPERTS_PER_GROUP + i2
    wts_ref[0:1, :] = gp * v1 / vsum
    wts_ref[1:2, :] = gp * v2 / vsum


def _router(x_units, mod_units, nrm, wr, br):
    n_units, lt, _ = x_units.shape
    u = ROUTER_TILE // lt
    t = n_units * lt
    return pl.pallas_call(
        _router_kernel,
        grid=(n_units // u,),
        in_specs=[
            pl.BlockSpec((u, lt, D_MODEL), lambda i: (i, 0, 0)),
            pl.BlockSpec((u, 6, D_MODEL), lambda i: (i, 0, 0)),
            pl.BlockSpec((1, D_MODEL), lambda i: (0, 0)),
            pl.BlockSpec((ROUTER_ROWS, D_MODEL), lambda i: (0, 0)),
            pl.BlockSpec((ROUTER_ROWS, 1), lambda i: (0, 0)),
        ],
        out_specs=[
            pl.BlockSpec((ROUTER_TILE, D_MODEL), lambda i: (i, 0)),
            pl.BlockSpec((TOPK, ROUTER_TILE), lambda i: (0, i)),
            pl.BlockSpec((TOPK, ROUTER_TILE), lambda i: (0, i)),
        ],
        out_shape=[
            jax.ShapeDtypeStruct((t, D_MODEL), F32),
            jax.ShapeDtypeStruct((TOPK, t), jnp.int32),
            jax.ShapeDtypeStruct((TOPK, t), F32),
        ],
        compiler_params=pltpu.CompilerParams(dimension_semantics=("arbitrary",)),
        name="moe_router",
    )(x_units, mod_units, nrm, wr, br)


def _start_row_gather(idx_ref, n_rows, src_hbm, dst, sem):
    def body(r, carry):
        row = idx_ref[0, 0, r]
        pltpu.make_async_copy(src_hbm.at[pl.ds(row, 1)], dst.at[pl.ds(r, 1)], sem).start()
        return carry
    lax.fori_loop(0, n_rows, body, 0)


def _wait_row_gather(n_rows, src_hbm, dst, sem):
    pltpu.make_async_copy(src_hbm.at[pl.ds(0, n_rows)], dst, sem).wait()


def _experts_kernel(be_ref, nused_ref, src_cur_ref, src_nxt_ref, w_ref, h_hbm, wg_ref, wu_ref, wd_ref,
                    o_ref, buf, sem):
    i = pl.program_id(0)
    n_used = nused_ref[0]
    slot = i % 2

    @pl.when(jnp.logical_and(i == 0, n_used > 0))
    def _():
        _start_row_gather(src_cur_ref, EXPERT_BLOCK, h_hbm, buf.at[0], sem.at[0])

    @pl.when(i + 1 < n_used)
    def _():
        _start_row_gather(src_nxt_ref, EXPERT_BLOCK, h_hbm, buf.at[1 - slot], sem.at[1 - slot])

    @pl.when(i < n_used)
    def _():
        _wait_row_gather(EXPERT_BLOCK, h_hbm, buf.at[slot], sem.at[slot])
        xb = buf[slot].astype(BF16)
        a = _silu(_dot(xb, wg_ref[0])) * _dot(xb, wu_ref[0])
        o_ref[...] = _dot(a.astype(BF16), wd_ref[0]) * w_ref[...]

    @pl.when(i >= n_used)
    def _():
        o_ref[...] = jnp.zeros_like(o_ref)


def _experts(block_e, n_used, src_blocks, w_slot, h, wg, wu, wd):
    n_blocks = src_blocks.shape[0]
    grid_spec = pltpu.PrefetchScalarGridSpec(
        num_scalar_prefetch=2,
        grid=(n_blocks,),
        in_specs=[
            pl.BlockSpec((1, 1, EXPERT_BLOCK), lambda i, be, nu: (i, 0, 0), memory_space=pltpu.SMEM),
            pl.BlockSpec((1, 1, EXPERT_BLOCK), lambda i, be, nu: (jnp.minimum(i + 1, n_blocks - 1), 0, 0),
                         memory_space=pltpu.SMEM),
            pl.BlockSpec((EXPERT_BLOCK, 1), lambda i, be, nu: (i, 0)),
            pl.BlockSpec(memory_space=pl.ANY),
            pl.BlockSpec((1, D_MODEL, D_EXPERT), lambda i, be, nu: (be[i], 0, 0)),
            pl.BlockSpec((1, D_MODEL, D_EXPERT), lambda i, be, nu: (be[i], 0, 0)),
            pl.BlockSpec((1, D_EXPERT, D_MODEL), lambda i, be, nu: (be[i], 0, 0)),
        ],
        out_specs=pl.BlockSpec((EXPERT_BLOCK, D_MODEL), lambda i, be, nu: (i, 0)),
        scratch_shapes=[pltpu.VMEM((2, EXPERT_BLOCK, D_MODEL), F32), pltpu.SemaphoreType.DMA((2,))],
    )
    return pl.pallas_call(
        _experts_kernel,
        grid_spec=grid_spec,
        out_shape=jax.ShapeDtypeStruct((n_blocks * EXPERT_BLOCK, D_MODEL), F32),
        compiler_params=pltpu.CompilerParams(
            dimension_semantics=("arbitrary",), vmem_limit_bytes=VMEM_LIMIT),
        name="moe_experts",
    )(block_e, n_used, src_blocks, src_blocks, w_slot, h, wg, wu, wd)


def _combine_kernel(dst_cur_ref, dst_nxt_ref, x_ref, mod_ref, nrm_ref, y_hbm, o_ref, buf, sem, *, final_norm):
    i = pl.program_id(0)
    n = pl.num_programs(0)
    slot = i % 2
    rows = TOPK * COMBINE_TILE

    @pl.when(i == 0)
    def _():
        _start_row_gather(dst_cur_ref, rows, y_hbm, buf.at[0], sem.at[0])

    @pl.when(i + 1 < n)
    def _():
        _start_row_gather(dst_nxt_ref, rows, y_hbm, buf.at[1 - slot], sem.at[1 - slot])

    _wait_row_gather(rows, y_hbm, buf.at[slot], sem.at[slot])
    u, lt, _ = x_ref.shape
    y = buf[slot, 0:COMBINE_TILE, :] + buf[slot, COMBINE_TILE:rows, :]
    g2 = mod_ref[:, 5:6, :]
    out = x_ref[...] + g2 * y.reshape(u, lt, D_MODEL)
    if final_norm:
        out = out * lax.rsqrt(jnp.mean(out * out, axis=-1, keepdims=True) + NORM_EPS)
        out = out * nrm_ref[...].reshape(1, 1, D_MODEL)
    o_ref[...] = out


def _combine(dest_tiles, x_units, mod_units, nrm, y_slots, *, final_norm):
    n_units, lt, _ = x_units.shape
    u = COMBINE_TILE // lt
    n_tiles = n_units // u
    rows = TOPK * COMBINE_TILE
    return pl.pallas_call(
        functools.partial(_combine_kernel, final_norm=final_norm),
        grid=(n_tiles,),
        in_specs=[
            pl.BlockSpec((1, 1, rows), lambda i: (i, 0, 0), memory_space=pltpu.SMEM),
            pl.BlockSpec((1, 1, rows), lambda i: (jnp.minimum(i + 1, n_tiles - 1), 0, 0),
                         memory_space=pltpu.SMEM),
            pl.BlockSpec((u, lt, D_MODEL), lambda i: (i, 0, 0)),
            pl.BlockSpec((u, 6, D_MODEL), lambda i: (i, 0, 0)),
            pl.BlockSpec((1, D_MODEL), lambda i: (0, 0)),
            pl.BlockSpec(memory_space=pl.ANY),
        ],
        out_specs=pl.BlockSpec((u, lt, D_MODEL), lambda i: (i, 0, 0)),
        out_shape=jax.ShapeDtypeStruct(x_units.shape, F32),
        scratch_shapes=[pltpu.VMEM((2, rows, D_MODEL), F32), pltpu.SemaphoreType.DMA((2,))],
        compiler_params=pltpu.CompilerParams(dimension_semantics=("arbitrary",)),
        name="moe_combine",
    )(dest_tiles, dest_tiles, x_units, mod_units, nrm, y_slots)


def _routing_tables(eid, wts):
    t = eid.shape[1]
    a = TOPK * t
    n_blocks = a // EXPERT_BLOCK + N_EXPERTS
    flat_e = eid.reshape(a)
    order = jnp.argsort(flat_e, stable=True).astype(jnp.int32)
    sorted_e = flat_e[order]
    experts = jnp.arange(N_EXPERTS, dtype=jnp.int32)
    starts = jnp.searchsorted(sorted_e, experts, side='left').astype(jnp.int32)
    counts = jnp.searchsorted(sorted_e, experts, side='right').astype(jnp.int32) - starts
    padded = (counts + EXPERT_BLOCK - 1) // EXPERT_BLOCK * EXPERT_BLOCK
    pad_end = jnp.cumsum(padded).astype(jnp.int32)
    pad_start = pad_end - padded
    block_e = jnp.minimum(
        jnp.searchsorted(pad_end, jnp.arange(n_blocks, dtype=jnp.int32) * EXPERT_BLOCK, side='right'),
        N_EXPERTS - 1).astype(jnp.int32)
    n_used = (pad_end[-1:] // EXPERT_BLOCK).astype(jnp.int32)
    slot = jnp.arange(n_blocks * EXPERT_BLOCK, dtype=jnp.int32)
    slot_e = jnp.repeat(block_e, EXPERT_BLOCK)
    within = slot - pad_start[slot_e]
    valid = within < counts[slot_e]
    sorted_pos = jnp.clip(starts[slot_e] + within, 0, a - 1)
    slot_assign = order[sorted_pos]
    src_token = jnp.where(valid, slot_assign % t, 0).astype(jnp.int32)
    w_slot = jnp.where(valid, wts.reshape(a)[slot_assign], 0.0)
    dest_sorted = pad_start[sorted_e] + jnp.arange(a, dtype=jnp.int32) - starts[sorted_e]
    dest = jnp.zeros((a,), jnp.int32).at[order].set(dest_sorted, unique_indices=True)
    return block_e, n_used, src_token, w_slot, dest


def _moe_layer(x, mod_l, nrm_ffn, wr, br, wg, wu, wd, nrm_final, *, final_norm):
    bsz, seq, _ = x.shape
    t = bsz * seq
    lt = min(seq, COMBINE_TILE)
    per = seq // lt
    x_units = x.reshape(t // lt, lt, D_MODEL)
    mod_units = jnp.repeat(mod_l, per, axis=0) if per > 1 else mod_l
    h, eid, wts = _router(x_units, mod_units, nrm_ffn, wr, br)
    block_e, n_used, src_token, w_slot, dest = _routing_tables(eid, wts)
    n_blocks = block_e.shape[0]
    y_slots = _experts(block_e, n_used, src_token.reshape(n_blocks, 1, EXPERT_BLOCK),
                       w_slot.reshape(-1, 1), h, wg, wu, wd)
    n_tiles = t // COMBINE_TILE
    dest_tiles = dest.reshape(TOPK, n_tiles, COMBINE_TILE).transpose(1, 0, 2).reshape(
        n_tiles, 1, TOPK * COMBINE_TILE)
    out = _combine(dest_tiles, x_units, mod_units, nrm_final, y_slots, final_norm=final_norm)
    return out.reshape(bsz, seq, D_MODEL)


def kernel(x_prompt, x_sample, c_prompt, c_sample, state_hgrn, state_gla, w_ada, b_ada, norm_mix,
           norm_ffn, w_in, hg_lb, hg_onorm, w_gk2, b_gk, gla_onorm, w_br_a, w_br_b, w_out, w_rg, b_rg,
           w_re, b_re, w_e_gate, w_e_up, w_e_down, norm_final):
    bp = x_prompt.shape[0]
    bs = x_sample.shape[0]
    mod = _ada_mod(jnp.concatenate([c_prompt, c_sample], axis=0), w_ada, b_ada)
    mod = mod.reshape(DEPTH, bp + bs, 6, D_MODEL)

    glr0 = C_GOG + GLA_W
    win_r = jnp.concatenate(
        [w_in[:, :, :glr0], w_in[:, :, glr0 + GLA_GATE_RANK:], w_in[:, :, glr0:glr0 + GLA_GATE_RANK],
         jnp.zeros((DEPTH, D_MODEL, LANES - GLA_GATE_RANK), F32)], axis=2).astype(BF16)
    wgk2_p = jnp.concatenate(
        [w_gk2, jnp.zeros((DEPTH, LANES - GLA_GATE_RANK, GLA_KW), F32)], axis=1).astype(BF16)
    wa_b = w_br_a.astype(BF16)
    wb_b = w_br_b.astype(BF16)
    wo_b = w_out.astype(BF16)
    mall = jnp.asarray(_segment_sum_matrix(), BF16)
    masks = jnp.asarray(_level_masks(), F32)
    zpad = jnp.zeros((DEPTH, 8 - N_GROUPS, D_MODEL), F32)
    wr = jnp.concatenate([jnp.swapaxes(w_rg, 1, 2), zpad, jnp.swapaxes(w_re, 1, 2)], axis=1)
    br = jnp.concatenate([b_rg, jnp.zeros((DEPTH, 8 - N_GROUPS), F32), b_re], axis=1)[:, :, None]
    wg_b = w_e_gate.astype(BF16)
    wu_b = w_e_up.astype(BF16)
    wd_b = w_e_down.astype(BF16)
    nrm_f = norm_final.reshape(1, D_MODEL)

    def run(x, mod_g, shg, sgla, tb):
        bsz = x.shape[0]
        new_hg, new_gla = [], []
        for l in range(DEPTH):
            x, s1, s2 = _mixer(
                x, mod_g[l], norm_mix[l:l + 1], win_r[l], hg_lb, wgk2_p[l], b_gk[l:l + 1],
                hg_onorm[l:l + 1], gla_onorm[l:l + 1], wa_b[l], wb_b[l], wo_b[l],
                shg[l], sgla[l].reshape(bsz, GLA_KW // LANES, LANES, HEAD_DV), mall, masks,
                layer=l, tb=tb)
            new_hg.append(s1)
            new_gla.append(s2.reshape(bsz, GLA_HEADS, GLA_DK, HEAD_DV))
            x = _moe_layer(x, mod_g[l], norm_ffn[l:l + 1], wr[l], br[l], wg_b[l], wu_b[l], wd_b[l],
                           nrm_f, final_norm=(l == DEPTH - 1))
        return x, jnp.stack(new_hg), jnp.stack(new_gla)

    zeros_hg = jnp.zeros((DEPTH, bp, HG_HEADS, HG_DK, HEAD_DV), F32)
    zeros_gla = jnp.zeros((DEPTH, bp, GLA_HEADS, GLA_DK, HEAD_DV), F32)
    y_p, hg_p, gla_p = run(x_prompt, mod[:, :bp], zeros_hg, zeros_gla, 256)
    y_s, hg_s, gla_s = run(x_sample, mod[:, bp:], state_hgrn, state_gla, CHUNK)
    return (y_p, y_s, hg_p, gla_p, hg_s, gla_s)
```

```python
import functools

import numpy as np
import jax
import jax.numpy as jnp
from jax import lax
from jax.experimental import pallas as pl
from jax.experimental.pallas import tpu as pltpu

F32 = jnp.float32
BF16 = jnp.bfloat16

D_MODEL = 1024
DEPTH = 2
CHUNK = 64
NORM_EPS = 1e-6
LOG_FLOOR = 1e-30
HG_HEADS = 4
HG_DK = 128
HEAD_DV = 128
HG_KW = HG_HEADS * HG_DK
HG_W = HG_HEADS * HEAD_DV
GLA_HEADS = 4
GLA_DK = 64
GLA_KW = GLA_HEADS * GLA_DK
GLA_W = GLA_HEADS * HEAD_DV
GLA_GATE_RANK = 16
GLA_GATE_NORM = 16.0
N_GROUPS = 4
EXPERTS_PER_GROUP = 8
N_EXPERTS = N_GROUPS * EXPERTS_PER_GROUP
TOPK = 2
D_EXPERT = 512

LANES = 128
VMEM_LIMIT = 56 * 1024 * 1024

C_HQ = 0
C_HF = C_HQ + HG_KW
C_HI = C_HF + HG_KW
C_HOG = C_HI + HG_W
C_GQ = C_HOG + HG_W
C_GK = C_GQ + GLA_KW
C_GV = C_GK + GLA_KW
C_GOG = C_GV + GLA_W
C_GA = C_GOG + GLA_W
C_GB = C_GA + D_MODEL
C_GLR = C_GB + D_MODEL
IN_COLS_PAD = C_GLR + LANES
PROJ_TILE = 640
assert IN_COLS_PAD % PROJ_TILE == 0

LEVELS = (32, 16, 8, 4, 2)
N_MASKS = len(LEVELS) + 2
CUM_ROWS = (2 * len(LEVELS) + 1) * CHUNK


def _segment_sum_matrix():
    t = np.arange(CHUNK)[:, None]
    r = np.arange(CHUNK)[None, :]
    rows = []
    for m in LEVELS:
        same = (t // m) == (r // m)
        rows.append(same & (r <= t))
        rows.append(same & (r > t))
    rows.append(r <= t)
    return np.concatenate(rows, axis=0).astype(np.float32)


def _level_masks():
    t = np.arange(CHUNK)[:, None]
    s = np.arange(CHUNK)[None, :]
    masks = [t == s]
    for m in LEVELS + (1,):
        masks.append(((t // (2 * m)) == (s // (2 * m))) & ((t // m) % 2 == 1) & ((s // m) % 2 == 0))
    return np.stack(masks).astype(np.float32)


def _dot(a, b):
    return jnp.dot(a, b, preferred_element_type=F32)


def _dot_nt(a, b):
    return lax.dot_general(a, b, (((1,), (1,)), ((), ())), preferred_element_type=F32)


def _sigmoid(x):
    return 1.0 / (1.0 + jnp.exp(-x))


def _silu(x):
    return x * _sigmoid(x)


def _rms_mod(x, gain, scale, shift):
    y = x * lax.rsqrt(jnp.mean(x * x, axis=-1, keepdims=True) + NORM_EPS)
    return y * gain * (1.0 + scale) + shift


def _ada_kernel(c_ref, w_ref, b_ref, o_ref):
    c = c_ref[...]
    o_ref[0] = jnp.dot(_silu(c), w_ref[0], preferred_element_type=F32,
                       precision=lax.Precision.HIGHEST) + b_ref[0]


def _ada_mod(c_all, w_ada, b_ada):
    nb = c_all.shape[0]
    tn = 512
    return pl.pallas_call(
        _ada_kernel,
        grid=(DEPTH, 6 * D_MODEL // tn),
        in_specs=[
            pl.BlockSpec((nb, D_MODEL), lambda l, j: (0, 0)),
            pl.BlockSpec((1, D_MODEL, tn), lambda l, j: (l, 0, j)),
            pl.BlockSpec((1, 1, tn), lambda l, j: (l, 0, j)),
        ],
        out_specs=pl.BlockSpec((1, nb, tn), lambda l, j: (l, 0, j)),
        out_shape=jax.ShapeDtypeStruct((DEPTH, nb, 6 * D_MODEL), F32),
        name="ada_mod",
    )(c_all, w_ada, b_ada.reshape(DEPTH, 1, 6 * D_MODEL))


def _chunk_attention(q, k, v, g, s_ref, mall_ref, mask_ref, heads_per_tile):
    w = q.shape[1]
    n_tiles = w // LANES
    g_hi = g.astype(BF16)
    r1 = g - g_hi.astype(F32)
    g_mid = r1.astype(BF16)
    g_lo = (r1 - g_mid.astype(F32)).astype(BF16)
    mall = mall_ref[...]
    cums = _dot(mall, g_hi) + _dot(mall, g_mid) + _dot(mall, g_lo)
    b = cums[CUM_ROWS - CHUNK:CUM_ROWS]
    qs = [q]
    ks = [k]
    for i in range(len(LEVELS)):
        qs.append(q * jnp.exp(cums[2 * i * CHUNK:(2 * i + 1) * CHUNK]))
        ks.append(k * jnp.exp(cums[(2 * i + 1) * CHUNK:(2 * i + 2) * CHUNK]))
    qs.append(q * jnp.exp(g))
    ks.append(k)
    b_last = b[CHUNK - 1:CHUNK]
    q_in = q * jnp.exp(b)
    k_out = k * jnp.exp(b_last - b)
    e_last = jnp.exp(b_last)

    dk = LANES // heads_per_tile
    lane = lax.broadcasted_iota(jnp.int32, (CHUNK, LANES), 1)
    row = lax.broadcasted_iota(jnp.int32, (LANES, HEAD_DV), 0)
    outs = []
    for ti in range(n_tiles):
        sl = slice(ti * LANES, (ti + 1) * LANES)
        ks_t = [kk[:, sl].astype(BF16) for kk in ks]
        k_out_t = k_out[:, sl].T.astype(BF16)
        e_col = jnp.broadcast_to(e_last[:, sl], (LANES, LANES)).T
        s_old = s_ref[ti]
        s_old_b = s_old.astype(BF16)
        upd = None
        for j in range(heads_per_tile):
            head = ti * heads_per_tile + j
            if heads_per_tile == 1:
                sel = lambda a: a
            else:
                in_head = (lane // dk) == j
                sel = lambda a, in_head=in_head: jnp.where(in_head, a, 0.0)
            sc = jnp.zeros((CHUNK, CHUNK), F32)
            for i in range(N_MASKS):
                sc = sc + _dot_nt(sel(qs[i][:, sl]).astype(BF16), ks_t[i]) * mask_ref[i]
            vh = v[:, head * HEAD_DV:(head + 1) * HEAD_DV].astype(BF16)
            o = _dot(sc.astype(BF16), vh) + _dot(sel(q_in[:, sl]).astype(BF16), s_old_b)
            outs.append(o)
            u = _dot(k_out_t, vh)
            upd = u if upd is None else jnp.where((row // dk) == j, u, upd)
        s_ref[ti] = e_col * s_old + upd
    return jnp.concatenate(outs, axis=1)


def _head_norm_gate(o, gain, gate):
    outs = []
    for h in range(o.shape[1] // HEAD_DV):
        sl = slice(h * HEAD_DV, (h + 1) * HEAD_DV)
        oh = o[:, sl]
        oh = oh * lax.rsqrt(jnp.mean(oh * oh, axis=-1, keepdims=True) + NORM_EPS) * gain
        outs.append(oh * _silu(gate[:, sl]))
    return jnp.concatenate(outs, axis=1)


def _mixer_kernel(x_ref, mod_ref, nrm_ref, win_ref, lb_ref, wgk2_ref, bgk_ref, hgn_ref, glan_ref,
                  wa_ref, wb_ref, wo_ref, shg0_ref, sgla0_ref, mall_ref, mask_ref,
                  xo_ref, shg_o_ref, sgla_o_ref,
                  p_scr, k_scr, lg_scr, shg_scr, sgla_scr, *, layer, tb):
    j = pl.program_id(1)

    @pl.when(j == 0)
    def _():
        shg_scr[...] = shg0_ref[0]
        sgla_scr[...] = sgla0_ref[0]

    x = x_ref[0]
    sh1 = mod_ref[0, 0:1, :]
    sc1 = mod_ref[0, 1:2, :]
    g1 = mod_ref[0, 2:3, :]
    hb = _rms_mod(x, nrm_ref[...], sc1, sh1).astype(BF16)
    for c in range(0, IN_COLS_PAD, PROJ_TILE):
        p_scr[:, c:c + PROJ_TILE] = _dot(hb, win_ref[:, c:c + PROJ_TILE])

    lb_all = lb_ref[...]
    lb_max = jnp.max(lb_all, axis=0, keepdims=True)
    lb_exp = jnp.exp(lb_all - lb_max)
    sm = lb_exp / jnp.sum(lb_exp, axis=0, keepdims=True)
    lbl = jnp.clip(jnp.sum(sm[0:layer + 1], axis=0, keepdims=True) - sm[0:1], 0.0, 1.0)

    p_scr[:, C_HQ:C_HQ + HG_KW] = _silu(p_scr[:, C_HQ:C_HQ + HG_KW]) * (HG_DK ** -0.5)
    z = p_scr[:, C_HF:C_HF + HG_KW]
    f = lbl + (1.0 - lbl) * _sigmoid(z)
    p_scr[:, C_HF:C_HF + HG_KW] = jnp.log(jnp.maximum(f, LOG_FLOOR))
    k_scr[...] = (1.0 - lbl) * _sigmoid(-z)
    glr = p_scr[:, C_GLR:C_GLR + LANES].astype(BF16)
    gate = _dot(glr, wgk2_ref[...]) + bgk_ref[...]
    lg_scr[...] = (jnp.minimum(gate, 0.0) - jnp.log1p(jnp.exp(-jnp.abs(gate)))) * (1.0 / GLA_GATE_NORM)
    p_scr[:, C_GQ:C_GQ + GLA_KW] = p_scr[:, C_GQ:C_GQ + GLA_KW] * (GLA_DK ** -0.5)

    def chunk_body(ci, carry):
        rows = pl.ds(pl.multiple_of(ci * CHUNK, CHUNK), CHUNK)
        o_hg = _chunk_attention(p_scr[rows, C_HQ:C_HQ + HG_KW], k_scr[rows, :],
                                p_scr[rows, C_HI:C_HI + HG_W], p_scr[rows, C_HF:C_HF + HG_KW],
                                shg_scr, mall_ref, mask_ref, 1)
        p_scr[rows, C_HI:C_HI + HG_W] = o_hg
        o_gla = _chunk_attention(p_scr[rows, C_GQ:C_GQ + GLA_KW], p_scr[rows, C_GK:C_GK + GLA_KW],
                                 p_scr[rows, C_GV:C_GV + GLA_W], lg_scr[rows, :],
                                 sgla_scr, mall_ref, mask_ref, 2)
        p_scr[rows, C_GV:C_GV + GLA_W] = o_gla
        return carry

    lax.fori_loop(0, tb // CHUNK, chunk_body, 0)

    o_hg = _head_norm_gate(p_scr[:, C_HI:C_HI + HG_W], hgn_ref[...], p_scr[:, C_HOG:C_HOG + HG_W])
    o_gla = _head_norm_gate(p_scr[:, C_GV:C_GV + GLA_W], glan_ref[...], p_scr[:, C_GOG:C_GOG + GLA_W])
    ya = _dot(o_hg.astype(BF16), wa_ref[...])
    yb = _dot(o_gla.astype(BF16), wb_ref[...])
    merged = (_sigmoid(p_scr[:, C_GA:C_GA + D_MODEL]) * ya
              + _sigmoid(p_scr[:, C_GB:C_GB + D_MODEL]) * yb)
    m = _dot(merged.astype(BF16), wo_ref[...])
    xo_ref[0] = x + g1 * m

    @pl.when(j == pl.num_programs(1) - 1)
    def _():
        shg_o_ref[0] = shg_scr[...]
        sgla_o_ref[0] = sgla_scr[...]


def _const_spec(shape):
    nd = len(shape)
    return pl.BlockSpec(shape, lambda b, j, nd=nd: (0,) * nd, pipeline_mode=pl.Buffered(1))


def _mixer(x, mod, nrm, win, hg_lb, wgk2, bgk, hgn, glan, wa, wb, wo, shg0, sgla0, mall, masks,
           *, layer, tb):
    bsz, seq, _ = x.shape
    kern = functools.partial(_mixer_kernel, layer=layer, tb=tb)
    n_gla_tiles = GLA_KW // LANES
    return pl.pallas_call(
        kern,
        grid=(bsz, seq // tb),
        in_specs=[
            pl.BlockSpec((1, tb, D_MODEL), lambda b, j: (b, j, 0)),
            pl.BlockSpec((1, 6, D_MODEL), lambda b, j: (b, 0, 0)),
            _const_spec((1, D_MODEL)),
            _const_spec((D_MODEL, IN_COLS_PAD)),
            _const_spec((DEPTH, HG_KW)),
            _const_spec((LANES, GLA_KW)),
            _const_spec((1, GLA_KW)),
            _const_spec((1, HEAD_DV)),
            _const_spec((1, HEAD_DV)),
            _const_spec((HG_W, D_MODEL)),
            _const_spec((GLA_W, D_MODEL)),
            _const_spec((D_MODEL, D_MODEL)),
            pl.BlockSpec((1, HG_HEADS, HG_DK, HEAD_DV), lambda b, j: (b, 0, 0, 0)),
            pl.BlockSpec((1, n_gla_tiles, LANES, HEAD_DV), lambda b, j: (b, 0, 0, 0)),
            _const_spec((CUM_ROWS, CHUNK)),
            _const_spec((N_MASKS, CHUNK, CHUNK)),
        ],
        out_specs=[
            pl.BlockSpec((1, tb, D_MODEL), lambda b, j: (b, j, 0)),
            pl.BlockSpec((1, HG_HEADS, HG_DK, HEAD_DV), lambda b, j: (b, 0, 0, 0)),
            pl.BlockSpec((1, n_gla_tiles, LANES, HEAD_DV), lambda b, j: (b, 0, 0, 0)),
        ],
        out_shape=[
            jax.ShapeDtypeStruct((bsz, seq, D_MODEL), F32),
            jax.ShapeDtypeStruct((bsz, HG_HEADS, HG_DK, HEAD_DV), F32),
            jax.ShapeDtypeStruct((bsz, n_gla_tiles, LANES, HEAD_DV), F32),
        ],
        scratch_shapes=[
            pltpu.VMEM((tb, IN_COLS_PAD), F32),
            pltpu.VMEM((tb, HG_KW), F32),
            pltpu.VMEM((tb, GLA_KW), F32),
            pltpu.VMEM((HG_HEADS, HG_DK, HEAD_DV), F32),
            pltpu.VMEM((n_gla_tiles, LANES, HEAD_DV), F32),
        ],
        compiler_params=pltpu.CompilerParams(
            dimension_semantics=("arbitrary", "arbitrary"), vmem_limit_bytes=VMEM_LIMIT),
        name=f"mixer_l{layer}",
    )(x, mod, nrm, win, hg_lb, wgk2, bgk, hgn, glan, wa, wb, wo, shg0, sgla0, mall, masks)


ROUTER_ROWS = 8 + N_EXPERTS
MOE_TILE = 512
TILE_ASSIGN = TOPK * MOE_TILE
EXPERT_BLOCK = 512


def _first_argmax_rows(vals, n):
    ridx = lax.broadcasted_iota(jnp.int32, vals.shape, 0)
    vmax = jnp.max(vals, axis=0, keepdims=True)
    imax = jnp.min(jnp.where(vals == vmax, ridx, n), axis=0, keepdims=True)
    return vmax, imax


def _router_kernel(x_ref, mod_ref, nrm_ref, wr_ref, br_ref, tri_ref,
                   h_ref, eid_ref, rank_ref, wts_ref, cnt_ref, run_scr):
    @pl.when(pl.program_id(0) == 0)
    def _():
        run_scr[...] = jnp.zeros_like(run_scr)

    u, lt, _ = x_ref.shape
    x = x_ref[...]
    sh2 = mod_ref[:, 3:4, :]
    sc2 = mod_ref[:, 4:5, :]
    h = _rms_mod(x, nrm_ref[...].reshape(1, 1, D_MODEL), sc2, sh2).reshape(u * lt, D_MODEL)
    h_ref[...] = h
    logits = lax.dot_general(wr_ref[...], h, (((1,), (1,)), ((), ())), preferred_element_type=F32,
                             precision=lax.Precision.HIGHEST) + br_ref[...]
    gl = logits[0:N_GROUPS]
    gmax, gi = _first_argmax_rows(gl, N_GROUPS)
    gp = 1.0 / jnp.sum(jnp.exp(gl - gmax), axis=0, keepdims=True)
    le = logits[8:8 + EXPERTS_PER_GROUP]
    for g in range(1, N_GROUPS):
        le = jnp.where(gi == g, logits[8 + g * EXPERTS_PER_GROUP:8 + (g + 1) * EXPERTS_PER_GROUP], le)
    pe = jnp.exp(le - jnp.max(le, axis=0, keepdims=True))
    pe = pe / jnp.sum(pe, axis=0, keepdims=True)
    v1, i1 = _first_argmax_rows(pe, EXPERTS_PER_GROUP)
    ridx = lax.broadcasted_iota(jnp.int32, pe.shape, 0)
    v2, i2 = _first_argmax_rows(jnp.where(ridx == i1, -1.0, pe), EXPERTS_PER_GROUP)
    vsum = v1 + v2
    wts_ref[0:1, :] = gp * v1 / vsum
    wts_ref[1:2, :] = gp * v2 / vsum
    eflat = jnp.concatenate([gi * EXPERTS_PER_GROUP + i1, gi * EXPERTS_PER_GROUP + i2], axis=1)
    eid_ref[0] = eflat
    onehot = (eflat == lax.broadcasted_iota(jnp.int32, (N_EXPERTS, TILE_ASSIGN), 0)).astype(F32)
    before = _dot(onehot.astype(BF16), tri_ref[...]) + run_scr[...]
    rank_ref[0] = jnp.sum(onehot * before, axis=0, keepdims=True).astype(jnp.int32)
    run_scr[...] = run_scr[...] + jnp.sum(onehot, axis=1, keepdims=True)
    cnt_ref[...] = run_scr[...].astype(jnp.int32)


def _router(x_units, mod_units, nrm, wr, br, tri):
    n_units, lt, _ = x_units.shape
    u = MOE_TILE // lt
    n_tiles = n_units // u
    return pl.pallas_call(
        _router_kernel,
        grid=(n_tiles,),
        in_specs=[
            pl.BlockSpec((u, lt, D_MODEL), lambda i: (i, 0, 0)),
            pl.BlockSpec((u, 6, D_MODEL), lambda i: (i, 0, 0)),
            pl.BlockSpec((1, D_MODEL), lambda i: (0, 0)),
            pl.BlockSpec((ROUTER_ROWS, D_MODEL), lambda i: (0, 0)),
            pl.BlockSpec((ROUTER_ROWS, 1), lambda i: (0, 0)),
            pl.BlockSpec((TILE_ASSIGN, TILE_ASSIGN), lambda i: (0, 0)),
        ],
        out_specs=[
            pl.BlockSpec((MOE_TILE, D_MODEL), lambda i: (i, 0)),
            pl.BlockSpec((1, 1, TILE_ASSIGN), lambda i: (i, 0, 0)),
            pl.BlockSpec((1, 1, TILE_ASSIGN), lambda i: (i, 0, 0)),
            pl.BlockSpec((TOPK, MOE_TILE), lambda i: (0, i)),
            pl.BlockSpec((N_EXPERTS, 1), lambda i: (0, 0)),
        ],
        out_shape=[
            jax.ShapeDtypeStruct((n_tiles * MOE_TILE, D_MODEL), F32),
            jax.ShapeDtypeStruct((n_tiles, 1, TILE_ASSIGN), jnp.int32),
            jax.ShapeDtypeStruct((n_tiles, 1, TILE_ASSIGN), jnp.int32),
            jax.ShapeDtypeStruct((TOPK, n_tiles * MOE_TILE), F32),
            jax.ShapeDtypeStruct((N_EXPERTS, 1), jnp.int32),
        ],
        scratch_shapes=[pltpu.VMEM((N_EXPERTS, 1), F32)],
        compiler_params=pltpu.CompilerParams(dimension_semantics=("arbitrary",)),
        name="moe_router",
    )(x_units, mod_units, nrm, wr, br, tri)


def _start_row_gather(idx_ref, n_rows, src_hbm, dst, sem):
    def body(r, carry):
        row = idx_ref[0, 0, r]
        pltpu.make_async_copy(src_hbm.at[pl.ds(row, 1)], dst.at[pl.ds(r, 1)], sem).start()
        return carry
    lax.fori_loop(0, n_rows, body, 0, unroll=8)


def _wait_row_gather(n_rows, src_hbm, dst, sem):
    pltpu.make_async_copy(src_hbm.at[pl.ds(0, n_rows)], dst, sem).wait()


def _dispatch_kernel(pend_ref, padded_ref, dest_ref, h_ref, xs_hbm, zbuf, sem):
    n_blocks = xs_hbm.shape[0] // EXPERT_BLOCK

    def zero_block(first_row):
        return pltpu.make_async_copy(
            zbuf, xs_hbm.at[pl.ds(pl.multiple_of(first_row, EXPERT_BLOCK), EXPERT_BLOCK)], sem.at[0])

    @pl.when(pl.program_id(0) == 0)
    def _():
        zbuf[...] = jnp.zeros_like(zbuf)
        n_used = pend_ref[N_EXPERTS - 1] // EXPERT_BLOCK
        for e in range(N_EXPERTS):
            @pl.when(padded_ref[e] > 0)
            def _():
                zero_block(pend_ref[e] - EXPERT_BLOCK).start()
        lax.fori_loop(n_used, n_blocks, lambda b, c: (zero_block(b * EXPERT_BLOCK).start(), c)[1], 0)
        for e in range(N_EXPERTS):
            @pl.when(padded_ref[e] > 0)
            def _():
                zero_block(pend_ref[e] - EXPERT_BLOCK).wait()
        lax.fori_loop(n_used, n_blocks, lambda b, c: (zero_block(b * EXPERT_BLOCK).wait(), c)[1], 0)

    def body(t, carry):
        for k in range(TOPK):
            slot = dest_ref[0, 0, k * MOE_TILE + t]
            pltpu.make_async_copy(h_ref.at[pl.ds(t, 1)], xs_hbm.at[pl.ds(slot, 1)], sem.at[1]).start()
        return carry
    lax.fori_loop(0, MOE_TILE, body, 0, unroll=8)
    for k in range(TOPK):
        pltpu.make_async_copy(h_ref, xs_hbm.at[pl.ds(0, MOE_TILE)], sem.at[1]).wait()


def _dispatch(pad_end, padded, dest_tiles, h, n_slots):
    n_tiles = dest_tiles.shape[0]
    grid_spec = pltpu.PrefetchScalarGridSpec(
        num_scalar_prefetch=2,
        grid=(n_tiles,),
        in_specs=[
            pl.BlockSpec((1, 1, TILE_ASSIGN), lambda i, pe, pd: (i, 0, 0), memory_space=pltpu.SMEM),
            pl.BlockSpec((MOE_TILE, D_MODEL), lambda i, pe, pd: (i, 0)),
        ],
        out_specs=pl.BlockSpec(memory_space=pl.ANY),
        scratch_shapes=[pltpu.VMEM((EXPERT_BLOCK, D_MODEL), F32), pltpu.SemaphoreType.DMA((2,))],
    )
    return pl.pallas_call(
        _dispatch_kernel,
        grid_spec=grid_spec,
        out_shape=jax.ShapeDtypeStruct((n_slots, D_MODEL), F32),
        compiler_params=pltpu.CompilerParams(dimension_semantics=("arbitrary",)),
        name="moe_dispatch",
    )(pad_end, padded, dest_tiles, h)


def _experts_kernel(be_ref, nused_ref, x_ref, wg_ref, wu_ref, wd_ref, o_ref):
    @pl.when(pl.program_id(0) < nused_ref[0])
    def _():
        xb = x_ref[...].astype(BF16)
        a = _silu(_dot(xb, wg_ref[0])) * _dot(xb, wu_ref[0])
        o_ref[...] = _dot(a.astype(BF16), wd_ref[0])

    @pl.when(pl.program_id(0) >= nused_ref[0])
    def _():
        o_ref[...] = jnp.zeros_like(o_ref)


def _experts(block_e, n_used, xs, wg, wu, wd):
    n_blocks = xs.shape[0] // EXPERT_BLOCK

    def row_block(i, be, nu):
        return (jnp.minimum(i, nu[0] - 1), 0)

    def expert_block(i, be, nu):
        return (be[jnp.minimum(i, nu[0] - 1)], 0, 0)

    grid_spec = pltpu.PrefetchScalarGridSpec(
        num_scalar_prefetch=2,
        grid=(n_blocks,),
        in_specs=[
            pl.BlockSpec((EXPERT_BLOCK, D_MODEL), row_block),
            pl.BlockSpec((1, D_MODEL, D_EXPERT), expert_block),
            pl.BlockSpec((1, D_MODEL, D_EXPERT), expert_block),
            pl.BlockSpec((1, D_EXPERT, D_MODEL), expert_block),
        ],
        out_specs=pl.BlockSpec((EXPERT_BLOCK, D_MODEL), lambda i, be, nu: (i, 0)),
    )
    return pl.pallas_call(
        _experts_kernel,
        grid_spec=grid_spec,
        out_shape=jax.ShapeDtypeStruct(xs.shape, F32),
        compiler_params=pltpu.CompilerParams(
            dimension_semantics=("arbitrary",), vmem_limit_bytes=VMEM_LIMIT),
        name="moe_experts",
    )(block_e, n_used, xs, wg, wu, wd)


def _combine_kernel(dst_cur_ref, dst_nxt_ref, x_ref, mod_ref, wts_ref, nrm_ref, y_hbm, o_ref, buf, sem,
                    *, final_norm):
    i = pl.program_id(0)
    n = pl.num_programs(0)
    slot = i % 2

    @pl.when(i == 0)
    def _():
        _start_row_gather(dst_cur_ref, TILE_ASSIGN, y_hbm, buf.at[0], sem.at[0])

    @pl.when(i + 1 < n)
    def _():
        _start_row_gather(dst_nxt_ref, TILE_ASSIGN, y_hbm, buf.at[1 - slot], sem.at[1 - slot])

    _wait_row_gather(TILE_ASSIGN, y_hbm, buf.at[slot], sem.at[slot])
    u, lt, _ = x_ref.shape
    y = (wts_ref[:, 0:1] * buf[slot, 0:MOE_TILE, :] + wts_ref[:, 1:2] * buf[slot, MOE_TILE:TILE_ASSIGN, :])
    g2 = mod_ref[:, 5:6, :]
    out = x_ref[...] + g2 * y.reshape(u, lt, D_MODEL)
    if final_norm:
        out = out * lax.rsqrt(jnp.mean(out * out, axis=-1, keepdims=True) + NORM_EPS)
        out = out * nrm_ref[...].reshape(1, 1, D_MODEL)
    o_ref[...] = out


def _combine(dest_tiles, x_units, mod_units, wts_col, nrm, y_slots, *, final_norm):
    n_units, lt, _ = x_units.shape
    u = MOE_TILE // lt
    n_tiles = n_units // u
    return pl.pallas_call(
        functools.partial(_combine_kernel, final_norm=final_norm),
        grid=(n_tiles,),
        in_specs=[
            pl.BlockSpec((1, 1, TILE_ASSIGN), lambda i: (i, 0, 0), memory_space=pltpu.SMEM),
            pl.BlockSpec((1, 1, TILE_ASSIGN), lambda i: (jnp.minimum(i + 1, n_tiles - 1), 0, 0),
                         memory_space=pltpu.SMEM),
            pl.BlockSpec((u, lt, D_MODEL), lambda i: (i, 0, 0)),
            pl.BlockSpec((u, 6, D_MODEL), lambda i: (i, 0, 0)),
            pl.BlockSpec((MOE_TILE, TOPK), lambda i: (i, 0)),
            pl.BlockSpec((1, D_MODEL), lambda i: (0, 0)),
            pl.BlockSpec(memory_space=pl.ANY),
        ],
        out_specs=pl.BlockSpec((u, lt, D_MODEL), lambda i: (i, 0, 0)),
        out_shape=jax.ShapeDtypeStruct(x_units.shape, F32),
        scratch_shapes=[pltpu.VMEM((2, TILE_ASSIGN, D_MODEL), F32), pltpu.SemaphoreType.DMA((2,))],
        compiler_params=pltpu.CompilerParams(
            dimension_semantics=("arbitrary",), vmem_limit_bytes=VMEM_LIMIT),
        name="moe_combine",
    )(dest_tiles, dest_tiles, x_units, mod_units, wts_col, nrm, y_slots)


def _routing_tables(eid_tiles, rank_tiles, counts):
    n_blocks = eid_tiles.size // EXPERT_BLOCK + N_EXPERTS
    padded = (counts + EXPERT_BLOCK - 1) // EXPERT_BLOCK * EXPERT_BLOCK
    pad_end = jnp.cumsum(padded).astype(jnp.int32)
    pad_start = pad_end - padded
    block_start = jnp.arange(n_blocks, dtype=jnp.int32)[:, None] * EXPERT_BLOCK
    block_e = jnp.minimum(jnp.sum((block_start >= pad_end[None, :]).astype(jnp.int32), axis=1),
                          N_EXPERTS - 1).astype(jnp.int32)
    n_used = pad_end[-1:] // EXPERT_BLOCK
    experts = jnp.arange(N_EXPERTS, dtype=jnp.int32)
    first_slot = jnp.sum(jnp.where(eid_tiles[..., None] == experts, pad_start, 0), axis=-1)
    return block_e, n_used, pad_end, padded, first_slot + rank_tiles


def _moe_layer(x, mod_l, nrm_ffn, wr, br, tri, wg, wu, wd, nrm_final, *, final_norm):
    bsz, seq, _ = x.shape
    t = bsz * seq
    lt = min(seq, MOE_TILE)
    per = seq // lt
    x_units = x.reshape(t // lt, lt, D_MODEL)
    mod_units = jnp.repeat(mod_l, per, axis=0) if per > 1 else mod_l
    h, eid_tiles, rank_tiles, wts, counts = _router(x_units, mod_units, nrm_ffn, wr, br, tri)
    block_e, n_used, pad_end, padded, dest_tiles = _routing_tables(eid_tiles, rank_tiles, counts[:, 0])
    n_slots = block_e.shape[0] * EXPERT_BLOCK
    xs = _dispatch(pad_end, padded, dest_tiles, h, n_slots)
    y_slots = _experts(block_e, n_used, xs, wg, wu, wd)
    out = _combine(dest_tiles, x_units, mod_units, wts.T, nrm_final, y_slots, final_norm=final_norm)
    return out.reshape(bsz, seq, D_MODEL)


def kernel(x_prompt, x_sample, c_prompt, c_sample, state_hgrn, state_gla, w_ada, b_ada, norm_mix,
           norm_ffn, w_in, hg_lb, hg_onorm, w_gk2, b_gk, gla_onorm, w_br_a, w_br_b, w_out, w_rg, b_rg,
           w_re, b_re, w_e_gate, w_e_up, w_e_down, norm_final):
    bp = x_prompt.shape[0]
    bs = x_sample.shape[0]
    mod = _ada_mod(jnp.concatenate([c_prompt, c_sample], axis=0), w_ada, b_ada)
    mod = mod.reshape(DEPTH, bp + bs, 6, D_MODEL)

    glr0 = C_GOG + GLA_W
    win_r = jnp.concatenate(
        [w_in[:, :, :glr0], w_in[:, :, glr0 + GLA_GATE_RANK:], w_in[:, :, glr0:glr0 + GLA_GATE_RANK],
         jnp.zeros((DEPTH, D_MODEL, LANES - GLA_GATE_RANK), F32)], axis=2).astype(BF16)
    wgk2_p = jnp.concatenate(
        [w_gk2, jnp.zeros((DEPTH, LANES - GLA_GATE_RANK, GLA_KW), F32)], axis=1).astype(BF16)
    wa_b = w_br_a.astype(BF16)
    wb_b = w_br_b.astype(BF16)
    wo_b = w_out.astype(BF16)
    mall = jnp.asarray(_segment_sum_matrix(), BF16)
    masks = jnp.asarray(_level_masks(), F32)
    zpad = jnp.zeros((DEPTH, 8 - N_GROUPS, D_MODEL), F32)
    wr = jnp.concatenate([jnp.swapaxes(w_rg, 1, 2), zpad, jnp.swapaxes(w_re, 1, 2)], axis=1)
    br = jnp.concatenate([b_rg, jnp.zeros((DEPTH, 8 - N_GROUPS), F32), b_re], axis=1)[:, :, None]
    wg_b = w_e_gate.astype(BF16)
    wu_b = w_e_up.astype(BF16)
    wd_b = w_e_down.astype(BF16)
    nrm_f = norm_final.reshape(1, D_MODEL)
    assign = np.arange(TILE_ASSIGN)
    tri = jnp.asarray(assign[:, None] < assign[None, :], BF16)

    def run(x, mod_g, shg, sgla, tb):
        bsz = x.shape[0]
        new_hg, new_gla = [], []
        for l in range(DEPTH):
            x, s1, s2 = _mixer(
                x, mod_g[l], norm_mix[l:l + 1], win_r[l], hg_lb, wgk2_p[l], b_gk[l:l + 1],
                hg_onorm[l:l + 1], gla_onorm[l:l + 1], wa_b[l], wb_b[l], wo_b[l],
                shg[l], sgla[l].reshape(bsz, GLA_KW // LANES, LANES, HEAD_DV), mall, masks,
                layer=l, tb=tb)
            new_hg.append(s1)
            new_gla.append(s2.reshape(bsz, GLA_HEADS, GLA_DK, HEAD_DV))
            x = _moe_layer(x, mod_g[l], norm_ffn[l:l + 1], wr[l], br[l], tri, wg_b[l], wu_b[l], wd_b[l],
                           nrm_f, final_norm=(l == DEPTH - 1))
        return x, jnp.stack(new_hg), jnp.stack(new_gla)

    zeros_hg = jnp.zeros((DEPTH, bp, HG_HEADS, HG_DK, HEAD_DV), F32)
    zeros_gla = jnp.zeros((DEPTH, bp, GLA_HEADS, GLA_DK, HEAD_DV), F32)
    y_p, hg_p, gla_p = run(x_prompt, mod[:, :bp], zeros_hg, zeros_gla, 256)
    y_s, hg_s, gla_s = run(x_sample, mod[:, bp:], state_hgrn, state_gla, CHUNK)
    return (y_p, y_s, hg_p, gla_p, hg_s, gla_s)
```

```python
import functools

import numpy as np
import jax
import jax.numpy as jnp
from jax import lax
from jax.experimental import pallas as pl
from jax.experimental.pallas import tpu as pltpu

F32 = jnp.float32
BF16 = jnp.bfloat16

D_MODEL = 1024
DEPTH = 2
CHUNK = 64
NORM_EPS = 1e-6
LOG_FLOOR = 1e-30
HG_HEADS = 4
HG_DK = 128
HEAD_DV = 128
HG_KW = HG_HEADS * HG_DK
HG_W = HG_HEADS * HEAD_DV
GLA_HEADS = 4
GLA_DK = 64
GLA_KW = GLA_HEADS * GLA_DK
GLA_W = GLA_HEADS * HEAD_DV
GLA_GATE_RANK = 16
GLA_GATE_NORM = 16.0
N_GROUPS = 4
EXPERTS_PER_GROUP = 8
N_EXPERTS = N_GROUPS * EXPERTS_PER_GROUP
TOPK = 2
D_EXPERT = 512

LANES = 128
VMEM_LIMIT = 56 * 1024 * 1024

C_HQ = 0
C_HF = C_HQ + HG_KW
C_HI = C_HF + HG_KW
C_HOG = C_HI + HG_W
C_GQ = C_HOG + HG_W
C_GK = C_GQ + GLA_KW
C_GV = C_GK + GLA_KW
C_GOG = C_GV + GLA_W
C_GA = C_GOG + GLA_W
C_GB = C_GA + D_MODEL
C_GLR = C_GB + D_MODEL
IN_COLS_PAD = C_GLR + LANES
PROJ_TILE = 640
assert IN_COLS_PAD % PROJ_TILE == 0

class _ScorePlan:
    def __init__(self, levels, adjacent, diag_block):
        self.levels = levels
        self.adjacent = adjacent
        self.diag_block = diag_block
        self.cum_rows = (2 * len(levels) + 1) * CHUNK
        self.n_masks = len(levels) + 1 + int(adjacent)

    def segment_sum_matrix(self):
        t = np.arange(CHUNK)[:, None]
        r = np.arange(CHUNK)[None, :]
        rows = []
        for m in self.levels:
            same = (t // m) == (r // m)
            rows.append(same & (r <= t))
            rows.append(same & (r > t))
        rows.append(r <= t)
        return np.concatenate(rows, axis=0).astype(np.float32)

    def masks(self):
        t = np.arange(CHUNK)[:, None]
        s = np.arange(CHUNK)[None, :]
        masks = [((t // self.diag_block) == (s // self.diag_block)) & (s <= t)]
        for m in self.levels + ((1,) if self.adjacent else ()):
            masks.append(((t // (2 * m)) == (s // (2 * m))) & ((t // m) % 2 == 1) & ((s // m) % 2 == 0))
        return np.stack(masks).astype(np.float32)


SAFE_PLAN = _ScorePlan((32, 16, 8, 4, 2), True, 1)
FAST_PLAN = _ScorePlan((32, 16), False, 16)
FAST_BLOCK_DECAY_LIMIT = 60.0


def _dot(a, b):
    return jnp.dot(a, b, preferred_element_type=F32)


def _dot_nt(a, b):
    return lax.dot_general(a, b, (((1,), (1,)), ((), ())), preferred_element_type=F32)


def _sigmoid(x):
    return 1.0 / (1.0 + jnp.exp(-x))


def _silu(x):
    return x * _sigmoid(x)


def _rms_mod(x, gain, scale, shift):
    y = x * lax.rsqrt(jnp.mean(x * x, axis=-1, keepdims=True) + NORM_EPS)
    return y * gain * (1.0 + scale) + shift


def _ada_kernel(c_ref, w_ref, b_ref, o_ref):
    c = c_ref[...]
    o_ref[0] = jnp.dot(_silu(c), w_ref[0], preferred_element_type=F32,
                       precision=lax.Precision.HIGHEST) + b_ref[0]


def _ada_mod(c_all, w_ada, b_ada):
    nb = c_all.shape[0]
    tn = 512
    return pl.pallas_call(
        _ada_kernel,
        grid=(DEPTH, 6 * D_MODEL // tn),
        in_specs=[
            pl.BlockSpec((nb, D_MODEL), lambda l, j: (0, 0)),
            pl.BlockSpec((1, D_MODEL, tn), lambda l, j: (l, 0, j)),
            pl.BlockSpec((1, 1, tn), lambda l, j: (l, 0, j)),
        ],
        out_specs=pl.BlockSpec((1, nb, tn), lambda l, j: (l, 0, j)),
        out_shape=jax.ShapeDtypeStruct((DEPTH, nb, 6 * D_MODEL), F32),
        name="ada_mod",
    )(c_all, w_ada, b_ada.reshape(DEPTH, 1, 6 * D_MODEL))


def _chunk_attention(q, k, v, g, s_ref, mall_ref, mask_ref, heads_per_tile, plan):
    w = q.shape[1]
    n_tiles = w // LANES
    g_hi = g.astype(BF16)
    r1 = g - g_hi.astype(F32)
    g_mid = r1.astype(BF16)
    g_lo = (r1 - g_mid.astype(F32)).astype(BF16)
    mall = mall_ref[...]
    cums = _dot(mall, g_hi) + _dot(mall, g_mid) + _dot(mall, g_lo)
    b = cums[plan.cum_rows - CHUNK:plan.cum_rows]
    level_q = []
    level_k = []
    for i in range(len(plan.levels)):
        level_q.append(q * jnp.exp(cums[2 * i * CHUNK:(2 * i + 1) * CHUNK]))
        level_k.append(k * jnp.exp(cums[(2 * i + 1) * CHUNK:(2 * i + 2) * CHUNK]))
    if plan.diag_block == 1:
        qs = [q]
        ks = [k]
    else:
        i = plan.levels.index(plan.diag_block)
        qs = [level_q[i]]
        ks = [k * jnp.exp(-cums[2 * i * CHUNK:(2 * i + 1) * CHUNK])]
    qs += level_q
    ks += level_k
    if plan.adjacent:
        qs.append(q * jnp.exp(g))
        ks.append(k)
    b_last = b[CHUNK - 1:CHUNK]
    q_in = q * jnp.exp(b)
    k_out = k * jnp.exp(b_last - b)
    e_last = jnp.exp(b_last)

    dk = LANES // heads_per_tile
    lane = lax.broadcasted_iota(jnp.int32, (CHUNK, LANES), 1)
    row = lax.broadcasted_iota(jnp.int32, (LANES, HEAD_DV), 0)
    outs = []
    for ti in range(n_tiles):
        sl = slice(ti * LANES, (ti + 1) * LANES)
        ks_t = [kk[:, sl].astype(BF16) for kk in ks]
        k_out_t = k_out[:, sl].T.astype(BF16)
        e_col = jnp.broadcast_to(e_last[:, sl], (LANES, LANES)).T
        s_old = s_ref[ti]
        s_old_b = s_old.astype(BF16)
        upd = None
        for j in range(heads_per_tile):
            head = ti * heads_per_tile + j
            if heads_per_tile == 1:
                sel = lambda a: a
            else:
                in_head = (lane // dk) == j
                sel = lambda a, in_head=in_head: jnp.where(in_head, a, 0.0)
            sc = jnp.zeros((CHUNK, CHUNK), F32)
            for i in range(plan.n_masks):
                sc = sc + _dot_nt(sel(qs[i][:, sl]).astype(BF16), ks_t[i]) * mask_ref[i]
            vh = v[:, head * HEAD_DV:(head + 1) * HEAD_DV].astype(BF16)
            o = _dot(sc.astype(BF16), vh) + _dot(sel(q_in[:, sl]).astype(BF16), s_old_b)
            outs.append(o)
            u = _dot(k_out_t, vh)
            upd = u if upd is None else jnp.where((row // dk) == j, u, upd)
        s_ref[ti] = e_col * s_old + upd
    return jnp.concatenate(outs, axis=1)


def _head_norm_gate(o, gain, gate):
    outs = []
    for h in range(o.shape[1] // HEAD_DV):
        sl = slice(h * HEAD_DV, (h + 1) * HEAD_DV)
        oh = o[:, sl]
        oh = oh * lax.rsqrt(jnp.mean(oh * oh, axis=-1, keepdims=True) + NORM_EPS) * gain
        outs.append(oh * _silu(gate[:, sl]))
    return jnp.concatenate(outs, axis=1)


def _mixer_kernel(x_ref, mod_ref, nrm_ref, win_ref, lb_ref, wgk2_ref, bgk_ref, hgn_ref, glan_ref,
                  wa_ref, wb_ref, wo_ref, shg0_ref, sgla0_ref,
                  mall_fast_ref, mask_fast_ref, mall_safe_ref, mask_safe_ref,
                  xo_ref, shg_o_ref, sgla_o_ref,
                  p_scr, k_scr, lg_scr, shg_scr, sgla_scr, *, layer, tb):
    j = pl.program_id(1)

    @pl.when(j == 0)
    def _():
        shg_scr[...] = shg0_ref[0]
        sgla_scr[...] = sgla0_ref[0]

    x = x_ref[0]
    sh1 = mod_ref[0, 0:1, :]
    sc1 = mod_ref[0, 1:2, :]
    g1 = mod_ref[0, 2:3, :]
    hb = _rms_mod(x, nrm_ref[...], sc1, sh1).astype(BF16)
    for c in range(0, IN_COLS_PAD, PROJ_TILE):
        p_scr[:, c:c + PROJ_TILE] = _dot(hb, win_ref[:, c:c + PROJ_TILE])

    lb_all = lb_ref[...]
    lb_max = jnp.max(lb_all, axis=0, keepdims=True)
    lb_exp = jnp.exp(lb_all - lb_max)
    sm = lb_exp / jnp.sum(lb_exp, axis=0, keepdims=True)
    lbl = jnp.clip(jnp.sum(sm[0:layer + 1], axis=0, keepdims=True) - sm[0:1], 0.0, 1.0)

    p_scr[:, C_HQ:C_HQ + HG_KW] = _silu(p_scr[:, C_HQ:C_HQ + HG_KW]) * (HG_DK ** -0.5)
    z = p_scr[:, C_HF:C_HF + HG_KW]
    f = lbl + (1.0 - lbl) * _sigmoid(z)
    p_scr[:, C_HF:C_HF + HG_KW] = jnp.log(jnp.maximum(f, LOG_FLOOR))
    k_scr[...] = (1.0 - lbl) * _sigmoid(-z)
    glr = p_scr[:, C_GLR:C_GLR + LANES].astype(BF16)
    gate = _dot(glr, wgk2_ref[...]) + bgk_ref[...]
    lg_scr[...] = (jnp.minimum(gate, 0.0) - jnp.log1p(jnp.exp(-jnp.abs(gate)))) * (1.0 / GLA_GATE_NORM)
    p_scr[:, C_GQ:C_GQ + GLA_KW] = p_scr[:, C_GQ:C_GQ + GLA_KW] * (GLA_DK ** -0.5)

    def run_chunks(plan, mall_ref, mask_ref):
        def chunk_body(ci, carry):
            rows = pl.ds(pl.multiple_of(ci * CHUNK, CHUNK), CHUNK)
            o_hg = _chunk_attention(p_scr[rows, C_HQ:C_HQ + HG_KW], k_scr[rows, :],
                                    p_scr[rows, C_HI:C_HI + HG_W], p_scr[rows, C_HF:C_HF + HG_KW],
                                    shg_scr, mall_ref, mask_ref, 1, plan)
            p_scr[rows, C_HI:C_HI + HG_W] = o_hg
            o_gla = _chunk_attention(p_scr[rows, C_GQ:C_GQ + GLA_KW], p_scr[rows, C_GK:C_GK + GLA_KW],
                                     p_scr[rows, C_GV:C_GV + GLA_W], lg_scr[rows, :],
                                     sgla_scr, mall_ref, mask_ref, 2, plan)
            p_scr[rows, C_GV:C_GV + GLA_W] = o_gla
            return carry

        lax.fori_loop(0, tb // CHUNK, chunk_body, 0)

    blk = FAST_PLAN.diag_block
    min_hg = jnp.min(jnp.sum(p_scr[:, C_HF:C_HF + HG_KW].reshape(tb // blk, blk, HG_KW), axis=1))
    min_gla = jnp.min(jnp.sum(lg_scr[...].reshape(tb // blk, blk, GLA_KW), axis=1))
    bounded = jnp.minimum(min_hg, min_gla) >= -FAST_BLOCK_DECAY_LIMIT

    @pl.when(bounded)
    def _():
        run_chunks(FAST_PLAN, mall_fast_ref, mask_fast_ref)

    @pl.when(jnp.logical_not(bounded))
    def _():
        run_chunks(SAFE_PLAN, mall_safe_ref, mask_safe_ref)

    o_hg = _head_norm_gate(p_scr[:, C_HI:C_HI + HG_W], hgn_ref[...], p_scr[:, C_HOG:C_HOG + HG_W])
    o_gla = _head_norm_gate(p_scr[:, C_GV:C_GV + GLA_W], glan_ref[...], p_scr[:, C_GOG:C_GOG + GLA_W])
    ya = _dot(o_hg.astype(BF16), wa_ref[...])
    yb = _dot(o_gla.astype(BF16), wb_ref[...])
    merged = (_sigmoid(p_scr[:, C_GA:C_GA + D_MODEL]) * ya
              + _sigmoid(p_scr[:, C_GB:C_GB + D_MODEL]) * yb)
    m = _dot(merged.astype(BF16), wo_ref[...])
    xo_ref[0] = x + g1 * m

    @pl.when(j == pl.num_programs(1) - 1)
    def _():
        shg_o_ref[0] = shg_scr[...]
        sgla_o_ref[0] = sgla_scr[...]


def _const_spec(shape):
    nd = len(shape)
    return pl.BlockSpec(shape, lambda b, j, nd=nd: (0,) * nd, pipeline_mode=pl.Buffered(1))


def _mixer(x, mod, nrm, win, hg_lb, wgk2, bgk, hgn, glan, wa, wb, wo, shg0, sgla0, plan_consts,
           *, layer, tb):
    bsz, seq, _ = x.shape
    kern = functools.partial(_mixer_kernel, layer=layer, tb=tb)
    n_gla_tiles = GLA_KW // LANES
    return pl.pallas_call(
        kern,
        grid=(bsz, seq // tb),
        in_specs=[
            pl.BlockSpec((1, tb, D_MODEL), lambda b, j: (b, j, 0)),
            pl.BlockSpec((1, 6, D_MODEL), lambda b, j: (b, 0, 0)),
            _const_spec((1, D_MODEL)),
            _const_spec((D_MODEL, IN_COLS_PAD)),
            _const_spec((DEPTH, HG_KW)),
            _const_spec((LANES, GLA_KW)),
            _const_spec((1, GLA_KW)),
            _const_spec((1, HEAD_DV)),
            _const_spec((1, HEAD_DV)),
            _const_spec((HG_W, D_MODEL)),
            _const_spec((GLA_W, D_MODEL)),
            _const_spec((D_MODEL, D_MODEL)),
            pl.BlockSpec((1, HG_HEADS, HG_DK, HEAD_DV), lambda b, j: (b, 0, 0, 0)),
            pl.BlockSpec((1, n_gla_tiles, LANES, HEAD_DV), lambda b, j: (b, 0, 0, 0)),
            _const_spec((FAST_PLAN.cum_rows, CHUNK)),
            _const_spec((FAST_PLAN.n_masks, CHUNK, CHUNK)),
            _const_spec((SAFE_PLAN.cum_rows, CHUNK)),
            _const_spec((SAFE_PLAN.n_masks, CHUNK, CHUNK)),
        ],
        out_specs=[
            pl.BlockSpec((1, tb, D_MODEL), lambda b, j: (b, j, 0)),
            pl.BlockSpec((1, HG_HEADS, HG_DK, HEAD_DV), lambda b, j: (b, 0, 0, 0)),
            pl.BlockSpec((1, n_gla_tiles, LANES, HEAD_DV), lambda b, j: (b, 0, 0, 0)),
        ],
        out_shape=[
            jax.ShapeDtypeStruct((bsz, seq, D_MODEL), F32),
            jax.ShapeDtypeStruct((bsz, HG_HEADS, HG_DK, HEAD_DV), F32),
            jax.ShapeDtypeStruct((bsz, n_gla_tiles, LANES, HEAD_DV), F32),
        ],
        scratch_shapes=[
            pltpu.VMEM((tb, IN_COLS_PAD), F32),
            pltpu.VMEM((tb, HG_KW), F32),
            pltpu.VMEM((tb, GLA_KW), F32),
            pltpu.VMEM((HG_HEADS, HG_DK, HEAD_DV), F32),
            pltpu.VMEM((n_gla_tiles, LANES, HEAD_DV), F32),
        ],
        compiler_params=pltpu.CompilerParams(
            dimension_semantics=("arbitrary", "arbitrary"), vmem_limit_bytes=VMEM_LIMIT),
        name=f"mixer_l{layer}",
    )(x, mod, nrm, win, hg_lb, wgk2, bgk, hgn, glan, wa, wb, wo, shg0, sgla0, *plan_consts)


ROUTER_ROWS = 8 + N_EXPERTS
MOE_TILE = 512
TILE_ASSIGN = TOPK * MOE_TILE
EXPERT_BLOCK = 512


def _first_argmax_rows(vals, n):
    ridx = lax.broadcasted_iota(jnp.int32, vals.shape, 0)
    vmax = jnp.max(vals, axis=0, keepdims=True)
    imax = jnp.min(jnp.where(vals == vmax, ridx, n), axis=0, keepdims=True)
    return vmax, imax


def _router_kernel(x_ref, mod_ref, nrm_ref, wr_ref, br_ref, tri_ref,
                   h_ref, eid_ref, rank_ref, wts_ref, cnt_ref, run_scr):
    @pl.when(pl.program_id(0) == 0)
    def _():
        run_scr[...] = jnp.zeros_like(run_scr)

    u, lt, _ = x_ref.shape
    x = x_ref[...]
    sh2 = mod_ref[:, 3:4, :]
    sc2 = mod_ref[:, 4:5, :]
    h = _rms_mod(x, nrm_ref[...].reshape(1, 1, D_MODEL), sc2, sh2).reshape(u * lt, D_MODEL)
    h_ref[...] = h
    logits = lax.dot_general(wr_ref[...], h, (((1,), (1,)), ((), ())), preferred_element_type=F32,
                             precision=lax.Precision.HIGHEST) + br_ref[...]
    gl = logits[0:N_GROUPS]
    gmax, gi = _first_argmax_rows(gl, N_GROUPS)
    gp = 1.0 / jnp.sum(jnp.exp(gl - gmax), axis=0, keepdims=True)
    le = logits[8:8 + EXPERTS_PER_GROUP]
    for g in range(1, N_GROUPS):
        le = jnp.where(gi == g, logits[8 + g * EXPERTS_PER_GROUP:8 + (g + 1) * EXPERTS_PER_GROUP], le)
    pe = jnp.exp(le - jnp.max(le, axis=0, keepdims=True))
    pe = pe / jnp.sum(pe, axis=0, keepdims=True)
    v1, i1 = _first_argmax_rows(pe, EXPERTS_PER_GROUP)
    ridx = lax.broadcasted_iota(jnp.int32, pe.shape, 0)
    v2, i2 = _first_argmax_rows(jnp.where(ridx == i1, -1.0, pe), EXPERTS_PER_GROUP)
    vsum = v1 + v2
    wts_ref[0:1, :] = gp * v1 / vsum
    wts_ref[1:2, :] = gp * v2 / vsum
    eflat = jnp.concatenate([gi * EXPERTS_PER_GROUP + i1, gi * EXPERTS_PER_GROUP + i2], axis=1)
    eid_ref[0] = eflat
    onehot = (eflat == lax.broadcasted_iota(jnp.int32, (N_EXPERTS, TILE_ASSIGN), 0)).astype(F32)
    before = _dot(onehot.astype(BF16), tri_ref[...]) + run_scr[...]
    rank_ref[0] = jnp.sum(onehot * before, axis=0, keepdims=True).astype(jnp.int32)
    run_scr[...] = run_scr[...] + jnp.sum(onehot, axis=1, keepdims=True)
    cnt_ref[...] = run_scr[...].astype(jnp.int32)


def _router(x_units, mod_units, nrm, wr, br, tri):
    n_units, lt, _ = x_units.shape
    u = MOE_TILE // lt
    n_tiles = n_units // u
    return pl.pallas_call(
        _router_kernel,
        grid=(n_tiles,),
        in_specs=[
            pl.BlockSpec((u, lt, D_MODEL), lambda i: (i, 0, 0)),
            pl.BlockSpec((u, 6, D_MODEL), lambda i: (i, 0, 0)),
            pl.BlockSpec((1, D_MODEL), lambda i: (0, 0)),
            pl.BlockSpec((ROUTER_ROWS, D_MODEL), lambda i: (0, 0)),
            pl.BlockSpec((ROUTER_ROWS, 1), lambda i: (0, 0)),
            pl.BlockSpec((TILE_ASSIGN, TILE_ASSIGN), lambda i: (0, 0)),
        ],
        out_specs=[
            pl.BlockSpec((MOE_TILE, D_MODEL), lambda i: (i, 0)),
            pl.BlockSpec((1, 1, TILE_ASSIGN), lambda i: (i, 0, 0)),
            pl.BlockSpec((1, 1, TILE_ASSIGN), lambda i: (i, 0, 0)),
            pl.BlockSpec((TOPK, MOE_TILE), lambda i: (0, i)),
            pl.BlockSpec((N_EXPERTS, 1), lambda i: (0, 0)),
        ],
        out_shape=[
            jax.ShapeDtypeStruct((n_tiles * MOE_TILE, D_MODEL), F32),
            jax.ShapeDtypeStruct((n_tiles, 1, TILE_ASSIGN), jnp.int32),
            jax.ShapeDtypeStruct((n_tiles, 1, TILE_ASSIGN), jnp.int32),
            jax.ShapeDtypeStruct((TOPK, n_tiles * MOE_TILE), F32),
            jax.ShapeDtypeStruct((N_EXPERTS, 1), jnp.int32),
        ],
        scratch_shapes=[pltpu.VMEM((N_EXPERTS, 1), F32)],
        compiler_params=pltpu.CompilerParams(dimension_semantics=("arbitrary",)),
        name="moe_router",
    )(x_units, mod_units, nrm, wr, br, tri)


def _start_row_gather(idx_ref, n_rows, src_hbm, dst, sem):
    def body(r, carry):
        row = idx_ref[0, 0, r]
        pltpu.make_async_copy(src_hbm.at[pl.ds(row, 1)], dst.at[pl.ds(r, 1)], sem).start()
        return carry
    lax.fori_loop(0, n_rows, body, 0, unroll=8)


def _wait_row_gather(n_rows, src_hbm, dst, sem):
    pltpu.make_async_copy(src_hbm.at[pl.ds(0, n_rows)], dst, sem).wait()


def _dispatch_kernel(pend_ref, padded_ref, dest_ref, h_ref, xs_hbm, zbuf, sem):
    n_blocks = xs_hbm.shape[0] // EXPERT_BLOCK

    def zero_block(first_row):
        return pltpu.make_async_copy(
            zbuf, xs_hbm.at[pl.ds(pl.multiple_of(first_row, EXPERT_BLOCK), EXPERT_BLOCK)], sem.at[0])

    @pl.when(pl.program_id(0) == 0)
    def _():
        zbuf[...] = jnp.zeros_like(zbuf)
        n_used = pend_ref[N_EXPERTS - 1] // EXPERT_BLOCK
        for e in range(N_EXPERTS):
            @pl.when(padded_ref[e] > 0)
            def _():
                zero_block(pend_ref[e] - EXPERT_BLOCK).start()
        lax.fori_loop(n_used, n_blocks, lambda b, c: (zero_block(b * EXPERT_BLOCK).start(), c)[1], 0)
        for e in range(N_EXPERTS):
            @pl.when(padded_ref[e] > 0)
            def _():
                zero_block(pend_ref[e] - EXPERT_BLOCK).wait()
        lax.fori_loop(n_used, n_blocks, lambda b, c: (zero_block(b * EXPERT_BLOCK).wait(), c)[1], 0)

    def body(t, carry):
        for k in range(TOPK):
            slot = dest_ref[0, 0, k * MOE_TILE + t]
            pltpu.make_async_copy(h_ref.at[pl.ds(t, 1)], xs_hbm.at[pl.ds(slot, 1)], sem.at[1]).start()
        return carry
    lax.fori_loop(0, MOE_TILE, body, 0, unroll=8)
    for k in range(TOPK):
        pltpu.make_async_copy(h_ref, xs_hbm.at[pl.ds(0, MOE_TILE)], sem.at[1]).wait()


def _dispatch(pad_end, padded, dest_tiles, h, n_slots):
    n_tiles = dest_tiles.shape[0]
    grid_spec = pltpu.PrefetchScalarGridSpec(
        num_scalar_prefetch=2,
        grid=(n_tiles,),
        in_specs=[
            pl.BlockSpec((1, 1, TILE_ASSIGN), lambda i, pe, pd: (i, 0, 0), memory_space=pltpu.SMEM),
            pl.BlockSpec((MOE_TILE, D_MODEL), lambda i, pe, pd: (i, 0)),
        ],
        out_specs=pl.BlockSpec(memory_space=pl.ANY),
        scratch_shapes=[pltpu.VMEM((EXPERT_BLOCK, D_MODEL), F32), pltpu.SemaphoreType.DMA((2,))],
    )
    return pl.pallas_call(
        _dispatch_kernel,
        grid_spec=grid_spec,
        out_shape=jax.ShapeDtypeStruct((n_slots, D_MODEL), F32),
        compiler_params=pltpu.CompilerParams(dimension_semantics=("arbitrary",)),
        name="moe_dispatch",
    )(pad_end, padded, dest_tiles, h)


def _experts_kernel(be_ref, nused_ref, x_ref, wg_ref, wu_ref, wd_ref, o_ref):
    @pl.when(pl.program_id(0) < nused_ref[0])
    def _():
        xb = x_ref[...].astype(BF16)
        a = _silu(_dot(xb, wg_ref[0])) * _dot(xb, wu_ref[0])
        o_ref[...] = _dot(a.astype(BF16), wd_ref[0])

    @pl.when(pl.program_id(0) >= nused_ref[0])
    def _():
        o_ref[...] = jnp.zeros_like(o_ref)


def _experts(block_e, n_used, xs, wg, wu, wd):
    n_blocks = xs.shape[0] // EXPERT_BLOCK

    def row_block(i, be, nu):
        return (jnp.minimum(i, nu[0] - 1), 0)

    def expert_block(i, be, nu):
        return (be[jnp.minimum(i, nu[0] - 1)], 0, 0)

    grid_spec = pltpu.PrefetchScalarGridSpec(
        num_scalar_prefetch=2,
        grid=(n_blocks,),
        in_specs=[
            pl.BlockSpec((EXPERT_BLOCK, D_MODEL), row_block),
            pl.BlockSpec((1, D_MODEL, D_EXPERT), expert_block),
            pl.BlockSpec((1, D_MODEL, D_EXPERT), expert_block),
            pl.BlockSpec((1, D_EXPERT, D_MODEL), expert_block),
        ],
        out_specs=pl.BlockSpec((EXPERT_BLOCK, D_MODEL), lambda i, be, nu: (i, 0)),
    )
    return pl.pallas_call(
        _experts_kernel,
        grid_spec=grid_spec,
        out_shape=jax.ShapeDtypeStruct(xs.shape, F32),
        compiler_params=pltpu.CompilerParams(
            dimension_semantics=("arbitrary",), vmem_limit_bytes=VMEM_LIMIT),
        name="moe_experts",
    )(block_e, n_used, xs, wg, wu, wd)


def _combine_kernel(dst_cur_ref, dst_nxt_ref, x_ref, mod_ref, wts_ref, nrm_ref, y_hbm, o_ref, buf, sem,
                    *, final_norm):
    i = pl.program_id(0)
    n = pl.num_programs(0)
    slot = i % 2

    @pl.when(i == 0)
    def _():
        _start_row_gather(dst_cur_ref, TILE_ASSIGN, y_hbm, buf.at[0], sem.at[0])

    @pl.when(i + 1 < n)
    def _():
        _start_row_gather(dst_nxt_ref, TILE_ASSIGN, y_hbm, buf.at[1 - slot], sem.at[1 - slot])

    _wait_row_gather(TILE_ASSIGN, y_hbm, buf.at[slot], sem.at[slot])
    u, lt, _ = x_ref.shape
    y = (wts_ref[:, 0:1] * buf[slot, 0:MOE_TILE, :] + wts_ref[:, 1:2] * buf[slot, MOE_TILE:TILE_ASSIGN, :])
    g2 = mod_ref[:, 5:6, :]
    out = x_ref[...] + g2 * y.reshape(u, lt, D_MODEL)
    if final_norm:
        out = out * lax.rsqrt(jnp.mean(out * out, axis=-1, keepdims=True) + NORM_EPS)
        out = out * nrm_ref[...].reshape(1, 1, D_MODEL)
    o_ref[...] = out


def _combine(dest_tiles, x_units, mod_units, wts_col, nrm, y_slots, *, final_norm):
    n_units, lt, _ = x_units.shape
    u = MOE_TILE // lt
    n_tiles = n_units // u
    return pl.pallas_call(
        functools.partial(_combine_kernel, final_norm=final_norm),
        grid=(n_tiles,),
        in_specs=[
            pl.BlockSpec((1, 1, TILE_ASSIGN), lambda i: (i, 0, 0), memory_space=pltpu.SMEM),
            pl.BlockSpec((1, 1, TILE_ASSIGN), lambda i: (jnp.minimum(i + 1, n_tiles - 1), 0, 0),
                         memory_space=pltpu.SMEM),
            pl.BlockSpec((u, lt, D_MODEL), lambda i: (i, 0, 0)),
            pl.BlockSpec((u, 6, D_MODEL), lambda i: (i, 0, 0)),
            pl.BlockSpec((MOE_TILE, TOPK), lambda i: (i, 0)),
            pl.BlockSpec((1, D_MODEL), lambda i: (0, 0)),
            pl.BlockSpec(memory_space=pl.ANY),
        ],
        out_specs=pl.BlockSpec((u, lt, D_MODEL), lambda i: (i, 0, 0)),
        out_shape=jax.ShapeDtypeStruct(x_units.shape, F32),
        scratch_shapes=[pltpu.VMEM((2, TILE_ASSIGN, D_MODEL), F32), pltpu.SemaphoreType.DMA((2,))],
        compiler_params=pltpu.CompilerParams(
            dimension_semantics=("arbitrary",), vmem_limit_bytes=VMEM_LIMIT),
        name="moe_combine",
    )(dest_tiles, dest_tiles, x_units, mod_units, wts_col, nrm, y_slots)


def _routing_tables(eid_tiles, rank_tiles, counts):
    n_blocks = eid_tiles.size // EXPERT_BLOCK + N_EXPERTS
    padded = (counts + EXPERT_BLOCK - 1) // EXPERT_BLOCK * EXPERT_BLOCK
    pad_end = jnp.cumsum(padded).astype(jnp.int32)
    pad_start = pad_end - padded
    block_start = jnp.arange(n_blocks, dtype=jnp.int32)[:, None] * EXPERT_BLOCK
    block_e = jnp.minimum(jnp.sum((block_start >= pad_end[None, :]).astype(jnp.int32), axis=1),
                          N_EXPERTS - 1).astype(jnp.int32)
    n_used = pad_end[-1:] // EXPERT_BLOCK
    experts = jnp.arange(N_EXPERTS, dtype=jnp.int32)
    first_slot = jnp.sum(jnp.where(eid_tiles[..., None] == experts, pad_start, 0), axis=-1)
    return block_e, n_used, pad_end, padded, first_slot + rank_tiles


def _moe_layer(x, mod_l, nrm_ffn, wr, br, tri, wg, wu, wd, nrm_final, *, final_norm):
    bsz, seq, _ = x.shape
    t = bsz * seq
    lt = min(seq, MOE_TILE)
    per = seq // lt
    x_units = x.reshape(t // lt, lt, D_MODEL)
    mod_units = jnp.repeat(mod_l, per, axis=0) if per > 1 else mod_l
    h, eid_tiles, rank_tiles, wts, counts = _router(x_units, mod_units, nrm_ffn, wr, br, tri)
    block_e, n_used, pad_end, padded, dest_tiles = _routing_tables(eid_tiles, rank_tiles, counts[:, 0])
    n_slots = block_e.shape[0] * EXPERT_BLOCK
    xs = _dispatch(pad_end, padded, dest_tiles, h, n_slots)
    y_slots = _experts(block_e, n_used, xs, wg, wu, wd)
    out = _combine(dest_tiles, x_units, mod_units, wts.T, nrm_final, y_slots, final_norm=final_norm)
    return out.reshape(bsz, seq, D_MODEL)


def kernel(x_prompt, x_sample, c_prompt, c_sample, state_hgrn, state_gla, w_ada, b_ada, norm_mix,
           norm_ffn, w_in, hg_lb, hg_onorm, w_gk2, b_gk, gla_onorm, w_br_a, w_br_b, w_out, w_rg, b_rg,
           w_re, b_re, w_e_gate, w_e_up, w_e_down, norm_final):
    bp = x_prompt.shape[0]
    bs = x_sample.shape[0]
    mod = _ada_mod(jnp.concatenate([c_prompt, c_sample], axis=0), w_ada, b_ada)
    mod = mod.reshape(DEPTH, bp + bs, 6, D_MODEL)

    glr0 = C_GOG + GLA_W
    win_r = jnp.concatenate(
        [w_in[:, :, :glr0], w_in[:, :, glr0 + GLA_GATE_RANK:], w_in[:, :, glr0:glr0 + GLA_GATE_RANK],
         jnp.zeros((DEPTH, D_MODEL, LANES - GLA_GATE_RANK), F32)], axis=2).astype(BF16)
    wgk2_p = jnp.concatenate(
        [w_gk2, jnp.zeros((DEPTH, LANES - GLA_GATE_RANK, GLA_KW), F32)], axis=1).astype(BF16)
    wa_b = w_br_a.astype(BF16)
    wb_b = w_br_b.astype(BF16)
    wo_b = w_out.astype(BF16)
    plan_consts = []
    for plan in (FAST_PLAN, SAFE_PLAN):
        plan_consts += [jnp.asarray(plan.segment_sum_matrix(), BF16), jnp.asarray(plan.masks(), F32)]
    zpad = jnp.zeros((DEPTH, 8 - N_GROUPS, D_MODEL), F32)
    wr = jnp.concatenate([jnp.swapaxes(w_rg, 1, 2), zpad, jnp.swapaxes(w_re, 1, 2)], axis=1)
    br = jnp.concatenate([b_rg, jnp.zeros((DEPTH, 8 - N_GROUPS), F32), b_re], axis=1)[:, :, None]
    wg_b = w_e_gate.astype(BF16)
    wu_b = w_e_up.astype(BF16)
    wd_b = w_e_down.astype(BF16)
    nrm_f = norm_final.reshape(1, D_MODEL)
    assign = np.arange(TILE_ASSIGN)
    tri = jnp.asarray(assign[:, None] < assign[None, :], BF16)

    def run(x, mod_g, shg, sgla, tb):
        bsz = x.shape[0]
        new_hg, new_gla = [], []
        for l in range(DEPTH):
            x, s1, s2 = _mixer(
                x, mod_g[l], norm_mix[l:l + 1], win_r[l], hg_lb, wgk2_p[l], b_gk[l:l + 1],
                hg_onorm[l:l + 1], gla_onorm[l:l + 1], wa_b[l], wb_b[l], wo_b[l],
                shg[l], sgla[l].reshape(bsz, GLA_KW // LANES, LANES, HEAD_DV), plan_consts,
                layer=l, tb=tb)
            new_hg.append(s1)
            new_gla.append(s2.reshape(bsz, GLA_HEADS, GLA_DK, HEAD_DV))
            x = _moe_layer(x, mod_g[l], norm_ffn[l:l + 1], wr[l], br[l], tri, wg_b[l], wu_b[l], wd_b[l],
                           nrm_f, final_norm=(l == DEPTH - 1))
        return x, jnp.stack(new_hg), jnp.stack(new_gla)

    zeros_hg = jnp.zeros((DEPTH, bp, HG_HEADS, HG_DK, HEAD_DV), F32)
    zeros_gla = jnp.zeros((DEPTH, bp, GLA_HEADS, GLA_DK, HEAD_DV), F32)
    y_p, hg_p, gla_p = run(x_prompt, mod[:, :bp], zeros_hg, zeros_gla, 256)
    y_s, hg_s, gla_s = run(x_sample, mod[:, bp:], state_hgrn, state_gla, CHUNK)
    return (y_p, y_s, hg_p, gla_p, hg_s, gla_s)
```

```python
import functools

import numpy as np
import jax
import jax.numpy as jnp
from jax import lax
from jax.experimental import pallas as pl
from jax.experimental.pallas import tpu as pltpu

F32 = jnp.float32
BF16 = jnp.bfloat16

D_MODEL = 1024
DEPTH = 2
CHUNK = 64
NORM_EPS = 1e-6
LOG_FLOOR = 1e-30
HG_HEADS = 4
HG_DK = 128
HEAD_DV = 128
HG_KW = HG_HEADS * HG_DK
HG_W = HG_HEADS * HEAD_DV
GLA_HEADS = 4
GLA_DK = 64
GLA_KW = GLA_HEADS * GLA_DK
GLA_W = GLA_HEADS * HEAD_DV
GLA_GATE_RANK = 16
GLA_GATE_NORM = 16.0
N_GROUPS = 4
EXPERTS_PER_GROUP = 8
N_EXPERTS = N_GROUPS * EXPERTS_PER_GROUP
TOPK = 2
D_EXPERT = 512

LANES = 128
VMEM_LIMIT = 56 * 1024 * 1024

C_HQ = 0
C_HF = C_HQ + HG_KW
C_HI = C_HF + HG_KW
C_HOG = C_HI + HG_W
C_GQ = C_HOG + HG_W
C_GK = C_GQ + GLA_KW
C_GV = C_GK + GLA_KW
C_GOG = C_GV + GLA_W
C_GA = C_GOG + GLA_W
C_GB = C_GA + D_MODEL
C_GLR = C_GB + D_MODEL
IN_COLS_PAD = C_GLR + LANES
PROJ_TILE = 640
assert IN_COLS_PAD % PROJ_TILE == 0

class _ScorePlan:
    def __init__(self, levels, adjacent, diag_block):
        self.levels = levels
        self.adjacent = adjacent
        self.diag_block = diag_block
        self.cum_rows = (2 * len(levels) + 1) * CHUNK
        self.n_masks = len(levels) + 1 + int(adjacent)

    def segment_sum_matrix(self):
        t = np.arange(CHUNK)[:, None]
        r = np.arange(CHUNK)[None, :]
        rows = []
        for m in self.levels:
            same = (t // m) == (r // m)
            rows.append(same & (r <= t))
            rows.append(same & (r > t))
        rows.append(r <= t)
        return np.concatenate(rows, axis=0).astype(np.float32)

    def masks(self):
        t = np.arange(CHUNK)[:, None]
        s = np.arange(CHUNK)[None, :]
        masks = [((t // self.diag_block) == (s // self.diag_block)) & (s <= t)]
        for m in self.levels + ((1,) if self.adjacent else ()):
            masks.append(((t // (2 * m)) == (s // (2 * m))) & ((t // m) % 2 == 1) & ((s // m) % 2 == 0))
        return np.stack(masks).astype(np.float32)


SAFE_PLAN = _ScorePlan((32, 16, 8, 4, 2), True, 1)
FAST_PLAN = _ScorePlan((32, 16), False, 16)
FAST_BLOCK_DECAY_LIMIT = 60.0


def _dot(a, b):
    return jnp.dot(a, b, preferred_element_type=F32)


def _dot_nt(a, b):
    return lax.dot_general(a, b, (((1,), (1,)), ((), ())), preferred_element_type=F32)


def _sigmoid(x):
    return 1.0 / (1.0 + jnp.exp(-x))


def _silu(x):
    return x * _sigmoid(x)


def _rms_mod(x, gain, scale, shift):
    y = x * lax.rsqrt(jnp.mean(x * x, axis=-1, keepdims=True) + NORM_EPS)
    return y * gain * (1.0 + scale) + shift


def _ada_kernel(c_ref, w_ref, b_ref, o_ref):
    c = c_ref[...]
    o_ref[0] = jnp.dot(_silu(c), w_ref[0], preferred_element_type=F32,
                       precision=lax.Precision.HIGHEST) + b_ref[0]


def _ada_mod(c_all, w_ada, b_ada):
    nb = c_all.shape[0]
    tn = 512
    return pl.pallas_call(
        _ada_kernel,
        grid=(DEPTH, 6 * D_MODEL // tn),
        in_specs=[
            pl.BlockSpec((nb, D_MODEL), lambda l, j: (0, 0)),
            pl.BlockSpec((1, D_MODEL, tn), lambda l, j: (l, 0, j)),
            pl.BlockSpec((1, 1, tn), lambda l, j: (l, 0, j)),
        ],
        out_specs=pl.BlockSpec((1, nb, tn), lambda l, j: (l, 0, j)),
        out_shape=jax.ShapeDtypeStruct((DEPTH, nb, 6 * D_MODEL), F32),
        name="ada_mod",
    )(c_all, w_ada, b_ada.reshape(DEPTH, 1, 6 * D_MODEL))


def _chunk_attention(q, k, v, g, states, mall_ref, mask_ref, heads_per_tile, plan):
    w = q.shape[1]
    n_tiles = w // LANES
    g_hi = g.astype(BF16)
    r1 = g - g_hi.astype(F32)
    g_mid = r1.astype(BF16)
    g_lo = (r1 - g_mid.astype(F32)).astype(BF16)
    mall = mall_ref[...]
    cums = _dot(mall, g_hi) + _dot(mall, g_mid) + _dot(mall, g_lo)
    b = cums[plan.cum_rows - CHUNK:plan.cum_rows]
    level_q = []
    level_k = []
    for i in range(len(plan.levels)):
        level_q.append(q * jnp.exp(cums[2 * i * CHUNK:(2 * i + 1) * CHUNK]))
        level_k.append(k * jnp.exp(cums[(2 * i + 1) * CHUNK:(2 * i + 2) * CHUNK]))
    if plan.diag_block == 1:
        qs = [q]
        ks = [k]
    else:
        i = plan.levels.index(plan.diag_block)
        qs = [level_q[i]]
        ks = [k * jnp.exp(-cums[2 * i * CHUNK:(2 * i + 1) * CHUNK])]
    qs += level_q
    ks += level_k
    if plan.adjacent:
        qs.append(q * jnp.exp(g))
        ks.append(k)
    b_last = b[CHUNK - 1:CHUNK]
    q_in = q * jnp.exp(b)
    k_out = k * jnp.exp(b_last - b)
    e_last = jnp.exp(b_last)

    dk = LANES // heads_per_tile
    lane = lax.broadcasted_iota(jnp.int32, (CHUNK, LANES), 1)
    row = lax.broadcasted_iota(jnp.int32, (LANES, HEAD_DV), 0)
    outs = []
    new_states = []
    for ti in range(n_tiles):
        sl = slice(ti * LANES, (ti + 1) * LANES)
        ks_t =[kk[:, sl].astype(BF16) for kk in ks]
        k_out_t = k_out[:, sl].T.astype(BF16)
        e_col = jnp.broadcast_to(e_last[:, sl], (LANES, LANES)).T
        s_old = states[ti]
        s_old_b = s_old.astype(BF16)
        upd = None
        for j in range(heads_per_tile):
            head = ti * heads_per_tile + j
            if heads_per_tile == 1:
                sel = lambda a: a
            else:
                in_head = (lane // dk) == j
                sel = lambda a, in_head=in_head: jnp.where(in_head, a, 0.0)
            sc = jnp.zeros((CHUNK, CHUNK), F32)
            for i in range(plan.n_masks):
                sc = sc + _dot_nt(sel(qs[i][:, sl]).astype(BF16), ks_t[i]) * mask_ref[i]
            vh = v[:, head * HEAD_DV:(head + 1) * HEAD_DV].astype(BF16)
            o = _dot(sc.astype(BF16), vh) + _dot(sel(q_in[:, sl]).astype(BF16), s_old_b)
            outs.append(o)
            u = _dot(k_out_t, vh)
            upd = u if upd is None else jnp.where((row // dk) == j, u, upd)
        new_states.append(e_col * s_old + upd)
    return jnp.concatenate(outs, axis=1), new_states


def _head_norm_gate(o, gain, gate):
    outs = []
    for h in range(o.shape[1] // HEAD_DV):
        sl = slice(h * HEAD_DV, (h + 1) * HEAD_DV)
        oh = o[:, sl]
        oh = oh * lax.rsqrt(jnp.mean(oh * oh, axis=-1, keepdims=True) + NORM_EPS) * gain
        outs.append(oh * _silu(gate[:, sl]))
    return jnp.concatenate(outs, axis=1)


def _mixer_kernel(x_ref, mod_ref, nrm_ref, win_ref, lb_ref, wgk2_ref, bgk_ref, hgn_ref, glan_ref,
                  wa_ref, wb_ref, wo_ref, shg0_ref, sgla0_ref,
                  mall_fast_ref, mask_fast_ref, mall_safe_ref, mask_safe_ref,
                  xo_ref, shg_o_ref, sgla_o_ref,
                  p_scr, k_scr, lg_scr, shg_scr, sgla_scr, *, layer, tb):
    j = pl.program_id(1)

    @pl.when(j == 0)
    def _():
        shg_scr[...] = shg0_ref[0]
        sgla_scr[...] = sgla0_ref[0]

    x = x_ref[0]
    sh1 = mod_ref[0, 0:1, :]
    sc1 = mod_ref[0, 1:2, :]
    g1 = mod_ref[0, 2:3, :]
    hb = _rms_mod(x, nrm_ref[...], sc1, sh1).astype(BF16)
    for c in range(0, IN_COLS_PAD, PROJ_TILE):
        p_scr[:, c:c + PROJ_TILE] = _dot(hb, win_ref[:, c:c + PROJ_TILE])

    lb_all = lb_ref[...]
    lb_max = jnp.max(lb_all, axis=0, keepdims=True)
    lb_exp = jnp.exp(lb_all - lb_max)
    sm = lb_exp / jnp.sum(lb_exp, axis=0, keepdims=True)
    lbl = jnp.clip(jnp.sum(sm[0:layer + 1], axis=0, keepdims=True) - sm[0:1], 0.0, 1.0)

    p_scr[:, C_HQ:C_HQ + HG_KW] = _silu(p_scr[:, C_HQ:C_HQ + HG_KW]) * (HG_DK ** -0.5)
    z = p_scr[:, C_HF:C_HF + HG_KW]
    f = lbl + (1.0 - lbl) * _sigmoid(z)
    p_scr[:, C_HF:C_HF + HG_KW] = jnp.log(jnp.maximum(f, LOG_FLOOR))
    k_scr[...] = (1.0 - lbl) * _sigmoid(-z)
    glr = p_scr[:, C_GLR:C_GLR + LANES].astype(BF16)
    gate = _dot(glr, wgk2_ref[...]) + bgk_ref[...]
    lg_scr[...] = (jnp.minimum(gate, 0.0) - jnp.log1p(jnp.exp(-jnp.abs(gate)))) * (1.0 / GLA_GATE_NORM)
    p_scr[:, C_GQ:C_GQ + GLA_KW] = p_scr[:, C_GQ:C_GQ + GLA_KW] * (GLA_DK ** -0.5)

    n_hg_tiles = HG_KW // LANES
    n_gla_tiles = GLA_KW // LANES

    def one_chunk(rows, states, plan, mall_ref, mask_ref):
        o_hg, st_hg = _chunk_attention(p_scr[rows, C_HQ:C_HQ + HG_KW], k_scr[rows, :],
                                       p_scr[rows, C_HI:C_HI + HG_W], p_scr[rows, C_HF:C_HF + HG_KW],
                                       states[:n_hg_tiles], mall_ref, mask_ref, 1, plan)
        p_scr[rows, C_HI:C_HI + HG_W] = o_hg
        o_gla, st_gla = _chunk_attention(p_scr[rows, C_GQ:C_GQ + GLA_KW], p_scr[rows, C_GK:C_GK + GLA_KW],
                                         p_scr[rows, C_GV:C_GV + GLA_W], lg_scr[rows, :],
                                         states[n_hg_tiles:], mall_ref, mask_ref, 2, plan)
        p_scr[rows, C_GV:C_GV + GLA_W] = o_gla
        return st_hg + st_gla

    def load_states():
        return [shg_scr[t] for t in range(n_hg_tiles)] + [sgla_scr[t] for t in range(n_gla_tiles)]

    def store_states(states):
        for t in range(n_hg_tiles):
            shg_scr[t] = states[t]
        for t in range(n_gla_tiles):
            sgla_scr[t] = states[n_hg_tiles + t]

    def run_chunks(plan, mall_ref, mask_ref, unrolled):
        if unrolled:
            states = load_states()
            for ci in range(tb // CHUNK):
                states = one_chunk(pl.ds(ci * CHUNK, CHUNK), states, plan, mall_ref, mask_ref)
            store_states(states)
        else:
            def chunk_body(ci, carry):
                rows = pl.ds(pl.multiple_of(ci * CHUNK, CHUNK), CHUNK)
                store_states(one_chunk(rows, load_states(), plan, mall_ref, mask_ref))
                return carry

            lax.fori_loop(0, tb // CHUNK, chunk_body, 0)

    blk = FAST_PLAN.diag_block
    min_hg = jnp.min(jnp.sum(p_scr[:, C_HF:C_HF + HG_KW].reshape(tb // blk, blk, HG_KW), axis=1))
    min_gla = jnp.min(jnp.sum(lg_scr[...].reshape(tb // blk, blk, GLA_KW), axis=1))
    bounded = jnp.minimum(min_hg, min_gla) >= -FAST_BLOCK_DECAY_LIMIT

    @pl.when(bounded)
    def _():
        run_chunks(FAST_PLAN, mall_fast_ref, mask_fast_ref, True)

    @pl.when(jnp.logical_not(bounded))
    def _():
        run_chunks(SAFE_PLAN, mall_safe_ref, mask_safe_ref, False)

    o_hg = _head_norm_gate(p_scr[:, C_HI:C_HI + HG_W], hgn_ref[...], p_scr[:, C_HOG:C_HOG + HG_W])
    o_gla = _head_norm_gate(p_scr[:, C_GV:C_GV + GLA_W], glan_ref[...], p_scr[:, C_GOG:C_GOG + GLA_W])
    ya = _dot(o_hg.astype(BF16), wa_ref[...])
    yb = _dot(o_gla.astype(BF16), wb_ref[...])
    merged = (_sigmoid(p_scr[:, C_GA:C_GA + D_MODEL]) * ya
              + _sigmoid(p_scr[:, C_GB:C_GB + D_MODEL]) * yb)
    m = _dot(merged.astype(BF16), wo_ref[...])
    xo_ref[0] = x + g1 * m

    @pl.when(j == pl.num_programs(1) - 1)
    def _():
        shg_o_ref[0] = shg_scr[...]
        sgla_o_ref[0] = sgla_scr[...]


def _const_spec(shape):
    nd = len(shape)
    return pl.BlockSpec(shape, lambda b, j, nd=nd: (0,) * nd, pipeline_mode=pl.Buffered(1))


def _mixer(x, mod, nrm, win, hg_lb, wgk2, bgk, hgn, glan, wa, wb, wo, shg0, sgla0, plan_consts,
           *, layer, tb):
    bsz, seq, _ = x.shape
    kern = functools.partial(_mixer_kernel, layer=layer, tb=tb)
    n_gla_tiles = GLA_KW // LANES
    return pl.pallas_call(
        kern,
        grid=(bsz, seq // tb),
        in_specs=[
            pl.BlockSpec((1, tb, D_MODEL), lambda b, j: (b, j, 0)),
            pl.BlockSpec((1, 6, D_MODEL), lambda b, j: (b, 0, 0)),
            _const_spec((1, D_MODEL)),
            _const_spec((D_MODEL, IN_COLS_PAD)),
            _const_spec((DEPTH, HG_KW)),
            _const_spec((LANES, GLA_KW)),
            _const_spec((1, GLA_KW)),
            _const_spec((1, HEAD_DV)),
            _const_spec((1, HEAD_DV)),
            _const_spec((HG_W, D_MODEL)),
            _const_spec((GLA_W, D_MODEL)),
            _const_spec((D_MODEL, D_MODEL)),
            pl.BlockSpec((1, HG_HEADS, HG_DK, HEAD_DV), lambda b, j: (b, 0, 0, 0)),
            pl.BlockSpec((1, n_gla_tiles, LANES, HEAD_DV), lambda b, j: (b, 0, 0, 0)),
            _const_spec((FAST_PLAN.cum_rows, CHUNK)),
            _const_spec((FAST_PLAN.n_masks, CHUNK, CHUNK)),
            _const_spec((SAFE_PLAN.cum_rows, CHUNK)),
            _const_spec((SAFE_PLAN.n_masks, CHUNK, CHUNK)),
        ],
        out_specs=[
            pl.BlockSpec((1, tb, D_MODEL), lambda b, j: (b, j, 0)),
            pl.BlockSpec((1, HG_HEADS, HG_DK, HEAD_DV), lambda b, j: (b, 0, 0, 0)),
            pl.BlockSpec((1, n_gla_tiles, LANES, HEAD_DV), lambda b, j: (b, 0, 0, 0)),
        ],
        out_shape=[
            jax.ShapeDtypeStruct((bsz, seq, D_MODEL), F32),
            jax.ShapeDtypeStruct((bsz, HG_HEADS, HG_DK, HEAD_DV), F32),
            jax.ShapeDtypeStruct((bsz, n_gla_tiles, LANES, HEAD_DV), F32),
        ],
        scratch_shapes=[
            pltpu.VMEM((tb, IN_COLS_PAD), F32),
            pltpu.VMEM((tb, HG_KW), F32),
            pltpu.VMEM((tb, GLA_KW), F32),
            pltpu.VMEM((HG_HEADS, HG_DK, HEAD_DV), F32),
            pltpu.VMEM((n_gla_tiles, LANES, HEAD_DV), F32),
        ],
        compiler_params=pltpu.CompilerParams(
            dimension_semantics=("arbitrary", "arbitrary"), vmem_limit_bytes=VMEM_LIMIT),
        name=f"mixer_l{layer}",
    )(x, mod, nrm, win, hg_lb, wgk2, bgk, hgn, glan, wa, wb, wo, shg0, sgla0, *plan_consts)


ROUTER_ROWS = 8 + N_EXPERTS
MOE_TILE = 512
TILE_ASSIGN = TOPK * MOE_TILE
EXPERT_BLOCK = 512
DMA_PRIORITIES = 2


def _first_argmax_rows(vals, n):
    ridx = lax.broadcasted_iota(jnp.int32, vals.shape, 0)
    vmax = jnp.max(vals, axis=0, keepdims=True)
    imax = jnp.min(jnp.where(vals == vmax, ridx, n), axis=0, keepdims=True)
    return vmax, imax


def _router_kernel(x_ref, mod_ref, nrm_ref, wr_ref, br_ref, tri_ref,
                   h_ref, eid_ref, rank_ref, wts_ref, cnt_ref, run_scr):
    @pl.when(pl.program_id(0) == 0)
    def _():
        run_scr[...] = jnp.zeros_like(run_scr)

    u, lt, _ = x_ref.shape
    x = x_ref[...]
    sh2 = mod_ref[:, 3:4, :]
    sc2 = mod_ref[:, 4:5, :]
    h = _rms_mod(x, nrm_ref[...].reshape(1, 1, D_MODEL), sc2, sh2).reshape(u * lt, D_MODEL)
    h_ref[...] = h
    logits = lax.dot_general(wr_ref[...], h, (((1,), (1,)), ((), ())), preferred_element_type=F32,
                             precision=lax.Precision.HIGHEST) + br_ref[...]
    gl = logits[0:N_GROUPS]
    gmax, gi = _first_argmax_rows(gl, N_GROUPS)
    gp = 1.0 / jnp.sum(jnp.exp(gl - gmax), axis=0, keepdims=True)
    le = logits[8:8 + EXPERTS_PER_GROUP]
    for g in range(1, N_GROUPS):
        le = jnp.where(gi == g, logits[8 + g * EXPERTS_PER_GROUP:8 + (g + 1) * EXPERTS_PER_GROUP], le)
    pe = jnp.exp(le - jnp.max(le, axis=0, keepdims=True))
    pe = pe / jnp.sum(pe, axis=0, keepdims=True)
    v1, i1 = _first_argmax_rows(pe, EXPERTS_PER_GROUP)
    ridx = lax.broadcasted_iota(jnp.int32, pe.shape, 0)
    v2, i2 = _first_argmax_rows(jnp.where(ridx == i1, -1.0, pe), EXPERTS_PER_GROUP)
    vsum = v1 + v2
    wts_ref[0:1, :] = gp * v1 / vsum
    wts_ref[1:2, :] = gp * v2 / vsum
    eflat = jnp.concatenate([gi * EXPERTS_PER_GROUP + i1, gi * EXPERTS_PER_GROUP + i2], axis=1)
    eid_ref[0] = eflat
    onehot = (eflat == lax.broadcasted_iota(jnp.int32, (N_EXPERTS, TILE_ASSIGN), 0)).astype(F32)
    before = _dot(onehot.astype(BF16), tri_ref[...]) + run_scr[...]
    rank_ref[0] = jnp.sum(onehot * before, axis=0, keepdims=True).astype(jnp.int32)
    run_scr[...] = run_scr[...] + jnp.sum(onehot, axis=1, keepdims=True)
    cnt_ref[...] = run_scr[...].astype(jnp.int32)


def _router(x_units, mod_units, nrm, wr, br, tri):
    n_units, lt, _ = x_units.shape
    u = MOE_TILE // lt
    n_tiles = n_units // u
    return pl.pallas_call(
        _router_kernel,
        grid=(n_tiles,),
        in_specs=[
            pl.BlockSpec((u, lt, D_MODEL), lambda i: (i, 0, 0)),
            pl.BlockSpec((u, 6, D_MODEL), lambda i: (i, 0, 0)),
            pl.BlockSpec((1, D_MODEL), lambda i: (0, 0)),
            pl.BlockSpec((ROUTER_ROWS, D_MODEL), lambda i: (0, 0)),
            pl.BlockSpec((ROUTER_ROWS, 1), lambda i: (0, 0)),
            pl.BlockSpec((TILE_ASSIGN, TILE_ASSIGN), lambda i: (0, 0)),
        ],
        out_specs=[
            pl.BlockSpec((MOE_TILE, D_MODEL), lambda i: (i, 0)),
            pl.BlockSpec((1, 1, TILE_ASSIGN), lambda i: (i, 0, 0)),
            pl.BlockSpec((1, 1, TILE_ASSIGN), lambda i: (i, 0, 0)),
            pl.BlockSpec((TOPK, MOE_TILE), lambda i: (0, i)),
            pl.BlockSpec((N_EXPERTS, 1), lambda i: (0, 0)),
        ],
        out_shape=[
            jax.ShapeDtypeStruct((n_tiles * MOE_TILE, D_MODEL), F32),
            jax.ShapeDtypeStruct((n_tiles, 1, TILE_ASSIGN), jnp.int32),
            jax.ShapeDtypeStruct((n_tiles, 1, TILE_ASSIGN), jnp.int32),
            jax.ShapeDtypeStruct((TOPK, n_tiles * MOE_TILE), F32),
            jax.ShapeDtypeStruct((N_EXPERTS, 1), jnp.int32),
        ],
        scratch_shapes=[pltpu.VMEM((N_EXPERTS, 1), F32)],
        compiler_params=pltpu.CompilerParams(dimension_semantics=("arbitrary",)),
        name="moe_router",
    )(x_units, mod_units, nrm, wr, br, tri)


def _start_row_gather(idx_ref, n_rows, src_hbm, dst, sem):
    def body(i, carry):
        for p in range(DMA_PRIORITIES):
            r = i * DMA_PRIORITIES + p
            row = idx_ref[0, 0, r]
            pltpu.make_async_copy(src_hbm.at[pl.ds(row, 1)], dst.at[pl.ds(r, 1)], sem).start(priority=p)
        return carry
    lax.fori_loop(0, n_rows // DMA_PRIORITIES, body, 0, unroll=4)


def _wait_row_gather(n_rows, src_hbm, dst, sem):
    pltpu.make_async_copy(src_hbm.at[pl.ds(0, n_rows)], dst, sem).wait()


def _dispatch_kernel(pend_ref, padded_ref, dest_ref, h_ref, xs_hbm, zbuf, sem):
    n_blocks = xs_hbm.shape[0] // EXPERT_BLOCK

    def zero_block(first_row):
        return pltpu.make_async_copy(
            zbuf, xs_hbm.at[pl.ds(pl.multiple_of(first_row, EXPERT_BLOCK), EXPERT_BLOCK)], sem.at[0])

    @pl.when(pl.program_id(0) == 0)
    def _():
        zbuf[...] = jnp.zeros_like(zbuf)
        n_used = pend_ref[N_EXPERTS - 1] // EXPERT_BLOCK
        for e in range(N_EXPERTS):
            @pl.when(padded_ref[e] > 0)
            def _():
                zero_block(pend_ref[e] - EXPERT_BLOCK).start()
        lax.fori_loop(n_used, n_blocks, lambda b, c: (zero_block(b * EXPERT_BLOCK).start(), c)[1], 0)
        for e in range(N_EXPERTS):
            @pl.when(padded_ref[e] > 0)
            def _():
                zero_block(pend_ref[e] - EXPERT_BLOCK).wait()
        lax.fori_loop(n_used, n_blocks, lambda b, c: (zero_block(b * EXPERT_BLOCK).wait(), c)[1], 0)

    def body(t, carry):
        for k in range(TOPK):
            slot = dest_ref[0, 0, k * MOE_TILE + t]
            pltpu.make_async_copy(h_ref.at[pl.ds(t, 1)], xs_hbm.at[pl.ds(slot, 1)],
                                  sem.at[1]).start(priority=k % DMA_PRIORITIES)
        return carry
    lax.fori_loop(0, MOE_TILE, body, 0, unroll=8)
    for k in range(TOPK):
        pltpu.make_async_copy(h_ref, xs_hbm.at[pl.ds(0, MOE_TILE)], sem.at[1]).wait()


def _dispatch(pad_end, padded, dest_tiles, h, n_slots):
    n_tiles = dest_tiles.shape[0]
    grid_spec = pltpu.PrefetchScalarGridSpec(
        num_scalar_prefetch=2,
        grid=(n_tiles,),
        in_specs=[
            pl.BlockSpec((1, 1, TILE_ASSIGN), lambda i, pe, pd: (i, 0, 0), memory_space=pltpu.SMEM),
            pl.BlockSpec((MOE_TILE, D_MODEL), lambda i, pe, pd: (i, 0)),
        ],
        out_specs=pl.BlockSpec(memory_space=pl.ANY),
        scratch_shapes=[pltpu.VMEM((EXPERT_BLOCK, D_MODEL), F32), pltpu.SemaphoreType.DMA((2,))],
    )
    return pl.pallas_call(
        _dispatch_kernel,
        grid_spec=grid_spec,
        out_shape=jax.ShapeDtypeStruct((n_slots, D_MODEL), F32),
        compiler_params=pltpu.CompilerParams(dimension_semantics=("arbitrary",)),
        name="moe_dispatch",
    )(pad_end, padded, dest_tiles, h)


def _experts_kernel(be_ref, nused_ref, x_ref, wg_ref, wu_ref, wd_ref, o_ref):
    @pl.when(pl.program_id(0) < nused_ref[0])
    def _():
        xb = x_ref[...].astype(BF16)
        a = _silu(_dot(xb, wg_ref[0])) * _dot(xb, wu_ref[0])
        o_ref[...] = _dot(a.astype(BF16), wd_ref[0])

    @pl.when(pl.program_id(0) >= nused_ref[0])
    def _():
        o_ref[...] = jnp.zeros_like(o_ref)


def _experts(block_e, n_used, xs, wg, wu, wd):
    n_blocks = xs.shape[0] // EXPERT_BLOCK

    def row_block(i, be, nu):
        return (jnp.minimum(i, nu[0] - 1), 0)

    def expert_block(i, be, nu):
        return (be[jnp.minimum(i, nu[0] - 1)], 0, 0)

    grid_spec = pltpu.PrefetchScalarGridSpec(
        num_scalar_prefetch=2,
        grid=(n_blocks,),
        in_specs=[
            pl.BlockSpec((EXPERT_BLOCK, D_MODEL), row_block),
            pl.BlockSpec((1, D_MODEL, D_EXPERT), expert_block),
            pl.BlockSpec((1, D_MODEL, D_EXPERT), expert_block),
            pl.BlockSpec((1, D_EXPERT, D_MODEL), expert_block),
        ],
        out_specs=pl.BlockSpec((EXPERT_BLOCK, D_MODEL), lambda i, be, nu: (i, 0)),
    )
    return pl.pallas_call(
        _experts_kernel,
        grid_spec=grid_spec,
        out_shape=jax.ShapeDtypeStruct(xs.shape, F32),
        compiler_params=pltpu.CompilerParams(
            dimension_semantics=("arbitrary",), vmem_limit_bytes=VMEM_LIMIT),
        name="moe_experts",
    )(block_e, n_used, xs, wg, wu, wd)


def _combine_kernel(dst_cur_ref, dst_nxt_ref, x_ref, mod_ref, wts_ref, nrm_ref, y_hbm, o_ref, buf, sem,
                    *, final_norm):
    i = pl.program_id(0)
    n = pl.num_programs(0)
    slot = i % 2

    @pl.when(i == 0)
    def _():
        _start_row_gather(dst_cur_ref, TILE_ASSIGN, y_hbm, buf.at[0], sem.at[0])

    @pl.when(i + 1 < n)
    def _():
        _start_row_gather(dst_nxt_ref, TILE_ASSIGN, y_hbm, buf.at[1 - slot], sem.at[1 - slot])

    _wait_row_gather(TILE_ASSIGN, y_hbm, buf.at[slot], sem.at[slot])
    u, lt, _ = x_ref.shape
    y = (wts_ref[:, 0:1] * buf[slot, 0:MOE_TILE, :] + wts_ref[:, 1:2] * buf[slot, MOE_TILE:TILE_ASSIGN, :])
    g2 = mod_ref[:, 5:6, :]
    out = x_ref[...] + g2 * y.reshape(u, lt, D_MODEL)
    if final_norm:
        out = out * lax.rsqrt(jnp.mean(out * out, axis=-1, keepdims=True) + NORM_EPS)
        out = out * nrm_ref[...].reshape(1, 1, D_MODEL)
    o_ref[...] = out


def _combine(dest_tiles, x_units, mod_units, wts_col, nrm, y_slots, *, final_norm):
    n_units, lt, _ = x_units.shape
    u = MOE_TILE // lt
    n_tiles = n_units // u
    return pl.pallas_call(
        functools.partial(_combine_kernel, final_norm=final_norm),
        grid=(n_tiles,),
        in_specs=[
            pl.BlockSpec((1, 1, TILE_ASSIGN), lambda i: (i, 0, 0), memory_space=pltpu.SMEM),
            pl.BlockSpec((1, 1, TILE_ASSIGN), lambda i: (jnp.minimum(i + 1, n_tiles - 1), 0, 0),
                         memory_space=pltpu.SMEM),
            pl.BlockSpec((u, lt, D_MODEL), lambda i: (i, 0, 0)),
            pl.BlockSpec((u, 6, D_MODEL), lambda i: (i, 0, 0)),
            pl.BlockSpec((MOE_TILE, TOPK), lambda i: (i, 0)),
            pl.BlockSpec((1, D_MODEL), lambda i: (0, 0)),
            pl.BlockSpec(memory_space=pl.ANY),
        ],
        out_specs=pl.BlockSpec((u, lt, D_MODEL), lambda i: (i, 0, 0)),
        out_shape=jax.ShapeDtypeStruct(x_units.shape, F32),
        scratch_shapes=[pltpu.VMEM((2, TILE_ASSIGN, D_MODEL), F32), pltpu.SemaphoreType.DMA((2,))],
        compiler_params=pltpu.CompilerParams(
            dimension_semantics=("arbitrary",), vmem_limit_bytes=VMEM_LIMIT),
        name="moe_combine",
    )(dest_tiles, dest_tiles, x_units, mod_units, wts_col, nrm, y_slots)


def _routing_tables(eid_tiles, rank_tiles, counts):
    n_blocks = eid_tiles.size // EXPERT_BLOCK + N_EXPERTS
    padded = (counts + EXPERT_BLOCK - 1) // EXPERT_BLOCK * EXPERT_BLOCK
    pad_end = jnp.cumsum(padded).astype(jnp.int32)
    pad_start = pad_end - padded
    block_start = jnp.arange(n_blocks, dtype=jnp.int32)[:, None] * EXPERT_BLOCK
    block_e = jnp.minimum(jnp.sum((block_start >= pad_end[None, :]).astype(jnp.int32), axis=1),
                          N_EXPERTS - 1).astype(jnp.int32)
    n_used = pad_end[-1:] // EXPERT_BLOCK
    experts = jnp.arange(N_EXPERTS, dtype=jnp.int32)
    first_slot = jnp.sum(jnp.where(eid_tiles[..., None] == experts, pad_start, 0), axis=-1)
    return block_e, n_used, pad_end, padded, first_slot + rank_tiles


def _moe_layer(x, mod_l, nrm_ffn, wr, br, tri, wg, wu, wd, nrm_final, *, final_norm):
    bsz, seq, _ = x.shape
    t = bsz * seq
    lt = min(seq, MOE_TILE)
    per = seq // lt
    x_units = x.reshape(t // lt, lt, D_MODEL)
    mod_units = jnp.repeat(mod_l, per, axis=0) if per > 1 else mod_l
    h, eid_tiles, rank_tiles, wts, counts = _router(x_units, mod_units, nrm_ffn, wr, br, tri)
    block_e, n_used, pad_end, padded, dest_tiles = _routing_tables(eid_tiles, rank_tiles, counts[:, 0])
    n_slots = block_e.shape[0] * EXPERT_BLOCK
    xs = _dispatch(pad_end, padded, dest_tiles, h, n_slots)
    y_slots = _experts(block_e, n_used, xs, wg, wu, wd)
    out = _combine(dest_tiles, x_units, mod_units, wts.T, nrm_final, y_slots, final_norm=final_norm)
    return out.reshape(bsz, seq, D_MODEL)


def kernel(x_prompt, x_sample, c_prompt, c_sample, state_hgrn, state_gla, w_ada, b_ada, norm_mix,
           norm_ffn, w_in, hg_lb, hg_onorm, w_gk2, b_gk, gla_onorm, w_br_a, w_br_b, w_out, w_rg, b_rg,
           w_re, b_re, w_e_gate, w_e_up, w_e_down, norm_final):
    bp = x_prompt.shape[0]
    bs = x_sample.shape[0]
    mod = _ada_mod(jnp.concatenate([c_prompt, c_sample], axis=0), w_ada, b_ada)
    mod = mod.reshape(DEPTH, bp + bs, 6, D_MODEL)

    glr0 = C_GOG + GLA_W
    win_r = jnp.concatenate(
        [w_in[:, :, :glr0], w_in[:, :, glr0 + GLA_GATE_RANK:], w_in[:, :, glr0:glr0 + GLA_GATE_RANK],
         jnp.zeros((DEPTH, D_MODEL, LANES - GLA_GATE_RANK), F32)], axis=2).astype(BF16)
    wgk2_p = jnp.concatenate(
        [w_gk2, jnp.zeros((DEPTH, LANES - GLA_GATE_RANK, GLA_KW), F32)], axis=1).astype(BF16)
    wa_b = w_br_a.astype(BF16)
    wb_b = w_br_b.astype(BF16)
    wo_b = w_out.astype(BF16)
    plan_consts = []
    for plan in (FAST_PLAN, SAFE_PLAN):
        plan_consts += [jnp.asarray(plan.segment_sum_matrix(), BF16), jnp.asarray(plan.masks(), F32)]
    zpad = jnp.zeros((DEPTH, 8 - N_GROUPS, D_MODEL), F32)
    wr = jnp.concatenate([jnp.swapaxes(w_rg, 1, 2), zpad, jnp.swapaxes(w_re, 1, 2)], axis=1)
    br = jnp.concatenate([b_rg, jnp.zeros((DEPTH, 8 - N_GROUPS), F32), b_re], axis=1)[:, :, None]
    wg_b = w_e_gate.astype(BF16)
    wu_b = w_e_up.astype(BF16)
    wd_b = w_e_down.astype(BF16)
    nrm_f = norm_final.reshape(1, D_MODEL)
    assign = np.arange(TILE_ASSIGN)
    tri = jnp.asarray(assign[:, None] < assign[None, :], BF16)

    def run(x, mod_g, shg, sgla, tb):
        bsz = x.shape[0]
        new_hg, new_gla = [], []
        for l in range(DEPTH):
            x, s1, s2 = _mixer(
                x, mod_g[l], norm_mix[l:l + 1], win_r[l], hg_lb, wgk2_p[l], b_gk[l:l + 1],
                hg_onorm[l:l + 1], gla_onorm[l:l + 1], wa_b[l], wb_b[l], wo_b[l],
                shg[l], sgla[l].reshape(bsz, GLA_KW // LANES, LANES, HEAD_DV), plan_consts,
                layer=l, tb=tb)
            new_hg.append(s1)
            new_gla.append(s2.reshape(bsz, GLA_HEADS, GLA_DK, HEAD_DV))
            x = _moe_layer(x, mod_g[l], norm_ffn[l:l + 1], wr[l], br[l], tri, wg_b[l], wu_b[l], wd_b[l],
                           nrm_f, final_norm=(l == DEPTH - 1))
        return x, jnp.stack(new_hg), jnp.stack(new_gla)

    zeros_hg = jnp.zeros((DEPTH, bp, HG_HEADS, HG_DK, HEAD_DV), F32)
    zeros_gla = jnp.zeros((DEPTH, bp, GLA_HEADS, GLA_DK, HEAD_DV), F32)
    y_p, hg_p, gla_p = run(x_prompt, mod[:, :bp], zeros_hg, zeros_gla, 256)
    y_s, hg_s, gla_s = run(x_sample, mod[:, bp:], state_hgrn, state_gla, CHUNK)
    return (y_p, y_s, hg_p, gla_p, hg_s, gla_s)
```

```python
import functools

import numpy as np
import jax
import jax.numpy as jnp
from jax import lax
from jax.experimental import pallas as pl
from jax.experimental.pallas import tpu as pltpu

F32 = jnp.float32
BF16 = jnp.bfloat16

D_MODEL = 1024
DEPTH = 2
CHUNK = 64
NORM_EPS = 1e-6
LOG_FLOOR = 1e-30
HG_HEADS = 4
HG_DK = 128
HEAD_DV = 128
HG_KW = HG_HEADS * HG_DK
HG_W = HG_HEADS * HEAD_DV
GLA_HEADS = 4
GLA_DK = 64
GLA_KW = GLA_HEADS * GLA_DK
GLA_W = GLA_HEADS * HEAD_DV
GLA_GATE_RANK = 16
GLA_GATE_NORM = 16.0
N_GROUPS = 4
EXPERTS_PER_GROUP = 8
N_EXPERTS = N_GROUPS * EXPERTS_PER_GROUP
TOPK = 2
D_EXPERT = 512

LANES = 128
VMEM_LIMIT = 56 * 1024 * 1024

C_HQ = 0
C_HF = C_HQ + HG_KW
C_HI = C_HF + HG_KW
C_HOG = C_HI + HG_W
C_GQ = C_HOG + HG_W
C_GK = C_GQ + GLA_KW
C_GV = C_GK + GLA_KW
C_GOG = C_GV + GLA_W
C_GA = C_GOG + GLA_W
C_GB = C_GA + D_MODEL
C_GLR = C_GB + D_MODEL
IN_COLS_PAD = C_GLR + LANES
PROJ_TILE = 640
assert IN_COLS_PAD % PROJ_TILE == 0

class _ScorePlan:
    def __init__(self, levels, adjacent, diag_block):
        self.levels = levels
        self.adjacent = adjacent
        self.diag_block = diag_block
        self.cum_rows = (2 * len(levels) + 1) * CHUNK
        self.n_masks = len(levels) + 1 + int(adjacent)

    def segment_sum_matrix(self):
        t = np.arange(CHUNK)[:, None]
        r = np.arange(CHUNK)[None, :]
        rows = []
        for m in self.levels:
            same = (t // m) == (r // m)
            rows.append(same & (r <= t))
            rows.append(same & (r > t))
        rows.append(r <= t)
        return np.concatenate(rows, axis=0).astype(np.float32)

    def masks(self):
        t = np.arange(CHUNK)[:, None]
        s = np.arange(CHUNK)[None, :]
        masks = [((t // self.diag_block) == (s // self.diag_block)) & (s <= t)]
        for m in self.levels + ((1,) if self.adjacent else ()):
            masks.append(((t // (2 * m)) == (s // (2 * m))) & ((t // m) % 2 == 1) & ((s // m) % 2 == 0))
        return np.stack(masks).astype(np.float32)


SAFE_PLAN = _ScorePlan((32, 16, 8, 4, 2), True, 1)


def _dot(a, b):
    return jnp.dot(a, b, preferred_element_type=F32)


def _dot_nt(a, b):
    return lax.dot_general(a, b, (((1,), (1,)), ((), ())), preferred_element_type=F32)


def _sigmoid(x):
    return 1.0 / (1.0 + jnp.exp(-x))


def _silu(x):
    return x * _sigmoid(x)


def _rms_mod(x, gain, scale, shift):
    y = x * lax.rsqrt(jnp.mean(x * x, axis=-1, keepdims=True) + NORM_EPS)
    return y * gain * (1.0 + scale) + shift


def _ada_kernel(c_ref, w_ref, b_ref, o_ref):
    c = c_ref[...]
    o_ref[0] = jnp.dot(_silu(c), w_ref[0], preferred_element_type=F32,
                       precision=lax.Precision.HIGHEST) + b_ref[0]


def _ada_mod(c_all, w_ada, b_ada):
    nb = c_all.shape[0]
    tn = 512
    return pl.pallas_call(
        _ada_kernel,
        grid=(DEPTH, 6 * D_MODEL // tn),
        in_specs=[
            pl.BlockSpec((nb, D_MODEL), lambda l, j: (0, 0)),
            pl.BlockSpec((1, D_MODEL, tn), lambda l, j: (l, 0, j)),
            pl.BlockSpec((1, 1, tn), lambda l, j: (l, 0, j)),
        ],
        out_specs=pl.BlockSpec((1, nb, tn), lambda l, j: (l, 0, j)),
        out_shape=jax.ShapeDtypeStruct((DEPTH, nb, 6 * D_MODEL), F32),
        name="ada_mod",
    )(c_all, w_ada, b_ada.reshape(DEPTH, 1, 6 * D_MODEL))


def _chunk_attention(q, k, v, g, states, mall_ref, mask_ref, heads_per_tile, plan):
    w = q.shape[1]
    n_tiles = w // LANES
    g_hi = g.astype(BF16)
    r1 = g - g_hi.astype(F32)
    g_mid = r1.astype(BF16)
    g_lo = (r1 - g_mid.astype(F32)).astype(BF16)
    mall = mall_ref[...]
    cums = _dot(mall, g_hi) + _dot(mall, g_mid) + _dot(mall, g_lo)
    b = cums[plan.cum_rows - CHUNK:plan.cum_rows]
    level_q = []
    level_k = []
    for i in range(len(plan.levels)):
        level_q.append(q * jnp.exp(cums[2 * i * CHUNK:(2 * i + 1) * CHUNK]))
        level_k.append(k * jnp.exp(cums[(2 * i + 1) * CHUNK:(2 * i + 2) * CHUNK]))
    if plan.diag_block == 1:
        qs = [q]
        ks = [k]
    else:
        i = plan.levels.index(plan.diag_block)
        qs = [level_q[i]]
        ks = [k * jnp.exp(-cums[2 * i * CHUNK:(2 * i + 1) * CHUNK])]
    qs += level_q
    ks += level_k
    if plan.adjacent:
        qs.append(q * jnp.exp(g))
        ks.append(k)
    b_last = b[CHUNK - 1:CHUNK]
    q_in = q * jnp.exp(b)
    k_out = k * jnp.exp(b_last - b)
    e_last = jnp.exp(b_last)

    dk = LANES // heads_per_tile
    lane = lax.broadcasted_iota(jnp.int32, (CHUNK, LANES), 1)
    row = lax.broadcasted_iota(jnp.int32, (LANES, HEAD_DV), 0)
    outs = []
    new_states = []
    for ti in range(n_tiles):
        sl = slice(ti * LANES, (ti + 1) * LANES)
        ks_t =[kk[:, sl].astype(BF16) for kk in ks]
        k_out_t = k_out[:, sl].T.astype(BF16)
        e_col = jnp.broadcast_to(e_last[:, sl], (LANES, LANES)).T
        s_old = states[ti]
        s_old_b = s_old.astype(BF16)
        upd = None
        for j in range(heads_per_tile):
            head = ti * heads_per_tile + j
            if heads_per_tile == 1:
                sel = lambda a: a
            else:
                in_head = (lane // dk) == j
                sel = lambda a, in_head=in_head: jnp.where(in_head, a, 0.0)
            sc = jnp.zeros((CHUNK, CHUNK), F32)
            for i in range(plan.n_masks):
                sc = sc + _dot_nt(sel(qs[i][:, sl]).astype(BF16), ks_t[i]) * mask_ref[i]
            vh = v[:, head * HEAD_DV:(head + 1) * HEAD_DV].astype(BF16)
            o = _dot(sc.astype(BF16), vh) + _dot(sel(q_in[:, sl]).astype(BF16), s_old_b)
            outs.append(o)
            u = _dot(k_out_t, vh)
            upd = u if upd is None else jnp.where((row // dk) == j, u, upd)
        new_states.append(e_col * s_old + upd)
    return jnp.concatenate(outs, axis=1), new_states


FAST_BLOCK = 16
N_SUB = CHUNK // FAST_BLOCK
SLAB_ROWS = FAST_BLOCK * (N_SUB * (N_SUB - 1) // 2) + CHUNK
FAST_BLOCK_DECAY_LIMIT = 60.0


def _slab_mask():
    t = np.arange(CHUNK)[:, None]
    cols = []
    for i in range(1, N_SUB):
        cols.append(np.broadcast_to(t // FAST_BLOCK == i, (CHUNK, i * FAST_BLOCK)))
    s = np.arange(CHUNK)[None, :]
    cols.append((t // FAST_BLOCK == s // FAST_BLOCK) & (s <= t))
    return np.concatenate(cols, axis=1).astype(np.float32)


def _chunk_attention_fast(q, k, v, g, states, tril_ref, slab_mask_ref, heads_per_tile):
    w = q.shape[1]
    n_tiles = w // LANES
    g_hi = g.astype(BF16)
    r1 = g - g_hi.astype(F32)
    g_mid = r1.astype(BF16)
    g_lo = (r1 - g_mid.astype(F32)).astype(BF16)
    tril = tril_ref[...]
    b = _dot(tril, g_hi) + _dot(tril, g_mid) + _dot(tril, g_lo)
    ends = [b[(i + 1) * FAST_BLOCK - 1:(i + 1) * FAST_BLOCK] for i in range(N_SUB)]

    def per_block(rows):
        return jnp.concatenate([jnp.broadcast_to(r, (FAST_BLOCK, w)) for r in rows], axis=0)

    zero = jnp.zeros((1, w), F32)
    b_start = per_block([zero] + ends[:-1])
    q_blk = q * jnp.exp(b - b_start)
    k_diag = k * jnp.exp(b_start - b)
    k_end = k * jnp.exp(per_block(ends) - b)
    slabs = []
    for i in range(1, N_SUB):
        for jb in range(i):
            blk = k_end[jb * FAST_BLOCK:(jb + 1) * FAST_BLOCK]
            slabs.append(blk if jb == i - 1 else blk * jnp.exp(ends[i - 1] - ends[jb]))
    slabs.append(k_diag)
    k_slab = jnp.concatenate(slabs, axis=0)
    q_in = q_blk * per_block([jnp.ones((1, w), F32)] + [jnp.exp(e) for e in ends[:-1]])
    k_out = k_end * per_block([jnp.exp(ends[-1] - e) for e in ends[:-1]] + [jnp.ones((1, w), F32)])
    e_last = jnp.exp(ends[-1])

    dk = LANES // heads_per_tile
    lane = lax.broadcasted_iota(jnp.int32, (CHUNK, LANES), 1)
    row = lax.broadcasted_iota(jnp.int32, (LANES, HEAD_DV), 0)
    slab_mask = slab_mask_ref[...]
    outs = []
    new_states = []
    for ti in range(n_tiles):
        sl = slice(ti * LANES, (ti + 1) * LANES)
        k_slab_t = k_slab[:, sl].astype(BF16)
        k_out_t = k_out[:, sl].T.astype(BF16)
        e_col = jnp.broadcast_to(e_last[:, sl], (LANES, LANES)).T
        s_old = states[ti]
        s_old_b = s_old.astype(BF16)
        upd = None
        for j in range(heads_per_tile):
            head = ti * heads_per_tile + j
            if heads_per_tile == 1:
                sel = lambda a: a
            else:
                in_head = (lane // dk) == j
                sel = lambda a, in_head=in_head: jnp.where(in_head, a, 0.0)
            sc = _dot_nt(sel(q_blk[:, sl]).astype(BF16), k_slab_t) * slab_mask
            vh = v[:, head * HEAD_DV:(head + 1) * HEAD_DV].astype(BF16)
            v_slab = jnp.concatenate([vh[:i * FAST_BLOCK] for i in range(1, N_SUB)] + [vh], axis=0)
            o = _dot(sc.astype(BF16), v_slab) + _dot(sel(q_in[:, sl]).astype(BF16), s_old_b)
            outs.append(o)
            u = _dot(k_out_t, vh)
            upd = u if upd is None else jnp.where((row // dk) == j, u, upd)
        new_states.append(e_col * s_old + upd)
    return jnp.concatenate(outs, axis=1), new_states


def _head_norm_gate(o, gain, gate):
    outs = []
    for h in range(o.shape[1] // HEAD_DV):
        sl = slice(h * HEAD_DV, (h + 1) * HEAD_DV)
        oh = o[:, sl]
        oh = oh * lax.rsqrt(jnp.mean(oh * oh, axis=-1, keepdims=True) + NORM_EPS) * gain
        outs.append(oh * _silu(gate[:, sl]))
    return jnp.concatenate(outs, axis=1)


def _mixer_kernel(x_ref, mod_ref, nrm_ref, win_ref, lb_ref, wgk2_ref, bgk_ref, hgn_ref, glan_ref,
                  wa_ref, wb_ref, wo_ref, shg0_ref, sgla0_ref,
                  tril_ref, slab_mask_ref, mall_safe_ref, mask_safe_ref,
                  xo_ref, shg_o_ref, sgla_o_ref,
                  p_scr, k_scr, lg_scr, shg_scr, sgla_scr, *, layer, tb):
    j = pl.program_id(1)

    @pl.when(j == 0)
    def _():
        shg_scr[...] = shg0_ref[0]
        sgla_scr[...] = sgla0_ref[0]

    x = x_ref[0]
    sh1 = mod_ref[0, 0:1, :]
    sc1 = mod_ref[0, 1:2, :]
    g1 = mod_ref[0, 2:3, :]
    hb = _rms_mod(x, nrm_ref[...], sc1, sh1).astype(BF16)
    for c in range(0, IN_COLS_PAD, PROJ_TILE):
        p_scr[:, c:c + PROJ_TILE] = _dot(hb, win_ref[:, c:c + PROJ_TILE])

    lb_all = lb_ref[...]
    lb_max = jnp.max(lb_all, axis=0, keepdims=True)
    lb_exp = jnp.exp(lb_all - lb_max)
    sm = lb_exp / jnp.sum(lb_exp, axis=0, keepdims=True)
    lbl = jnp.clip(jnp.sum(sm[0:layer + 1], axis=0, keepdims=True) - sm[0:1], 0.0, 1.0)

    p_scr[:, C_HQ:C_HQ + HG_KW] = _silu(p_scr[:, C_HQ:C_HQ + HG_KW]) * (HG_DK ** -0.5)
    z = p_scr[:, C_HF:C_HF + HG_KW]
    f = lbl + (1.0 - lbl) * _sigmoid(z)
    p_scr[:, C_HF:C_HF + HG_KW] = jnp.log(jnp.maximum(f, LOG_FLOOR))
    k_scr[...] = (1.0 - lbl) * _sigmoid(-z)
    glr = p_scr[:, C_GLR:C_GLR + LANES].astype(BF16)
    gate = _dot(glr, wgk2_ref[...]) + bgk_ref[...]
    lg_scr[...] = (jnp.minimum(gate, 0.0) - jnp.log1p(jnp.exp(-jnp.abs(gate)))) * (1.0 / GLA_GATE_NORM)
    p_scr[:, C_GQ:C_GQ + GLA_KW] = p_scr[:, C_GQ:C_GQ + GLA_KW] * (GLA_DK ** -0.5)

    n_hg_tiles = HG_KW // LANES
    n_gla_tiles = GLA_KW // LANES

    def one_chunk(rows, states, attend):
        o_hg, st_hg = attend(p_scr[rows, C_HQ:C_HQ + HG_KW], k_scr[rows, :],
                             p_scr[rows, C_HI:C_HI + HG_W], p_scr[rows, C_HF:C_HF + HG_KW],
                             states[:n_hg_tiles], 1)
        p_scr[rows, C_HI:C_HI + HG_W] = o_hg
        o_gla, st_gla = attend(p_scr[rows, C_GQ:C_GQ + GLA_KW], p_scr[rows, C_GK:C_GK + GLA_KW],
                               p_scr[rows, C_GV:C_GV + GLA_W], lg_scr[rows, :],
                               states[n_hg_tiles:], 2)
        p_scr[rows, C_GV:C_GV + GLA_W] = o_gla
        return st_hg + st_gla

    def attend_fast(q, k, v, g, states, heads_per_tile):
        return _chunk_attention_fast(q, k, v, g, states, tril_ref, slab_mask_ref, heads_per_tile)

    def attend_safe(q, k, v, g, states, heads_per_tile):
        return _chunk_attention(q, k, v, g, states, mall_safe_ref, mask_safe_ref, heads_per_tile, SAFE_PLAN)

    def load_states():
        return [shg_scr[t] for t in range(n_hg_tiles)] + [sgla_scr[t] for t in range(n_gla_tiles)]

    def store_states(states):
        for t in range(n_hg_tiles):
            shg_scr[t] = states[t]
        for t in range(n_gla_tiles):
            sgla_scr[t] = states[n_hg_tiles + t]

    def run_chunks(attend, unrolled):
        if unrolled:
            states = load_states()
            for ci in range(tb // CHUNK):
                states = one_chunk(pl.ds(ci * CHUNK, CHUNK), states, attend)
            store_states(states)
        else:
            def chunk_body(ci, carry):
                rows = pl.ds(pl.multiple_of(ci * CHUNK, CHUNK), CHUNK)
                store_states(one_chunk(rows, load_states(), attend))
                return carry

            lax.fori_loop(0, tb // CHUNK, chunk_body, 0)

    blk = FAST_BLOCK
    min_hg = jnp.min(jnp.sum(p_scr[:, C_HF:C_HF + HG_KW].reshape(tb // blk, blk, HG_KW), axis=1))
    min_gla = jnp.min(jnp.sum(lg_scr[...].reshape(tb // blk, blk, GLA_KW), axis=1))
    bounded = jnp.minimum(min_hg, min_gla) >= -FAST_BLOCK_DECAY_LIMIT

    @pl.when(bounded)
    def _():
        run_chunks(attend_fast, True)

    @pl.when(jnp.logical_not(bounded))
    def _():
        run_chunks(attend_safe, False)

    o_hg = _head_norm_gate(p_scr[:, C_HI:C_HI + HG_W], hgn_ref[...], p_scr[:, C_HOG:C_HOG + HG_W])
    o_gla = _head_norm_gate(p_scr[:, C_GV:C_GV + GLA_W], glan_ref[...], p_scr[:, C_GOG:C_GOG + GLA_W])
    ya = _dot(o_hg.astype(BF16), wa_ref[...])
    yb = _dot(o_gla.astype(BF16), wb_ref[...])
    merged = (_sigmoid(p_scr[:, C_GA:C_GA + D_MODEL]) * ya
              + _sigmoid(p_scr[:, C_GB:C_GB + D_MODEL]) * yb)
    m = _dot(merged.astype(BF16), wo_ref[...])
    xo_ref[0] = x + g1 * m

    @pl.when(j == pl.num_programs(1) - 1)
    def _():
        shg_o_ref[0] = shg_scr[...]
        sgla_o_ref[0] = sgla_scr[...]


def _const_spec(shape):
    nd = len(shape)
    return pl.BlockSpec(shape, lambda b, j, nd=nd: (0,) * nd, pipeline_mode=pl.Buffered(1))


def _mixer(x, mod, nrm, win, hg_lb, wgk2, bgk, hgn, glan, wa, wb, wo, shg0, sgla0, plan_consts,
           *, layer, tb):
    bsz, seq, _ = x.shape
    kern = functools.partial(_mixer_kernel, layer=layer, tb=tb)
    n_gla_tiles = GLA_KW // LANES
    return pl.pallas_call(
        kern,
        grid=(bsz, seq // tb),
        in_specs=[
            pl.BlockSpec((1, tb, D_MODEL), lambda b, j: (b, j, 0)),
            pl.BlockSpec((1, 6, D_MODEL), lambda b, j: (b, 0, 0)),
            _const_spec((1, D_MODEL)),
            _const_spec((D_MODEL, IN_COLS_PAD)),
            _const_spec((DEPTH, HG_KW)),
            _const_spec((LANES, GLA_KW)),
            _const_spec((1, GLA_KW)),
            _const_spec((1, HEAD_DV)),
            _const_spec((1, HEAD_DV)),
            _const_spec((HG_W, D_MODEL)),
            _const_spec((GLA_W, D_MODEL)),
            _const_spec((D_MODEL, D_MODEL)),
            pl.BlockSpec((1, HG_HEADS, HG_DK, HEAD_DV), lambda b, j: (b, 0, 0, 0)),
            pl.BlockSpec((1, n_gla_tiles, LANES, HEAD_DV), lambda b, j: (b, 0, 0, 0)),
            _const_spec((CHUNK, CHUNK)),
            _const_spec((CHUNK, SLAB_ROWS)),
            _const_spec((SAFE_PLAN.cum_rows, CHUNK)),
            _const_spec((SAFE_PLAN.n_masks, CHUNK, CHUNK)),
        ],
        out_specs=[
            pl.BlockSpec((1, tb, D_MODEL), lambda b, j: (b, j, 0)),
            pl.BlockSpec((1, HG_HEADS, HG_DK, HEAD_DV), lambda b, j: (b, 0, 0, 0)),
            pl.BlockSpec((1, n_gla_tiles, LANES, HEAD_DV), lambda b, j: (b, 0, 0, 0)),
        ],
        out_shape=[
            jax.ShapeDtypeStruct((bsz, seq, D_MODEL), F32),
            jax.ShapeDtypeStruct((bsz, HG_HEADS, HG_DK, HEAD_DV), F32),
            jax.ShapeDtypeStruct((bsz, n_gla_tiles, LANES, HEAD_DV), F32),
        ],
        scratch_shapes=[
            pltpu.VMEM((tb, IN_COLS_PAD), F32),
            pltpu.VMEM((tb, HG_KW), F32),
            pltpu.VMEM((tb, GLA_KW), F32),
            pltpu.VMEM((HG_HEADS, HG_DK, HEAD_DV), F32),
            pltpu.VMEM((n_gla_tiles, LANES, HEAD_DV), F32),
        ],
        compiler_params=pltpu.CompilerParams(
            dimension_semantics=("arbitrary", "arbitrary"), vmem_limit_bytes=VMEM_LIMIT),
        name=f"mixer_l{layer}",
    )(x, mod, nrm, win, hg_lb, wgk2, bgk, hgn, glan, wa, wb, wo, shg0, sgla0, *plan_consts)


ROUTER_ROWS = 8 + N_EXPERTS
MOE_TILE = 512
TILE_ASSIGN = TOPK * MOE_TILE
EXPERT_BLOCK = 512


def _first_argmax_rows(vals, n):
    ridx = lax.broadcasted_iota(jnp.int32, vals.shape, 0)
    vmax = jnp.max(vals, axis=0, keepdims=True)
    imax = jnp.min(jnp.where(vals == vmax, ridx, n), axis=0, keepdims=True)
    return vmax, imax


def _router_kernel(x_ref, mod_ref, nrm_ref, wr_ref, br_ref, tri_ref,
                   h_ref, eid_ref, rank_ref, wts_ref, cnt_ref, run_scr):
    @pl.when(pl.program_id(0) == 0)
    def _():
        run_scr[...] = jnp.zeros_like(run_scr)

    u, lt, _ = x_ref.shape
    x = x_ref[...]
    sh2 = mod_ref[:, 3:4, :]
    sc2 = mod_ref[:, 4:5, :]
    h = _rms_mod(x, nrm_ref[...].reshape(1, 1, D_MODEL), sc2, sh2).reshape(u * lt, D_MODEL)
    h_ref[...] = h
    logits = lax.dot_general(wr_ref[...], h, (((1,), (1,)), ((), ())), preferred_element_type=F32,
                             precision=lax.Precision.HIGHEST) + br_ref[...]
    gl = logits[0:N_GROUPS]
    gmax, gi = _first_argmax_rows(gl, N_GROUPS)
    gp = 1.0 / jnp.sum(jnp.exp(gl - gmax), axis=0, keepdims=True)
    le = logits[8:8 + EXPERTS_PER_GROUP]
    for g in range(1, N_GROUPS):
        le = jnp.where(gi == g, logits[8 + g * EXPERTS_PER_GROUP:8 + (g + 1) * EXPERTS_PER_GROUP], le)
    pe = jnp.exp(le - jnp.max(le, axis=0, keepdims=True))
    pe = pe / jnp.sum(pe, axis=0, keepdims=True)
    v1, i1 = _first_argmax_rows(pe, EXPERTS_PER_GROUP)
    ridx = lax.broadcasted_iota(jnp.int32, pe.shape, 0)
    v2, i2 = _first_argmax_rows(jnp.where(ridx == i1, -1.0, pe), EXPERTS_PER_GROUP)
    vsum = v1 + v2
    wts_ref[0:1, :] = gp * v1 / vsum
    wts_ref[1:2, :] = gp * v2 / vsum
    eflat = jnp.concatenate([gi * EXPERTS_PER_GROUP + i1, gi * EXPERTS_PER_GROUP + i2], axis=1)
    eid_ref[0] = eflat
    onehot = (eflat == lax.broadcasted_iota(jnp.int32, (N_EXPERTS, TILE_ASSIGN), 0)).astype(F32)
    before = _dot(onehot.astype(BF16), tri_ref[...]) + run_scr[...]
    rank_ref[0] = jnp.sum(onehot * before, axis=0, keepdims=True).astype(jnp.int32)
    run_scr[...] = run_scr[...] + jnp.sum(onehot, axis=1, keepdims=True)
    cnt_ref[...] = run_scr[...].astype(jnp.int32)


def _router(x_units, mod_units, nrm, wr, br, tri):
    n_units, lt, _ = x_units.shape
    u = MOE_TILE // lt
    n_tiles = n_units // u
    return pl.pallas_call(
        _router_kernel,
        grid=(n_tiles,),
        in_specs=[
            pl.BlockSpec((u, lt, D_MODEL), lambda i: (i, 0, 0)),
            pl.BlockSpec((u, 6, D_MODEL), lambda i: (i, 0, 0)),
            pl.BlockSpec((1, D_MODEL), lambda i: (0, 0)),
            pl.BlockSpec((ROUTER_ROWS, D_MODEL), lambda i: (0, 0)),
            pl.BlockSpec((ROUTER_ROWS, 1), lambda i: (0, 0)),
            pl.BlockSpec((TILE_ASSIGN, TILE_ASSIGN), lambda i: (0, 0)),
        ],
        out_specs=[
            pl.BlockSpec((MOE_TILE, D_MODEL), lambda i: (i, 0)),
            pl.BlockSpec((1, 1, TILE_ASSIGN), lambda i: (i, 0, 0)),
            pl.BlockSpec((1, 1, TILE_ASSIGN), lambda i: (i, 0, 0)),
            pl.BlockSpec((TOPK, MOE_TILE), lambda i: (0, i)),
            pl.BlockSpec((N_EXPERTS, 1), lambda i: (0, 0)),
        ],
        out_shape=[
            jax.ShapeDtypeStruct((n_tiles * MOE_TILE, D_MODEL), F32),
            jax.ShapeDtypeStruct((n_tiles, 1, TILE_ASSIGN), jnp.int32),
            jax.ShapeDtypeStruct((n_tiles, 1, TILE_ASSIGN), jnp.int32),
            jax.ShapeDtypeStruct((TOPK, n_tiles * MOE_TILE), F32),
            jax.ShapeDtypeStruct((N_EXPERTS, 1), jnp.int32),
        ],
        scratch_shapes=[pltpu.VMEM((N_EXPERTS, 1), F32)],
        compiler_params=pltpu.CompilerParams(dimension_semantics=("arbitrary",)),
        name="moe_router",
    )(x_units, mod_units, nrm, wr, br, tri)


def _start_row_gather(idx_ref, n_rows, src_hbm, dst, sem):
    def body(r, carry):
        row = idx_ref[0, 0, r]
        pltpu.make_async_copy(src_hbm.at[pl.ds(row, 1)], dst.at[pl.ds(r, 1)], sem).start()
        return carry
    lax.fori_loop(0, n_rows, body, 0, unroll=8)


def _wait_row_gather(n_rows, src_hbm, dst, sem):
    pltpu.make_async_copy(src_hbm.at[pl.ds(0, n_rows)], dst, sem).wait()


def _dispatch_kernel(pend_ref, padded_ref, dest_ref, h_ref, xs_hbm, zbuf, sem):
    n_blocks = xs_hbm.shape[0] // EXPERT_BLOCK

    def zero_block(first_row):
        return pltpu.make_async_copy(
            zbuf, xs_hbm.at[pl.ds(pl.multiple_of(first_row, EXPERT_BLOCK), EXPERT_BLOCK)], sem.at[0])

    @pl.when(pl.program_id(0) == 0)
    def _():
        zbuf[...] = jnp.zeros_like(zbuf)
        n_used = pend_ref[N_EXPERTS - 1] // EXPERT_BLOCK
        for e in range(N_EXPERTS):
            @pl.when(padded_ref[e] > 0)
            def _():
                zero_block(pend_ref[e] - EXPERT_BLOCK).start()
        lax.fori_loop(n_used, n_blocks, lambda b, c: (zero_block(b * EXPERT_BLOCK).start(), c)[1], 0)
        for e in range(N_EXPERTS):
            @pl.when(padded_ref[e] > 0)
            def _():
                zero_block(pend_ref[e] - EXPERT_BLOCK).wait()
        lax.fori_loop(n_used, n_blocks, lambda b, c: (zero_block(b * EXPERT_BLOCK).wait(), c)[1], 0)

    def body(t, carry):
        for k in range(TOPK):
            slot = dest_ref[0, 0, k * MOE_TILE + t]
            pltpu.make_async_copy(h_ref.at[pl.ds(t, 1)], xs_hbm.at[pl.ds(slot, 1)], sem.at[1]).start()
        return carry
    lax.fori_loop(0, MOE_TILE, body, 0, unroll=8)
    for k in range(TOPK):
        pltpu.make_async_copy(h_ref, xs_hbm.at[pl.ds(0, MOE_TILE)], sem.at[1]).wait()


def _dispatch(pad_end, padded, dest_tiles, h, n_slots):
    n_tiles = dest_tiles.shape[0]
    grid_spec = pltpu.PrefetchScalarGridSpec(
        num_scalar_prefetch=2,
        grid=(n_tiles,),
        in_specs=[
            pl.BlockSpec((1, 1, TILE_ASSIGN), lambda i, pe, pd: (i, 0, 0), memory_space=pltpu.SMEM),
            pl.BlockSpec((MOE_TILE, D_MODEL), lambda i, pe, pd: (i, 0)),
        ],
        out_specs=pl.BlockSpec(memory_space=pl.ANY),
        scratch_shapes=[pltpu.VMEM((EXPERT_BLOCK, D_MODEL), F32), pltpu.SemaphoreType.DMA((2,))],
    )
    return pl.pallas_call(
        _dispatch_kernel,
        grid_spec=grid_spec,
        out_shape=jax.ShapeDtypeStruct((n_slots, D_MODEL), F32),
        compiler_params=pltpu.CompilerParams(dimension_semantics=("arbitrary",)),
        name="moe_dispatch",
    )(pad_end, padded, dest_tiles, h)


def _experts_kernel(be_ref, nused_ref, x_ref, wg_ref, wu_ref, wd_ref, o_ref):
    @pl.when(pl.program_id(0) < nused_ref[0])
    def _():
        xb = x_ref[...].astype(BF16)
        a = _silu(_dot(xb, wg_ref[0])) * _dot(xb, wu_ref[0])
        o_ref[...] = _dot(a.astype(BF16), wd_ref[0])

    @pl.when(pl.program_id(0) >= nused_ref[0])
    def _():
        o_ref[...] = jnp.zeros_like(o_ref)


def _experts(block_e, n_used, xs, wg, wu, wd):
    n_blocks = xs.shape[0] // EXPERT_BLOCK

    def row_block(i, be, nu):
        return (jnp.minimum(i, nu[0] - 1), 0)

    def expert_block(i, be, nu):
        return (be[jnp.minimum(i, nu[0] - 1)], 0, 0)

    grid_spec = pltpu.PrefetchScalarGridSpec(
        num_scalar_prefetch=2,
        grid=(n_blocks,),
        in_specs=[
            pl.BlockSpec((EXPERT_BLOCK, D_MODEL), row_block),
            pl.BlockSpec((1, D_MODEL, D_EXPERT), expert_block),
            pl.BlockSpec((1, D_MODEL, D_EXPERT), expert_block),
            pl.BlockSpec((1, D_EXPERT, D_MODEL), expert_block),
        ],
        out_specs=pl.BlockSpec((EXPERT_BLOCK, D_MODEL), lambda i, be, nu: (i, 0)),
    )
    return pl.pallas_call(
        _experts_kernel,
        grid_spec=grid_spec,
        out_shape=jax.ShapeDtypeStruct(xs.shape, F32),
        compiler_params=pltpu.CompilerParams(
            dimension_semantics=("arbitrary",), vmem_limit_bytes=VMEM_LIMIT),
        name="moe_experts",
    )(block_e, n_used, xs, wg, wu, wd)


def _combine_kernel(dst_cur_ref, dst_nxt_ref, x_ref, mod_ref, wts_ref, nrm_ref, y_hbm, o_ref, buf, sem,
                    *, final_norm):
    i = pl.program_id(0)
    n = pl.num_programs(0)
    slot = i % 2

    @pl.when(i == 0)
    def _():
        _start_row_gather(dst_cur_ref, TILE_ASSIGN, y_hbm, buf.at[0], sem.at[0])

    @pl.when(i + 1 < n)
    def _():
        _start_row_gather(dst_nxt_ref, TILE_ASSIGN, y_hbm, buf.at[1 - slot], sem.at[1 - slot])

    _wait_row_gather(TILE_ASSIGN, y_hbm, buf.at[slot], sem.at[slot])
    u, lt, _ = x_ref.shape
    y = (wts_ref[:, 0:1] * buf[slot, 0:MOE_TILE, :] + wts_ref[:, 1:2] * buf[slot, MOE_TILE:TILE_ASSIGN, :])
    g2 = mod_ref[:, 5:6, :]
    out = x_ref[...] + g2 * y.reshape(u, lt, D_MODEL)
    if final_norm:
        out = out * lax.rsqrt(jnp.mean(out * out, axis=-1, keepdims=True) + NORM_EPS)
        out = out * nrm_ref[...].reshape(1, 1, D_MODEL)
    o_ref[...] = out


def _combine(dest_tiles, x_units, mod_units, wts_col, nrm, y_slots, *, final_norm):
    n_units, lt, _ = x_units.shape
    u = MOE_TILE // lt
    n_tiles = n_units // u
    return pl.pallas_call(
        functools.partial(_combine_kernel, final_norm=final_norm),
        grid=(n_tiles,),
        in_specs=[
            pl.BlockSpec((1, 1, TILE_ASSIGN), lambda i: (i, 0, 0), memory_space=pltpu.SMEM),
            pl.BlockSpec((1, 1, TILE_ASSIGN), lambda i: (jnp.minimum(i + 1, n_tiles - 1), 0, 0),
                         memory_space=pltpu.SMEM),
            pl.BlockSpec((u, lt, D_MODEL), lambda i: (i, 0, 0)),
            pl.BlockSpec((u, 6, D_MODEL), lambda i: (i, 0, 0)),
            pl.BlockSpec((MOE_TILE, TOPK), lambda i: (i, 0)),
            pl.BlockSpec((1, D_MODEL), lambda i: (0, 0)),
            pl.BlockSpec(memory_space=pl.ANY),
        ],
        out_specs=pl.BlockSpec((u, lt, D_MODEL), lambda i: (i, 0, 0)),
        out_shape=jax.ShapeDtypeStruct(x_units.shape, F32),
        scratch_shapes=[pltpu.VMEM((2, TILE_ASSIGN, D_MODEL), F32), pltpu.SemaphoreType.DMA((2,))],
        compiler_params=pltpu.CompilerParams(
            dimension_semantics=("arbitrary",), vmem_limit_bytes=VMEM_LIMIT),
        name="moe_combine",
    )(dest_tiles, dest_tiles, x_units, mod_units, wts_col, nrm, y_slots)


def _routing_tables(eid_tiles, rank_tiles, counts):
    n_blocks = eid_tiles.size // EXPERT_BLOCK + N_EXPERTS
    padded = (counts + EXPERT_BLOCK - 1) // EXPERT_BLOCK * EXPERT_BLOCK
    pad_end = jnp.cumsum(padded).astype(jnp.int32)
    pad_start = pad_end - padded
    block_start = jnp.arange(n_blocks, dtype=jnp.int32)[:, None] * EXPERT_BLOCK
    block_e = jnp.minimum(jnp.sum((block_start >= pad_end[None, :]).astype(jnp.int32), axis=1),
                          N_EXPERTS - 1).astype(jnp.int32)
    n_used = pad_end[-1:] // EXPERT_BLOCK
    experts = jnp.arange(N_EXPERTS, dtype=jnp.int32)
    first_slot = jnp.sum(jnp.where(eid_tiles[..., None] == experts, pad_start, 0), axis=-1)
    return block_e, n_used, pad_end, padded, first_slot + rank_tiles


def _moe_layer(x, mod_l, nrm_ffn, wr, br, tri, wg, wu, wd, nrm_final, *, final_norm):
    bsz, seq, _ = x.shape
    t = bsz * seq
    lt = min(seq, MOE_TILE)
    per = seq // lt
    x_units = x.reshape(t // lt, lt, D_MODEL)
    mod_units = jnp.repeat(mod_l, per, axis=0) if per > 1 else mod_l
    h, eid_tiles, rank_tiles, wts, counts = _router(x_units, mod_units, nrm_ffn, wr, br, tri)
    block_e, n_used, pad_end, padded, dest_tiles = _routing_tables(eid_tiles, rank_tiles, counts[:, 0])
    n_slots = block_e.shape[0] * EXPERT_BLOCK
    xs = _dispatch(pad_end, padded, dest_tiles, h, n_slots)
    y_slots = _experts(block_e, n_used, xs, wg, wu, wd)
    out = _combine(dest_tiles, x_units, mod_units, wts.T, nrm_final, y_slots, final_norm=final_norm)
    return out.reshape(bsz, seq, D_MODEL)


def kernel(x_prompt, x_sample, c_prompt, c_sample, state_hgrn, state_gla, w_ada, b_ada, norm_mix,
           norm_ffn, w_in, hg_lb, hg_onorm, w_gk2, b_gk, gla_onorm, w_br_a, w_br_b, w_out, w_rg, b_rg,
           w_re, b_re, w_e_gate, w_e_up, w_e_down, norm_final):
    bp = x_prompt.shape[0]
    bs = x_sample.shape[0]
    mod = _ada_mod(jnp.concatenate([c_prompt, c_sample], axis=0), w_ada, b_ada)
    mod = mod.reshape(DEPTH, bp + bs, 6, D_MODEL)

    glr0 = C_GOG + GLA_W
    win_r = jnp.concatenate(
        [w_in[:, :, :glr0], w_in[:, :, glr0 + GLA_GATE_RANK:], w_in[:, :, glr0:glr0 + GLA_GATE_RANK],
         jnp.zeros((DEPTH, D_MODEL, LANES - GLA_GATE_RANK), F32)], axis=2).astype(BF16)
    wgk2_p = jnp.concatenate(
        [w_gk2, jnp.zeros((DEPTH, LANES - GLA_GATE_RANK, GLA_KW), F32)], axis=1).astype(BF16)
    wa_b = w_br_a.astype(BF16)
    wb_b = w_br_b.astype(BF16)
    wo_b = w_out.astype(BF16)
    plan_consts = [jnp.asarray(np.tril(np.ones((CHUNK, CHUNK), np.float32)), BF16),
                   jnp.asarray(_slab_mask(), F32),
                   jnp.asarray(SAFE_PLAN.segment_sum_matrix(), BF16), jnp.asarray(SAFE_PLAN.masks(), F32)]
    zpad = jnp.zeros((DEPTH, 8 - N_GROUPS, D_MODEL), F32)
    wr = jnp.concatenate([jnp.swapaxes(w_rg, 1, 2), zpad, jnp.swapaxes(w_re, 1, 2)], axis=1)
    br = jnp.concatenate([b_rg, jnp.zeros((DEPTH, 8 - N_GROUPS), F32), b_re], axis=1)[:, :, None]
    wg_b = w_e_gate.astype(BF16)
    wu_b = w_e_up.astype(BF16)
    wd_b = w_e_down.astype(BF16)
    nrm_f = norm_final.reshape(1, D_MODEL)
    assign = np.arange(TILE_ASSIGN)
    tri = jnp.asarray(assign[:, None] < assign[None, :], BF16)

    def run(x, mod_g, shg, sgla, tb):
        bsz = x.shape[0]
        new_hg, new_gla = [], []
        for l in range(DEPTH):
            x, s1, s2 = _mixer(
                x, mod_g[l], norm_mix[l:l + 1], win_r[l], hg_lb, wgk2_p[l], b_gk[l:l + 1],
                hg_onorm[l:l + 1], gla_onorm[l:l + 1], wa_b[l], wb_b[l], wo_b[l],
                shg[l], sgla[l].reshape(bsz, GLA_KW // LANES, LANES, HEAD_DV), plan_consts,
                layer=l, tb=tb)
            new_hg.append(s1)
            new_gla.append(s2.reshape(bsz, GLA_HEADS, GLA_DK, HEAD_DV))
            x = _moe_layer(x, mod_g[l], norm_ffn[l:l + 1], wr[l], br[l], tri, wg_b[l], wu_b[l], wd_b[l],
                           nrm_f, final_norm=(l == DEPTH - 1))
        return x, jnp.stack(new_hg), jnp.stack(new_gla)

    zeros_hg = jnp.zeros((DEPTH, bp, HG_HEADS, HG_DK, HEAD_DV), F32)
    zeros_gla = jnp.zeros((DEPTH, bp, GLA_HEADS, GLA_DK, HEAD_DV), F32)
    y_p, hg_p, gla_p = run(x_prompt, mod[:, :bp], zeros_hg, zeros_gla, 256)
    y_s, hg_s, gla_s = run(x_sample, mod[:, bp:], state_hgrn, state_gla, CHUNK)
    return (y_p, y_s, hg_p, gla_p, hg_s, gla_s)
```

```python
import functools

import numpy as np
import jax
import jax.numpy as jnp
from jax import lax
from jax.experimental import pallas as pl
from jax.experimental.pallas import tpu as pltpu

F32 = jnp.float32
BF16 = jnp.bfloat16

D_MODEL = 1024
DEPTH = 2
CHUNK = 64
NORM_EPS = 1e-6
LOG_FLOOR = 1e-30
HG_HEADS = 4
HG_DK = 128
HEAD_DV = 128
HG_KW = HG_HEADS * HG_DK
HG_W = HG_HEADS * HEAD_DV
GLA_HEADS = 4
GLA_DK = 64
GLA_KW = GLA_HEADS * GLA_DK
GLA_W = GLA_HEADS * HEAD_DV
GLA_GATE_RANK = 16
GLA_GATE_NORM = 16.0
N_GROUPS = 4
EXPERTS_PER_GROUP = 8
N_EXPERTS = N_GROUPS * EXPERTS_PER_GROUP
TOPK = 2
D_EXPERT = 512

LANES = 128
VMEM_LIMIT = 56 * 1024 * 1024

C_HQ = 0
C_HF = C_HQ + HG_KW
C_HI = C_HF + HG_KW
C_HOG = C_HI + HG_W
C_GQ = C_HOG + HG_W
C_GK = C_GQ + GLA_KW
C_GV = C_GK + GLA_KW
C_GOG = C_GV + GLA_W
C_GA = C_GOG + GLA_W
C_GB = C_GA + D_MODEL
C_GLR = C_GB + D_MODEL
IN_COLS_PAD = C_GLR + LANES
PROJ_TILE = 640
assert IN_COLS_PAD % PROJ_TILE == 0

class _ScorePlan:
    def __init__(self, levels, adjacent, diag_block):
        self.levels = levels
        self.adjacent = adjacent
        self.diag_block = diag_block
        self.cum_rows = (2 * len(levels) + 1) * CHUNK
        self.n_masks = len(levels) + 1 + int(adjacent)

    def segment_sum_matrix(self):
        t = np.arange(CHUNK)[:, None]
        r = np.arange(CHUNK)[None, :]
        rows = []
        for m in self.levels:
            same = (t // m) == (r // m)
            rows.append(same & (r <= t))
            rows.append(same & (r > t))
        rows.append(r <= t)
        return np.concatenate(rows, axis=0).astype(np.float32)

    def masks(self):
        t = np.arange(CHUNK)[:, None]
        s = np.arange(CHUNK)[None, :]
        masks = [((t // self.diag_block) == (s // self.diag_block)) & (s <= t)]
        for m in self.levels + ((1,) if self.adjacent else ()):
            masks.append(((t // (2 * m)) == (s // (2 * m))) & ((t // m) % 2 == 1) & ((s // m) % 2 == 0))
        return np.stack(masks).astype(np.float32)


SAFE_PLAN = _ScorePlan((32, 16, 8, 4, 2), True, 1)


def _dot(a, b):
    return jnp.dot(a, b, preferred_element_type=F32)


def _dot_nt(a, b):
    return lax.dot_general(a, b, (((1,), (1,)), ((), ())), preferred_element_type=F32)


def _sigmoid(x):
    return 1.0 / (1.0 + jnp.exp(-x))


def _silu(x):
    return x * _sigmoid(x)


def _rms_mod(x, gain, scale, shift):
    y = x * lax.rsqrt(jnp.mean(x * x, axis=-1, keepdims=True) + NORM_EPS)
    return y * gain * (1.0 + scale) + shift


def _ada_kernel(c_ref, w_ref, b_ref, o_ref):
    c = c_ref[...]
    o_ref[0] = jnp.dot(_silu(c), w_ref[0], preferred_element_type=F32,
                       precision=lax.Precision.HIGHEST) + b_ref[0]


def _ada_mod(c_all, w_ada, b_ada):
    nb = c_all.shape[0]
    tn = 512
    return pl.pallas_call(
        _ada_kernel,
        grid=(DEPTH, 6 * D_MODEL // tn),
        in_specs=[
            pl.BlockSpec((nb, D_MODEL), lambda l, j: (0, 0)),
            pl.BlockSpec((1, D_MODEL, tn), lambda l, j: (l, 0, j)),
            pl.BlockSpec((1, 1, tn), lambda l, j: (l, 0, j)),
        ],
        out_specs=pl.BlockSpec((1, nb, tn), lambda l, j: (l, 0, j)),
        out_shape=jax.ShapeDtypeStruct((DEPTH, nb, 6 * D_MODEL), F32),
        name="ada_mod",
    )(c_all, w_ada, b_ada.reshape(DEPTH, 1, 6 * D_MODEL))


def _chunk_attention(q, k, v, g, states, mall_ref, mask_ref, heads_per_tile, plan):
    w = q.shape[1]
    n_tiles = w // LANES
    g_hi = g.astype(BF16)
    r1 = g - g_hi.astype(F32)
    g_mid = r1.astype(BF16)
    g_lo = (r1 - g_mid.astype(F32)).astype(BF16)
    mall = mall_ref[...]
    cums = _dot(mall, g_hi) + _dot(mall, g_mid) + _dot(mall, g_lo)
    b = cums[plan.cum_rows - CHUNK:plan.cum_rows]
    level_q = []
    level_k = []
    for i in range(len(plan.levels)):
        level_q.append(q * jnp.exp(cums[2 * i * CHUNK:(2 * i + 1) * CHUNK]))
        level_k.append(k * jnp.exp(cums[(2 * i + 1) * CHUNK:(2 * i + 2) * CHUNK]))
    if plan.diag_block == 1:
        qs = [q]
        ks = [k]
    else:
        i = plan.levels.index(plan.diag_block)
        qs = [level_q[i]]
        ks = [k * jnp.exp(-cums[2 * i * CHUNK:(2 * i + 1) * CHUNK])]
    qs += level_q
    ks += level_k
    if plan.adjacent:
        qs.append(q * jnp.exp(g))
        ks.append(k)
    b_last = b[CHUNK - 1:CHUNK]
    q_in = q * jnp.exp(b)
    k_out = k * jnp.exp(b_last - b)
    e_last = jnp.exp(b_last)

    dk = LANES // heads_per_tile
    lane = lax.broadcasted_iota(jnp.int32, (CHUNK, LANES), 1)
    row = lax.broadcasted_iota(jnp.int32, (LANES, HEAD_DV), 0)
    outs = []
    new_states = []
    for ti in range(n_tiles):
        sl = slice(ti * LANES, (ti + 1) * LANES)
        ks_t =[kk[:, sl].astype(BF16) for kk in ks]
        k_out_t = k_out[:, sl].T.astype(BF16)
        e_col = jnp.broadcast_to(e_last[:, sl], (LANES, LANES)).T
        s_old = states[ti]
        s_old_b = s_old.astype(BF16)
        upd = None
        for j in range(heads_per_tile):
            head = ti * heads_per_tile + j
            if heads_per_tile == 1:
                sel = lambda a: a
            else:
                in_head = (lane // dk) == j
                sel = lambda a, in_head=in_head: jnp.where(in_head, a, 0.0)
            sc = jnp.zeros((CHUNK, CHUNK), F32)
            for i in range(plan.n_masks):
                sc = sc + _dot_nt(sel(qs[i][:, sl]).astype(BF16), ks_t[i]) * mask_ref[i]
            vh = v[:, head * HEAD_DV:(head + 1) * HEAD_DV].astype(BF16)
            o = _dot(sc.astype(BF16), vh) + _dot(sel(q_in[:, sl]).astype(BF16), s_old_b)
            outs.append(o)
            u = _dot(k_out_t, vh)
            upd = u if upd is None else jnp.where((row // dk) == j, u, upd)
        new_states.append(e_col * s_old + upd)
    return jnp.concatenate(outs, axis=1), new_states


FAST_BLOCK = 16
N_SUB = CHUNK // FAST_BLOCK
SLAB_ROWS = FAST_BLOCK * (N_SUB * (N_SUB - 1) // 2) + CHUNK
FAST_BLOCK_DECAY_LIMIT = 60.0


def _slab_mask():
    t = np.arange(CHUNK)[:, None]
    cols = []
    for i in range(1, N_SUB):
        cols.append(np.broadcast_to(t // FAST_BLOCK == i, (CHUNK, i * FAST_BLOCK)))
    s = np.arange(CHUNK)[None, :]
    cols.append((t // FAST_BLOCK == s // FAST_BLOCK) & (s <= t))
    return np.concatenate(cols, axis=1).astype(np.float32)


def _block_attention_fast(q, k, v, g, states, tril_ref, slab_mask_ref, heads_per_tile):
    rows, w = q.shape
    n_chunks = rows // CHUNK
    n_tiles = w // LANES
    dk = LANES // heads_per_tile
    n_heads = n_tiles * heads_per_tile

    g_hi = g.astype(BF16)
    r1 = g - g_hi.astype(F32)
    g_mid = r1.astype(BF16)
    g_lo = (r1 - g_mid.astype(F32)).astype(BF16)
    tril = tril_ref[...]
    b = _dot(tril, g_hi) + _dot(tril, g_mid) + _dot(tril, g_lo)

    def end_row(c, i):
        r = c * CHUNK + (i + 1) * FAST_BLOCK
        return b[r - 1:r]

    def per_block(row_of):
        return jnp.concatenate([jnp.broadcast_to(row_of(c, i), (FAST_BLOCK, w))
                                for c in range(n_chunks) for i in range(N_SUB)], axis=0)

    zero = jnp.zeros((1, w), F32)
    b_start = per_block(lambda c, i: zero if i == 0 else end_row(c, i - 1))
    b_end = per_block(end_row)
    q_blk = q * jnp.exp(b - b_start)
    k_diag = k * jnp.exp(b_start - b)
    k_end = k * jnp.exp(b_end - b)
    q_in = q_blk * jnp.exp(b_start)
    k_out = k_end * jnp.exp(per_block(lambda c, i: end_row(c, N_SUB - 1)) - b_end)

    lane = lax.broadcasted_iota(jnp.int32, (CHUNK, LANES), 1)
    row = lax.broadcasted_iota(jnp.int32, (LANES, HEAD_DV), 0)
    slab_mask = slab_mask_ref[...]

    def sel(a, j):
        return a if heads_per_tile == 1 else jnp.where((lane // dk) == j, a, 0.0)

    v_b = v.astype(BF16)

    scores = {}
    for c in range(n_chunks):
        r0 = c * CHUNK
        slabs = []
        for i in range(1, N_SUB):
            for jb in range(i):
                blk = k_end[r0 + jb * FAST_BLOCK:r0 + (jb + 1) * FAST_BLOCK]
                slabs.append(blk if jb == i - 1 else blk * jnp.exp(end_row(c, i - 1) - end_row(c, jb)))
        slabs.append(k_diag[r0:r0 + CHUNK])
        k_slab = jnp.concatenate(slabs, axis=0).astype(BF16)
        for ti in range(n_tiles):
            sl = slice(ti * LANES, (ti + 1) * LANES)
            for j in range(heads_per_tile):
                qh = sel(q_blk[r0:r0 + CHUNK, sl], j).astype(BF16)
                scores[c, ti * heads_per_tile + j] = (_dot_nt(qh, k_slab[:, sl]) * slab_mask).astype(BF16)

    entering = [list(states)]
    for c in range(n_chunks):
        r0 = c * CHUNK
        nxt = []
        for ti in range(n_tiles):
            sl = slice(ti * LANES, (ti + 1) * LANES)
            k_out_t = k_out[r0:r0 + CHUNK, sl].T.astype(BF16)
            upd = None
            for j in range(heads_per_tile):
                head = ti * heads_per_tile + j
                u = _dot(k_out_t, v_b[r0:r0 + CHUNK, head * HEAD_DV:(head + 1) * HEAD_DV])
                upd = u if upd is None else jnp.where((row // dk) == j, u, upd)
            e_col = jnp.broadcast_to(jnp.exp(end_row(c, N_SUB - 1)[:, sl]), (LANES, LANES)).T
            nxt.append(e_col * entering[c][ti] + upd)
        entering.append(nxt)

    out_rows = []
    for c in range(n_chunks):
        r0 = c * CHUNK
        outs = []
        for head in range(n_heads):
            ti, j = divmod(head, heads_per_tile)
            sl = slice(ti * LANES, (ti + 1) * LANES)
            vh = v_b[r0:r0 + CHUNK, head * HEAD_DV:(head + 1) * HEAD_DV]
            v_slab = jnp.concatenate([vh[:i * FAST_BLOCK] for i in range(1, N_SUB)] + [vh], axis=0)
            outs.append(_dot(scores[c, head], v_slab)
                        + _dot(sel(q_in[r0:r0 + CHUNK, sl], j).astype(BF16), entering[c][ti].astype(BF16)))
        out_rows.append(jnp.concatenate(outs, axis=1))
    return jnp.concatenate(out_rows, axis=0), entering[n_chunks]


def _head_norm_gate(o, gain, gate):
    outs = []
    for h in range(o.shape[1] // HEAD_DV):
        sl = slice(h * HEAD_DV, (h + 1) * HEAD_DV)
        oh = o[:, sl]
        oh = oh * lax.rsqrt(jnp.mean(oh * oh, axis=-1, keepdims=True) + NORM_EPS) * gain
        outs.append(oh * _silu(gate[:, sl]))
    return jnp.concatenate(outs, axis=1)


def _mixer_kernel(x_ref, mod_ref, nrm_ref, win_ref, lb_ref, wgk2_ref, bgk_ref, hgn_ref, glan_ref,
                  wa_ref, wb_ref, wo_ref, shg0_ref, sgla0_ref,
                  tril_ref, slab_mask_ref, mall_safe_ref, mask_safe_ref,
                  xo_ref, shg_o_ref, sgla_o_ref,
                  p_scr, k_scr, lg_scr, shg_scr, sgla_scr, *, layer, tb):
    j = pl.program_id(1)

    @pl.when(j == 0)
    def _():
        shg_scr[...] = shg0_ref[0]
        sgla_scr[...] = sgla0_ref[0]

    x = x_ref[0]
    sh1 = mod_ref[0, 0:1, :]
    sc1 = mod_ref[0, 1:2, :]
    g1 = mod_ref[0, 2:3, :]
    hb = _rms_mod(x, nrm_ref[...], sc1, sh1).astype(BF16)
    for c in range(0, IN_COLS_PAD, PROJ_TILE):
        p_scr[:, c:c + PROJ_TILE] = _dot(hb, win_ref[:, c:c + PROJ_TILE])

    lb_all = lb_ref[...]
    lb_max = jnp.max(lb_all, axis=0, keepdims=True)
    lb_exp = jnp.exp(lb_all - lb_max)
    sm = lb_exp / jnp.sum(lb_exp, axis=0, keepdims=True)
    lbl = jnp.clip(jnp.sum(sm[0:layer + 1], axis=0, keepdims=True) - sm[0:1], 0.0, 1.0)

    p_scr[:, C_HQ:C_HQ + HG_KW] = _silu(p_scr[:, C_HQ:C_HQ + HG_KW]) * (HG_DK ** -0.5)
    z = p_scr[:, C_HF:C_HF + HG_KW]
    f = lbl + (1.0 - lbl) * _sigmoid(z)
    p_scr[:, C_HF:C_HF + HG_KW] = jnp.log(jnp.maximum(f, LOG_FLOOR))
    k_scr[...] = (1.0 - lbl) * _sigmoid(-z)
    glr = p_scr[:, C_GLR:C_GLR + LANES].astype(BF16)
    gate = _dot(glr, wgk2_ref[...]) + bgk_ref[...]
    lg_scr[...] = (jnp.minimum(gate, 0.0) - jnp.log1p(jnp.exp(-jnp.abs(gate)))) * (1.0 / GLA_GATE_NORM)
    p_scr[:, C_GQ:C_GQ + GLA_KW] = p_scr[:, C_GQ:C_GQ + GLA_KW] * (GLA_DK ** -0.5)

    n_hg_tiles = HG_KW // LANES
    n_gla_tiles = GLA_KW // LANES

    def one_chunk(rows, states, attend):
        o_hg, st_hg = attend(p_scr[rows, C_HQ:C_HQ + HG_KW], k_scr[rows, :],
                             p_scr[rows, C_HI:C_HI + HG_W], p_scr[rows, C_HF:C_HF + HG_KW],
                             states[:n_hg_tiles], 1)
        p_scr[rows, C_HI:C_HI + HG_W] = o_hg
        o_gla, st_gla = attend(p_scr[rows, C_GQ:C_GQ + GLA_KW], p_scr[rows, C_GK:C_GK + GLA_KW],
                               p_scr[rows, C_GV:C_GV + GLA_W], lg_scr[rows, :],
                               states[n_hg_tiles:], 2)
        p_scr[rows, C_GV:C_GV + GLA_W] = o_gla
        return st_hg + st_gla

    def attend_safe(q, k, v, g, states, heads_per_tile):
        return _chunk_attention(q, k, v, g, states, mall_safe_ref, mask_safe_ref, heads_per_tile, SAFE_PLAN)

    def load_states():
        return [shg_scr[t] for t in range(n_hg_tiles)] + [sgla_scr[t] for t in range(n_gla_tiles)]

    def store_states(states):
        for t in range(n_hg_tiles):
            shg_scr[t] = states[t]
        for t in range(n_gla_tiles):
            sgla_scr[t] = states[n_hg_tiles + t]

    def run_block_fast():
        states = load_states()
        o_hg, st_hg = _block_attention_fast(
            p_scr[:, C_HQ:C_HQ + HG_KW], k_scr[...], p_scr[:, C_HI:C_HI + HG_W],
            p_scr[:, C_HF:C_HF + HG_KW], states[:n_hg_tiles], tril_ref, slab_mask_ref, 1)
        p_scr[:, C_HI:C_HI + HG_W] = o_hg
        o_gla, st_gla = _block_attention_fast(
            p_scr[:, C_GQ:C_GQ + GLA_KW], p_scr[:, C_GK:C_GK + GLA_KW], p_scr[:, C_GV:C_GV + GLA_W],
            lg_scr[...], states[n_hg_tiles:], tril_ref, slab_mask_ref, 2)
        p_scr[:, C_GV:C_GV + GLA_W] = o_gla
        store_states(st_hg + st_gla)

    def run_chunks_safe():
        def chunk_body(ci, carry):
            rows = pl.ds(pl.multiple_of(ci * CHUNK, CHUNK), CHUNK)
            store_states(one_chunk(rows, load_states(), attend_safe))
            return carry

        lax.fori_loop(0, tb // CHUNK, chunk_body, 0)

    blk = FAST_BLOCK
    min_hg = jnp.min(jnp.sum(p_scr[:, C_HF:C_HF + HG_KW].reshape(tb // blk, blk, HG_KW), axis=1))
    min_gla = jnp.min(jnp.sum(lg_scr[...].reshape(tb // blk, blk, GLA_KW), axis=1))
    bounded = jnp.minimum(min_hg, min_gla) >= -FAST_BLOCK_DECAY_LIMIT

    @pl.when(bounded)
    def _():
        run_block_fast()

    @pl.when(jnp.logical_not(bounded))
    def _():
        run_chunks_safe()

    o_hg = _head_norm_gate(p_scr[:, C_HI:C_HI + HG_W], hgn_ref[...], p_scr[:, C_HOG:C_HOG + HG_W])
    o_gla = _head_norm_gate(p_scr[:, C_GV:C_GV + GLA_W], glan_ref[...], p_scr[:, C_GOG:C_GOG + GLA_W])
    ya = _dot(o_hg.astype(BF16), wa_ref[...])
    yb = _dot(o_gla.astype(BF16), wb_ref[...])
    merged = (_sigmoid(p_scr[:, C_GA:C_GA + D_MODEL]) * ya
              + _sigmoid(p_scr[:, C_GB:C_GB + D_MODEL]) * yb)
    m = _dot(merged.astype(BF16), wo_ref[...])
    xo_ref[0] = x + g1 * m

    @pl.when(j == pl.num_programs(1) - 1)
    def _():
        shg_o_ref[0] = shg_scr[...]
        sgla_o_ref[0] = sgla_scr[...]


def _const_spec(shape):
    nd = len(shape)
    return pl.BlockSpec(shape, lambda b, j, nd=nd: (0,) * nd, pipeline_mode=pl.Buffered(1))


def _mixer(x, mod, nrm, win, hg_lb, wgk2, bgk, hgn, glan, wa, wb, wo, shg0, sgla0, plan_consts,
           *, layer, tb):
    bsz, seq, _ = x.shape
    kern = functools.partial(_mixer_kernel, layer=layer, tb=tb)
    n_gla_tiles = GLA_KW // LANES
    return pl.pallas_call(
        kern,
        grid=(bsz, seq // tb),
        in_specs=[
            pl.BlockSpec((1, tb, D_MODEL), lambda b, j: (b, j, 0)),
            pl.BlockSpec((1, 6, D_MODEL), lambda b, j: (b, 0, 0)),
            _const_spec((1, D_MODEL)),
            _const_spec((D_MODEL, IN_COLS_PAD)),
            _const_spec((DEPTH, HG_KW)),
            _const_spec((LANES, GLA_KW)),
            _const_spec((1, GLA_KW)),
            _const_spec((1, HEAD_DV)),
            _const_spec((1, HEAD_DV)),
            _const_spec((HG_W, D_MODEL)),
            _const_spec((GLA_W, D_MODEL)),
            _const_spec((D_MODEL, D_MODEL)),
            pl.BlockSpec((1, HG_HEADS, HG_DK, HEAD_DV), lambda b, j: (b, 0, 0, 0)),
            pl.BlockSpec((1, n_gla_tiles, LANES, HEAD_DV), lambda b, j: (b, 0, 0, 0)),
            _const_spec((tb, tb)),
            _const_spec((CHUNK, SLAB_ROWS)),
            _const_spec((SAFE_PLAN.cum_rows, CHUNK)),
            _const_spec((SAFE_PLAN.n_masks, CHUNK, CHUNK)),
        ],
        out_specs=[
            pl.BlockSpec((1, tb, D_MODEL), lambda b, j: (b, j, 0)),
            pl.BlockSpec((1, HG_HEADS, HG_DK, HEAD_DV), lambda b, j: (b, 0, 0, 0)),
            pl.BlockSpec((1, n_gla_tiles, LANES, HEAD_DV), lambda b, j: (b, 0, 0, 0)),
        ],
        out_shape=[
            jax.ShapeDtypeStruct((bsz, seq, D_MODEL), F32),
            jax.ShapeDtypeStruct((bsz, HG_HEADS, HG_DK, HEAD_DV), F32),
            jax.ShapeDtypeStruct((bsz, n_gla_tiles, LANES, HEAD_DV), F32),
        ],
        scratch_shapes=[
            pltpu.VMEM((tb, IN_COLS_PAD), F32),
            pltpu.VMEM((tb, HG_KW), F32),
            pltpu.VMEM((tb, GLA_KW), F32),
            pltpu.VMEM((HG_HEADS, HG_DK, HEAD_DV), F32),
            pltpu.VMEM((n_gla_tiles, LANES, HEAD_DV), F32),
        ],
        compiler_params=pltpu.CompilerParams(
            dimension_semantics=("arbitrary", "arbitrary"), vmem_limit_bytes=VMEM_LIMIT),
        name=f"mixer_l{layer}",
    )(x, mod, nrm, win, hg_lb, wgk2, bgk, hgn, glan, wa, wb, wo, shg0, sgla0, *plan_consts)


ROUTER_ROWS = 8 + N_EXPERTS
MOE_TILE = 512
TILE_ASSIGN = TOPK * MOE_TILE
EXPERT_BLOCK = 512


def _first_argmax_rows(vals, n):
    ridx = lax.broadcasted_iota(jnp.int32, vals.shape, 0)
    vmax = jnp.max(vals, axis=0, keepdims=True)
    imax = jnp.min(jnp.where(vals == vmax, ridx, n), axis=0, keepdims=True)
    return vmax, imax


def _router_kernel(x_ref, mod_ref, nrm_ref, wr_ref, br_ref, tri_ref,
                   h_ref, eid_ref, rank_ref, wts_ref, cnt_ref, run_scr):
    @pl.when(pl.program_id(0) == 0)
    def _():
        run_scr[...] = jnp.zeros_like(run_scr)

    u, lt, _ = x_ref.shape
    x = x_ref[...]
    sh2 = mod_ref[:, 3:4, :]
    sc2 = mod_ref[:, 4:5, :]
    h = _rms_mod(x, nrm_ref[...].reshape(1, 1, D_MODEL), sc2, sh2).reshape(u * lt, D_MODEL)
    h_ref[...] = h
    logits = lax.dot_general(wr_ref[...], h, (((1,), (1,)), ((), ())), preferred_element_type=F32,
                             precision=lax.Precision.HIGHEST) + br_ref[...]
    gl = logits[0:N_GROUPS]
    gmax, gi = _first_argmax_rows(gl, N_GROUPS)
    gp = 1.0 / jnp.sum(jnp.exp(gl - gmax), axis=0, keepdims=True)
    le = logits[8:8 + EXPERTS_PER_GROUP]
    for g in range(1, N_GROUPS):
        le = jnp.where(gi == g, logits[8 + g * EXPERTS_PER_GROUP:8 + (g + 1) * EXPERTS_PER_GROUP], le)
    pe = jnp.exp(le - jnp.max(le, axis=0, keepdims=True))
    pe = pe / jnp.sum(pe, axis=0, keepdims=True)
    v1, i1 = _first_argmax_rows(pe, EXPERTS_PER_GROUP)
    ridx = lax.broadcasted_iota(jnp.int32, pe.shape, 0)
    v2, i2 = _first_argmax_rows(jnp.where(ridx == i1, -1.0, pe), EXPERTS_PER_GROUP)
    vsum = v1 + v2
    wts_ref[0:1, :] = gp * v1 / vsum
    wts_ref[1:2, :] = gp * v2 / vsum
    eflat = jnp.concatenate([gi * EXPERTS_PER_GROUP + i1, gi * EXPERTS_PER_GROUP + i2], axis=1)
    eid_ref[0] = eflat
    onehot = (eflat == lax.broadcasted_iota(jnp.int32, (N_EXPERTS, TILE_ASSIGN), 0)).astype(F32)
    before = _dot(onehot.astype(BF16), tri_ref[...]) + run_scr[...]
    rank_ref[0] = jnp.sum(onehot * before, axis=0, keepdims=True).astype(jnp.int32)
    run_scr[...] = run_scr[...] + jnp.sum(onehot, axis=1, keepdims=True)
    cnt_ref[...] = run_scr[...].astype(jnp.int32)


def _router(x_units, mod_units, nrm, wr, br, tri):
    n_units, lt, _ = x_units.shape
    u = MOE_TILE // lt
    n_tiles = n_units // u
    return pl.pallas_call(
        _router_kernel,
        grid=(n_tiles,),
        in_specs=[
            pl.BlockSpec((u, lt, D_MODEL), lambda i: (i, 0, 0)),
            pl.BlockSpec((u, 6, D_MODEL), lambda i: (i, 0, 0)),
            pl.BlockSpec((1, D_MODEL), lambda i: (0, 0)),
            pl.BlockSpec((ROUTER_ROWS, D_MODEL), lambda i: (0, 0)),
            pl.BlockSpec((ROUTER_ROWS, 1), lambda i: (0, 0)),
            pl.BlockSpec((TILE_ASSIGN, TILE_ASSIGN), lambda i: (0, 0)),
        ],
        out_specs=[
            pl.BlockSpec((MOE_TILE, D_MODEL), lambda i: (i, 0)),
            pl.BlockSpec((1, 1, TILE_ASSIGN), lambda i: (i, 0, 0)),
            pl.BlockSpec((1, 1, TILE_ASSIGN), lambda i: (i, 0, 0)),
            pl.BlockSpec((TOPK, MOE_TILE), lambda i: (0, i)),
            pl.BlockSpec((N_EXPERTS, 1), lambda i: (0, 0)),
        ],
        out_shape=[
            jax.ShapeDtypeStruct((n_tiles * MOE_TILE, D_MODEL), F32),
            jax.ShapeDtypeStruct((n_tiles, 1, TILE_ASSIGN), jnp.int32),
            jax.ShapeDtypeStruct((n_tiles, 1, TILE_ASSIGN), jnp.int32),
            jax.ShapeDtypeStruct((TOPK, n_tiles * MOE_TILE), F32),
            jax.ShapeDtypeStruct((N_EXPERTS, 1), jnp.int32),
        ],
        scratch_shapes=[pltpu.VMEM((N_EXPERTS, 1), F32)],
        compiler_params=pltpu.CompilerParams(dimension_semantics=("arbitrary",)),
        name="moe_router",
    )(x_units, mod_units, nrm, wr, br, tri)


def _start_row_gather(idx_ref, n_rows, src_hbm, dst, sem):
    def body(r, carry):
        row = idx_ref[0, 0, r]
        pltpu.make_async_copy(src_hbm.at[pl.ds(row, 1)], dst.at[pl.ds(r, 1)], sem).start()
        return carry
    lax.fori_loop(0, n_rows, body, 0, unroll=8)


def _wait_row_gather(n_rows, src_hbm, dst, sem):
    pltpu.make_async_copy(src_hbm.at[pl.ds(0, n_rows)], dst, sem).wait()


def _dispatch_kernel(pend_ref, padded_ref, dest_ref, h_ref, xs_hbm, zbuf, sem):
    n_blocks = xs_hbm.shape[0] // EXPERT_BLOCK

    def zero_block(first_row):
        return pltpu.make_async_copy(
            zbuf, xs_hbm.at[pl.ds(pl.multiple_of(first_row, EXPERT_BLOCK), EXPERT_BLOCK)], sem.at[0])

    @pl.when(pl.program_id(0) == 0)
    def _():
        zbuf[...] = jnp.zeros_like(zbuf)
        n_used = pend_ref[N_EXPERTS - 1] // EXPERT_BLOCK
        for e in range(N_EXPERTS):
            @pl.when(padded_ref[e] > 0)
            def _():
                zero_block(pend_ref[e] - EXPERT_BLOCK).start()
        lax.fori_loop(n_used, n_blocks, lambda b, c: (zero_block(b * EXPERT_BLOCK).start(), c)[1], 0)
        for e in range(N_EXPERTS):
            @pl.when(padded_ref[e] > 0)
            def _():
                zero_block(pend_ref[e] - EXPERT_BLOCK).wait()
        lax.fori_loop(n_used, n_blocks, lambda b, c: (zero_block(b * EXPERT_BLOCK).wait(), c)[1], 0)

    def body(t, carry):
        for k in range(TOPK):
            slot = dest_ref[0, 0, k * MOE_TILE + t]
            pltpu.make_async_copy(h_ref.at[pl.ds(t, 1)], xs_hbm.at[pl.ds(slot, 1)], sem.at[1]).start()
        return carry
    lax.fori_loop(0, MOE_TILE, body, 0, unroll=8)
    for k in range(TOPK):
        pltpu.make_async_copy(h_ref, xs_hbm.at[pl.ds(0, MOE_TILE)], sem.at[1]).wait()


def _dispatch(pad_end, padded, dest_tiles, h, n_slots):
    n_tiles = dest_tiles.shape[0]
    grid_spec = pltpu.PrefetchScalarGridSpec(
        num_scalar_prefetch=2,
        grid=(n_tiles,),
        in_specs=[
            pl.BlockSpec((1, 1, TILE_ASSIGN), lambda i, pe, pd: (i, 0, 0), memory_space=pltpu.SMEM),
            pl.BlockSpec((MOE_TILE, D_MODEL), lambda i, pe, pd: (i, 0)),
        ],
        out_specs=pl.BlockSpec(memory_space=pl.ANY),
        scratch_shapes=[pltpu.VMEM((EXPERT_BLOCK, D_MODEL), F32), pltpu.SemaphoreType.DMA((2,))],
    )
    return pl.pallas_call(
        _dispatch_kernel,
        grid_spec=grid_spec,
        out_shape=jax.ShapeDtypeStruct((n_slots, D_MODEL), F32),
        compiler_params=pltpu.CompilerParams(dimension_semantics=("arbitrary",)),
        name="moe_dispatch",
    )(pad_end, padded, dest_tiles, h)


def _experts_kernel(be_ref, nused_ref, x_ref, wg_ref, wu_ref, wd_ref, o_ref):
    @pl.when(pl.program_id(0) < nused_ref[0])
    def _():
        xb = x_ref[...].astype(BF16)
        a = _silu(_dot(xb, wg_ref[0])) * _dot(xb, wu_ref[0])
        o_ref[...] = _dot(a.astype(BF16), wd_ref[0])

    @pl.when(pl.program_id(0) >= nused_ref[0])
    def _():
        o_ref[...] = jnp.zeros_like(o_ref)


def _experts(block_e, n_used, xs, wg, wu, wd):
    n_blocks = xs.shape[0] // EXPERT_BLOCK

    def row_block(i, be, nu):
        return (jnp.minimum(i, nu[0] - 1), 0)

    def expert_block(i, be, nu):
        return (be[jnp.minimum(i, nu[0] - 1)], 0, 0)

    grid_spec = pltpu.PrefetchScalarGridSpec(
        num_scalar_prefetch=2,
        grid=(n_blocks,),
        in_specs=[
            pl.BlockSpec((EXPERT_BLOCK, D_MODEL), row_block),
            pl.BlockSpec((1, D_MODEL, D_EXPERT), expert_block),
            pl.BlockSpec((1, D_MODEL, D_EXPERT), expert_block),
            pl.BlockSpec((1, D_EXPERT, D_MODEL), expert_block),
        ],
        out_specs=pl.BlockSpec((EXPERT_BLOCK, D_MODEL), lambda i, be, nu: (i, 0)),
    )
    return pl.pallas_call(
        _experts_kernel,
        grid_spec=grid_spec,
        out_shape=jax.ShapeDtypeStruct(xs.shape, F32),
        compiler_params=pltpu.CompilerParams(
            dimension_semantics=("arbitrary",), vmem_limit_bytes=VMEM_LIMIT),
        name="moe_experts",
    )(block_e, n_used, xs, wg, wu, wd)


def _combine_kernel(dst_cur_ref, dst_nxt_ref, x_ref, mod_ref, wts_ref, nrm_ref, y_hbm, o_ref, buf, sem,
                    *, final_norm):
    i = pl.program_id(0)
    n = pl.num_programs(0)
    slot = i % 2

    @pl.when(i == 0)
    def _():
        _start_row_gather(dst_cur_ref, TILE_ASSIGN, y_hbm, buf.at[0], sem.at[0])

    @pl.when(i + 1 < n)
    def _():
        _start_row_gather(dst_nxt_ref, TILE_ASSIGN, y_hbm, buf.at[1 - slot], sem.at[1 - slot])

    _wait_row_gather(TILE_ASSIGN, y_hbm, buf.at[slot], sem.at[slot])
    u, lt, _ = x_ref.shape
    y = (wts_ref[:, 0:1] * buf[slot, 0:MOE_TILE, :] + wts_ref[:, 1:2] * buf[slot, MOE_TILE:TILE_ASSIGN, :])
    g2 = mod_ref[:, 5:6, :]
    out = x_ref[...] + g2 * y.reshape(u, lt, D_MODEL)
    if final_norm:
        out = out * lax.rsqrt(jnp.mean(out * out, axis=-1, keepdims=True) + NORM_EPS)
        out = out * nrm_ref[...].reshape(1, 1, D_MODEL)
    o_ref[...] = out


def _combine(dest_tiles, x_units, mod_units, wts_col, nrm, y_slots, *, final_norm):
    n_units, lt, _ = x_units.shape
    u = MOE_TILE // lt
    n_tiles = n_units // u
    return pl.pallas_call(
        functools.partial(_combine_kernel, final_norm=final_norm),
        grid=(n_tiles,),
        in_specs=[
            pl.BlockSpec((1, 1, TILE_ASSIGN), lambda i: (i, 0, 0), memory_space=pltpu.SMEM),
            pl.BlockSpec((1, 1, TILE_ASSIGN), lambda i: (jnp.minimum(i + 1, n_tiles - 1), 0, 0),
                         memory_space=pltpu.SMEM),
            pl.BlockSpec((u, lt, D_MODEL), lambda i: (i, 0, 0)),
            pl.BlockSpec((u, 6, D_MODEL), lambda i: (i, 0, 0)),
            pl.BlockSpec((MOE_TILE, TOPK), lambda i: (i, 0)),
            pl.BlockSpec((1, D_MODEL), lambda i: (0, 0)),
            pl.BlockSpec(memory_space=pl.ANY),
        ],
        out_specs=pl.BlockSpec((u, lt, D_MODEL), lambda i: (i, 0, 0)),
        out_shape=jax.ShapeDtypeStruct(x_units.shape, F32),
        scratch_shapes=[pltpu.VMEM((2, TILE_ASSIGN, D_MODEL), F32), pltpu.SemaphoreType.DMA((2,))],
        compiler_params=pltpu.CompilerParams(
            dimension_semantics=("arbitrary",), vmem_limit_bytes=VMEM_LIMIT),
        name="moe_combine",
    )(dest_tiles, dest_tiles, x_units, mod_units, wts_col, nrm, y_slots)


def _routing_tables(eid_tiles, rank_tiles, counts):
    n_blocks = eid_tiles.size // EXPERT_BLOCK + N_EXPERTS
    padded = (counts + EXPERT_BLOCK - 1) // EXPERT_BLOCK * EXPERT_BLOCK
    pad_end = jnp.cumsum(padded).astype(jnp.int32)
    pad_start = pad_end - padded
    block_start = jnp.arange(n_blocks, dtype=jnp.int32)[:, None] * EXPERT_BLOCK
    block_e = jnp.minimum(jnp.sum((block_start >= pad_end[None, :]).astype(jnp.int32), axis=1),
                          N_EXPERTS - 1).astype(jnp.int32)
    n_used = pad_end[-1:] // EXPERT_BLOCK
    experts = jnp.arange(N_EXPERTS, dtype=jnp.int32)
    first_slot = jnp.sum(jnp.where(eid_tiles[..., None] == experts, pad_start, 0), axis=-1)
    return block_e, n_used, pad_end, padded, first_slot + rank_tiles


def _moe_layer(x, mod_l, nrm_ffn, wr, br, tri, wg, wu, wd, nrm_final, *, final_norm):
    bsz, seq, _ = x.shape
    t = bsz * seq
    lt = min(seq, MOE_TILE)
    per = seq // lt
    x_units = x.reshape(t // lt, lt, D_MODEL)
    mod_units = jnp.repeat(mod_l, per, axis=0) if per > 1 else mod_l
    h, eid_tiles, rank_tiles, wts, counts = _router(x_units, mod_units, nrm_ffn, wr, br, tri)
    block_e, n_used, pad_end, padded, dest_tiles = _routing_tables(eid_tiles, rank_tiles, counts[:, 0])
    n_slots = block_e.shape[0] * EXPERT_BLOCK
    xs = _dispatch(pad_end, padded, dest_tiles, h, n_slots)
    y_slots = _experts(block_e, n_used, xs, wg, wu, wd)
    out = _combine(dest_tiles, x_units, mod_units, wts.T, nrm_final, y_slots, final_norm=final_norm)
    return out.reshape(bsz, seq, D_MODEL)


def kernel(x_prompt, x_sample, c_prompt, c_sample, state_hgrn, state_gla, w_ada, b_ada, norm_mix,
           norm_ffn, w_in, hg_lb, hg_onorm, w_gk2, b_gk, gla_onorm, w_br_a, w_br_b, w_out, w_rg, b_rg,
           w_re, b_re, w_e_gate, w_e_up, w_e_down, norm_final):
    bp = x_prompt.shape[0]
    bs = x_sample.shape[0]
    mod = _ada_mod(jnp.concatenate([c_prompt, c_sample], axis=0), w_ada, b_ada)
    mod = mod.reshape(DEPTH, bp + bs, 6, D_MODEL)

    glr0 = C_GOG + GLA_W
    win_r = jnp.concatenate(
        [w_in[:, :, :glr0], w_in[:, :, glr0 + GLA_GATE_RANK:], w_in[:, :, glr0:glr0 + GLA_GATE_RANK],
         jnp.zeros((DEPTH, D_MODEL, LANES - GLA_GATE_RANK), F32)], axis=2).astype(BF16)
    wgk2_p = jnp.concatenate(
        [w_gk2, jnp.zeros((DEPTH, LANES - GLA_GATE_RANK, GLA_KW), F32)], axis=1).astype(BF16)
    wa_b = w_br_a.astype(BF16)
    wb_b = w_br_b.astype(BF16)
    wo_b = w_out.astype(BF16)
    def plan_consts(tb):
        r = np.arange(tb)
        chunk_tril = (r[:, None] // CHUNK == r[None, :] // CHUNK) & (r[None, :] <= r[:, None])
        return [jnp.asarray(chunk_tril, BF16), jnp.asarray(_slab_mask(), F32),
                jnp.asarray(SAFE_PLAN.segment_sum_matrix(), BF16), jnp.asarray(SAFE_PLAN.masks(), F32)]
    zpad = jnp.zeros((DEPTH, 8 - N_GROUPS, D_MODEL), F32)
    wr = jnp.concatenate([jnp.swapaxes(w_rg, 1, 2), zpad, jnp.swapaxes(w_re, 1, 2)], axis=1)
    br = jnp.concatenate([b_rg, jnp.zeros((DEPTH, 8 - N_GROUPS), F32), b_re], axis=1)[:, :, None]
    wg_b = w_e_gate.astype(BF16)
    wu_b = w_e_up.astype(BF16)
    wd_b = w_e_down.astype(BF16)
    nrm_f = norm_final.reshape(1, D_MODEL)
    assign = np.arange(TILE_ASSIGN)
    tri = jnp.asarray(assign[:, None] < assign[None, :], BF16)

    def run(x, mod_g, shg, sgla, tb):
        bsz = x.shape[0]
        new_hg, new_gla = [], []
        for l in range(DEPTH):
            x, s1, s2 = _mixer(
                x, mod_g[l], norm_mix[l:l + 1], win_r[l], hg_lb, wgk2_p[l], b_gk[l:l + 1],
                hg_onorm[l:l + 1], gla_onorm[l:l + 1], wa_b[l], wb_b[l], wo_b[l],
                shg[l], sgla[l].reshape(bsz, GLA_KW // LANES, LANES, HEAD_DV), plan_consts(tb),
                layer=l, tb=tb)
            new_hg.append(s1)
            new_gla.append(s2.reshape(bsz, GLA_HEADS, GLA_DK, HEAD_DV))
            x = _moe_layer(x, mod_g[l], norm_ffn[l:l + 1], wr[l], br[l], tri, wg_b[l], wu_b[l], wd_b[l],
                           nrm_f, final_norm=(l == DEPTH - 1))
        return x, jnp.stack(new_hg), jnp.stack(new_gla)

    zeros_hg = jnp.zeros((DEPTH, bp, HG_HEADS, HG_DK, HEAD_DV), F32)
    zeros_gla = jnp.zeros((DEPTH, bp, GLA_HEADS, GLA_DK, HEAD_DV), F32)
    y_p, hg_p, gla_p = run(x_prompt, mod[:, :bp], zeros_hg, zeros_gla, 256)
    y_s, hg_s, gla_s = run(x_sample, mod[:, bp:], state_hgrn, state_gla, CHUNK)
    return (y_p, y_s, hg_p, gla_p, hg_s, gla_s)
```

```python
import functools

import numpy as np
import jax
import jax.numpy as jnp
from jax import lax
from jax.experimental import pallas as pl
from jax.experimental.pallas import tpu as pltpu

F32 = jnp.float32
BF16 = jnp.bfloat16

D_MODEL = 1024
DEPTH = 2
CHUNK = 64
NORM_EPS = 1e-6
LOG_FLOOR = 1e-30
HG_HEADS = 4
HG_DK = 128
HEAD_DV = 128
HG_KW = HG_HEADS * HG_DK
HG_W = HG_HEADS * HEAD_DV
GLA_HEADS = 4
GLA_DK = 64
GLA_KW = GLA_HEADS * GLA_DK
GLA_W = GLA_HEADS * HEAD_DV
GLA_GATE_RANK = 16
GLA_GATE_NORM = 16.0
N_GROUPS = 4
EXPERTS_PER_GROUP = 8
N_EXPERTS = N_GROUPS * EXPERTS_PER_GROUP
TOPK = 2
D_EXPERT = 512

LANES = 128
VMEM_LIMIT = 56 * 1024 * 1024

C_HQ = 0
C_HF = C_HQ + HG_KW
C_HI = C_HF + HG_KW
C_HOG = C_HI + HG_W
C_GQ = C_HOG + HG_W
C_GK = C_GQ + GLA_KW
C_GV = C_GK + GLA_KW
C_GOG = C_GV + GLA_W
C_GA = C_GOG + GLA_W
C_GB = C_GA + D_MODEL
C_GLR = C_GB + D_MODEL
IN_COLS_PAD = C_GLR + LANES
PROJ_TILE = 640
assert IN_COLS_PAD % PROJ_TILE == 0

class _ScorePlan:
    def __init__(self, levels, adjacent, diag_block):
        self.levels = levels
        self.adjacent = adjacent
        self.diag_block = diag_block
        self.cum_rows = (2 * len(levels) + 1) * CHUNK
        self.n_masks = len(levels) + 1 + int(adjacent)

    def segment_sum_matrix(self):
        t = np.arange(CHUNK)[:, None]
        r = np.arange(CHUNK)[None, :]
        rows = []
        for m in self.levels:
            same = (t // m) == (r // m)
            rows.append(same & (r <= t))
            rows.append(same & (r > t))
        rows.append(r <= t)
        return np.concatenate(rows, axis=0).astype(np.float32)

    def masks(self):
        t = np.arange(CHUNK)[:, None]
        s = np.arange(CHUNK)[None, :]
        masks = [((t // self.diag_block) == (s // self.diag_block)) & (s <= t)]
        for m in self.levels + ((1,) if self.adjacent else ()):
            masks.append(((t // (2 * m)) == (s // (2 * m))) & ((t // m) % 2 == 1) & ((s // m) % 2 == 0))
        return np.stack(masks).astype(np.float32)


SAFE_PLAN = _ScorePlan((32, 16, 8, 4, 2), True, 1)


def _dot(a, b):
    return jnp.dot(a, b, preferred_element_type=F32)


def _dot_nt(a, b):
    return lax.dot_general(a, b, (((1,), (1,)), ((), ())), preferred_element_type=F32)


def _sigmoid(x):
    return 1.0 / (1.0 + jnp.exp(-x))


def _silu(x):
    return x * _sigmoid(x)


def _rms_mod(x, gain, scale, shift):
    y = x * lax.rsqrt(jnp.mean(x * x, axis=-1, keepdims=True) + NORM_EPS)
    return y * gain * (1.0 + scale) + shift


def _ada_kernel(c_ref, w_ref, b_ref, o_ref):
    c = c_ref[...]
    o_ref[0] = jnp.dot(_silu(c), w_ref[0], preferred_element_type=F32,
                       precision=lax.Precision.HIGHEST) + b_ref[0]


def _ada_mod(c_all, w_ada, b_ada):
    nb = c_all.shape[0]
    tn = 512
    return pl.pallas_call(
        _ada_kernel,
        grid=(DEPTH, 6 * D_MODEL // tn),
        in_specs=[
            pl.BlockSpec((nb, D_MODEL), lambda l, j: (0, 0)),
            pl.BlockSpec((1, D_MODEL, tn), lambda l, j: (l, 0, j)),
            pl.BlockSpec((1, 1, tn), lambda l, j: (l, 0, j)),
        ],
        out_specs=pl.BlockSpec((1, nb, tn), lambda l, j: (l, 0, j)),
        out_shape=jax.ShapeDtypeStruct((DEPTH, nb, 6 * D_MODEL), F32),
        name="ada_mod",
    )(c_all, w_ada, b_ada.reshape(DEPTH, 1, 6 * D_MODEL))


def _chunk_attention(q, k, v, g, states, mall_ref, mask_ref, heads_per_tile, plan):
    w = q.shape[1]
    n_tiles = w // LANES
    g_hi = g.astype(BF16)
    r1 = g - g_hi.astype(F32)
    g_mid = r1.astype(BF16)
    g_lo = (r1 - g_mid.astype(F32)).astype(BF16)
    mall = mall_ref[...]
    cums = _dot(mall, g_hi) + _dot(mall, g_mid) + _dot(mall, g_lo)
    b = cums[plan.cum_rows - CHUNK:plan.cum_rows]
    level_q = []
    level_k = []
    for i in range(len(plan.levels)):
        level_q.append(q * jnp.exp(cums[2 * i * CHUNK:(2 * i + 1) * CHUNK]))
        level_k.append(k * jnp.exp(cums[(2 * i + 1) * CHUNK:(2 * i + 2) * CHUNK]))
    if plan.diag_block == 1:
        qs = [q]
        ks = [k]
    else:
        i = plan.levels.index(plan.diag_block)
        qs = [level_q[i]]
        ks = [k * jnp.exp(-cums[2 * i * CHUNK:(2 * i + 1) * CHUNK])]
    qs += level_q
    ks += level_k
    if plan.adjacent:
        qs.append(q * jnp.exp(g))
        ks.append(k)
    b_last = b[CHUNK - 1:CHUNK]
    q_in = q * jnp.exp(b)
    k_out = k * jnp.exp(b_last - b)
    e_last = jnp.exp(b_last)

    dk = LANES // heads_per_tile
    lane = lax.broadcasted_iota(jnp.int32, (CHUNK, LANES), 1)
    row = lax.broadcasted_iota(jnp.int32, (LANES, HEAD_DV), 0)
    outs = []
    new_states = []
    for ti in range(n_tiles):
        sl = slice(ti * LANES, (ti + 1) * LANES)
        ks_t =[kk[:, sl].astype(BF16) for kk in ks]
        k_out_t = k_out[:, sl].T.astype(BF16)
        e_col = jnp.broadcast_to(e_last[:, sl], (LANES, LANES)).T
        s_old = states[ti]
        s_old_b = s_old.astype(BF16)
        upd = None
        for j in range(heads_per_tile):
            head = ti * heads_per_tile + j
            if heads_per_tile == 1:
                sel = lambda a: a
            else:
                in_head = (lane // dk) == j
                sel = lambda a, in_head=in_head: jnp.where(in_head, a, 0.0)
            sc = jnp.zeros((CHUNK, CHUNK), F32)
            for i in range(plan.n_masks):
                sc = sc + _dot_nt(sel(qs[i][:, sl]).astype(BF16), ks_t[i]) * mask_ref[i]
            vh = v[:, head * HEAD_DV:(head + 1) * HEAD_DV].astype(BF16)
            o = _dot(sc.astype(BF16), vh) + _dot(sel(q_in[:, sl]).astype(BF16), s_old_b)
            outs.append(o)
            u = _dot(k_out_t, vh)
            upd = u if upd is None else jnp.where((row // dk) == j, u, upd)
        new_states.append(e_col * s_old + upd)
    return jnp.concatenate(outs, axis=1), new_states


FAST_BLOCK = 16
N_SUB = CHUNK // FAST_BLOCK
SLAB_ROWS = FAST_BLOCK * (N_SUB * (N_SUB - 1) // 2) + CHUNK
FAST_BLOCK_DECAY_LIMIT = 60.0


def _slab_mask():
    t = np.arange(CHUNK)[:, None]
    cols = []
    for i in range(1, N_SUB):
        cols.append(np.broadcast_to(t // FAST_BLOCK == i, (CHUNK, i * FAST_BLOCK)))
    s = np.arange(CHUNK)[None, :]
    cols.append((t // FAST_BLOCK == s // FAST_BLOCK) & (s <= t))
    return np.concatenate(cols, axis=1).astype(np.float32)


def _block_attention_fast(q, k, v, g, states, tril_ref, slab_mask_ref, heads_per_tile):
    rows, w = q.shape
    n_chunks = rows // CHUNK
    n_tiles = w // LANES
    dk = LANES // heads_per_tile
    n_heads = n_tiles * heads_per_tile

    g_hi = g.astype(BF16)
    r1 = g - g_hi.astype(F32)
    g_mid = r1.astype(BF16)
    g_lo = (r1 - g_mid.astype(F32)).astype(BF16)
    tril = tril_ref[...]
    b = _dot(tril, g_hi) + _dot(tril, g_mid) + _dot(tril, g_lo)

    def end_row(c, i):
        r = c * CHUNK + (i + 1) * FAST_BLOCK
        return b[r - 1:r]

    def per_block(row_of):
        return jnp.concatenate([jnp.broadcast_to(row_of(c, i), (FAST_BLOCK, w))
                                for c in range(n_chunks) for i in range(N_SUB)], axis=0)

    zero = jnp.zeros((1, w), F32)
    b_start = per_block(lambda c, i: zero if i == 0 else end_row(c, i - 1))
    b_end = per_block(end_row)
    q_blk = q * jnp.exp(b - b_start)
    k_diag = k * jnp.exp(b_start - b)
    k_end = k * jnp.exp(b_end - b)
    q_in = q_blk * jnp.exp(b_start)
    k_out = k_end * jnp.exp(per_block(lambda c, i: end_row(c, N_SUB - 1)) - b_end)

    lane = lax.broadcasted_iota(jnp.int32, (CHUNK, LANES), 1)
    row = lax.broadcasted_iota(jnp.int32, (LANES, HEAD_DV), 0)
    slab_mask = slab_mask_ref[...]

    def sel(a, j):
        return a if heads_per_tile == 1 else jnp.where((lane // dk) == j, a, 0.0)

    v_b = v.astype(BF16)

    scores = {}
    for c in range(n_chunks):
        r0 = c * CHUNK
        slabs = []
        for i in range(1, N_SUB):
            for jb in range(i):
                blk = k_end[r0 + jb * FAST_BLOCK:r0 + (jb + 1) * FAST_BLOCK]
                slabs.append(blk if jb == i - 1 else blk * jnp.exp(end_row(c, i - 1) - end_row(c, jb)))
        slabs.append(k_diag[r0:r0 + CHUNK])
        k_slab = jnp.concatenate(slabs, axis=0).astype(BF16)
        for ti in range(n_tiles):
            sl = slice(ti * LANES, (ti + 1) * LANES)
            for j in range(heads_per_tile):
                qh = sel(q_blk[r0:r0 + CHUNK, sl], j).astype(BF16)
                scores[c, ti * heads_per_tile + j] = (_dot_nt(qh, k_slab[:, sl]) * slab_mask).astype(BF16)

    entering = [list(states)]
    for c in range(n_chunks):
        r0 = c * CHUNK
        nxt = []
        for ti in range(n_tiles):
            sl = slice(ti * LANES, (ti + 1) * LANES)
            k_out_t = k_out[r0:r0 + CHUNK, sl].T.astype(BF16)
            upd = None
            for j in range(heads_per_tile):
                head = ti * heads_per_tile + j
                u = _dot(k_out_t, v_b[r0:r0 + CHUNK, head * HEAD_DV:(head + 1) * HEAD_DV])
                upd = u if upd is None else jnp.where((row // dk) == j, u, upd)
            e_col = jnp.broadcast_to(jnp.exp(end_row(c, N_SUB - 1)[:, sl]), (LANES, LANES)).T
            nxt.append(e_col * entering[c][ti] + upd)
        entering.append(nxt)

    out_rows = []
    for c in range(n_chunks):
        r0 = c * CHUNK
        outs = []
        for head in range(n_heads):
            ti, j = divmod(head, heads_per_tile)
            sl = slice(ti * LANES, (ti + 1) * LANES)
            vh = v_b[r0:r0 + CHUNK, head * HEAD_DV:(head + 1) * HEAD_DV]
            v_slab = jnp.concatenate([vh[:i * FAST_BLOCK] for i in range(1, N_SUB)] + [vh], axis=0)
            outs.append(_dot(scores[c, head], v_slab)
                        + _dot(sel(q_in[r0:r0 + CHUNK, sl], j).astype(BF16), entering[c][ti].astype(BF16)))
        out_rows.append(jnp.concatenate(outs, axis=1))
    return jnp.concatenate(out_rows, axis=0), entering[n_chunks]


def _head_norm_gate(o, gain, gate):
    outs = []
    for h in range(o.shape[1] // HEAD_DV):
        sl = slice(h * HEAD_DV, (h + 1) * HEAD_DV)
        oh = o[:, sl]
        oh = oh * lax.rsqrt(jnp.mean(oh * oh, axis=-1, keepdims=True) + NORM_EPS) * gain
        outs.append(oh * _silu(gate[:, sl]))
    return jnp.concatenate(outs, axis=1)


def _mixer_kernel(x_ref, mod_ref, nrm_ref, win_ref, lb_ref, wgk2_ref, bgk_ref, hgn_ref, glan_ref,
                  wa_ref, wb_ref, wo_ref, shg0_ref, sgla0_ref,
                  tril_ref, slab_mask_ref, mall_safe_ref, mask_safe_ref,
                  xo_ref, shg_o_ref, sgla_o_ref,
                  p_scr, k_scr, lg_scr, shg_scr, sgla_scr, *, layer, tb):
    j = pl.program_id(1)

    @pl.when(j == 0)
    def _():
        shg_scr[...] = shg0_ref[0]
        sgla_scr[...] = sgla0_ref[0]

    x = x_ref[0]
    sh1 = mod_ref[0, 0:1, :]
    sc1 = mod_ref[0, 1:2, :]
    g1 = mod_ref[0, 2:3, :]
    hb = _rms_mod(x, nrm_ref[...], sc1, sh1).astype(BF16)
    for c in range(0, IN_COLS_PAD, PROJ_TILE):
        p_scr[:, c:c + PROJ_TILE] = _dot(hb, win_ref[:, c:c + PROJ_TILE])

    lb_all = lb_ref[...]
    lb_max = jnp.max(lb_all, axis=0, keepdims=True)
    lb_exp = jnp.exp(lb_all - lb_max)
    sm = lb_exp / jnp.sum(lb_exp, axis=0, keepdims=True)
    lbl = jnp.clip(jnp.sum(sm[0:layer + 1], axis=0, keepdims=True) - sm[0:1], 0.0, 1.0)

    p_scr[:, C_HQ:C_HQ + HG_KW] = _silu(p_scr[:, C_HQ:C_HQ + HG_KW]) * (HG_DK ** -0.5)
    z = p_scr[:, C_HF:C_HF + HG_KW]
    f = lbl + (1.0 - lbl) * _sigmoid(z)
    p_scr[:, C_HF:C_HF + HG_KW] = jnp.log(jnp.maximum(f, LOG_FLOOR))
    k_scr[...] = (1.0 - lbl) * _sigmoid(-z)
    glr = p_scr[:, C_GLR:C_GLR + LANES].astype(BF16)
    gate = _dot(glr, wgk2_ref[...]) + bgk_ref[...]
    lg_scr[...] = (jnp.minimum(gate, 0.0) - jnp.log1p(jnp.exp(-jnp.abs(gate)))) * (1.0 / GLA_GATE_NORM)
    p_scr[:, C_GQ:C_GQ + GLA_KW] = p_scr[:, C_GQ:C_GQ + GLA_KW] * (GLA_DK ** -0.5)

    n_hg_tiles = HG_KW // LANES
    n_gla_tiles = GLA_KW // LANES

    def one_chunk(rows, states, attend):
        o_hg, st_hg = attend(p_scr[rows, C_HQ:C_HQ + HG_KW], k_scr[rows, :],
                             p_scr[rows, C_HI:C_HI + HG_W], p_scr[rows, C_HF:C_HF + HG_KW],
                             states[:n_hg_tiles], 1)
        p_scr[rows, C_HI:C_HI + HG_W] = o_hg
        o_gla, st_gla = attend(p_scr[rows, C_GQ:C_GQ + GLA_KW], p_scr[rows, C_GK:C_GK + GLA_KW],
                               p_scr[rows, C_GV:C_GV + GLA_W], lg_scr[rows, :],
                               states[n_hg_tiles:], 2)
        p_scr[rows, C_GV:C_GV + GLA_W] = o_gla
        return st_hg + st_gla

    def attend_safe(q, k, v, g, states, heads_per_tile):
        return _chunk_attention(q, k, v, g, states, mall_safe_ref, mask_safe_ref, heads_per_tile, SAFE_PLAN)

    def load_states():
        return [shg_scr[t] for t in range(n_hg_tiles)] + [sgla_scr[t] for t in range(n_gla_tiles)]

    def store_states(states):
        for t in range(n_hg_tiles):
            shg_scr[t] = states[t]
        for t in range(n_gla_tiles):
            sgla_scr[t] = states[n_hg_tiles + t]

    def run_block_fast():
        states = load_states()
        o_hg, st_hg = _block_attention_fast(
            p_scr[:, C_HQ:C_HQ + HG_KW], k_scr[...], p_scr[:, C_HI:C_HI + HG_W],
            p_scr[:, C_HF:C_HF + HG_KW], states[:n_hg_tiles], tril_ref, slab_mask_ref, 1)
        p_scr[:, C_HI:C_HI + HG_W] = o_hg
        o_gla, st_gla = _block_attention_fast(
            p_scr[:, C_GQ:C_GQ + GLA_KW], p_scr[:, C_GK:C_GK + GLA_KW], p_scr[:, C_GV:C_GV + GLA_W],
            lg_scr[...], states[n_hg_tiles:], tril_ref, slab_mask_ref, 2)
        p_scr[:, C_GV:C_GV + GLA_W] = o_gla
        store_states(st_hg + st_gla)

    def run_chunks_safe():
        def chunk_body(ci, carry):
            rows = pl.ds(pl.multiple_of(ci * CHUNK, CHUNK), CHUNK)
            store_states(one_chunk(rows, load_states(), attend_safe))
            return carry

        lax.fori_loop(0, tb // CHUNK, chunk_body, 0)

    blk = FAST_BLOCK
    min_hg = jnp.min(jnp.sum(p_scr[:, C_HF:C_HF + HG_KW].reshape(tb // blk, blk, HG_KW), axis=1))
    min_gla = jnp.min(jnp.sum(lg_scr[...].reshape(tb // blk, blk, GLA_KW), axis=1))
    bounded = jnp.minimum(min_hg, min_gla) >= -FAST_BLOCK_DECAY_LIMIT

    @pl.when(bounded)
    def _():
        run_block_fast()

    @pl.when(jnp.logical_not(bounded))
    def _():
        run_chunks_safe()

    o_hg = _head_norm_gate(p_scr[:, C_HI:C_HI + HG_W], hgn_ref[...], p_scr[:, C_HOG:C_HOG + HG_W])
    o_gla = _head_norm_gate(p_scr[:, C_GV:C_GV + GLA_W], glan_ref[...], p_scr[:, C_GOG:C_GOG + GLA_W])
    ya = _dot(o_hg.astype(BF16), wa_ref[...])
    yb = _dot(o_gla.astype(BF16), wb_ref[...])
    merged = (_sigmoid(p_scr[:, C_GA:C_GA + D_MODEL]) * ya
              + _sigmoid(p_scr[:, C_GB:C_GB + D_MODEL]) * yb)
    m = _dot(merged.astype(BF16), wo_ref[...])
    xo_ref[0] = x + g1 * m

    @pl.when(j == pl.num_programs(1) - 1)
    def _():
        shg_o_ref[0] = shg_scr[...]
        sgla_o_ref[0] = sgla_scr[...]


def _const_spec(shape):
    nd = len(shape)
    return pl.BlockSpec(shape, lambda b, j, nd=nd: (0,) * nd, pipeline_mode=pl.Buffered(1))


def _mixer(x, mod, nrm, win, hg_lb, wgk2, bgk, hgn, glan, wa, wb, wo, shg0, sgla0, plan_consts,
           *, layer, tb):
    bsz, seq, _ = x.shape
    kern = functools.partial(_mixer_kernel, layer=layer, tb=tb)
    n_gla_tiles = GLA_KW // LANES
    return pl.pallas_call(
        kern,
        grid=(bsz, seq // tb),
        in_specs=[
            pl.BlockSpec((1, tb, D_MODEL), lambda b, j: (b, j, 0)),
            pl.BlockSpec((1, 6, D_MODEL), lambda b, j: (b, 0, 0)),
            _const_spec((1, D_MODEL)),
            _const_spec((D_MODEL, IN_COLS_PAD)),
            _const_spec((DEPTH, HG_KW)),
            _const_spec((LANES, GLA_KW)),
            _const_spec((1, GLA_KW)),
            _const_spec((1, HEAD_DV)),
            _const_spec((1, HEAD_DV)),
            _const_spec((HG_W, D_MODEL)),
            _const_spec((GLA_W, D_MODEL)),
            _const_spec((D_MODEL, D_MODEL)),
            pl.BlockSpec((1, HG_HEADS, HG_DK, HEAD_DV), lambda b, j: (b, 0, 0, 0)),
            pl.BlockSpec((1, n_gla_tiles, LANES, HEAD_DV), lambda b, j: (b, 0, 0, 0)),
            _const_spec((tb, tb)),
            _const_spec((CHUNK, SLAB_ROWS)),
            _const_spec((SAFE_PLAN.cum_rows, CHUNK)),
            _const_spec((SAFE_PLAN.n_masks, CHUNK, CHUNK)),
        ],
        out_specs=[
            pl.BlockSpec((1, tb, D_MODEL), lambda b, j: (b, j, 0)),
            pl.BlockSpec((1, HG_HEADS, HG_DK, HEAD_DV), lambda b, j: (b, 0, 0, 0)),
            pl.BlockSpec((1, n_gla_tiles, LANES, HEAD_DV), lambda b, j: (b, 0, 0, 0)),
        ],
        out_shape=[
            jax.ShapeDtypeStruct((bsz, seq, D_MODEL), F32),
            jax.ShapeDtypeStruct((bsz, HG_HEADS, HG_DK, HEAD_DV), F32),
            jax.ShapeDtypeStruct((bsz, n_gla_tiles, LANES, HEAD_DV), F32),
        ],
        scratch_shapes=[
            pltpu.VMEM((tb, IN_COLS_PAD), F32),
            pltpu.VMEM((tb, HG_KW), F32),
            pltpu.VMEM((tb, GLA_KW), F32),
            pltpu.VMEM((HG_HEADS, HG_DK, HEAD_DV), F32),
            pltpu.VMEM((n_gla_tiles, LANES, HEAD_DV), F32),
        ],
        compiler_params=pltpu.CompilerParams(
            dimension_semantics=("arbitrary", "arbitrary"), vmem_limit_bytes=VMEM_LIMIT),
        name=f"mixer_l{layer}",
    )(x, mod, nrm, win, hg_lb, wgk2, bgk, hgn, glan, wa, wb, wo, shg0, sgla0, *plan_consts)


ROUTER_ROWS = 8 + N_EXPERTS
MOE_TILE = 512
TILE_ASSIGN = TOPK * MOE_TILE
EXPERT_BLOCK = 512


ROW_TILE = (D_MODEL // LANES, LANES)


def _store_row_tiles(ref, rows):
    for c in range(ROW_TILE[0]):
        ref[:, c, :] = rows[:, c * LANES:(c + 1) * LANES]


def _load_row_tiles(ref):
    return jnp.concatenate([ref[:, c, :] for c in range(ROW_TILE[0])], axis=1)


def _first_argmax_rows(vals, n):
    ridx = lax.broadcasted_iota(jnp.int32, vals.shape, 0)
    vmax = jnp.max(vals, axis=0, keepdims=True)
    imax = jnp.min(jnp.where(vals == vmax, ridx, n), axis=0, keepdims=True)
    return vmax, imax


def _router_kernel(x_ref, mod_ref, nrm_ref, wr_ref, br_ref, tri_ref,
                   h_ref, eid_ref, rank_ref, wts_ref, cnt_ref, run_scr):
    @pl.when(pl.program_id(0) == 0)
    def _():
        run_scr[...] = jnp.zeros_like(run_scr)

    u, lt, _ = x_ref.shape
    x = x_ref[...]
    sh2 = mod_ref[:, 3:4, :]
    sc2 = mod_ref[:, 4:5, :]
    h = _rms_mod(x, nrm_ref[...].reshape(1, 1, D_MODEL), sc2, sh2).reshape(u * lt, D_MODEL)
    _store_row_tiles(h_ref, h)
    logits = lax.dot_general(wr_ref[...], h, (((1,), (1,)), ((), ())), preferred_element_type=F32,
                             precision=lax.Precision.HIGHEST) + br_ref[...]
    gl = logits[0:N_GROUPS]
    gmax, gi = _first_argmax_rows(gl, N_GROUPS)
    gp = 1.0 / jnp.sum(jnp.exp(gl - gmax), axis=0, keepdims=True)
    le = logits[8:8 + EXPERTS_PER_GROUP]
    for g in range(1, N_GROUPS):
        le = jnp.where(gi == g, logits[8 + g * EXPERTS_PER_GROUP:8 + (g + 1) * EXPERTS_PER_GROUP], le)
    pe = jnp.exp(le - jnp.max(le, axis=0, keepdims=True))
    pe = pe / jnp.sum(pe, axis=0, keepdims=True)
    v1, i1 = _first_argmax_rows(pe, EXPERTS_PER_GROUP)
    ridx = lax.broadcasted_iota(jnp.int32, pe.shape, 0)
    v2, i2 = _first_argmax_rows(jnp.where(ridx == i1, -1.0, pe), EXPERTS_PER_GROUP)
    vsum = v1 + v2
    wts_ref[0:1, :] = gp * v1 / vsum
    wts_ref[1:2, :] = gp * v2 / vsum
    eflat = jnp.concatenate([gi * EXPERTS_PER_GROUP + i1, gi * EXPERTS_PER_GROUP + i2], axis=1)
    eid_ref[0] = eflat
    onehot = (eflat == lax.broadcasted_iota(jnp.int32, (N_EXPERTS, TILE_ASSIGN), 0)).astype(F32)
    before = _dot(onehot.astype(BF16), tri_ref[...]) + run_scr[...]
    rank_ref[0] = jnp.sum(onehot * before, axis=0, keepdims=True).astype(jnp.int32)
    run_scr[...] = run_scr[...] + jnp.sum(onehot, axis=1, keepdims=True)
    cnt_ref[...] = run_scr[...].astype(jnp.int32)


def _router(x_units, mod_units, nrm, wr, br, tri):
    n_units, lt, _ = x_units.shape
    u = MOE_TILE // lt
    n_tiles = n_units // u
    return pl.pallas_call(
        _router_kernel,
        grid=(n_tiles,),
        in_specs=[
            pl.BlockSpec((u, lt, D_MODEL), lambda i: (i, 0, 0)),
            pl.BlockSpec((u, 6, D_MODEL), lambda i: (i, 0, 0)),
            pl.BlockSpec((1, D_MODEL), lambda i: (0, 0)),
            pl.BlockSpec((ROUTER_ROWS, D_MODEL), lambda i: (0, 0)),
            pl.BlockSpec((ROUTER_ROWS, 1), lambda i: (0, 0)),
            pl.BlockSpec((TILE_ASSIGN, TILE_ASSIGN), lambda i: (0, 0)),
        ],
        out_specs=[
            pl.BlockSpec((MOE_TILE,) + ROW_TILE, lambda i: (i, 0, 0)),
            pl.BlockSpec((1, 1, TILE_ASSIGN), lambda i: (i, 0, 0)),
            pl.BlockSpec((1, 1, TILE_ASSIGN), lambda i: (i, 0, 0)),
            pl.BlockSpec((TOPK, MOE_TILE), lambda i: (0, i)),
            pl.BlockSpec((N_EXPERTS, 1), lambda i: (0, 0)),
        ],
        out_shape=[
            jax.ShapeDtypeStruct((n_tiles * MOE_TILE,) + ROW_TILE, F32),
            jax.ShapeDtypeStruct((n_tiles, 1, TILE_ASSIGN), jnp.int32),
            jax.ShapeDtypeStruct((n_tiles, 1, TILE_ASSIGN), jnp.int32),
            jax.ShapeDtypeStruct((TOPK, n_tiles * MOE_TILE), F32),
            jax.ShapeDtypeStruct((N_EXPERTS, 1), jnp.int32),
        ],
        scratch_shapes=[pltpu.VMEM((N_EXPERTS, 1), F32)],
        compiler_params=pltpu.CompilerParams(dimension_semantics=("arbitrary",)),
        name="moe_router",
    )(x_units, mod_units, nrm, wr, br, tri)


def _start_row_gather(idx_ref, n_rows, src_hbm, dst, sem):
    def body(r, carry):
        row = idx_ref[0, 0, r]
        pltpu.make_async_copy(src_hbm.at[pl.ds(row, 1)], dst.at[pl.ds(r, 1)], sem).start()
        return carry
    lax.fori_loop(0, n_rows, body, 0, unroll=8)


def _wait_row_gather(n_rows, src_hbm, dst, sem):
    pltpu.make_async_copy(src_hbm.at[pl.ds(0, n_rows)], dst, sem).wait()


def _dispatch_kernel(pend_ref, padded_ref, dest_ref, h_ref, xs_hbm, zbuf, sem):
    n_blocks = xs_hbm.shape[0] // EXPERT_BLOCK

    def zero_block(first_row):
        return pltpu.make_async_copy(
            zbuf, xs_hbm.at[pl.ds(pl.multiple_of(first_row, EXPERT_BLOCK), EXPERT_BLOCK)], sem.at[0])

    @pl.when(pl.program_id(0) == 0)
    def _():
        zbuf[...] = jnp.zeros_like(zbuf)
        n_used = pend_ref[N_EXPERTS - 1] // EXPERT_BLOCK
        for e in range(N_EXPERTS):
            @pl.when(padded_ref[e] > 0)
            def _():
                zero_block(pend_ref[e] - EXPERT_BLOCK).start()
        lax.fori_loop(n_used, n_blocks, lambda b, c: (zero_block(b * EXPERT_BLOCK).start(), c)[1], 0)
        for e in range(N_EXPERTS):
            @pl.when(padded_ref[e] > 0)
            def _():
                zero_block(pend_ref[e] - EXPERT_BLOCK).wait()
        lax.fori_loop(n_used, n_blocks, lambda b, c: (zero_block(b * EXPERT_BLOCK).wait(), c)[1], 0)

    def body(t, carry):
        for k in range(TOPK):
            slot = dest_ref[0, 0, k * MOE_TILE + t]
            pltpu.make_async_copy(h_ref.at[pl.ds(t, 1)], xs_hbm.at[pl.ds(slot, 1)], sem.at[1]).start()
        return carry
    lax.fori_loop(0, MOE_TILE, body, 0, unroll=8)
    for k in range(TOPK):
        pltpu.make_async_copy(h_ref, xs_hbm.at[pl.ds(0, MOE_TILE)], sem.at[1]).wait()


def _dispatch(pad_end, padded, dest_tiles, h, n_slots):
    n_tiles = dest_tiles.shape[0]
    grid_spec = pltpu.PrefetchScalarGridSpec(
        num_scalar_prefetch=2,
        grid=(n_tiles,),
        in_specs=[
            pl.BlockSpec((1, 1, TILE_ASSIGN), lambda i, pe, pd: (i, 0, 0), memory_space=pltpu.SMEM),
            pl.BlockSpec((MOE_TILE,) + ROW_TILE, lambda i, pe, pd: (i, 0, 0)),
        ],
        out_specs=pl.BlockSpec(memory_space=pl.ANY),
        scratch_shapes=[pltpu.VMEM((EXPERT_BLOCK,) + ROW_TILE, F32), pltpu.SemaphoreType.DMA((2,))],
    )
    return pl.pallas_call(
        _dispatch_kernel,
        grid_spec=grid_spec,
        out_shape=jax.ShapeDtypeStruct((n_slots,) + ROW_TILE, F32),
        compiler_params=pltpu.CompilerParams(dimension_semantics=("arbitrary",)),
        name="moe_dispatch",
    )(pad_end, padded, dest_tiles, h)


def _experts_kernel(be_ref, nused_ref, x_ref, wg_ref, wu_ref, wd_ref, o_ref):
    @pl.when(pl.program_id(0) < nused_ref[0])
    def _():
        xb = _load_row_tiles(x_ref).astype(BF16)
        a = _silu(_dot(xb, wg_ref[0])) * _dot(xb, wu_ref[0])
        _store_row_tiles(o_ref, _dot(a.astype(BF16), wd_ref[0]))

    @pl.when(pl.program_id(0) >= nused_ref[0])
    def _():
        o_ref[...] = jnp.zeros_like(o_ref)


def _experts(block_e, n_used, xs, wg, wu, wd):
    n_blocks = xs.shape[0] // EXPERT_BLOCK

    def row_block(i, be, nu):
        return (jnp.minimum(i, nu[0] - 1), 0, 0)

    def expert_block(i, be, nu):
        return (be[jnp.minimum(i, nu[0] - 1)], 0, 0)

    grid_spec = pltpu.PrefetchScalarGridSpec(
        num_scalar_prefetch=2,
        grid=(n_blocks,),
        in_specs=[
            pl.BlockSpec((EXPERT_BLOCK,) + ROW_TILE, row_block),
            pl.BlockSpec((1, D_MODEL, D_EXPERT), expert_block),
            pl.BlockSpec((1, D_MODEL, D_EXPERT), expert_block),
            pl.BlockSpec((1, D_EXPERT, D_MODEL), expert_block),
        ],
        out_specs=pl.BlockSpec((EXPERT_BLOCK,) + ROW_TILE, lambda i, be, nu: (i, 0, 0)),
    )
    return pl.pallas_call(
        _experts_kernel,
        grid_spec=grid_spec,
        out_shape=jax.ShapeDtypeStruct(xs.shape, F32),
        compiler_params=pltpu.CompilerParams(
            dimension_semantics=("arbitrary",), vmem_limit_bytes=VMEM_LIMIT),
        name="moe_experts",
    )(block_e, n_used, xs, wg, wu, wd)


def _combine_kernel(dst_cur_ref, dst_nxt_ref, x_ref, mod_ref, wts_ref, nrm_ref, y_hbm, o_ref, buf, sem,
                    *, final_norm):
    i = pl.program_id(0)
    n = pl.num_programs(0)
    slot = i % 2

    @pl.when(i == 0)
    def _():
        _start_row_gather(dst_cur_ref, TILE_ASSIGN, y_hbm, buf.at[0], sem.at[0])

    @pl.when(i + 1 < n)
    def _():
        _start_row_gather(dst_nxt_ref, TILE_ASSIGN, y_hbm, buf.at[1 - slot], sem.at[1 - slot])

    _wait_row_gather(TILE_ASSIGN, y_hbm, buf.at[slot], sem.at[slot])
    u, lt, _ = x_ref.shape
    y = (wts_ref[:, 0:1] * _load_row_tiles(buf.at[slot, pl.ds(0, MOE_TILE)])
         + wts_ref[:, 1:2] * _load_row_tiles(buf.at[slot, pl.ds(MOE_TILE, MOE_TILE)]))
    g2 = mod_ref[:, 5:6, :]
    out = x_ref[...] + g2 * y.reshape(u, lt, D_MODEL)
    if final_norm:
        out = out * lax.rsqrt(jnp.mean(out * out, axis=-1, keepdims=True) + NORM_EPS)
        out = out * nrm_ref[...].reshape(1, 1, D_MODEL)
    o_ref[...] = out


def _combine(dest_tiles, x_units, mod_units, wts_col, nrm, y_slots, *, final_norm):
    n_units, lt, _ = x_units.shape
    u = MOE_TILE // lt
    n_tiles = n_units // u
    return pl.pallas_call(
        functools.partial(_combine_kernel, final_norm=final_norm),
        grid=(n_tiles,),
        in_specs=[
            pl.BlockSpec((1, 1, TILE_ASSIGN), lambda i: (i, 0, 0), memory_space=pltpu.SMEM),
            pl.BlockSpec((1, 1, TILE_ASSIGN), lambda i: (jnp.minimum(i + 1, n_tiles - 1), 0, 0),
                         memory_space=pltpu.SMEM),
            pl.BlockSpec((u, lt, D_MODEL), lambda i: (i, 0, 0)),
            pl.BlockSpec((u, 6, D_MODEL), lambda i: (i, 0, 0)),
            pl.BlockSpec((MOE_TILE, TOPK), lambda i: (i, 0)),
            pl.BlockSpec((1, D_MODEL), lambda i: (0, 0)),
            pl.BlockSpec(memory_space=pl.ANY),
        ],
        out_specs=pl.BlockSpec((u, lt, D_MODEL), lambda i: (i, 0, 0)),
        out_shape=jax.ShapeDtypeStruct(x_units.shape, F32),
        scratch_shapes=[pltpu.VMEM((2, TILE_ASSIGN) + ROW_TILE, F32), pltpu.SemaphoreType.DMA((2,))],
        compiler_params=pltpu.CompilerParams(
            dimension_semantics=("arbitrary",), vmem_limit_bytes=VMEM_LIMIT),
        name="moe_combine",
    )(dest_tiles, dest_tiles, x_units, mod_units, wts_col, nrm, y_slots)


def _routing_tables(eid_tiles, rank_tiles, counts):
    n_blocks = eid_tiles.size // EXPERT_BLOCK + N_EXPERTS
    padded = (counts + EXPERT_BLOCK - 1) // EXPERT_BLOCK * EXPERT_BLOCK
    pad_end = jnp.cumsum(padded).astype(jnp.int32)
    pad_start = pad_end - padded
    block_start = jnp.arange(n_blocks, dtype=jnp.int32)[:, None] * EXPERT_BLOCK
    block_e = jnp.minimum(jnp.sum((block_start >= pad_end[None, :]).astype(jnp.int32), axis=1),
                          N_EXPERTS - 1).astype(jnp.int32)
    n_used = pad_end[-1:] // EXPERT_BLOCK
    experts = jnp.arange(N_EXPERTS, dtype=jnp.int32)
    first_slot = jnp.sum(jnp.where(eid_tiles[..., None] == experts, pad_start, 0), axis=-1)
    return block_e, n_used, pad_end, padded, first_slot + rank_tiles


def _moe_layer(x, mod_l, nrm_ffn, wr, br, tri, wg, wu, wd, nrm_final, *, final_norm):
    bsz, seq, _ = x.shape
    t = bsz * seq
    lt = min(seq, MOE_TILE)
    per = seq // lt
    x_units = x.reshape(t // lt, lt, D_MODEL)
    mod_units = jnp.repeat(mod_l, per, axis=0) if per > 1 else mod_l
    h, eid_tiles, rank_tiles, wts, counts = _router(x_units, mod_units, nrm_ffn, wr, br, tri)
    block_e, n_used, pad_end, padded, dest_tiles = _routing_tables(eid_tiles, rank_tiles, counts[:, 0])
    n_slots = block_e.shape[0] * EXPERT_BLOCK
    xs = _dispatch(pad_end, padded, dest_tiles, h, n_slots)
    y_slots = _experts(block_e, n_used, xs, wg, wu, wd)
    out = _combine(dest_tiles, x_units, mod_units, wts.T, nrm_final, y_slots, final_norm=final_norm)
    return out.reshape(bsz, seq, D_MODEL)


def kernel(x_prompt, x_sample, c_prompt, c_sample, state_hgrn, state_gla, w_ada, b_ada, norm_mix,
           norm_ffn, w_in, hg_lb, hg_onorm, w_gk2, b_gk, gla_onorm, w_br_a, w_br_b, w_out, w_rg, b_rg,
           w_re, b_re, w_e_gate, w_e_up, w_e_down, norm_final):
    bp = x_prompt.shape[0]
    bs = x_sample.shape[0]
    mod = _ada_mod(jnp.concatenate([c_prompt, c_sample], axis=0), w_ada, b_ada)
    mod = mod.reshape(DEPTH, bp + bs, 6, D_MODEL)

    glr0 = C_GOG + GLA_W
    win_r = jnp.concatenate(
        [w_in[:, :, :glr0], w_in[:, :, glr0 + GLA_GATE_RANK:], w_in[:, :, glr0:glr0 + GLA_GATE_RANK],
         jnp.zeros((DEPTH, D_MODEL, LANES - GLA_GATE_RANK), F32)], axis=2).astype(BF16)
    wgk2_p = jnp.concatenate(
        [w_gk2, jnp.zeros((DEPTH, LANES - GLA_GATE_RANK, GLA_KW), F32)], axis=1).astype(BF16)
    wa_b = w_br_a.astype(BF16)
    wb_b = w_br_b.astype(BF16)
    wo_b = w_out.astype(BF16)
    def plan_consts(tb):
        r = np.arange(tb)
        chunk_tril = (r[:, None] // CHUNK == r[None, :] // CHUNK) & (r[None, :] <= r[:, None])
        return [jnp.asarray(chunk_tril, BF16), jnp.asarray(_slab_mask(), F32),
                jnp.asarray(SAFE_PLAN.segment_sum_matrix(), BF16), jnp.asarray(SAFE_PLAN.masks(), F32)]
    zpad = jnp.zeros((DEPTH, 8 - N_GROUPS, D_MODEL), F32)
    wr = jnp.concatenate([jnp.swapaxes(w_rg, 1, 2), zpad, jnp.swapaxes(w_re, 1, 2)], axis=1)
    br = jnp.concatenate([b_rg, jnp.zeros((DEPTH, 8 - N_GROUPS), F32), b_re], axis=1)[:, :, None]
    wg_b = w_e_gate.astype(BF16)
    wu_b = w_e_up.astype(BF16)
    wd_b = w_e_down.astype(BF16)
    nrm_f = norm_final.reshape(1, D_MODEL)
    assign = np.arange(TILE_ASSIGN)
    tri = jnp.asarray(assign[:, None] < assign[None, :], BF16)

    def run(x, mod_g, shg, sgla, tb):
        bsz = x.shape[0]
        new_hg, new_gla = [], []
        for l in range(DEPTH):
            x, s1, s2 = _mixer(
                x, mod_g[l], norm_mix[l:l + 1], win_r[l], hg_lb, wgk2_p[l], b_gk[l:l + 1],
                hg_onorm[l:l + 1], gla_onorm[l:l + 1], wa_b[l], wb_b[l], wo_b[l],
                shg[l], sgla[l].reshape(bsz, GLA_KW // LANES, LANES, HEAD_DV), plan_consts(tb),
                layer=l, tb=tb)
            new_hg.append(s1)
            new_gla.append(s2.reshape(bsz, GLA_HEADS, GLA_DK, HEAD_DV))
            x = _moe_layer(x, mod_g[l], norm_ffn[l:l + 1], wr[l], br[l], tri, wg_b[l], wu_b[l], wd_b[l],
                           nrm_f, final_norm=(l == DEPTH - 1))
        return x, jnp.stack(new_hg), jnp.stack(new_gla)

    zeros_hg = jnp.zeros((DEPTH, bp, HG_HEADS, HG_DK, HEAD_DV), F32)
    zeros_gla = jnp.zeros((DEPTH, bp, GLA_HEADS, GLA_DK, HEAD_DV), F32)
    y_p, hg_p, gla_p = run(x_prompt, mod[:, :bp], zeros_hg, zeros_gla, 256)
    y_s, hg_s, gla_s = run(x_sample, mod[:, bp:], state_hgrn, state_gla, CHUNK)
    return (y_p, y_s, hg_p, gla_p, hg_s, gla_s)
```

```python
import functools

import numpy as np
import jax
import jax.numpy as jnp
from jax import lax
from jax.experimental import pallas as pl
from jax.experimental.pallas import tpu as pltpu

F32 = jnp.float32
BF16 = jnp.bfloat16

D_MODEL = 1024
DEPTH = 2
CHUNK = 64
NORM_EPS = 1e-6
LOG_FLOOR = 1e-30
HG_HEADS = 4
HG_DK = 128
HEAD_DV = 128
HG_KW = HG_HEADS * HG_DK
HG_W = HG_HEADS * HEAD_DV
GLA_HEADS = 4
GLA_DK = 64
GLA_KW = GLA_HEADS * GLA_DK
GLA_W = GLA_HEADS * HEAD_DV
GLA_GATE_RANK = 16
GLA_GATE_NORM = 16.0
N_GROUPS = 4
EXPERTS_PER_GROUP = 8
N_EXPERTS = N_GROUPS * EXPERTS_PER_GROUP
TOPK = 2
D_EXPERT = 512

LANES = 128
VMEM_LIMIT = 56 * 1024 * 1024

C_HQ = 0
C_HF = C_HQ + HG_KW
C_HI = C_HF + HG_KW
C_HOG = C_HI + HG_W
C_GQ = C_HOG + HG_W
C_GK = C_GQ + GLA_KW
C_GV = C_GK + GLA_KW
C_GOG = C_GV + GLA_W
C_GA = C_GOG + GLA_W
C_GB = C_GA + D_MODEL
C_GLR = C_GB + D_MODEL
IN_COLS_PAD = C_GLR + LANES
MXU_WIDTH = 256
PROJ_TILE = 4 * MXU_WIDTH

class _ScorePlan:
    def __init__(self, levels, adjacent, diag_block):
        self.levels = levels
        self.adjacent = adjacent
        self.diag_block = diag_block
        self.cum_rows = (2 * len(levels) + 1) * CHUNK
        self.n_masks = len(levels) + 1 + int(adjacent)

    def segment_sum_matrix(self):
        t = np.arange(CHUNK)[:, None]
        r = np.arange(CHUNK)[None, :]
        rows = []
        for m in self.levels:
            same = (t // m) == (r // m)
            rows.append(same & (r <= t))
            rows.append(same & (r > t))
        rows.append(r <= t)
        return np.concatenate(rows, axis=0).astype(np.float32)

    def masks(self):
        t = np.arange(CHUNK)[:, None]
        s = np.arange(CHUNK)[None, :]
        masks = [((t // self.diag_block) == (s // self.diag_block)) & (s <= t)]
        for m in self.levels + ((1,) if self.adjacent else ()):
            masks.append(((t // (2 * m)) == (s // (2 * m))) & ((t // m) % 2 == 1) & ((s // m) % 2 == 0))
        return np.stack(masks).astype(np.float32)


SAFE_PLAN = _ScorePlan((32, 16, 8, 4, 2), True, 1)


def _dot(a, b):
    return jnp.dot(a, b, preferred_element_type=F32)


def _dot_nt(a, b):
    return lax.dot_general(a, b, (((1,), (1,)), ((), ())), preferred_element_type=F32)


def _sigmoid(x):
    return 1.0 / (1.0 + jnp.exp(-x))


def _silu(x):
    return x * _sigmoid(x)


def _rms_mod(x, gain, scale, shift):
    y = x * lax.rsqrt(jnp.mean(x * x, axis=-1, keepdims=True) + NORM_EPS)
    return y * gain * (1.0 + scale) + shift


def _ada_kernel(c_ref, w_ref, b_ref, o_ref):
    c = c_ref[...]
    o_ref[0] = jnp.dot(_silu(c), w_ref[0], preferred_element_type=F32,
                       precision=lax.Precision.HIGHEST) + b_ref[0]


def _ada_mod(c_all, w_ada, b_ada):
    nb = c_all.shape[0]
    tn = 512
    return pl.pallas_call(
        _ada_kernel,
        grid=(DEPTH, 6 * D_MODEL // tn),
        in_specs=[
            pl.BlockSpec((nb, D_MODEL), lambda l, j: (0, 0)),
            pl.BlockSpec((1, D_MODEL, tn), lambda l, j: (l, 0, j)),
            pl.BlockSpec((1, 1, tn), lambda l, j: (l, 0, j)),
        ],
        out_specs=pl.BlockSpec((1, nb, tn), lambda l, j: (l, 0, j)),
        out_shape=jax.ShapeDtypeStruct((DEPTH, nb, 6 * D_MODEL), F32),
        name="ada_mod",
    )(c_all, w_ada, b_ada.reshape(DEPTH, 1, 6 * D_MODEL))


def _chunk_attention(q, k, v, g, states, mall_ref, mask_ref, heads_per_tile, plan):
    w = q.shape[1]
    n_tiles = w // LANES
    g_hi = g.astype(BF16)
    r1 = g - g_hi.astype(F32)
    g_mid = r1.astype(BF16)
    g_lo = (r1 - g_mid.astype(F32)).astype(BF16)
    mall = mall_ref[...]
    cums = _dot(mall, g_hi) + _dot(mall, g_mid) + _dot(mall, g_lo)
    b = cums[plan.cum_rows - CHUNK:plan.cum_rows]
    level_q = []
    level_k = []
    for i in range(len(plan.levels)):
        level_q.append(q * jnp.exp(cums[2 * i * CHUNK:(2 * i + 1) * CHUNK]))
        level_k.append(k * jnp.exp(cums[(2 * i + 1) * CHUNK:(2 * i + 2) * CHUNK]))
    if plan.diag_block == 1:
        qs = [q]
        ks = [k]
    else:
        i = plan.levels.index(plan.diag_block)
        qs = [level_q[i]]
        ks = [k * jnp.exp(-cums[2 * i * CHUNK:(2 * i + 1) * CHUNK])]
    qs += level_q
    ks += level_k
    if plan.adjacent:
        qs.append(q * jnp.exp(g))
        ks.append(k)
    b_last = b[CHUNK - 1:CHUNK]
    q_in = q * jnp.exp(b)
    k_out = k * jnp.exp(b_last - b)
    e_last = jnp.exp(b_last)

    dk = LANES // heads_per_tile
    lane = lax.broadcasted_iota(jnp.int32, (CHUNK, LANES), 1)
    row = lax.broadcasted_iota(jnp.int32, (LANES, HEAD_DV), 0)
    outs = []
    new_states = []
    for ti in range(n_tiles):
        sl = slice(ti * LANES, (ti + 1) * LANES)
        ks_t =[kk[:, sl].astype(BF16) for kk in ks]
        k_out_t = k_out[:, sl].T.astype(BF16)
        e_col = jnp.broadcast_to(e_last[:, sl], (LANES, LANES)).T
        s_old = states[ti]
        s_old_b = s_old.astype(BF16)
        upd = None
        for j in range(heads_per_tile):
            head = ti * heads_per_tile + j
            if heads_per_tile == 1:
                sel = lambda a: a
            else:
                in_head = (lane // dk) == j
                sel = lambda a, in_head=in_head: jnp.where(in_head, a, 0.0)
            sc = jnp.zeros((CHUNK, CHUNK), F32)
            for i in range(plan.n_masks):
                sc = sc + _dot_nt(sel(qs[i][:, sl]).astype(BF16), ks_t[i]) * mask_ref[i]
            vh = v[:, head * HEAD_DV:(head + 1) * HEAD_DV].astype(BF16)
            o = _dot(sc.astype(BF16), vh) + _dot(sel(q_in[:, sl]).astype(BF16), s_old_b)
            outs.append(o)
            u = _dot(k_out_t, vh)
            upd = u if upd is None else jnp.where((row // dk) == j, u, upd)
        new_states.append(e_col * s_old + upd)
    return jnp.concatenate(outs, axis=1), new_states


FAST_BLOCK = 16
N_SUB = CHUNK // FAST_BLOCK
SLAB_ROWS = FAST_BLOCK * (N_SUB * (N_SUB - 1) // 2) + CHUNK
FAST_BLOCK_DECAY_LIMIT = 60.0


def _slab_mask():
    t = np.arange(CHUNK)[:, None]
    cols = []
    for i in range(1, N_SUB):
        cols.append(np.broadcast_to(t // FAST_BLOCK == i, (CHUNK, i * FAST_BLOCK)))
    s = np.arange(CHUNK)[None, :]
    cols.append((t // FAST_BLOCK == s // FAST_BLOCK) & (s <= t))
    return np.concatenate(cols, axis=1).astype(np.float32)


def _block_attention_fast(q, k, v, g, states, tril_ref, slab_mask_ref, heads_per_tile):
    rows, w = q.shape
    n_chunks = rows // CHUNK
    n_tiles = w // LANES
    dk = LANES // heads_per_tile
    n_heads = n_tiles * heads_per_tile

    g_hi = g.astype(BF16)
    r1 = g - g_hi.astype(F32)
    g_mid = r1.astype(BF16)
    g_lo = (r1 - g_mid.astype(F32)).astype(BF16)
    tril = tril_ref[...]
    b = _dot(tril, g_hi) + _dot(tril, g_mid) + _dot(tril, g_lo)

    def end_row(c, i):
        r = c * CHUNK + (i + 1) * FAST_BLOCK
        return b[r - 1:r]

    def per_block(row_of):
        return jnp.concatenate([jnp.broadcast_to(row_of(c, i), (FAST_BLOCK, w))
                                for c in range(n_chunks) for i in range(N_SUB)], axis=0)

    zero = jnp.zeros((1, w), F32)
    b_start = per_block(lambda c, i: zero if i == 0 else end_row(c, i - 1))
    b_end = per_block(end_row)
    q_blk = q * jnp.exp(b - b_start)
    k_diag = k * jnp.exp(b_start - b)
    k_end = k * jnp.exp(b_end - b)
    q_in = q_blk * jnp.exp(b_start)
    k_out = k_end * jnp.exp(per_block(lambda c, i: end_row(c, N_SUB - 1)) - b_end)

    lane = lax.broadcasted_iota(jnp.int32, (CHUNK, LANES), 1)
    row = lax.broadcasted_iota(jnp.int32, (LANES, HEAD_DV), 0)
    slab_mask = slab_mask_ref[...]

    def sel(a, j):
        return a if heads_per_tile == 1 else jnp.where((lane // dk) == j, a, 0.0)

    v_b = v.astype(BF16)

    scores = {}
    for c in range(n_chunks):
        r0 = c * CHUNK
        slabs = []
        for i in range(1, N_SUB):
            for jb in range(i):
                blk = k_end[r0 + jb * FAST_BLOCK:r0 + (jb + 1) * FAST_BLOCK]
                slabs.append(blk if jb == i - 1 else blk * jnp.exp(end_row(c, i - 1) - end_row(c, jb)))
        slabs.append(k_diag[r0:r0 + CHUNK])
        k_slab = jnp.concatenate(slabs, axis=0).astype(BF16)
        for ti in range(n_tiles):
            sl = slice(ti * LANES, (ti + 1) * LANES)
            for j in range(heads_per_tile):
                qh = sel(q_blk[r0:r0 + CHUNK, sl], j).astype(BF16)
                scores[c, ti * heads_per_tile + j] = (_dot_nt(qh, k_slab[:, sl]) * slab_mask).astype(BF16)

    entering = [list(states)]
    for c in range(n_chunks):
        r0 = c * CHUNK
        nxt = []
        for ti in range(n_tiles):
            sl = slice(ti * LANES, (ti + 1) * LANES)
            k_out_t = k_out[r0:r0 + CHUNK, sl].T.astype(BF16)
            upd = None
            for j in range(heads_per_tile):
                head = ti * heads_per_tile + j
                u = _dot(k_out_t, v_b[r0:r0 + CHUNK, head * HEAD_DV:(head + 1) * HEAD_DV])
                upd = u if upd is None else jnp.where((row // dk) == j, u, upd)
            e_col = jnp.broadcast_to(jnp.exp(end_row(c, N_SUB - 1)[:, sl]), (LANES, LANES)).T
            nxt.append(e_col * entering[c][ti] + upd)
        entering.append(nxt)

    out_rows = []
    for c in range(n_chunks):
        r0 = c * CHUNK
        outs = []
        for head in range(n_heads):
            ti, j = divmod(head, heads_per_tile)
            sl = slice(ti * LANES, (ti + 1) * LANES)
            vh = v_b[r0:r0 + CHUNK, head * HEAD_DV:(head + 1) * HEAD_DV]
            v_slab = jnp.concatenate([vh[:i * FAST_BLOCK] for i in range(1, N_SUB)] + [vh], axis=0)
            outs.append(_dot(scores[c, head], v_slab)
                        + _dot(sel(q_in[r0:r0 + CHUNK, sl], j).astype(BF16), entering[c][ti].astype(BF16)))
        out_rows.append(jnp.concatenate(outs, axis=1))
    return jnp.concatenate(out_rows, axis=0), entering[n_chunks]


def _head_norm_gate(o, gain, gate):
    outs = []
    for h in range(o.shape[1] // HEAD_DV):
        sl = slice(h * HEAD_DV, (h + 1) * HEAD_DV)
        oh = o[:, sl]
        oh = oh * lax.rsqrt(jnp.mean(oh * oh, axis=-1, keepdims=True) + NORM_EPS) * gain
        outs.append(oh * _silu(gate[:, sl]))
    return jnp.concatenate(outs, axis=1)


def _mixer_kernel(x_ref, mod_ref, nrm_ref, win_ref, lb_ref, wgk2_ref, bgk_ref, hgn_ref, glan_ref,
                  wa_ref, wb_ref, wo_ref, shg0_ref, sgla0_ref,
                  tril_ref, slab_mask_ref, mall_safe_ref, mask_safe_ref,
                  xo_ref, shg_o_ref, sgla_o_ref,
                  p_scr, k_scr, lg_scr, shg_scr, sgla_scr, *, layer, tb):
    j = pl.program_id(1)

    @pl.when(j == 0)
    def _():
        shg_scr[...] = shg0_ref[0]
        sgla_scr[...] = sgla0_ref[0]

    x = x_ref[0]
    sh1 = mod_ref[0, 0:1, :]
    sc1 = mod_ref[0, 1:2, :]
    g1 = mod_ref[0, 2:3, :]
    hb = _rms_mod(x, nrm_ref[...], sc1, sh1).astype(BF16)
    for c in range(0, IN_COLS_PAD, PROJ_TILE):
        c1 = min(c + PROJ_TILE, IN_COLS_PAD)
        p_scr[:, c:c1] = _dot(hb, win_ref[:, c:c1])

    lb_all = lb_ref[...]
    lb_max = jnp.max(lb_all, axis=0, keepdims=True)
    lb_exp = jnp.exp(lb_all - lb_max)
    sm = lb_exp / jnp.sum(lb_exp, axis=0, keepdims=True)
    lbl = jnp.clip(jnp.sum(sm[0:layer + 1], axis=0, keepdims=True) - sm[0:1], 0.0, 1.0)

    p_scr[:, C_HQ:C_HQ + HG_KW] = _silu(p_scr[:, C_HQ:C_HQ + HG_KW]) * (HG_DK ** -0.5)
    z = p_scr[:, C_HF:C_HF + HG_KW]
    f = lbl + (1.0 - lbl) * _sigmoid(z)
    p_scr[:, C_HF:C_HF + HG_KW] = jnp.log(jnp.maximum(f, LOG_FLOOR))
    k_scr[...] = (1.0 - lbl) * _sigmoid(-z)
    glr = p_scr[:, C_GLR:C_GLR + LANES].astype(BF16)
    gate = _dot(glr, wgk2_ref[...]) + bgk_ref[...]
    lg_scr[...] = (jnp.minimum(gate, 0.0) - jnp.log1p(jnp.exp(-jnp.abs(gate)))) * (1.0 / GLA_GATE_NORM)
    p_scr[:, C_GQ:C_GQ + GLA_KW] = p_scr[:, C_GQ:C_GQ + GLA_KW] * (GLA_DK ** -0.5)

    n_hg_tiles = HG_KW // LANES
    n_gla_tiles = GLA_KW // LANES

    def one_chunk(rows, states, attend):
        o_hg, st_hg = attend(p_scr[rows, C_HQ:C_HQ + HG_KW], k_scr[rows, :],
                             p_scr[rows, C_HI:C_HI + HG_W], p_scr[rows, C_HF:C_HF + HG_KW],
                             states[:n_hg_tiles], 1)
        p_scr[rows, C_HI:C_HI + HG_W] = o_hg
        o_gla, st_gla = attend(p_scr[rows, C_GQ:C_GQ + GLA_KW], p_scr[rows, C_GK:C_GK + GLA_KW],
                               p_scr[rows, C_GV:C_GV + GLA_W], lg_scr[rows, :],
                               states[n_hg_tiles:], 2)
        p_scr[rows, C_GV:C_GV + GLA_W] = o_gla
        return st_hg + st_gla

    def attend_safe(q, k, v, g, states, heads_per_tile):
        return _chunk_attention(q, k, v, g, states, mall_safe_ref, mask_safe_ref, heads_per_tile, SAFE_PLAN)

    def load_states():
        return [shg_scr[t] for t in range(n_hg_tiles)] + [sgla_scr[t] for t in range(n_gla_tiles)]

    def store_states(states):
        for t in range(n_hg_tiles):
            shg_scr[t] = states[t]
        for t in range(n_gla_tiles):
            sgla_scr[t] = states[n_hg_tiles + t]

    def run_block_fast():
        states = load_states()
        o_hg, st_hg = _block_attention_fast(
            p_scr[:, C_HQ:C_HQ + HG_KW], k_scr[...], p_scr[:, C_HI:C_HI + HG_W],
            p_scr[:, C_HF:C_HF + HG_KW], states[:n_hg_tiles], tril_ref, slab_mask_ref, 1)
        p_scr[:, C_HI:C_HI + HG_W] = o_hg
        o_gla, st_gla = _block_attention_fast(
            p_scr[:, C_GQ:C_GQ + GLA_KW], p_scr[:, C_GK:C_GK + GLA_KW], p_scr[:, C_GV:C_GV + GLA_W],
            lg_scr[...], states[n_hg_tiles:], tril_ref, slab_mask_ref, 2)
        p_scr[:, C_GV:C_GV + GLA_W] = o_gla
        store_states(st_hg + st_gla)

    def run_chunks_safe():
        def chunk_body(ci, carry):
            rows = pl.ds(pl.multiple_of(ci * CHUNK, CHUNK), CHUNK)
            store_states(one_chunk(rows, load_states(), attend_safe))
            return carry

        lax.fori_loop(0, tb // CHUNK, chunk_body, 0)

    blk = FAST_BLOCK
    min_hg = jnp.min(jnp.sum(p_scr[:, C_HF:C_HF + HG_KW].reshape(tb // blk, blk, HG_KW), axis=1))
    min_gla = jnp.min(jnp.sum(lg_scr[...].reshape(tb // blk, blk, GLA_KW), axis=1))
    bounded = jnp.minimum(min_hg, min_gla) >= -FAST_BLOCK_DECAY_LIMIT

    @pl.when(bounded)
    def _():
        run_block_fast()

    @pl.when(jnp.logical_not(bounded))
    def _():
        run_chunks_safe()

    o_hg = _head_norm_gate(p_scr[:, C_HI:C_HI + HG_W], hgn_ref[...], p_scr[:, C_HOG:C_HOG + HG_W])
    o_gla = _head_norm_gate(p_scr[:, C_GV:C_GV + GLA_W], glan_ref[...], p_scr[:, C_GOG:C_GOG + GLA_W])
    ya = _dot(o_hg.astype(BF16), wa_ref[...])
    yb = _dot(o_gla.astype(BF16), wb_ref[...])
    merged = (_sigmoid(p_scr[:, C_GA:C_GA + D_MODEL]) * ya
              + _sigmoid(p_scr[:, C_GB:C_GB + D_MODEL]) * yb)
    m = _dot(merged.astype(BF16), wo_ref[...])
    xo_ref[0] = x + g1 * m

    @pl.when(j == pl.num_programs(1) - 1)
    def _():
        shg_o_ref[0] = shg_scr[...]
        sgla_o_ref[0] = sgla_scr[...]


def _const_spec(shape):
    nd = len(shape)
    return pl.BlockSpec(shape, lambda b, j, nd=nd: (0,) * nd, pipeline_mode=pl.Buffered(1))


def _mixer(x, mod, nrm, win, hg_lb, wgk2, bgk, hgn, glan, wa, wb, wo, shg0, sgla0, plan_consts,
           *, layer, tb):
    bsz, seq, _ = x.shape
    kern = functools.partial(_mixer_kernel, layer=layer, tb=tb)
    n_gla_tiles = GLA_KW // LANES
    return pl.pallas_call(
        kern,
        grid=(bsz, seq // tb),
        in_specs=[
            pl.BlockSpec((1, tb, D_MODEL), lambda b, j: (b, j, 0)),
            pl.BlockSpec((1, 6, D_MODEL), lambda b, j: (b, 0, 0)),
            _const_spec((1, D_MODEL)),
            _const_spec((D_MODEL, IN_COLS_PAD)),
            _const_spec((DEPTH, HG_KW)),
            _const_spec((LANES, GLA_KW)),
            _const_spec((1, GLA_KW)),
            _const_spec((1, HEAD_DV)),
            _const_spec((1, HEAD_DV)),
            _const_spec((HG_W, D_MODEL)),
            _const_spec((GLA_W, D_MODEL)),
            _const_spec((D_MODEL, D_MODEL)),
            pl.BlockSpec((1, HG_HEADS, HG_DK, HEAD_DV), lambda b, j: (b, 0, 0, 0)),
            pl.BlockSpec((1, n_gla_tiles, LANES, HEAD_DV), lambda b, j: (b, 0, 0, 0)),
            _const_spec((tb, tb)),
            _const_spec((CHUNK, SLAB_ROWS)),
            _const_spec((SAFE_PLAN.cum_rows, CHUNK)),
            _const_spec((SAFE_PLAN.n_masks, CHUNK, CHUNK)),
        ],
        out_specs=[
            pl.BlockSpec((1, tb, D_MODEL), lambda b, j: (b, j, 0)),
            pl.BlockSpec((1, HG_HEADS, HG_DK, HEAD_DV), lambda b, j: (b, 0, 0, 0)),
            pl.BlockSpec((1, n_gla_tiles, LANES, HEAD_DV), lambda b, j: (b, 0, 0, 0)),
        ],
        out_shape=[
            jax.ShapeDtypeStruct((bsz, seq, D_MODEL), F32),
            jax.ShapeDtypeStruct((bsz, HG_HEADS, HG_DK, HEAD_DV), F32),
            jax.ShapeDtypeStruct((bsz, n_gla_tiles, LANES, HEAD_DV), F32),
        ],
        scratch_shapes=[
            pltpu.VMEM((tb, IN_COLS_PAD), F32),
            pltpu.VMEM((tb, HG_KW), F32),
            pltpu.VMEM((tb, GLA_KW), F32),
            pltpu.VMEM((HG_HEADS, HG_DK, HEAD_DV), F32),
            pltpu.VMEM((n_gla_tiles, LANES, HEAD_DV), F32),
        ],
        compiler_params=pltpu.CompilerParams(
            dimension_semantics=("arbitrary", "arbitrary"), vmem_limit_bytes=VMEM_LIMIT),
        name=f"mixer_l{layer}",
    )(x, mod, nrm, win, hg_lb, wgk2, bgk, hgn, glan, wa, wb, wo, shg0, sgla0, *plan_consts)


ROUTER_ROWS = 8 + N_EXPERTS
MOE_TILE = 512
TILE_ASSIGN = TOPK * MOE_TILE
EXPERT_BLOCK = 512


def _first_argmax_rows(vals, n):
    ridx = lax.broadcasted_iota(jnp.int32, vals.shape, 0)
    vmax = jnp.max(vals, axis=0, keepdims=True)
    imax = jnp.min(jnp.where(vals == vmax, ridx, n), axis=0, keepdims=True)
    return vmax, imax


def _router_kernel(x_ref, mod_ref, nrm_ref, wr_ref, br_ref, tri_ref,
                   h_ref, eid_ref, rank_ref, wts_ref, cnt_ref, run_scr):
    @pl.when(pl.program_id(0) == 0)
    def _():
        run_scr[...] = jnp.zeros_like(run_scr)

    u, lt, _ = x_ref.shape
    x = x_ref[...]
    sh2 = mod_ref[:, 3:4, :]
    sc2 = mod_ref[:, 4:5, :]
    h = _rms_mod(x, nrm_ref[...].reshape(1, 1, D_MODEL), sc2, sh2).reshape(u * lt, D_MODEL)
    h_ref[...] = h
    logits = lax.dot_general(wr_ref[...], h, (((1,), (1,)), ((), ())), preferred_element_type=F32,
                             precision=lax.Precision.HIGHEST) + br_ref[...]
    gl = logits[0:N_GROUPS]
    gmax, gi = _first_argmax_rows(gl, N_GROUPS)
    gp = 1.0 / jnp.sum(jnp.exp(gl - gmax), axis=0, keepdims=True)
    le = logits[8:8 + EXPERTS_PER_GROUP]
    for g in range(1, N_GROUPS):
        le = jnp.where(gi == g, logits[8 + g * EXPERTS_PER_GROUP:8 + (g + 1) * EXPERTS_PER_GROUP], le)
    pe = jnp.exp(le - jnp.max(le, axis=0, keepdims=True))
    pe = pe / jnp.sum(pe, axis=0, keepdims=True)
    v1, i1 = _first_argmax_rows(pe, EXPERTS_PER_GROUP)
    ridx = lax.broadcasted_iota(jnp.int32, pe.shape, 0)
    v2, i2 = _first_argmax_rows(jnp.where(ridx == i1, -1.0, pe), EXPERTS_PER_GROUP)
    vsum = v1 + v2
    wts_ref[0:1, :] = gp * v1 / vsum
    wts_ref[1:2, :] = gp * v2 / vsum
    eflat = jnp.concatenate([gi * EXPERTS_PER_GROUP + i1, gi * EXPERTS_PER_GROUP + i2], axis=1)
    eid_ref[0] = eflat
    onehot = (eflat == lax.broadcasted_iota(jnp.int32, (N_EXPERTS, TILE_ASSIGN), 0)).astype(F32)
    before = _dot(onehot.astype(BF16), tri_ref[...]) + run_scr[...]
    rank_ref[0] = jnp.sum(onehot * before, axis=0, keepdims=True).astype(jnp.int32)
    run_scr[...] = run_scr[...] + jnp.sum(onehot, axis=1, keepdims=True)
    cnt_ref[...] = run_scr[...].astype(jnp.int32)


def _router(x_units, mod_units, nrm, wr, br, tri):
    n_units, lt, _ = x_units.shape
    u = MOE_TILE // lt
    n_tiles = n_units // u
    return pl.pallas_call(
        _router_kernel,
        grid=(n_tiles,),
        in_specs=[
            pl.BlockSpec((u, lt, D_MODEL), lambda i: (i, 0, 0)),
            pl.BlockSpec((u, 6, D_MODEL), lambda i: (i, 0, 0)),
            pl.BlockSpec((1, D_MODEL), lambda i: (0, 0)),
            pl.BlockSpec((ROUTER_ROWS, D_MODEL), lambda i: (0, 0)),
            pl.BlockSpec((ROUTER_ROWS, 1), lambda i: (0, 0)),
            pl.BlockSpec((TILE_ASSIGN, TILE_ASSIGN), lambda i: (0, 0)),
        ],
        out_specs=[
            pl.BlockSpec((MOE_TILE, D_MODEL), lambda i: (i, 0)),
            pl.BlockSpec((1, 1, TILE_ASSIGN), lambda i: (i, 0, 0)),
            pl.BlockSpec((1, 1, TILE_ASSIGN), lambda i: (i, 0, 0)),
            pl.BlockSpec((TOPK, MOE_TILE), lambda i: (0, i)),
            pl.BlockSpec((N_EXPERTS, 1), lambda i: (0, 0)),
        ],
        out_shape=[
            jax.ShapeDtypeStruct((n_tiles * MOE_TILE, D_MODEL), F32),
            jax.ShapeDtypeStruct((n_tiles, 1, TILE_ASSIGN), jnp.int32),
            jax.ShapeDtypeStruct((n_tiles, 1, TILE_ASSIGN), jnp.int32),
            jax.ShapeDtypeStruct((TOPK, n_tiles * MOE_TILE), F32),
            jax.ShapeDtypeStruct((N_EXPERTS, 1), jnp.int32),
        ],
        scratch_shapes=[pltpu.VMEM((N_EXPERTS, 1), F32)],
        compiler_params=pltpu.CompilerParams(dimension_semantics=("arbitrary",)),
        name="moe_router",
    )(x_units, mod_units, nrm, wr, br, tri)


def _start_row_gather(idx_ref, n_rows, src_hbm, dst, sem):
    def body(r, carry):
        row = idx_ref[0, 0, r]
        pltpu.make_async_copy(src_hbm.at[pl.ds(row, 1)], dst.at[pl.ds(r, 1)], sem).start()
        return carry
    lax.fori_loop(0, n_rows, body, 0, unroll=8)


def _wait_row_gather(n_rows, src_hbm, dst, sem):
    pltpu.make_async_copy(src_hbm.at[pl.ds(0, n_rows)], dst, sem).wait()


def _dispatch_kernel(pend_ref, padded_ref, dest_ref, h_ref, xs_hbm, zbuf, sem):
    n_blocks = xs_hbm.shape[0] // EXPERT_BLOCK

    def zero_block(first_row):
        return pltpu.make_async_copy(
            zbuf, xs_hbm.at[pl.ds(pl.multiple_of(first_row, EXPERT_BLOCK), EXPERT_BLOCK)], sem.at[0])

    @pl.when(pl.program_id(0) == 0)
    def _():
        zbuf[...] = jnp.zeros_like(zbuf)
        n_used = pend_ref[N_EXPERTS - 1] // EXPERT_BLOCK
        for e in range(N_EXPERTS):
            @pl.when(padded_ref[e] > 0)
            def _():
                zero_block(pend_ref[e] - EXPERT_BLOCK).start()
        lax.fori_loop(n_used, n_blocks, lambda b, c: (zero_block(b * EXPERT_BLOCK).start(), c)[1], 0)
        for e in range(N_EXPERTS):
            @pl.when(padded_ref[e] > 0)
            def _():
                zero_block(pend_ref[e] - EXPERT_BLOCK).wait()
        lax.fori_loop(n_used, n_blocks, lambda b, c: (zero_block(b * EXPERT_BLOCK).wait(), c)[1], 0)

    def body(t, carry):
        for k in range(TOPK):
            slot = dest_ref[0, 0, k * MOE_TILE + t]
            pltpu.make_async_copy(h_ref.at[pl.ds(t, 1)], xs_hbm.at[pl.ds(slot, 1)], sem.at[1]).start()
        return carry
    lax.fori_loop(0, MOE_TILE, body, 0, unroll=8)
    for k in range(TOPK):
        pltpu.make_async_copy(h_ref, xs_hbm.at[pl.ds(0, MOE_TILE)], sem.at[1]).wait()


def _dispatch(pad_end, padded, dest_tiles, h, n_slots):
    n_tiles = dest_tiles.shape[0]
    grid_spec = pltpu.PrefetchScalarGridSpec(
        num_scalar_prefetch=2,
        grid=(n_tiles,),
        in_specs=[
            pl.BlockSpec((1, 1, TILE_ASSIGN), lambda i, pe, pd: (i, 0, 0), memory_space=pltpu.SMEM),
            pl.BlockSpec((MOE_TILE, D_MODEL), lambda i, pe, pd: (i, 0)),
        ],
        out_specs=pl.BlockSpec(memory_space=pl.ANY),
        scratch_shapes=[pltpu.VMEM((EXPERT_BLOCK, D_MODEL), F32), pltpu.SemaphoreType.DMA((2,))],
    )
    return pl.pallas_call(
        _dispatch_kernel,
        grid_spec=grid_spec,
        out_shape=jax.ShapeDtypeStruct((n_slots, D_MODEL), F32),
        compiler_params=pltpu.CompilerParams(dimension_semantics=("arbitrary",)),
        name="moe_dispatch",
    )(pad_end, padded, dest_tiles, h)


def _experts_kernel(be_ref, nused_ref, x_ref, wg_ref, wu_ref, wd_ref, o_ref):
    @pl.when(pl.program_id(0) < nused_ref[0])
    def _():
        xb = x_ref[...].astype(BF16)
        a = _silu(_dot(xb, wg_ref[0])) * _dot(xb, wu_ref[0])
        o_ref[...] = _dot(a.astype(BF16), wd_ref[0])

    @pl.when(pl.program_id(0) >= nused_ref[0])
    def _():
        o_ref[...] = jnp.zeros_like(o_ref)


def _experts(block_e, n_used, xs, wg, wu, wd):
    n_blocks = xs.shape[0] // EXPERT_BLOCK

    def row_block(i, be, nu):
        return (jnp.minimum(i, nu[0] - 1), 0)

    def expert_block(i, be, nu):
        return (be[jnp.minimum(i, nu[0] - 1)], 0, 0)

    grid_spec = pltpu.PrefetchScalarGridSpec(
        num_scalar_prefetch=2,
        grid=(n_blocks,),
        in_specs=[
            pl.BlockSpec((EXPERT_BLOCK, D_MODEL), row_block),
            pl.BlockSpec((1, D_MODEL, D_EXPERT), expert_block),
            pl.BlockSpec((1, D_MODEL, D_EXPERT), expert_block),
            pl.BlockSpec((1, D_EXPERT, D_MODEL), expert_block),
        ],
        out_specs=pl.BlockSpec((EXPERT_BLOCK, D_MODEL), lambda i, be, nu: (i, 0)),
    )
    return pl.pallas_call(
        _experts_kernel,
        grid_spec=grid_spec,
        out_shape=jax.ShapeDtypeStruct(xs.shape, F32),
        compiler_params=pltpu.CompilerParams(
            dimension_semantics=("arbitrary",), vmem_limit_bytes=VMEM_LIMIT),
        name="moe_experts",
    )(block_e, n_used, xs, wg, wu, wd)


def _combine_kernel(dst_cur_ref, dst_nxt_ref, x_ref, mod_ref, wts_ref, nrm_ref, y_hbm, o_ref, buf, sem,
                    *, final_norm):
    i = pl.program_id(0)
    n = pl.num_programs(0)
    slot = i % 2

    @pl.when(i == 0)
    def _():
        _start_row_gather(dst_cur_ref, TILE_ASSIGN, y_hbm, buf.at[0], sem.at[0])

    @pl.when(i + 1 < n)
    def _():
        _start_row_gather(dst_nxt_ref, TILE_ASSIGN, y_hbm, buf.at[1 - slot], sem.at[1 - slot])

    _wait_row_gather(TILE_ASSIGN, y_hbm, buf.at[slot], sem.at[slot])
    u, lt, _ = x_ref.shape
    y = (wts_ref[:, 0:1] * buf[slot, 0:MOE_TILE, :] + wts_ref[:, 1:2] * buf[slot, MOE_TILE:TILE_ASSIGN, :])
    g2 = mod_ref[:, 5:6, :]
    out = x_ref[...] + g2 * y.reshape(u, lt, D_MODEL)
    if final_norm:
        out = out * lax.rsqrt(jnp.mean(out * out, axis=-1, keepdims=True) + NORM_EPS)
        out = out * nrm_ref[...].reshape(1, 1, D_MODEL)
    o_ref[...] = out


def _combine(dest_tiles, x_units, mod_units, wts_col, nrm, y_slots, *, final_norm):
    n_units, lt, _ = x_units.shape
    u = MOE_TILE // lt
    n_tiles = n_units // u
    return pl.pallas_call(
        functools.partial(_combine_kernel, final_norm=final_norm),
        grid=(n_tiles,),
        in_specs=[
            pl.BlockSpec((1, 1, TILE_ASSIGN), lambda i: (i, 0, 0), memory_space=pltpu.SMEM),
            pl.BlockSpec((1, 1, TILE_ASSIGN), lambda i: (jnp.minimum(i + 1, n_tiles - 1), 0, 0),
                         memory_space=pltpu.SMEM),
            pl.BlockSpec((u, lt, D_MODEL), lambda i: (i, 0, 0)),
            pl.BlockSpec((u, 6, D_MODEL), lambda i: (i, 0, 0)),
            pl.BlockSpec((MOE_TILE, TOPK), lambda i: (i, 0)),
            pl.BlockSpec((1, D_MODEL), lambda i: (0, 0)),
            pl.BlockSpec(memory_space=pl.ANY),
        ],
        out_specs=pl.BlockSpec((u, lt, D_MODEL), lambda i: (i, 0, 0)),
        out_shape=jax.ShapeDtypeStruct(x_units.shape, F32),
        scratch_shapes=[pltpu.VMEM((2, TILE_ASSIGN, D_MODEL), F32), pltpu.SemaphoreType.DMA((2,))],
        compiler_params=pltpu.CompilerParams(
            dimension_semantics=("arbitrary",), vmem_limit_bytes=VMEM_LIMIT),
        name="moe_combine",
    )(dest_tiles, dest_tiles, x_units, mod_units, wts_col, nrm, y_slots)


def _routing_tables(eid_tiles, rank_tiles, counts):
    n_blocks = eid_tiles.size // EXPERT_BLOCK + N_EXPERTS
    padded = (counts + EXPERT_BLOCK - 1) // EXPERT_BLOCK * EXPERT_BLOCK
    pad_end = jnp.cumsum(padded).astype(jnp.int32)
    pad_start = pad_end - padded
    block_start = jnp.arange(n_blocks, dtype=jnp.int32)[:, None] * EXPERT_BLOCK
    block_e = jnp.minimum(jnp.sum((block_start >= pad_end[None, :]).astype(jnp.int32), axis=1),
                          N_EXPERTS - 1).astype(jnp.int32)
    n_used = pad_end[-1:] // EXPERT_BLOCK
    experts = jnp.arange(N_EXPERTS, dtype=jnp.int32)
    first_slot = jnp.sum(jnp.where(eid_tiles[..., None] == experts, pad_start, 0), axis=-1)
    return block_e, n_used, pad_end, padded, first_slot + rank_tiles


def _moe_layer(x, mod_l, nrm_ffn, wr, br, tri, wg, wu, wd, nrm_final, *, final_norm):
    bsz, seq, _ = x.shape
    t = bsz * seq
    lt = min(seq, MOE_TILE)
    per = seq // lt
    x_units = x.reshape(t // lt, lt, D_MODEL)
    mod_units = jnp.repeat(mod_l, per, axis=0) if per > 1 else mod_l
    h, eid_tiles, rank_tiles, wts, counts = _router(x_units, mod_units, nrm_ffn, wr, br, tri)
    block_e, n_used, pad_end, padded, dest_tiles = _routing_tables(eid_tiles, rank_tiles, counts[:, 0])
    n_slots = block_e.shape[0] * EXPERT_BLOCK
    xs = _dispatch(pad_end, padded, dest_tiles, h, n_slots)
    y_slots = _experts(block_e, n_used, xs, wg, wu, wd)
    out = _combine(dest_tiles, x_units, mod_units, wts.T, nrm_final, y_slots, final_norm=final_norm)
    return out.reshape(bsz, seq, D_MODEL)


def kernel(x_prompt, x_sample, c_prompt, c_sample, state_hgrn, state_gla, w_ada, b_ada, norm_mix,
           norm_ffn, w_in, hg_lb, hg_onorm, w_gk2, b_gk, gla_onorm, w_br_a, w_br_b, w_out, w_rg, b_rg,
           w_re, b_re, w_e_gate, w_e_up, w_e_down, norm_final):
    bp = x_prompt.shape[0]
    bs = x_sample.shape[0]
    mod = _ada_mod(jnp.concatenate([c_prompt, c_sample], axis=0), w_ada, b_ada)
    mod = mod.reshape(DEPTH, bp + bs, 6, D_MODEL)

    glr0 = C_GOG + GLA_W
    win_r = jnp.concatenate(
        [w_in[:, :, :glr0], w_in[:, :, glr0 + GLA_GATE_RANK:], w_in[:, :, glr0:glr0 + GLA_GATE_RANK],
         jnp.zeros((DEPTH, D_MODEL, LANES - GLA_GATE_RANK), F32)], axis=2).astype(BF16)
    wgk2_p = jnp.concatenate(
        [w_gk2, jnp.zeros((DEPTH, LANES - GLA_GATE_RANK, GLA_KW), F32)], axis=1).astype(BF16)
    wa_b = w_br_a.astype(BF16)
    wb_b = w_br_b.astype(BF16)
    wo_b = w_out.astype(BF16)
    def plan_consts(tb):
        r = np.arange(tb)
        chunk_tril = (r[:, None] // CHUNK == r[None, :] // CHUNK) & (r[None, :] <= r[:, None])
        return [jnp.asarray(chunk_tril, BF16), jnp.asarray(_slab_mask(), F32),
                jnp.asarray(SAFE_PLAN.segment_sum_matrix(), BF16), jnp.asarray(SAFE_PLAN.masks(), F32)]
    zpad = jnp.zeros((DEPTH, 8 - N_GROUPS, D_MODEL), F32)
    wr = jnp.concatenate([jnp.swapaxes(w_rg, 1, 2), zpad, jnp.swapaxes(w_re, 1, 2)], axis=1)
    br = jnp.concatenate([b_rg, jnp.zeros((DEPTH, 8 - N_GROUPS), F32), b_re], axis=1)[:, :, None]
    wg_b = w_e_gate.astype(BF16)
    wu_b = w_e_up.astype(BF16)
    wd_b = w_e_down.astype(BF16)
    nrm_f = norm_final.reshape(1, D_MODEL)
    assign = np.arange(TILE_ASSIGN)
    tri = jnp.asarray(assign[:, None] < assign[None, :], BF16)

    def run(x, mod_g, shg, sgla, tb):
        bsz = x.shape[0]
        new_hg, new_gla = [], []
        for l in range(DEPTH):
            x, s1, s2 = _mixer(
                x, mod_g[l], norm_mix[l:l + 1], win_r[l], hg_lb, wgk2_p[l], b_gk[l:l + 1],
                hg_onorm[l:l + 1], gla_onorm[l:l + 1], wa_b[l], wb_b[l], wo_b[l],
                shg[l], sgla[l].reshape(bsz, GLA_KW // LANES, LANES, HEAD_DV), plan_consts(tb),
                layer=l, tb=tb)
            new_hg.append(s1)
            new_gla.append(s2.reshape(bsz, GLA_HEADS, GLA_DK, HEAD_DV))
            x = _moe_layer(x, mod_g[l], norm_ffn[l:l + 1], wr[l], br[l], tri, wg_b[l], wu_b[l], wd_b[l],
                           nrm_f, final_norm=(l == DEPTH - 1))
        return x, jnp.stack(new_hg), jnp.stack(new_gla)

    zeros_hg = jnp.zeros((DEPTH, bp, HG_HEADS, HG_DK, HEAD_DV), F32)
    zeros_gla = jnp.zeros((DEPTH, bp, GLA_HEADS, GLA_DK, HEAD_DV), F32)
    y_p, hg_p, gla_p = run(x_prompt, mod[:, :bp], zeros_hg, zeros_gla, 256)
    y_s, hg_s, gla_s = run(x_sample, mod[:, bp:], state_hgrn, state_gla, CHUNK)
    return (y_p, y_s, hg_p, gla_p, hg_s, gla_s)
```

```python
import functools

import numpy as np
import jax
import jax.numpy as jnp
from jax import lax
from jax.experimental import pallas as pl
from jax.experimental.pallas import tpu as pltpu

F32 = jnp.float32
BF16 = jnp.bfloat16

D_MODEL = 1024
DEPTH = 2
CHUNK = 64
NORM_EPS = 1e-6
LOG_FLOOR = 1e-30
HG_HEADS = 4
HG_DK = 128
HEAD_DV = 128
HG_KW = HG_HEADS * HG_DK
HG_W = HG_HEADS * HEAD_DV
GLA_HEADS = 4
GLA_DK = 64
GLA_KW = GLA_HEADS * GLA_DK
GLA_W = GLA_HEADS * HEAD_DV
GLA_GATE_RANK = 16
GLA_GATE_NORM = 16.0
N_GROUPS = 4
EXPERTS_PER_GROUP = 8
N_EXPERTS = N_GROUPS * EXPERTS_PER_GROUP
TOPK = 2
D_EXPERT = 512

LANES = 128
VMEM_LIMIT = 56 * 1024 * 1024

C_HQ = 0
C_HF = C_HQ + HG_KW
C_HI = C_HF + HG_KW
C_HOG = C_HI + HG_W
C_GQ = C_HOG + HG_W
C_GK = C_GQ + GLA_KW
C_GV = C_GK + GLA_KW
C_GOG = C_GV + GLA_W
C_GA = C_GOG + GLA_W
C_GB = C_GA + D_MODEL
C_GLR = C_GB + D_MODEL
IN_COLS_PAD = C_GLR + LANES
MXU_WIDTH = 256
PROJ_TILE = 4 * MXU_WIDTH

class _ScorePlan:
    def __init__(self, levels, adjacent, diag_block):
        self.levels = levels
        self.adjacent = adjacent
        self.diag_block = diag_block
        self.cum_rows = (2 * len(levels) + 1) * CHUNK
        self.n_masks = len(levels) + 1 + int(adjacent)

    def segment_sum_matrix(self):
        t = np.arange(CHUNK)[:, None]
        r = np.arange(CHUNK)[None, :]
        rows = []
        for m in self.levels:
            same = (t // m) == (r // m)
            rows.append(same & (r <= t))
            rows.append(same & (r > t))
        rows.append(r <= t)
        return np.concatenate(rows, axis=0).astype(np.float32)

    def masks(self):
        t = np.arange(CHUNK)[:, None]
        s = np.arange(CHUNK)[None, :]
        masks = [((t // self.diag_block) == (s // self.diag_block)) & (s <= t)]
        for m in self.levels + ((1,) if self.adjacent else ()):
            masks.append(((t // (2 * m)) == (s // (2 * m))) & ((t // m) % 2 == 1) & ((s // m) % 2 == 0))
        return np.stack(masks).astype(np.float32)


SAFE_PLAN = _ScorePlan((32, 16, 8, 4, 2), True, 1)


def _dot(a, b):
    return jnp.dot(a, b, preferred_element_type=F32)


def _dot_nt(a, b):
    return lax.dot_general(a, b, (((1,), (1,)), ((), ())), preferred_element_type=F32)


def _sigmoid(x):
    return 1.0 / (1.0 + jnp.exp(-x))


def _silu(x):
    return x * _sigmoid(x)


def _rms_mod(x, gain, scale, shift):
    y = x * lax.rsqrt(jnp.mean(x * x, axis=-1, keepdims=True) + NORM_EPS)
    return y * gain * (1.0 + scale) + shift


def _ada_kernel(c_ref, w_ref, b_ref, o_ref):
    c = c_ref[...]
    o_ref[0] = jnp.dot(_silu(c), w_ref[0], preferred_element_type=F32,
                       precision=lax.Precision.HIGHEST) + b_ref[0]


def _ada_mod(c_all, w_ada, b_ada):
    nb = c_all.shape[0]
    tn = 512
    return pl.pallas_call(
        _ada_kernel,
        grid=(DEPTH, 6 * D_MODEL // tn),
        in_specs=[
            pl.BlockSpec((nb, D_MODEL), lambda l, j: (0, 0)),
            pl.BlockSpec((1, D_MODEL, tn), lambda l, j: (l, 0, j)),
            pl.BlockSpec((1, 1, tn), lambda l, j: (l, 0, j)),
        ],
        out_specs=pl.BlockSpec((1, nb, tn), lambda l, j: (l, 0, j)),
        out_shape=jax.ShapeDtypeStruct((DEPTH, nb, 6 * D_MODEL), F32),
        name="ada_mod",
    )(c_all, w_ada, b_ada.reshape(DEPTH, 1, 6 * D_MODEL))


def _chunk_attention(q, k, v, g, states, mall_ref, mask_ref, heads_per_tile, plan):
    w = q.shape[1]
    n_tiles = w // LANES
    g_hi = g.astype(BF16)
    r1 = g - g_hi.astype(F32)
    g_mid = r1.astype(BF16)
    g_lo = (r1 - g_mid.astype(F32)).astype(BF16)
    mall = mall_ref[...]
    cums = _dot(mall, g_hi) + _dot(mall, g_mid) + _dot(mall, g_lo)
    b = cums[plan.cum_rows - CHUNK:plan.cum_rows]
    level_q = []
    level_k = []
    for i in range(len(plan.levels)):
        level_q.append(q * jnp.exp(cums[2 * i * CHUNK:(2 * i + 1) * CHUNK]))
        level_k.append(k * jnp.exp(cums[(2 * i + 1) * CHUNK:(2 * i + 2) * CHUNK]))
    if plan.diag_block == 1:
        qs = [q]
        ks = [k]
    else:
        i = plan.levels.index(plan.diag_block)
        qs = [level_q[i]]
        ks = [k * jnp.exp(-cums[2 * i * CHUNK:(2 * i + 1) * CHUNK])]
    qs += level_q
    ks += level_k
    if plan.adjacent:
        qs.append(q * jnp.exp(g))
        ks.append(k)
    b_last = b[CHUNK - 1:CHUNK]
    q_in = q * jnp.exp(b)
    k_out = k * jnp.exp(b_last - b)
    e_last = jnp.exp(b_last)

    dk = LANES // heads_per_tile
    lane = lax.broadcasted_iota(jnp.int32, (CHUNK, LANES), 1)
    row = lax.broadcasted_iota(jnp.int32, (LANES, HEAD_DV), 0)
    outs = []
    new_states = []
    for ti in range(n_tiles):
        sl = slice(ti * LANES, (ti + 1) * LANES)
        ks_t =[kk[:, sl].astype(BF16) for kk in ks]
        k_out_t = k_out[:, sl].T.astype(BF16)
        e_col = jnp.broadcast_to(e_last[:, sl], (LANES, LANES)).T
        s_old = states[ti]
        s_old_b = s_old.astype(BF16)
        upd = None
        for j in range(heads_per_tile):
            head = ti * heads_per_tile + j
            if heads_per_tile == 1:
                sel = lambda a: a
            else:
                in_head = (lane // dk) == j
                sel = lambda a, in_head=in_head: jnp.where(in_head, a, 0.0)
            sc = jnp.zeros((CHUNK, CHUNK), F32)
            for i in range(plan.n_masks):
                sc = sc + _dot_nt(sel(qs[i][:, sl]).astype(BF16), ks_t[i]) * mask_ref[i]
            vh = v[:, head * HEAD_DV:(head + 1) * HEAD_DV].astype(BF16)
            o = _dot(sc.astype(BF16), vh) + _dot(sel(q_in[:, sl]).astype(BF16), s_old_b)
            outs.append(o)
            u = _dot(k_out_t, vh)
            upd = u if upd is None else jnp.where((row // dk) == j, u, upd)
        new_states.append(e_col * s_old + upd)
    return jnp.concatenate(outs, axis=1), new_states


FAST_BLOCK = 16
N_SUB = CHUNK // FAST_BLOCK
SLAB_ROWS = FAST_BLOCK * (N_SUB * (N_SUB - 1) // 2) + CHUNK
FAST_BLOCK_DECAY_LIMIT = 60.0


def _slab_mask():
    t = np.arange(CHUNK)[:, None]
    cols = []
    for i in range(1, N_SUB):
        cols.append(np.broadcast_to(t // FAST_BLOCK == i, (CHUNK, i * FAST_BLOCK)))
    s = np.arange(CHUNK)[None, :]
    cols.append((t // FAST_BLOCK == s // FAST_BLOCK) & (s <= t))
    return np.concatenate(cols, axis=1).astype(np.float32)


def _block_attention_fast(q, k, v, g, states, tril_ref, slab_mask_ref, heads_per_tile):
    rows, w = q.shape
    n_chunks = rows // CHUNK
    n_tiles = w // LANES
    dk = LANES // heads_per_tile
    n_heads = n_tiles * heads_per_tile

    g_hi = g.astype(BF16)
    r1 = g - g_hi.astype(F32)
    g_mid = r1.astype(BF16)
    g_lo = (r1 - g_mid.astype(F32)).astype(BF16)
    tril = tril_ref[...]
    b = _dot(tril, g_hi) + _dot(tril, g_mid) + _dot(tril, g_lo)

    def end_row(c, i):
        r = c * CHUNK + (i + 1) * FAST_BLOCK
        return b[r - 1:r]

    def per_block(row_of):
        return jnp.concatenate([jnp.broadcast_to(row_of(c, i), (FAST_BLOCK, w))
                                for c in range(n_chunks) for i in range(N_SUB)], axis=0)

    zero = jnp.zeros((1, w), F32)
    b_start = per_block(lambda c, i: zero if i == 0 else end_row(c, i - 1))
    b_end = per_block(end_row)
    q_blk = q * jnp.exp(b - b_start)
    k_diag = k * jnp.exp(b_start - b)
    k_end = k * jnp.exp(b_end - b)
    q_in = q_blk * jnp.exp(b_start)
    k_out = k_end * jnp.exp(per_block(lambda c, i: end_row(c, N_SUB - 1)) - b_end)

    lane = lax.broadcasted_iota(jnp.int32, (CHUNK, LANES), 1)
    row = lax.broadcasted_iota(jnp.int32, (LANES, HEAD_DV), 0)
    slab_mask = slab_mask_ref[...]

    def sel(a, j):
        return a if heads_per_tile == 1 else jnp.where((lane // dk) == j, a, 0.0)

    v_b = v.astype(BF16)

    scores = {}
    for c in range(n_chunks):
        r0 = c * CHUNK
        slabs = []
        for i in range(1, N_SUB):
            for jb in range(i):
                blk = k_end[r0 + jb * FAST_BLOCK:r0 + (jb + 1) * FAST_BLOCK]
                slabs.append(blk if jb == i - 1 else blk * jnp.exp(end_row(c, i - 1) - end_row(c, jb)))
        slabs.append(k_diag[r0:r0 + CHUNK])
        k_slab = jnp.concatenate(slabs, axis=0).astype(BF16)
        for ti in range(n_tiles):
            sl = slice(ti * LANES, (ti + 1) * LANES)
            for j in range(heads_per_tile):
                qh = sel(q_blk[r0:r0 + CHUNK, sl], j).astype(BF16)
                scores[c, ti * heads_per_tile + j] = (_dot_nt(qh, k_slab[:, sl]) * slab_mask).astype(BF16)

    entering = [list(states)]
    for c in range(n_chunks):
        r0 = c * CHUNK
        nxt = []
        for ti in range(n_tiles):
            sl = slice(ti * LANES, (ti + 1) * LANES)
            k_out_t = k_out[r0:r0 + CHUNK, sl].T.astype(BF16)
            upd = None
            for j in range(heads_per_tile):
                head = ti * heads_per_tile + j
                u = _dot(k_out_t, v_b[r0:r0 + CHUNK, head * HEAD_DV:(head + 1) * HEAD_DV])
                upd = u if upd is None else jnp.where((row // dk) == j, u, upd)
            e_col = jnp.broadcast_to(jnp.exp(end_row(c, N_SUB - 1)[:, sl]), (LANES, LANES)).T
            nxt.append(e_col * entering[c][ti] + upd)
        entering.append(nxt)

    out_rows = []
    for c in range(n_chunks):
        r0 = c * CHUNK
        outs = []
        for head in range(n_heads):
            ti, j = divmod(head, heads_per_tile)
            sl = slice(ti * LANES, (ti + 1) * LANES)
            vh = v_b[r0:r0 + CHUNK, head * HEAD_DV:(head + 1) * HEAD_DV]
            v_slab = jnp.concatenate([vh[:i * FAST_BLOCK] for i in range(1, N_SUB)] + [vh], axis=0)
            outs.append(_dot(scores[c, head], v_slab)
                        + _dot(sel(q_in[r0:r0 + CHUNK, sl], j).astype(BF16), entering[c][ti].astype(BF16)))
        out_rows.append(jnp.concatenate(outs, axis=1))
    return jnp.concatenate(out_rows, axis=0), entering[n_chunks]


def _head_norm_gate(o, gain, gate):
    outs = []
    for h in range(o.shape[1] // HEAD_DV):
        sl = slice(h * HEAD_DV, (h + 1) * HEAD_DV)
        oh = o[:, sl]
        oh = oh * lax.rsqrt(jnp.mean(oh * oh, axis=-1, keepdims=True) + NORM_EPS) * gain
        outs.append(oh * _silu(gate[:, sl]))
    return jnp.concatenate(outs, axis=1)


def _mixer_kernel(x_ref, mod_ref, nrm_ref, win_ref, lb_ref, wgk2_ref, bgk_ref, hgn_ref, glan_ref,
                  wa_ref, wb_ref, wo_ref, shg0_ref, sgla0_ref,
                  tril_ref, slab_mask_ref, mall_safe_ref, mask_safe_ref, *rest, layer, tb, pending_moe):
    if pending_moe:
        dst_cur_ref, dst_nxt_ref, wts_ref, modp_ref, y_hbm = rest[:5]
        rest = rest[5:]
    xo_ref, shg_o_ref, sgla_o_ref, p_scr, k_scr, lg_scr, shg_scr, sgla_scr = rest[:8]
    j = pl.program_id(1)

    @pl.when(j == 0)
    def _():
        shg_scr[...] = shg0_ref[0]
        sgla_scr[...] = sgla0_ref[0]

    x = x_ref[0]
    n_prefetch = 0
    if pending_moe:
        cbuf, csem = rest[8:]
        n_prefetch = TOPK * tb
        step = pl.program_id(0) * pl.num_programs(1) + j
        last = pl.num_programs(0) * pl.num_programs(1) - 1
        slot = step % 2

        @pl.when(step == 0)
        def _():
            _start_row_gather(dst_cur_ref, n_prefetch, y_hbm, cbuf.at[0], csem.at[0])

        _wait_row_gather(n_prefetch, y_hbm, cbuf.at[slot], csem.at[slot])
        x = x + modp_ref[0, 5:6, :] * (wts_ref[:, 0:1] * cbuf[slot, 0:tb, :]
                                       + wts_ref[:, 1:2] * cbuf[slot, tb:n_prefetch, :])

    def prefetch_rows(r0, r1):
        for r in range(r0, r1):
            row = dst_nxt_ref[0, 0, r]
            pltpu.make_async_copy(y_hbm.at[pl.ds(row, 1)], cbuf.at[1 - slot, pl.ds(r, 1)],
                                  csem.at[1 - slot]).start()

    sh1 = mod_ref[0, 0:1, :]
    sc1 = mod_ref[0, 1:2, :]
    g1 = mod_ref[0, 2:3, :]
    hb = _rms_mod(x, nrm_ref[...], sc1, sh1).astype(BF16)
    col_tiles = list(range(0, IN_COLS_PAD, PROJ_TILE))
    for i, c in enumerate(col_tiles):
        c1 = min(c + PROJ_TILE, IN_COLS_PAD)
        p_scr[:, c:c1] = _dot(hb, win_ref[:, c:c1])
        if pending_moe:
            prefetch_rows(n_prefetch * i // len(col_tiles), n_prefetch * (i + 1) // len(col_tiles))

    lb_all = lb_ref[...]
    lb_max = jnp.max(lb_all, axis=0, keepdims=True)
    lb_exp = jnp.exp(lb_all - lb_max)
    sm = lb_exp / jnp.sum(lb_exp, axis=0, keepdims=True)
    lbl = jnp.clip(jnp.sum(sm[0:layer + 1], axis=0, keepdims=True) - sm[0:1], 0.0, 1.0)

    p_scr[:, C_HQ:C_HQ + HG_KW] = _silu(p_scr[:, C_HQ:C_HQ + HG_KW]) * (HG_DK ** -0.5)
    z = p_scr[:, C_HF:C_HF + HG_KW]
    f = lbl + (1.0 - lbl) * _sigmoid(z)
    p_scr[:, C_HF:C_HF + HG_KW] = jnp.log(jnp.maximum(f, LOG_FLOOR))
    k_scr[...] = (1.0 - lbl) * _sigmoid(-z)
    glr = p_scr[:, C_GLR:C_GLR + LANES].astype(BF16)
    gate = _dot(glr, wgk2_ref[...]) + bgk_ref[...]
    lg_scr[...] = (jnp.minimum(gate, 0.0) - jnp.log1p(jnp.exp(-jnp.abs(gate)))) * (1.0 / GLA_GATE_NORM)
    p_scr[:, C_GQ:C_GQ + GLA_KW] = p_scr[:, C_GQ:C_GQ + GLA_KW] * (GLA_DK ** -0.5)

    n_hg_tiles = HG_KW // LANES
    n_gla_tiles = GLA_KW // LANES

    def one_chunk(rows, states, attend):
        o_hg, st_hg = attend(p_scr[rows, C_HQ:C_HQ + HG_KW], k_scr[rows, :],
                             p_scr[rows, C_HI:C_HI + HG_W], p_scr[rows, C_HF:C_HF + HG_KW],
                             states[:n_hg_tiles], 1)
        p_scr[rows, C_HI:C_HI + HG_W] = o_hg
        o_gla, st_gla = attend(p_scr[rows, C_GQ:C_GQ + GLA_KW], p_scr[rows, C_GK:C_GK + GLA_KW],
                               p_scr[rows, C_GV:C_GV + GLA_W], lg_scr[rows, :],
                               states[n_hg_tiles:], 2)
        p_scr[rows, C_GV:C_GV + GLA_W] = o_gla
        return st_hg + st_gla

    def attend_safe(q, k, v, g, states, heads_per_tile):
        return _chunk_attention(q, k, v, g, states, mall_safe_ref, mask_safe_ref, heads_per_tile, SAFE_PLAN)

    def load_states():
        return [shg_scr[t] for t in range(n_hg_tiles)] + [sgla_scr[t] for t in range(n_gla_tiles)]

    def store_states(states):
        for t in range(n_hg_tiles):
            shg_scr[t] = states[t]
        for t in range(n_gla_tiles):
            sgla_scr[t] = states[n_hg_tiles + t]

    def run_block_fast():
        states = load_states()
        o_hg, st_hg = _block_attention_fast(
            p_scr[:, C_HQ:C_HQ + HG_KW], k_scr[...], p_scr[:, C_HI:C_HI + HG_W],
            p_scr[:, C_HF:C_HF + HG_KW], states[:n_hg_tiles], tril_ref, slab_mask_ref, 1)
        p_scr[:, C_HI:C_HI + HG_W] = o_hg
        o_gla, st_gla = _block_attention_fast(
            p_scr[:, C_GQ:C_GQ + GLA_KW], p_scr[:, C_GK:C_GK + GLA_KW], p_scr[:, C_GV:C_GV + GLA_W],
            lg_scr[...], states[n_hg_tiles:], tril_ref, slab_mask_ref, 2)
        p_scr[:, C_GV:C_GV + GLA_W] = o_gla
        store_states(st_hg + st_gla)

    def run_chunks_safe():
        def chunk_body(ci, carry):
            rows = pl.ds(pl.multiple_of(ci * CHUNK, CHUNK), CHUNK)
            store_states(one_chunk(rows, load_states(), attend_safe))
            return carry

        lax.fori_loop(0, tb // CHUNK, chunk_body, 0)

    blk = FAST_BLOCK
    min_hg = jnp.min(jnp.sum(p_scr[:, C_HF:C_HF + HG_KW].reshape(tb // blk, blk, HG_KW), axis=1))
    min_gla = jnp.min(jnp.sum(lg_scr[...].reshape(tb // blk, blk, GLA_KW), axis=1))
    bounded = jnp.minimum(min_hg, min_gla) >= -FAST_BLOCK_DECAY_LIMIT

    @pl.when(bounded)
    def _():
        run_block_fast()

    @pl.when(jnp.logical_not(bounded))
    def _():
        run_chunks_safe()

    o_hg = _head_norm_gate(p_scr[:, C_HI:C_HI + HG_W], hgn_ref[...], p_scr[:, C_HOG:C_HOG + HG_W])
    o_gla = _head_norm_gate(p_scr[:, C_GV:C_GV + GLA_W], glan_ref[...], p_scr[:, C_GOG:C_GOG + GLA_W])
    ya = _dot(o_hg.astype(BF16), wa_ref[...])
    yb = _dot(o_gla.astype(BF16), wb_ref[...])
    merged = (_sigmoid(p_scr[:, C_GA:C_GA + D_MODEL]) * ya
              + _sigmoid(p_scr[:, C_GB:C_GB + D_MODEL]) * yb)
    m = _dot(merged.astype(BF16), wo_ref[...])
    xo_ref[0] = x + g1 * m

    @pl.when(j == pl.num_programs(1) - 1)
    def _():
        shg_o_ref[0] = shg_scr[...]
        sgla_o_ref[0] = sgla_scr[...]

    if pending_moe:
        @pl.when(step == last)
        def _():
            _wait_row_gather(n_prefetch, y_hbm, cbuf.at[1 - slot], csem.at[1 - slot])


def _const_spec(shape):
    nd = len(shape)
    return pl.BlockSpec(shape, lambda b, j, nd=nd: (0,) * nd, pipeline_mode=pl.Buffered(1))


def _mixer(x, mod, nrm, win, hg_lb, wgk2, bgk, hgn, glan, wa, wb, wo, shg0, sgla0, plan_consts,
           *, layer, tb, pending_moe=None):
    bsz, seq, _ = x.shape
    nj = seq // tb
    kern = functools.partial(_mixer_kernel, layer=layer, tb=tb, pending_moe=pending_moe is not None)
    n_gla_tiles = GLA_KW // LANES
    extra_specs, extra_args, extra_scratch = [], [], []
    if pending_moe is not None:
        dest_steps, wts_col, mod_prev, y_slots = pending_moe
        n_steps = bsz * nj
        extra_specs = [
            pl.BlockSpec((1, 1, TOPK * tb), lambda b, j: (b * nj + j, 0, 0), memory_space=pltpu.SMEM),
            pl.BlockSpec((1, 1, TOPK * tb), lambda b, j: (jnp.minimum(b * nj + j + 1, n_steps - 1), 0, 0),
                         memory_space=pltpu.SMEM),
            pl.BlockSpec((tb, TOPK), lambda b, j: (b * nj + j, 0)),
            pl.BlockSpec((1, 6, D_MODEL), lambda b, j: (b, 0, 0)),
            pl.BlockSpec(memory_space=pl.ANY),
        ]
        extra_args = [dest_steps, dest_steps, wts_col, mod_prev, y_slots]
        extra_scratch = [pltpu.VMEM((2, TOPK * tb, D_MODEL), F32), pltpu.SemaphoreType.DMA((2,))]
    return pl.pallas_call(
        kern,
        grid=(bsz, nj),
        in_specs=[
            pl.BlockSpec((1, tb, D_MODEL), lambda b, j: (b, j, 0)),
            pl.BlockSpec((1, 6, D_MODEL), lambda b, j: (b, 0, 0)),
            _const_spec((1, D_MODEL)),
            _const_spec((D_MODEL, IN_COLS_PAD)),
            _const_spec((DEPTH, HG_KW)),
            _const_spec((LANES, GLA_KW)),
            _const_spec((1, GLA_KW)),
            _const_spec((1, HEAD_DV)),
            _const_spec((1, HEAD_DV)),
            _const_spec((HG_W, D_MODEL)),
            _const_spec((GLA_W, D_MODEL)),
            _const_spec((D_MODEL, D_MODEL)),
            pl.BlockSpec((1, HG_HEADS, HG_DK, HEAD_DV), lambda b, j: (b, 0, 0, 0)),
            pl.BlockSpec((1, n_gla_tiles, LANES, HEAD_DV), lambda b, j: (b, 0, 0, 0)),
            _const_spec((tb, tb)),
            _const_spec((CHUNK, SLAB_ROWS)),
            _const_spec((SAFE_PLAN.cum_rows, CHUNK)),
            _const_spec((SAFE_PLAN.n_masks, CHUNK, CHUNK)),
        ] + extra_specs,
        out_specs=[
            pl.BlockSpec((1, tb, D_MODEL), lambda b, j: (b, j, 0)),
            pl.BlockSpec((1, HG_HEADS, HG_DK, HEAD_DV), lambda b, j: (b, 0, 0, 0)),
            pl.BlockSpec((1, n_gla_tiles, LANES, HEAD_DV), lambda b, j: (b, 0, 0, 0)),
        ],
        out_shape=[
            jax.ShapeDtypeStruct((bsz, seq, D_MODEL), F32),
            jax.ShapeDtypeStruct((bsz, HG_HEADS, HG_DK, HEAD_DV), F32),
            jax.ShapeDtypeStruct((bsz, n_gla_tiles, LANES, HEAD_DV), F32),
        ],
        scratch_shapes=[
            pltpu.VMEM((tb, IN_COLS_PAD), F32),
            pltpu.VMEM((tb, HG_KW), F32),
            pltpu.VMEM((tb, GLA_KW), F32),
            pltpu.VMEM((HG_HEADS, HG_DK, HEAD_DV), F32),
            pltpu.VMEM((n_gla_tiles, LANES, HEAD_DV), F32),
        ] + extra_scratch,
        compiler_params=pltpu.CompilerParams(
            dimension_semantics=("arbitrary", "arbitrary"), vmem_limit_bytes=VMEM_LIMIT),
        name=f"mixer_l{layer}",
    )(x, mod, nrm, win, hg_lb, wgk2, bgk, hgn, glan, wa, wb, wo, shg0, sgla0, *plan_consts, *extra_args)


ROUTER_ROWS = 8 + N_EXPERTS
MOE_TILE = 512
TILE_ASSIGN = TOPK * MOE_TILE
EXPERT_BLOCK = 512


def _first_argmax_rows(vals, n):
    ridx = lax.broadcasted_iota(jnp.int32, vals.shape, 0)
    vmax = jnp.max(vals, axis=0, keepdims=True)
    imax = jnp.min(jnp.where(vals == vmax, ridx, n), axis=0, keepdims=True)
    return vmax, imax


def _router_kernel(x_ref, mod_ref, nrm_ref, wr_ref, br_ref, tri_ref,
                   h_ref, eid_ref, rank_ref, wts_ref, cnt_ref, run_scr):
    @pl.when(pl.program_id(0) == 0)
    def _():
        run_scr[...] = jnp.zeros_like(run_scr)

    u, lt, _ = x_ref.shape
    x = x_ref[...]
    sh2 = mod_ref[:, 3:4, :]
    sc2 = mod_ref[:, 4:5, :]
    h = _rms_mod(x, nrm_ref[...].reshape(1, 1, D_MODEL), sc2, sh2).reshape(u * lt, D_MODEL)
    h_ref[...] = h
    logits = lax.dot_general(wr_ref[...], h, (((1,), (1,)), ((), ())), preferred_element_type=F32,
                             precision=lax.Precision.HIGHEST) + br_ref[...]
    gl = logits[0:N_GROUPS]
    gmax, gi = _first_argmax_rows(gl, N_GROUPS)
    gp = 1.0 / jnp.sum(jnp.exp(gl - gmax), axis=0, keepdims=True)
    le = logits[8:8 + EXPERTS_PER_GROUP]
    for g in range(1, N_GROUPS):
        le = jnp.where(gi == g, logits[8 + g * EXPERTS_PER_GROUP:8 + (g + 1) * EXPERTS_PER_GROUP], le)
    pe = jnp.exp(le - jnp.max(le, axis=0, keepdims=True))
    pe = pe / jnp.sum(pe, axis=0, keepdims=True)
    v1, i1 = _first_argmax_rows(pe, EXPERTS_PER_GROUP)
    ridx = lax.broadcasted_iota(jnp.int32, pe.shape, 0)
    v2, i2 = _first_argmax_rows(jnp.where(ridx == i1, -1.0, pe), EXPERTS_PER_GROUP)
    vsum = v1 + v2
    wts_ref[0:1, :] = gp * v1 / vsum
    wts_ref[1:2, :] = gp * v2 / vsum
    eflat = jnp.concatenate([gi * EXPERTS_PER_GROUP + i1, gi * EXPERTS_PER_GROUP + i2], axis=1)
    eid_ref[0] = eflat
    onehot = (eflat == lax.broadcasted_iota(jnp.int32, (N_EXPERTS, TILE_ASSIGN), 0)).astype(F32)
    before = _dot(onehot.astype(BF16), tri_ref[...]) + run_scr[...]
    rank_ref[0] = jnp.sum(onehot * before, axis=0, keepdims=True).astype(jnp.int32)
    run_scr[...] = run_scr[...] + jnp.sum(onehot, axis=1, keepdims=True)
    cnt_ref[...] = run_scr[...].astype(jnp.int32)


def _router(x_units, mod_units, nrm, wr, br, tri):
    n_units, lt, _ = x_units.shape
    u = MOE_TILE // lt
    n_tiles = n_units // u
    return pl.pallas_call(
        _router_kernel,
        grid=(n_tiles,),
        in_specs=[
            pl.BlockSpec((u, lt, D_MODEL), lambda i: (i, 0, 0)),
            pl.BlockSpec((u, 6, D_MODEL), lambda i: (i, 0, 0)),
            pl.BlockSpec((1, D_MODEL), lambda i: (0, 0)),
            pl.BlockSpec((ROUTER_ROWS, D_MODEL), lambda i: (0, 0)),
            pl.BlockSpec((ROUTER_ROWS, 1), lambda i: (0, 0)),
            pl.BlockSpec((TILE_ASSIGN, TILE_ASSIGN), lambda i: (0, 0)),
        ],
        out_specs=[
            pl.BlockSpec((MOE_TILE, D_MODEL), lambda i: (i, 0)),
            pl.BlockSpec((1, 1, TILE_ASSIGN), lambda i: (i, 0, 0)),
            pl.BlockSpec((1, 1, TILE_ASSIGN), lambda i: (i, 0, 0)),
            pl.BlockSpec((TOPK, MOE_TILE), lambda i: (0, i)),
            pl.BlockSpec((N_EXPERTS, 1), lambda i: (0, 0)),
        ],
        out_shape=[
            jax.ShapeDtypeStruct((n_tiles * MOE_TILE, D_MODEL), F32),
            jax.ShapeDtypeStruct((n_tiles, 1, TILE_ASSIGN), jnp.int32),
            jax.ShapeDtypeStruct((n_tiles, 1, TILE_ASSIGN), jnp.int32),
            jax.ShapeDtypeStruct((TOPK, n_tiles * MOE_TILE), F32),
            jax.ShapeDtypeStruct((N_EXPERTS, 1), jnp.int32),
        ],
        scratch_shapes=[pltpu.VMEM((N_EXPERTS, 1), F32)],
        compiler_params=pltpu.CompilerParams(dimension_semantics=("arbitrary",)),
        name="moe_router",
    )(x_units, mod_units, nrm, wr, br, tri)


def _start_row_gather(idx_ref, n_rows, src_hbm, dst, sem):
    def body(r, carry):
        row = idx_ref[0, 0, r]
        pltpu.make_async_copy(src_hbm.at[pl.ds(row, 1)], dst.at[pl.ds(r, 1)], sem).start()
        return carry
    lax.fori_loop(0, n_rows, body, 0, unroll=8)


def _wait_row_gather(n_rows, src_hbm, dst, sem):
    pltpu.make_async_copy(src_hbm.at[pl.ds(0, n_rows)], dst, sem).wait()


def _dispatch_kernel(pend_ref, padded_ref, dest_ref, h_ref, xs_hbm, zbuf, sem):
    n_blocks = xs_hbm.shape[0] // EXPERT_BLOCK

    def zero_block(first_row):
        return pltpu.make_async_copy(
            zbuf, xs_hbm.at[pl.ds(pl.multiple_of(first_row, EXPERT_BLOCK), EXPERT_BLOCK)], sem.at[0])

    @pl.when(pl.program_id(0) == 0)
    def _():
        zbuf[...] = jnp.zeros_like(zbuf)
        n_used = pend_ref[N_EXPERTS - 1] // EXPERT_BLOCK
        for e in range(N_EXPERTS):
            @pl.when(padded_ref[e] > 0)
            def _():
                zero_block(pend_ref[e] - EXPERT_BLOCK).start()
        lax.fori_loop(n_used, n_blocks, lambda b, c: (zero_block(b * EXPERT_BLOCK).start(), c)[1], 0)
        for e in range(N_EXPERTS):
            @pl.when(padded_ref[e] > 0)
            def _():
                zero_block(pend_ref[e] - EXPERT_BLOCK).wait()
        lax.fori_loop(n_used, n_blocks, lambda b, c: (zero_block(b * EXPERT_BLOCK).wait(), c)[1], 0)

    def body(t, carry):
        for k in range(TOPK):
            slot = dest_ref[0, 0, k * MOE_TILE + t]
            pltpu.make_async_copy(h_ref.at[pl.ds(t, 1)], xs_hbm.at[pl.ds(slot, 1)], sem.at[1]).start()
        return carry
    lax.fori_loop(0, MOE_TILE, body, 0, unroll=8)
    for k in range(TOPK):
        pltpu.make_async_copy(h_ref, xs_hbm.at[pl.ds(0, MOE_TILE)], sem.at[1]).wait()


def _dispatch(pad_end, padded, dest_tiles, h, n_slots):
    n_tiles = dest_tiles.shape[0]
    grid_spec = pltpu.PrefetchScalarGridSpec(
        num_scalar_prefetch=2,
        grid=(n_tiles,),
        in_specs=[
            pl.BlockSpec((1, 1, TILE_ASSIGN), lambda i, pe, pd: (i, 0, 0), memory_space=pltpu.SMEM),
            pl.BlockSpec((MOE_TILE, D_MODEL), lambda i, pe, pd: (i, 0)),
        ],
        out_specs=pl.BlockSpec(memory_space=pl.ANY),
        scratch_shapes=[pltpu.VMEM((EXPERT_BLOCK, D_MODEL), F32), pltpu.SemaphoreType.DMA((2,))],
    )
    return pl.pallas_call(
        _dispatch_kernel,
        grid_spec=grid_spec,
        out_shape=jax.ShapeDtypeStruct((n_slots, D_MODEL), F32),
        compiler_params=pltpu.CompilerParams(dimension_semantics=("arbitrary",)),
        name="moe_dispatch",
    )(pad_end, padded, dest_tiles, h)


def _experts_kernel(be_ref, nused_ref, x_ref, wg_ref, wu_ref, wd_ref, o_ref):
    @pl.when(pl.program_id(0) < nused_ref[0])
    def _():
        xb = x_ref[...].astype(BF16)
        a = _silu(_dot(xb, wg_ref[0])) * _dot(xb, wu_ref[0])
        o_ref[...] = _dot(a.astype(BF16), wd_ref[0])

    @pl.when(pl.program_id(0) >= nused_ref[0])
    def _():
        o_ref[...] = jnp.zeros_like(o_ref)


def _experts(block_e, n_used, xs, wg, wu, wd):
    n_blocks = xs.shape[0] // EXPERT_BLOCK

    def row_block(i, be, nu):
        return (jnp.minimum(i, nu[0] - 1), 0)

    def expert_block(i, be, nu):
        return (be[jnp.minimum(i, nu[0] - 1)], 0, 0)

    grid_spec = pltpu.PrefetchScalarGridSpec(
        num_scalar_prefetch=2,
        grid=(n_blocks,),
        in_specs=[
            pl.BlockSpec((EXPERT_BLOCK, D_MODEL), row_block),
            pl.BlockSpec((1, D_MODEL, D_EXPERT), expert_block),
            pl.BlockSpec((1, D_MODEL, D_EXPERT), expert_block),
            pl.BlockSpec((1, D_EXPERT, D_MODEL), expert_block),
        ],
        out_specs=pl.BlockSpec((EXPERT_BLOCK, D_MODEL), lambda i, be, nu: (i, 0)),
    )
    return pl.pallas_call(
        _experts_kernel,
        grid_spec=grid_spec,
        out_shape=jax.ShapeDtypeStruct(xs.shape, F32),
        compiler_params=pltpu.CompilerParams(
            dimension_semantics=("arbitrary",), vmem_limit_bytes=VMEM_LIMIT),
        name="moe_experts",
    )(block_e, n_used, xs, wg, wu, wd)


def _combine_kernel(dst_cur_ref, dst_nxt_ref, x_ref, mod_ref, wts_ref, nrm_ref, y_hbm, o_ref, buf, sem,
                    *, final_norm):
    i = pl.program_id(0)
    n = pl.num_programs(0)
    slot = i % 2

    @pl.when(i == 0)
    def _():
        _start_row_gather(dst_cur_ref, TILE_ASSIGN, y_hbm, buf.at[0], sem.at[0])

    @pl.when(i + 1 < n)
    def _():
        _start_row_gather(dst_nxt_ref, TILE_ASSIGN, y_hbm, buf.at[1 - slot], sem.at[1 - slot])

    _wait_row_gather(TILE_ASSIGN, y_hbm, buf.at[slot], sem.at[slot])
    u, lt, _ = x_ref.shape
    y = (wts_ref[:, 0:1] * buf[slot, 0:MOE_TILE, :] + wts_ref[:, 1:2] * buf[slot, MOE_TILE:TILE_ASSIGN, :])
    g2 = mod_ref[:, 5:6, :]
    out = x_ref[...] + g2 * y.reshape(u, lt, D_MODEL)
    if final_norm:
        out = out * lax.rsqrt(jnp.mean(out * out, axis=-1, keepdims=True) + NORM_EPS)
        out = out * nrm_ref[...].reshape(1, 1, D_MODEL)
    o_ref[...] = out


def _combine(dest_tiles, x_units, mod_units, wts_col, nrm, y_slots, *, final_norm):
    n_units, lt, _ = x_units.shape
    u = MOE_TILE // lt
    n_tiles = n_units // u
    return pl.pallas_call(
        functools.partial(_combine_kernel, final_norm=final_norm),
        grid=(n_tiles,),
        in_specs=[
            pl.BlockSpec((1, 1, TILE_ASSIGN), lambda i: (i, 0, 0), memory_space=pltpu.SMEM),
            pl.BlockSpec((1, 1, TILE_ASSIGN), lambda i: (jnp.minimum(i + 1, n_tiles - 1), 0, 0),
                         memory_space=pltpu.SMEM),
            pl.BlockSpec((u, lt, D_MODEL), lambda i: (i, 0, 0)),
            pl.BlockSpec((u, 6, D_MODEL), lambda i: (i, 0, 0)),
            pl.BlockSpec((MOE_TILE, TOPK), lambda i: (i, 0)),
            pl.BlockSpec((1, D_MODEL), lambda i: (0, 0)),
            pl.BlockSpec(memory_space=pl.ANY),
        ],
        out_specs=pl.BlockSpec((u, lt, D_MODEL), lambda i: (i, 0, 0)),
        out_shape=jax.ShapeDtypeStruct(x_units.shape, F32),
        scratch_shapes=[pltpu.VMEM((2, TILE_ASSIGN, D_MODEL), F32), pltpu.SemaphoreType.DMA((2,))],
        compiler_params=pltpu.CompilerParams(
            dimension_semantics=("arbitrary",), vmem_limit_bytes=VMEM_LIMIT),
        name="moe_combine",
    )(dest_tiles, dest_tiles, x_units, mod_units, wts_col, nrm, y_slots)


def _routing_tables(eid_tiles, rank_tiles, counts):
    n_blocks = eid_tiles.size // EXPERT_BLOCK + N_EXPERTS
    padded = (counts + EXPERT_BLOCK - 1) // EXPERT_BLOCK * EXPERT_BLOCK
    pad_end = jnp.cumsum(padded).astype(jnp.int32)
    pad_start = pad_end - padded
    block_start = jnp.arange(n_blocks, dtype=jnp.int32)[:, None] * EXPERT_BLOCK
    block_e = jnp.minimum(jnp.sum((block_start >= pad_end[None, :]).astype(jnp.int32), axis=1),
                          N_EXPERTS - 1).astype(jnp.int32)
    n_used = pad_end[-1:] // EXPERT_BLOCK
    experts = jnp.arange(N_EXPERTS, dtype=jnp.int32)
    first_slot = jnp.sum(jnp.where(eid_tiles[..., None] == experts, pad_start, 0), axis=-1)
    return block_e, n_used, pad_end, padded, first_slot + rank_tiles


def _moe_units(x, mod_l):
    bsz, seq, _ = x.shape
    lt = min(seq, MOE_TILE)
    per = seq // lt
    x_units = x.reshape(bsz * seq // lt, lt, D_MODEL)
    mod_units = jnp.repeat(mod_l, per, axis=0) if per > 1 else mod_l
    return x_units, mod_units


def _moe_experts(x, mod_l, nrm_ffn, wr, br, tri, wg, wu, wd):
    x_units, mod_units = _moe_units(x, mod_l)
    h, eid_tiles, rank_tiles, wts, counts = _router(x_units, mod_units, nrm_ffn, wr, br, tri)
    block_e, n_used, pad_end, padded, dest_tiles = _routing_tables(eid_tiles, rank_tiles, counts[:, 0])
    n_slots = block_e.shape[0] * EXPERT_BLOCK
    xs = _dispatch(pad_end, padded, dest_tiles, h, n_slots)
    return dest_tiles, wts.T, _experts(block_e, n_used, xs, wg, wu, wd)


def _dest_per_step(dest_tiles, tb):
    n_tiles = dest_tiles.shape[0]
    per = MOE_TILE // tb
    d = dest_tiles.reshape(n_tiles, TOPK, per, tb).transpose(0, 2, 1, 3)
    return d.reshape(n_tiles * per, 1, TOPK * tb)


def kernel(x_prompt, x_sample, c_prompt, c_sample, state_hgrn, state_gla, w_ada, b_ada, norm_mix,
           norm_ffn, w_in, hg_lb, hg_onorm, w_gk2, b_gk, gla_onorm, w_br_a, w_br_b, w_out, w_rg, b_rg,
           w_re, b_re, w_e_gate, w_e_up, w_e_down, norm_final):
    bp = x_prompt.shape[0]
    bs = x_sample.shape[0]
    mod = _ada_mod(jnp.concatenate([c_prompt, c_sample], axis=0), w_ada, b_ada)
    mod = mod.reshape(DEPTH, bp + bs, 6, D_MODEL)

    glr0 = C_GOG + GLA_W
    win_r = jnp.concatenate(
        [w_in[:, :, :glr0], w_in[:, :, glr0 + GLA_GATE_RANK:], w_in[:, :, glr0:glr0 + GLA_GATE_RANK],
         jnp.zeros((DEPTH, D_MODEL, LANES - GLA_GATE_RANK), F32)], axis=2).astype(BF16)
    wgk2_p = jnp.concatenate(
        [w_gk2, jnp.zeros((DEPTH, LANES - GLA_GATE_RANK, GLA_KW), F32)], axis=1).astype(BF16)
    wa_b = w_br_a.astype(BF16)
    wb_b = w_br_b.astype(BF16)
    wo_b = w_out.astype(BF16)
    def plan_consts(tb):
        r = np.arange(tb)
        chunk_tril = (r[:, None] // CHUNK == r[None, :] // CHUNK) & (r[None, :] <= r[:, None])
        return [jnp.asarray(chunk_tril, BF16), jnp.asarray(_slab_mask(), F32),
                jnp.asarray(SAFE_PLAN.segment_sum_matrix(), BF16), jnp.asarray(SAFE_PLAN.masks(), F32)]
    zpad = jnp.zeros((DEPTH, 8 - N_GROUPS, D_MODEL), F32)
    wr = jnp.concatenate([jnp.swapaxes(w_rg, 1, 2), zpad, jnp.swapaxes(w_re, 1, 2)], axis=1)
    br = jnp.concatenate([b_rg, jnp.zeros((DEPTH, 8 - N_GROUPS), F32), b_re], axis=1)[:, :, None]
    wg_b = w_e_gate.astype(BF16)
    wu_b = w_e_up.astype(BF16)
    wd_b = w_e_down.astype(BF16)
    nrm_f = norm_final.reshape(1, D_MODEL)
    assign = np.arange(TILE_ASSIGN)
    tri = jnp.asarray(assign[:, None] < assign[None, :], BF16)

    def run(x, mod_g, shg, sgla, tb):
        bsz = x.shape[0]
        new_hg, new_gla = [], []
        pending = None
        for l in range(DEPTH):
            x, s1, s2 = _mixer(
                x, mod_g[l], norm_mix[l:l + 1], win_r[l], hg_lb, wgk2_p[l], b_gk[l:l + 1],
                hg_onorm[l:l + 1], gla_onorm[l:l + 1], wa_b[l], wb_b[l], wo_b[l],
                shg[l], sgla[l].reshape(bsz, GLA_KW // LANES, LANES, HEAD_DV), plan_consts(tb),
                layer=l, tb=tb, pending_moe=pending)
            new_hg.append(s1)
            new_gla.append(s2.reshape(bsz, GLA_HEADS, GLA_DK, HEAD_DV))
            dest_tiles, wts_col, y_slots = _moe_experts(
                x, mod_g[l], norm_ffn[l:l + 1], wr[l], br[l], tri, wg_b[l], wu_b[l], wd_b[l])
            pending = (_dest_per_step(dest_tiles, tb), wts_col, mod_g[l], y_slots)
        x_units, mod_units = _moe_units(x, mod_g[DEPTH - 1])
        y = _combine(dest_tiles, x_units, mod_units, wts_col, nrm_f, y_slots, final_norm=True)
        return y.reshape(x.shape), jnp.stack(new_hg), jnp.stack(new_gla)

    zeros_hg = jnp.zeros((DEPTH, bp, HG_HEADS, HG_DK, HEAD_DV), F32)
    zeros_gla = jnp.zeros((DEPTH, bp, GLA_HEADS, GLA_DK, HEAD_DV), F32)
    y_p, hg_p, gla_p = run(x_prompt, mod[:, :bp], zeros_hg, zeros_gla, 256)
    y_s, hg_s, gla_s = run(x_sample, mod[:, bp:], state_hgrn, state_gla, CHUNK)
    return (y_p, y_s, hg_p, gla_p, hg_s, gla_s)
```

```python
import functools

import numpy as np
import jax
import jax.numpy as jnp
from jax import lax
from jax.experimental import pallas as pl
from jax.experimental.pallas import tpu as pltpu

F32 = jnp.float32
BF16 = jnp.bfloat16

D_MODEL = 1024
DEPTH = 2
CHUNK = 64
NORM_EPS = 1e-6
LOG_FLOOR = 1e-30
HG_HEADS = 4
HG_DK = 128
HEAD_DV = 128
HG_KW = HG_HEADS * HG_DK
HG_W = HG_HEADS * HEAD_DV
GLA_HEADS = 4
GLA_DK = 64
GLA_KW = GLA_HEADS * GLA_DK
GLA_W = GLA_HEADS * HEAD_DV
GLA_GATE_RANK = 16
GLA_GATE_NORM = 16.0
N_GROUPS = 4
EXPERTS_PER_GROUP = 8
N_EXPERTS = N_GROUPS * EXPERTS_PER_GROUP
TOPK = 2
D_EXPERT = 512

LANES = 128
VMEM_LIMIT = 56 * 1024 * 1024

C_HQ = 0
C_HF = C_HQ + HG_KW
C_HI = C_HF + HG_KW
C_HOG = C_HI + HG_W
C_GQ = C_HOG + HG_W
C_GK = C_GQ + GLA_KW
C_GV = C_GK + GLA_KW
C_GOG = C_GV + GLA_W
C_GA = C_GOG + GLA_W
C_GB = C_GA + D_MODEL
C_GLR = C_GB + D_MODEL
IN_COLS_PAD = C_GLR + LANES
MXU_WIDTH = 256
PROJ_TILE = 4 * MXU_WIDTH

class _ScorePlan:
    def __init__(self, levels, adjacent, diag_block):
        self.levels = levels
        self.adjacent = adjacent
        self.diag_block = diag_block
        self.cum_rows = (2 * len(levels) + 1) * CHUNK
        self.n_masks = len(levels) + 1 + int(adjacent)

    def segment_sum_matrix(self):
        t = np.arange(CHUNK)[:, None]
        r = np.arange(CHUNK)[None, :]
        rows = []
        for m in self.levels:
            same = (t // m) == (r // m)
            rows.append(same & (r <= t))
            rows.append(same & (r > t))
        rows.append(r <= t)
        return np.concatenate(rows, axis=0).astype(np.float32)

    def masks(self):
        t = np.arange(CHUNK)[:, None]
        s = np.arange(CHUNK)[None, :]
        masks = [((t // self.diag_block) == (s // self.diag_block)) & (s <= t)]
        for m in self.levels + ((1,) if self.adjacent else ()):
            masks.append(((t // (2 * m)) == (s // (2 * m))) & ((t // m) % 2 == 1) & ((s // m) % 2 == 0))
        return np.stack(masks).astype(np.float32)


SAFE_PLAN = _ScorePlan((32, 16, 8, 4, 2), True, 1)


def _dot(a, b):
    return jnp.dot(a, b, preferred_element_type=F32)


def _dot_nt(a, b):
    return lax.dot_general(a, b, (((1,), (1,)), ((), ())), preferred_element_type=F32)


def _sigmoid(x):
    return 1.0 / (1.0 + jnp.exp(-x))


def _silu(x):
    return x * _sigmoid(x)


def _rms_mod(x, gain, scale, shift):
    y = x * lax.rsqrt(jnp.mean(x * x, axis=-1, keepdims=True) + NORM_EPS)
    return y * gain * (1.0 + scale) + shift


def _ada_kernel(c_ref, w_ref, b_ref, o_ref):
    c = c_ref[...]
    o_ref[0] = jnp.dot(_silu(c), w_ref[0], preferred_element_type=F32,
                       precision=lax.Precision.HIGHEST) + b_ref[0]


def _ada_mod(c_all, w_ada, b_ada):
    nb = c_all.shape[0]
    tn = 512
    return pl.pallas_call(
        _ada_kernel,
        grid=(DEPTH, 6 * D_MODEL // tn),
        in_specs=[
            pl.BlockSpec((nb, D_MODEL), lambda l, j: (0, 0)),
            pl.BlockSpec((1, D_MODEL, tn), lambda l, j: (l, 0, j)),
            pl.BlockSpec((1, 1, tn), lambda l, j: (l, 0, j)),
        ],
        out_specs=pl.BlockSpec((1, nb, tn), lambda l, j: (l, 0, j)),
        out_shape=jax.ShapeDtypeStruct((DEPTH, nb, 6 * D_MODEL), F32),
        name="ada_mod",
    )(c_all, w_ada, b_ada.reshape(DEPTH, 1, 6 * D_MODEL))


def _chunk_attention(q, k, v, g, states, mall_ref, mask_ref, heads_per_tile, plan):
    w = q.shape[1]
    n_tiles = w // LANES
    g_hi = g.astype(BF16)
    r1 = g - g_hi.astype(F32)
    g_mid = r1.astype(BF16)
    g_lo = (r1 - g_mid.astype(F32)).astype(BF16)
    mall = mall_ref[...]
    cums = _dot(mall, g_hi) + _dot(mall, g_mid) + _dot(mall, g_lo)
    b = cums[plan.cum_rows - CHUNK:plan.cum_rows]
    level_q = []
    level_k = []
    for i in range(len(plan.levels)):
        level_q.append(q * jnp.exp(cums[2 * i * CHUNK:(2 * i + 1) * CHUNK]))
        level_k.append(k * jnp.exp(cums[(2 * i + 1) * CHUNK:(2 * i + 2) * CHUNK]))
    if plan.diag_block == 1:
        qs = [q]
        ks = [k]
    else:
        i = plan.levels.index(plan.diag_block)
        qs = [level_q[i]]
        ks = [k * jnp.exp(-cums[2 * i * CHUNK:(2 * i + 1) * CHUNK])]
    qs += level_q
    ks += level_k
    if plan.adjacent:
        qs.append(q * jnp.exp(g))
        ks.append(k)
    b_last = b[CHUNK - 1:CHUNK]
    q_in = q * jnp.exp(b)
    k_out = k * jnp.exp(b_last - b)
    e_last = jnp.exp(b_last)

    dk = LANES // heads_per_tile
    lane = lax.broadcasted_iota(jnp.int32, (CHUNK, LANES), 1)
    row = lax.broadcasted_iota(jnp.int32, (LANES, HEAD_DV), 0)
    outs = []
    new_states = []
    for ti in range(n_tiles):
        sl = slice(ti * LANES, (ti + 1) * LANES)
        ks_t =[kk[:, sl].astype(BF16) for kk in ks]
        k_out_t = k_out[:, sl].T.astype(BF16)
        e_col = jnp.broadcast_to(e_last[:, sl], (LANES, LANES)).T
        s_old = states[ti]
        s_old_b = s_old.astype(BF16)
        upd = None
        for j in range(heads_per_tile):
            head = ti * heads_per_tile + j
            if heads_per_tile == 1:
                sel = lambda a: a
            else:
                in_head = (lane // dk) == j
                sel = lambda a, in_head=in_head: jnp.where(in_head, a, 0.0)
            sc = jnp.zeros((CHUNK, CHUNK), F32)
            for i in range(plan.n_masks):
                sc = sc + _dot_nt(sel(qs[i][:, sl]).astype(BF16), ks_t[i]) * mask_ref[i]
            vh = v[:, head * HEAD_DV:(head + 1) * HEAD_DV].astype(BF16)
            o = _dot(sc.astype(BF16), vh) + _dot(sel(q_in[:, sl]).astype(BF16), s_old_b)
            outs.append(o)
            u = _dot(k_out_t, vh)
            upd = u if upd is None else jnp.where((row // dk) == j, u, upd)
        new_states.append(e_col * s_old + upd)
    return jnp.concatenate(outs, axis=1), new_states


FAST_BLOCK = 16
N_SUB = CHUNK // FAST_BLOCK
SLAB_ROWS = FAST_BLOCK * (N_SUB * (N_SUB - 1) // 2) + CHUNK
FAST_BLOCK_DECAY_LIMIT = 60.0


def _slab_mask():
    t = np.arange(CHUNK)[:, None]
    cols = []
    for i in range(1, N_SUB):
        cols.append(np.broadcast_to(t // FAST_BLOCK == i, (CHUNK, i * FAST_BLOCK)))
    s = np.arange(CHUNK)[None, :]
    cols.append((t // FAST_BLOCK == s // FAST_BLOCK) & (s <= t))
    return np.concatenate(cols, axis=1).astype(np.float32)


def _block_attention_fast(q, k, v, g, states, tril_ref, slab_mask_ref, heads_per_tile):
    rows, w = q.shape
    n_chunks = rows // CHUNK
    n_tiles = w // LANES
    dk = LANES // heads_per_tile
    n_heads = n_tiles * heads_per_tile

    g_hi = g.astype(BF16)
    r1 = g - g_hi.astype(F32)
    g_mid = r1.astype(BF16)
    g_lo = (r1 - g_mid.astype(F32)).astype(BF16)
    tril = tril_ref[...]
    b = _dot(tril, g_hi) + _dot(tril, g_mid) + _dot(tril, g_lo)

    def end_row(c, i):
        r = c * CHUNK + (i + 1) * FAST_BLOCK
        return b[r - 1:r]

    def per_block(row_of):
        return jnp.concatenate([jnp.broadcast_to(row_of(c, i), (FAST_BLOCK, w))
                                for c in range(n_chunks) for i in range(N_SUB)], axis=0)

    zero = jnp.zeros((1, w), F32)
    b_start = per_block(lambda c, i: zero if i == 0 else end_row(c, i - 1))
    b_end = per_block(end_row)
    q_blk = q * jnp.exp(b - b_start)
    k_diag = k * jnp.exp(b_start - b)
    k_end = k * jnp.exp(b_end - b)
    q_in = q_blk * jnp.exp(b_start)
    k_out = k_end * jnp.exp(per_block(lambda c, i: end_row(c, N_SUB - 1)) - b_end)

    lane = lax.broadcasted_iota(jnp.int32, (CHUNK, LANES), 1)
    row = lax.broadcasted_iota(jnp.int32, (LANES, HEAD_DV), 0)
    slab_mask = slab_mask_ref[...]

    def sel(a, j):
        return a if heads_per_tile == 1 else jnp.where((lane // dk) == j, a, 0.0)

    v_b = v.astype(BF16)

    scores = {}
    for c in range(n_chunks):
        r0 = c * CHUNK
        slabs = []
        for i in range(1, N_SUB):
            for jb in range(i):
                blk = k_end[r0 + jb * FAST_BLOCK:r0 + (jb + 1) * FAST_BLOCK]
                slabs.append(blk if jb == i - 1 else blk * jnp.exp(end_row(c, i - 1) - end_row(c, jb)))
        slabs.append(k_diag[r0:r0 + CHUNK])
        k_slab = jnp.concatenate(slabs, axis=0).astype(BF16)
        for ti in range(n_tiles):
            sl = slice(ti * LANES, (ti + 1) * LANES)
            for j in range(heads_per_tile):
                qh = sel(q_blk[r0:r0 + CHUNK, sl], j).astype(BF16)
                scores[c, ti * heads_per_tile + j] = (_dot_nt(qh, k_slab[:, sl]) * slab_mask).astype(BF16)

    entering = [list(states)]
    for c in range(n_chunks):
        r0 = c * CHUNK
        nxt = []
        for ti in range(n_tiles):
            sl = slice(ti * LANES, (ti + 1) * LANES)
            k_out_t = k_out[r0:r0 + CHUNK, sl].T.astype(BF16)
            upd = None
            for j in range(heads_per_tile):
                head = ti * heads_per_tile + j
                u = _dot(k_out_t, v_b[r0:r0 + CHUNK, head * HEAD_DV:(head + 1) * HEAD_DV])
                upd = u if upd is None else jnp.where((row // dk) == j, u, upd)
            e_col = jnp.broadcast_to(jnp.exp(end_row(c, N_SUB - 1)[:, sl]), (LANES, LANES)).T
            nxt.append(e_col * entering[c][ti] + upd)
        entering.append(nxt)

    out_rows = []
    for c in range(n_chunks):
        r0 = c * CHUNK
        outs = []
        for head in range(n_heads):
            ti, j = divmod(head, heads_per_tile)
            sl = slice(ti * LANES, (ti + 1) * LANES)
            vh = v_b[r0:r0 + CHUNK, head * HEAD_DV:(head + 1) * HEAD_DV]
            v_slab = jnp.concatenate([vh[:i * FAST_BLOCK] for i in range(1, N_SUB)] + [vh], axis=0)
            outs.append(_dot(scores[c, head], v_slab)
                        + _dot(sel(q_in[r0:r0 + CHUNK, sl], j).astype(BF16), entering[c][ti].astype(BF16)))
        out_rows.append(jnp.concatenate(outs, axis=1))
    return jnp.concatenate(out_rows, axis=0), entering[n_chunks]


def _head_norm_gate(o, gain, gate):
    outs = []
    for h in range(o.shape[1] // HEAD_DV):
        sl = slice(h * HEAD_DV, (h + 1) * HEAD_DV)
        oh = o[:, sl]
        oh = oh * lax.rsqrt(jnp.mean(oh * oh, axis=-1, keepdims=True) + NORM_EPS) * gain
        outs.append(oh * _silu(gate[:, sl]))
    return jnp.concatenate(outs, axis=1)


def _mixer_kernel(x_ref, mod_ref, nrm_ref, win_ref, lb_ref, wgk2_ref, bgk_ref, hgn_ref, glan_ref,
                  wa_ref, wb_ref, wo_ref, shg0_ref, sgla0_ref,
                  tril_ref, slab_mask_ref, mall_safe_ref, mask_safe_ref, *rest, layer, tb, pending_moe):
    if pending_moe:
        dst_cur_ref, dst_nxt_ref, wts_ref, modp_ref, y_hbm = rest[:5]
        rest = rest[5:]
    xo_ref, shg_o_ref, sgla_o_ref, p_scr, k_scr, lg_scr, shg_scr, sgla_scr = rest[:8]
    j = pl.program_id(1)

    @pl.when(j == 0)
    def _():
        shg_scr[...] = shg0_ref[0]
        sgla_scr[...] = sgla0_ref[0]

    x = x_ref[0]
    n_prefetch = 0
    if pending_moe:
        cbuf, csem = rest[8:]
        n_prefetch = TOPK * tb
        step = pl.program_id(0) * pl.num_programs(1) + j
        last = pl.num_programs(0) * pl.num_programs(1) - 1
        slot = step % 2

        @pl.when(step == 0)
        def _():
            _start_row_gather(dst_cur_ref, n_prefetch, y_hbm, cbuf.at[0], csem.at[0])

        _wait_row_gather(n_prefetch, y_hbm, cbuf.at[slot], csem.at[slot])
        x = x + modp_ref[0, 5:6, :] * (wts_ref[:, 0:1] * cbuf[slot, 0:tb, :]
                                       + wts_ref[:, 1:2] * cbuf[slot, tb:n_prefetch, :])

    def prefetch_rows(r0, r1):
        for r in range(r0, r1):
            row = dst_nxt_ref[0, 0, r]
            pltpu.make_async_copy(y_hbm.at[pl.ds(row, 1)], cbuf.at[1 - slot, pl.ds(r, 1)],
                                  csem.at[1 - slot]).start()

    sh1 = mod_ref[0, 0:1, :]
    sc1 = mod_ref[0, 1:2, :]
    g1 = mod_ref[0, 2:3, :]
    hb = _rms_mod(x, nrm_ref[...], sc1, sh1).astype(BF16)
    col_tiles = list(range(0, IN_COLS_PAD, PROJ_TILE))
    for i, c in enumerate(col_tiles):
        c1 = min(c + PROJ_TILE, IN_COLS_PAD)
        p_scr[:, c:c1] = _dot(hb, win_ref[:, c:c1])
        if pending_moe:
            prefetch_rows(n_prefetch * i // len(col_tiles), n_prefetch * (i + 1) // len(col_tiles))

    lb_all = lb_ref[...]
    lb_max = jnp.max(lb_all, axis=0, keepdims=True)
    lb_exp = jnp.exp(lb_all - lb_max)
    sm = lb_exp / jnp.sum(lb_exp, axis=0, keepdims=True)
    lbl = jnp.clip(jnp.sum(sm[0:layer + 1], axis=0, keepdims=True) - sm[0:1], 0.0, 1.0)

    p_scr[:, C_HQ:C_HQ + HG_KW] = _silu(p_scr[:, C_HQ:C_HQ + HG_KW]) * (HG_DK ** -0.5)
    z = p_scr[:, C_HF:C_HF + HG_KW]
    f = lbl + (1.0 - lbl) * _sigmoid(z)
    p_scr[:, C_HF:C_HF + HG_KW] = jnp.log(jnp.maximum(f, LOG_FLOOR))
    k_scr[...] = (1.0 - lbl) * _sigmoid(-z)
    glr = p_scr[:, C_GLR:C_GLR + LANES].astype(BF16)
    gate = _dot(glr, wgk2_ref[...]) + bgk_ref[...]
    lg_scr[...] = (jnp.minimum(gate, 0.0) - jnp.log1p(jnp.exp(-jnp.abs(gate)))) * (1.0 / GLA_GATE_NORM)
    p_scr[:, C_GQ:C_GQ + GLA_KW] = p_scr[:, C_GQ:C_GQ + GLA_KW] * (GLA_DK ** -0.5)

    n_hg_tiles = HG_KW // LANES
    n_gla_tiles = GLA_KW // LANES

    def one_chunk(rows, states, attend):
        o_hg, st_hg = attend(p_scr[rows, C_HQ:C_HQ + HG_KW], k_scr[rows, :],
                             p_scr[rows, C_HI:C_HI + HG_W], p_scr[rows, C_HF:C_HF + HG_KW],
                             states[:n_hg_tiles], 1)
        p_scr[rows, C_HI:C_HI + HG_W] = o_hg
        o_gla, st_gla = attend(p_scr[rows, C_GQ:C_GQ + GLA_KW], p_scr[rows, C_GK:C_GK + GLA_KW],
                               p_scr[rows, C_GV:C_GV + GLA_W], lg_scr[rows, :],
                               states[n_hg_tiles:], 2)
        p_scr[rows, C_GV:C_GV + GLA_W] = o_gla
        return st_hg + st_gla

    def attend_safe(q, k, v, g, states, heads_per_tile):
        return _chunk_attention(q, k, v, g, states, mall_safe_ref, mask_safe_ref, heads_per_tile, SAFE_PLAN)

    def load_states():
        return [shg_scr[t] for t in range(n_hg_tiles)] + [sgla_scr[t] for t in range(n_gla_tiles)]

    def store_states(states):
        for t in range(n_hg_tiles):
            shg_scr[t] = states[t]
        for t in range(n_gla_tiles):
            sgla_scr[t] = states[n_hg_tiles + t]

    def run_block_fast():
        states = load_states()
        o_hg, st_hg = _block_attention_fast(
            p_scr[:, C_HQ:C_HQ + HG_KW], k_scr[...], p_scr[:, C_HI:C_HI + HG_W],
            p_scr[:, C_HF:C_HF + HG_KW], states[:n_hg_tiles], tril_ref, slab_mask_ref, 1)
        p_scr[:, C_HI:C_HI + HG_W] = o_hg
        o_gla, st_gla = _block_attention_fast(
            p_scr[:, C_GQ:C_GQ + GLA_KW], p_scr[:, C_GK:C_GK + GLA_KW], p_scr[:, C_GV:C_GV + GLA_W],
            lg_scr[...], states[n_hg_tiles:], tril_ref, slab_mask_ref, 2)
        p_scr[:, C_GV:C_GV + GLA_W] = o_gla
        store_states(st_hg + st_gla)

    def run_chunks_safe():
        def chunk_body(ci, carry):
            rows = pl.ds(pl.multiple_of(ci * CHUNK, CHUNK), CHUNK)
            store_states(one_chunk(rows, load_states(), attend_safe))
            return carry

        lax.fori_loop(0, tb // CHUNK, chunk_body, 0)

    blk = FAST_BLOCK
    min_hg = jnp.min(jnp.sum(p_scr[:, C_HF:C_HF + HG_KW].reshape(tb // blk, blk, HG_KW), axis=1))
    min_gla = jnp.min(jnp.sum(lg_scr[...].reshape(tb // blk, blk, GLA_KW), axis=1))
    bounded = jnp.minimum(min_hg, min_gla) >= -FAST_BLOCK_DECAY_LIMIT

    @pl.when(bounded)
    def _():
        run_block_fast()

    @pl.when(jnp.logical_not(bounded))
    def _():
        run_chunks_safe()

    o_hg = _head_norm_gate(p_scr[:, C_HI:C_HI + HG_W], hgn_ref[...], p_scr[:, C_HOG:C_HOG + HG_W])
    o_gla = _head_norm_gate(p_scr[:, C_GV:C_GV + GLA_W], glan_ref[...], p_scr[:, C_GOG:C_GOG + GLA_W])
    ya = _dot(o_hg.astype(BF16), wa_ref[...])
    yb = _dot(o_gla.astype(BF16), wb_ref[...])
    merged = (_sigmoid(p_scr[:, C_GA:C_GA + D_MODEL]) * ya
              + _sigmoid(p_scr[:, C_GB:C_GB + D_MODEL]) * yb)
    m = _dot(merged.astype(BF16), wo_ref[...])
    xo_ref[0] = x + g1 * m

    @pl.when(j == pl.num_programs(1) - 1)
    def _():
        shg_o_ref[0] = shg_scr[...]
        sgla_o_ref[0] = sgla_scr[...]

    if pending_moe:
        @pl.when(step == last)
        def _():
            _wait_row_gather(n_prefetch, y_hbm, cbuf.at[1 - slot], csem.at[1 - slot])


def _const_spec(shape):
    nd = len(shape)
    return pl.BlockSpec(shape, lambda b, j, nd=nd: (0,) * nd, pipeline_mode=pl.Buffered(1))


def _mixer(x, mod, nrm, win, hg_lb, wgk2, bgk, hgn, glan, wa, wb, wo, shg0, sgla0, plan_consts,
           *, layer, tb, pending_moe=None):
    bsz, seq, _ = x.shape
    nj = seq // tb
    kern = functools.partial(_mixer_kernel, layer=layer, tb=tb, pending_moe=pending_moe is not None)
    n_gla_tiles = GLA_KW // LANES
    extra_specs, extra_args, extra_scratch = [], [], []
    if pending_moe is not None:
        dest_steps, wts_col, mod_prev, y_slots = pending_moe
        n_steps = bsz * nj
        extra_specs = [
            pl.BlockSpec((1, 1, TOPK * tb), lambda b, j: (b * nj + j, 0, 0), memory_space=pltpu.SMEM),
            pl.BlockSpec((1, 1, TOPK * tb), lambda b, j: (jnp.minimum(b * nj + j + 1, n_steps - 1), 0, 0),
                         memory_space=pltpu.SMEM),
            pl.BlockSpec((tb, TOPK), lambda b, j: (b * nj + j, 0)),
            pl.BlockSpec((1, 6, D_MODEL), lambda b, j: (b, 0, 0)),
            pl.BlockSpec(memory_space=pl.ANY),
        ]
        extra_args = [dest_steps, dest_steps, wts_col, mod_prev, y_slots]
        extra_scratch = [pltpu.VMEM((2, TOPK * tb, D_MODEL), F32), pltpu.SemaphoreType.DMA((2,))]
    return pl.pallas_call(
        kern,
        grid=(bsz, nj),
        in_specs=[
            pl.BlockSpec((1, tb, D_MODEL), lambda b, j: (b, j, 0)),
            pl.BlockSpec((1, 6, D_MODEL), lambda b, j: (b, 0, 0)),
            _const_spec((1, D_MODEL)),
            _const_spec((D_MODEL, IN_COLS_PAD)),
            _const_spec((DEPTH, HG_KW)),
            _const_spec((LANES, GLA_KW)),
            _const_spec((1, GLA_KW)),
            _const_spec((1, HEAD_DV)),
            _const_spec((1, HEAD_DV)),
            _const_spec((HG_W, D_MODEL)),
            _const_spec((GLA_W, D_MODEL)),
            _const_spec((D_MODEL, D_MODEL)),
            pl.BlockSpec((1, HG_HEADS, HG_DK, HEAD_DV), lambda b, j: (b, 0, 0, 0)),
            pl.BlockSpec((1, n_gla_tiles, LANES, HEAD_DV), lambda b, j: (b, 0, 0, 0)),
            _const_spec((tb, tb)),
            _const_spec((CHUNK, SLAB_ROWS)),
            _const_spec((SAFE_PLAN.cum_rows, CHUNK)),
            _const_spec((SAFE_PLAN.n_masks, CHUNK, CHUNK)),
        ] + extra_specs,
        out_specs=[
            pl.BlockSpec((1, tb, D_MODEL), lambda b, j: (b, j, 0)),
            pl.BlockSpec((1, HG_HEADS, HG_DK, HEAD_DV), lambda b, j: (b, 0, 0, 0)),
            pl.BlockSpec((1, n_gla_tiles, LANES, HEAD_DV), lambda b, j: (b, 0, 0, 0)),
        ],
        out_shape=[
            jax.ShapeDtypeStruct((bsz, seq, D_MODEL), F32),
            jax.ShapeDtypeStruct((bsz, HG_HEADS, HG_DK, HEAD_DV), F32),
            jax.ShapeDtypeStruct((bsz, n_gla_tiles, LANES, HEAD_DV), F32),
        ],
        scratch_shapes=[
            pltpu.VMEM((tb, IN_COLS_PAD), F32),
            pltpu.VMEM((tb, HG_KW), F32),
            pltpu.VMEM((tb, GLA_KW), F32),
            pltpu.VMEM((HG_HEADS, HG_DK, HEAD_DV), F32),
            pltpu.VMEM((n_gla_tiles, LANES, HEAD_DV), F32),
        ] + extra_scratch,
        compiler_params=pltpu.CompilerParams(
            dimension_semantics=("arbitrary", "arbitrary"), vmem_limit_bytes=VMEM_LIMIT),
        name=f"mixer_l{layer}",
    )(x, mod, nrm, win, hg_lb, wgk2, bgk, hgn, glan, wa, wb, wo, shg0, sgla0, *plan_consts, *extra_args)


ROUTER_ROWS = 8 + N_EXPERTS
MOE_TILE = 512
TILE_ASSIGN = TOPK * MOE_TILE
EXPERT_BLOCK = 512


def _first_argmax_rows(vals, n):
    ridx = lax.broadcasted_iota(jnp.int32, vals.shape, 0)
    vmax = jnp.max(vals, axis=0, keepdims=True)
    imax = jnp.min(jnp.where(vals == vmax, ridx, n), axis=0, keepdims=True)
    return vmax, imax


def _router_kernel(x_ref, mod_ref, nrm_ref, wr_ref, br_ref, tri_ref,
                   h_ref, eid_ref, rank_ref, wts_ref, cnt_ref, run_scr):
    @pl.when(pl.program_id(0) == 0)
    def _():
        run_scr[...] = jnp.zeros_like(run_scr)

    u, lt, _ = x_ref.shape
    x = x_ref[...]
    sh2 = mod_ref[:, 3:4, :]
    sc2 = mod_ref[:, 4:5, :]
    h = _rms_mod(x, nrm_ref[...].reshape(1, 1, D_MODEL), sc2, sh2).reshape(u * lt, D_MODEL)
    h_ref[...] = h
    logits = lax.dot_general(wr_ref[...], h, (((1,), (1,)), ((), ())), preferred_element_type=F32,
                             precision=lax.Precision.HIGHEST) + br_ref[...]
    gl = logits[0:N_GROUPS]
    gmax, gi = _first_argmax_rows(gl, N_GROUPS)
    gp = 1.0 / jnp.sum(jnp.exp(gl - gmax), axis=0, keepdims=True)
    le = logits[8:8 + EXPERTS_PER_GROUP]
    for g in range(1, N_GROUPS):
        le = jnp.where(gi == g, logits[8 + g * EXPERTS_PER_GROUP:8 + (g + 1) * EXPERTS_PER_GROUP], le)
    pe = jnp.exp(le - jnp.max(le, axis=0, keepdims=True))
    pe = pe / jnp.sum(pe, axis=0, keepdims=True)
    v1, i1 = _first_argmax_rows(pe, EXPERTS_PER_GROUP)
    ridx = lax.broadcasted_iota(jnp.int32, pe.shape, 0)
    v2, i2 = _first_argmax_rows(jnp.where(ridx == i1, -1.0, pe), EXPERTS_PER_GROUP)
    vsum = v1 + v2
    wts_ref[0:1, :] = gp * v1 / vsum
    wts_ref[1:2, :] = gp * v2 / vsum
    eflat = jnp.concatenate([gi * EXPERTS_PER_GROUP + i1, gi * EXPERTS_PER_GROUP + i2], axis=1)
    eid_ref[0] = eflat
    onehot = (eflat == lax.broadcasted_iota(jnp.int32, (N_EXPERTS, TILE_ASSIGN), 0)).astype(F32)
    before = _dot(onehot.astype(BF16), tri_ref[...]) + run_scr[...]
    rank_ref[0] = jnp.sum(onehot * before, axis=0, keepdims=True).astype(jnp.int32)
    run_scr[...] = run_scr[...] + jnp.sum(onehot, axis=1, keepdims=True)
    cnt_ref[...] = run_scr[...].astype(jnp.int32)


def _router(x_units, mod_units, nrm, wr, br, tri):
    n_units, lt, _ = x_units.shape
    u = MOE_TILE // lt
    n_tiles = n_units // u
    return pl.pallas_call(
        _router_kernel,
        grid=(n_tiles,),
        in_specs=[
            pl.BlockSpec((u, lt, D_MODEL), lambda i: (i, 0, 0)),
            pl.BlockSpec((u, 6, D_MODEL), lambda i: (i, 0, 0)),
            pl.BlockSpec((1, D_MODEL), lambda i: (0, 0)),
            pl.BlockSpec((ROUTER_ROWS, D_MODEL), lambda i: (0, 0)),
            pl.BlockSpec((ROUTER_ROWS, 1), lambda i: (0, 0)),
            pl.BlockSpec((TILE_ASSIGN, TILE_ASSIGN), lambda i: (0, 0)),
        ],
        out_specs=[
            pl.BlockSpec((MOE_TILE, D_MODEL), lambda i: (i, 0)),
            pl.BlockSpec((1, 1, TILE_ASSIGN), lambda i: (i, 0, 0)),
            pl.BlockSpec((1, 1, TILE_ASSIGN), lambda i: (i, 0, 0)),
            pl.BlockSpec((TOPK, MOE_TILE), lambda i: (0, i)),
            pl.BlockSpec((N_EXPERTS, 1), lambda i: (0, 0)),
        ],
        out_shape=[
            jax.ShapeDtypeStruct((n_tiles * MOE_TILE, D_MODEL), F32),
            jax.ShapeDtypeStruct((n_tiles, 1, TILE_ASSIGN), jnp.int32),
            jax.ShapeDtypeStruct((n_tiles, 1, TILE_ASSIGN), jnp.int32),
            jax.ShapeDtypeStruct((TOPK, n_tiles * MOE_TILE), F32),
            jax.ShapeDtypeStruct((N_EXPERTS, 1), jnp.int32),
        ],
        scratch_shapes=[pltpu.VMEM((N_EXPERTS, 1), F32)],
        compiler_params=pltpu.CompilerParams(dimension_semantics=("arbitrary",)),
        name="moe_router",
    )(x_units, mod_units, nrm, wr, br, tri)


def _start_row_gather(idx_ref, n_rows, src_hbm, dst, sem):
    def body(r, carry):
        row = idx_ref[0, 0, r]
        pltpu.make_async_copy(src_hbm.at[pl.ds(row, 1)], dst.at[pl.ds(r, 1)], sem).start()
        return carry
    lax.fori_loop(0, n_rows, body, 0, unroll=8)


def _wait_row_gather(n_rows, src_hbm, dst, sem):
    pltpu.make_async_copy(src_hbm.at[pl.ds(0, n_rows)], dst, sem).wait()


def _dispatch_kernel(pend_ref, padded_ref, dest_ref, h_ref, xs_hbm, zbuf, sem):
    n_blocks = xs_hbm.shape[0] // EXPERT_BLOCK

    def zero_block(first_row):
        return pltpu.make_async_copy(
            zbuf, xs_hbm.at[pl.ds(pl.multiple_of(first_row, EXPERT_BLOCK), EXPERT_BLOCK)], sem.at[0])

    @pl.when(pl.program_id(0) == 0)
    def _():
        zbuf[...] = jnp.zeros_like(zbuf)
        n_used = pend_ref[N_EXPERTS - 1] // EXPERT_BLOCK
        for e in range(N_EXPERTS):
            @pl.when(padded_ref[e] > 0)
            def _():
                zero_block(pend_ref[e] - EXPERT_BLOCK).start()
        lax.fori_loop(n_used, n_blocks, lambda b, c: (zero_block(b * EXPERT_BLOCK).start(), c)[1], 0)
        for e in range(N_EXPERTS):
            @pl.when(padded_ref[e] > 0)
            def _():
                zero_block(pend_ref[e] - EXPERT_BLOCK).wait()
        lax.fori_loop(n_used, n_blocks, lambda b, c: (zero_block(b * EXPERT_BLOCK).wait(), c)[1], 0)

    def body(t, carry):
        for k in range(TOPK):
            slot = dest_ref[0, 0, k * MOE_TILE + t]
            pltpu.make_async_copy(h_ref.at[pl.ds(t, 1)], xs_hbm.at[pl.ds(slot, 1)], sem.at[1]).start()
        return carry
    lax.fori_loop(0, MOE_TILE, body, 0, unroll=8)
    for k in range(TOPK):
        pltpu.make_async_copy(h_ref, xs_hbm.at[pl.ds(0, MOE_TILE)], sem.at[1]).wait()


def _dispatch(pad_end, padded, dest_tiles, h, n_slots):
    n_tiles = dest_tiles.shape[0]
    grid_spec = pltpu.PrefetchScalarGridSpec(
        num_scalar_prefetch=2,
        grid=(n_tiles,),
        in_specs=[
            pl.BlockSpec((1, 1, TILE_ASSIGN), lambda i, pe, pd: (i, 0, 0), memory_space=pltpu.SMEM),
            pl.BlockSpec((MOE_TILE, D_MODEL), lambda i, pe, pd: (i, 0)),
        ],
        out_specs=pl.BlockSpec(memory_space=pl.ANY),
        scratch_shapes=[pltpu.VMEM((EXPERT_BLOCK, D_MODEL), F32), pltpu.SemaphoreType.DMA((2,))],
    )
    return pl.pallas_call(
        _dispatch_kernel,
        grid_spec=grid_spec,
        out_shape=jax.ShapeDtypeStruct((n_slots, D_MODEL), F32),
        compiler_params=pltpu.CompilerParams(dimension_semantics=("arbitrary",)),
        name="moe_dispatch",
    )(pad_end, padded, dest_tiles, h)


def _experts_kernel(be_ref, nused_ref, x_ref, wg_ref, wu_ref, wd_ref, o_ref, wg_b, wu_b, wd_b):
    i = pl.program_id(0)
    used = i < nused_ref[0]
    new_expert = jnp.logical_or(i == 0, be_ref[i] != be_ref[jnp.maximum(i - 1, 0)])

    @pl.when(jnp.logical_and(used, new_expert))
    def _():
        wg_b[...] = wg_ref[0].astype(BF16)
        wu_b[...] = wu_ref[0].astype(BF16)
        wd_b[...] = wd_ref[0].astype(BF16)

    @pl.when(used)
    def _():
        xb = x_ref[...].astype(BF16)
        a = _silu(_dot(xb, wg_b[...])) * _dot(xb, wu_b[...])
        o_ref[...] = _dot(a.astype(BF16), wd_b[...])

    @pl.when(pl.program_id(0) >= nused_ref[0])
    def _():
        o_ref[...] = jnp.zeros_like(o_ref)


def _experts(block_e, n_used, xs, wg, wu, wd):
    n_blocks = xs.shape[0] // EXPERT_BLOCK

    def row_block(i, be, nu):
        return (jnp.minimum(i, nu[0] - 1), 0)

    def expert_block(i, be, nu):
        return (be[jnp.minimum(i, nu[0] - 1)], 0, 0)

    grid_spec = pltpu.PrefetchScalarGridSpec(
        num_scalar_prefetch=2,
        grid=(n_blocks,),
        in_specs=[
            pl.BlockSpec((EXPERT_BLOCK, D_MODEL), row_block),
            pl.BlockSpec((1, D_MODEL, D_EXPERT), expert_block),
            pl.BlockSpec((1, D_MODEL, D_EXPERT), expert_block),
            pl.BlockSpec((1, D_EXPERT, D_MODEL), expert_block),
        ],
        out_specs=pl.BlockSpec((EXPERT_BLOCK, D_MODEL), lambda i, be, nu: (i, 0)),
        scratch_shapes=[pltpu.VMEM((D_MODEL, D_EXPERT), BF16), pltpu.VMEM((D_MODEL, D_EXPERT), BF16),
                        pltpu.VMEM((D_EXPERT, D_MODEL), BF16)],
    )
    return pl.pallas_call(
        _experts_kernel,
        grid_spec=grid_spec,
        out_shape=jax.ShapeDtypeStruct(xs.shape, F32),
        compiler_params=pltpu.CompilerParams(
            dimension_semantics=("arbitrary",), vmem_limit_bytes=VMEM_LIMIT),
        name="moe_experts",
    )(block_e, n_used, xs, wg, wu, wd)


def _combine_kernel(dst_cur_ref, dst_nxt_ref, x_ref, mod_ref, wts_ref, nrm_ref, y_hbm, o_ref, buf, sem,
                    *, final_norm):
    i = pl.program_id(0)
    n = pl.num_programs(0)
    slot = i % 2

    @pl.when(i == 0)
    def _():
        _start_row_gather(dst_cur_ref, TILE_ASSIGN, y_hbm, buf.at[0], sem.at[0])

    _wait_row_gather(TILE_ASSIGN, y_hbm, buf.at[slot], sem.at[slot])
    u, lt, _ = x_ref.shape
    n_pieces = MOE_TILE // CHUNK
    per_piece = TILE_ASSIGN // n_pieces
    for c in range(n_pieces):
        unit, r0 = divmod(c * CHUNK, lt)
        t0 = c * CHUNK
        y = (wts_ref[t0:t0 + CHUNK, 0:1] * buf[slot, t0:t0 + CHUNK, :]
             + wts_ref[t0:t0 + CHUNK, 1:2] * buf[slot, MOE_TILE + t0:MOE_TILE + t0 + CHUNK, :])
        out = x_ref[unit, r0:r0 + CHUNK, :] + mod_ref[unit, 5:6, :] * y
        if final_norm:
            out = out * lax.rsqrt(jnp.mean(out * out, axis=-1, keepdims=True) + NORM_EPS) * nrm_ref[...]
        o_ref[unit, r0:r0 + CHUNK, :] = out
        for r in range(c * per_piece, (c + 1) * per_piece):
            row = dst_nxt_ref[0, 0, r]
            pltpu.make_async_copy(y_hbm.at[pl.ds(row, 1)], buf.at[1 - slot, pl.ds(r, 1)],
                                  sem.at[1 - slot]).start()

    @pl.when(i == n - 1)
    def _():
        _wait_row_gather(TILE_ASSIGN, y_hbm, buf.at[1 - slot], sem.at[1 - slot])


def _combine(dest_tiles, x_units, mod_units, wts_col, nrm, y_slots, *, final_norm):
    n_units, lt, _ = x_units.shape
    u = MOE_TILE // lt
    n_tiles = n_units // u
    return pl.pallas_call(
        functools.partial(_combine_kernel, final_norm=final_norm),
        grid=(n_tiles,),
        in_specs=[
            pl.BlockSpec((1, 1, TILE_ASSIGN), lambda i: (i, 0, 0), memory_space=pltpu.SMEM),
            pl.BlockSpec((1, 1, TILE_ASSIGN), lambda i: (jnp.minimum(i + 1, n_tiles - 1), 0, 0),
                         memory_space=pltpu.SMEM),
            pl.BlockSpec((u, lt, D_MODEL), lambda i: (i, 0, 0)),
            pl.BlockSpec((u, 6, D_MODEL), lambda i: (i, 0, 0)),
            pl.BlockSpec((MOE_TILE, TOPK), lambda i: (i, 0)),
            pl.BlockSpec((1, D_MODEL), lambda i: (0, 0)),
            pl.BlockSpec(memory_space=pl.ANY),
        ],
        out_specs=pl.BlockSpec((u, lt, D_MODEL), lambda i: (i, 0, 0)),
        out_shape=jax.ShapeDtypeStruct(x_units.shape, F32),
        scratch_shapes=[pltpu.VMEM((2, TILE_ASSIGN, D_MODEL), F32), pltpu.SemaphoreType.DMA((2,))],
        compiler_params=pltpu.CompilerParams(
            dimension_semantics=("arbitrary",), vmem_limit_bytes=VMEM_LIMIT),
        name="moe_combine",
    )(dest_tiles, dest_tiles, x_units, mod_units, wts_col, nrm, y_slots)


def _routing_tables(eid_tiles, rank_tiles, counts):
    n_blocks = eid_tiles.size // EXPERT_BLOCK + N_EXPERTS
    padded = (counts + EXPERT_BLOCK - 1) // EXPERT_BLOCK * EXPERT_BLOCK
    pad_end = jnp.cumsum(padded).astype(jnp.int32)
    pad_start = pad_end - padded
    block_start = jnp.arange(n_blocks, dtype=jnp.int32)[:, None] * EXPERT_BLOCK
    block_e = jnp.minimum(jnp.sum((block_start >= pad_end[None, :]).astype(jnp.int32), axis=1),
                          N_EXPERTS - 1).astype(jnp.int32)
    n_used = pad_end[-1:] // EXPERT_BLOCK
    experts = jnp.arange(N_EXPERTS, dtype=jnp.int32)
    first_slot = jnp.sum(jnp.where(eid_tiles[..., None] == experts, pad_start, 0), axis=-1)
    return block_e, n_used, pad_end, padded, first_slot + rank_tiles


def _moe_units(x, mod_l):
    bsz, seq, _ = x.shape
    lt = min(seq, MOE_TILE)
    per = seq // lt
    x_units = x.reshape(bsz * seq // lt, lt, D_MODEL)
    mod_units = jnp.repeat(mod_l, per, axis=0) if per > 1 else mod_l
    return x_units, mod_units


def _moe_experts(x, mod_l, nrm_ffn, wr, br, tri, wg, wu, wd):
    x_units, mod_units = _moe_units(x, mod_l)
    h, eid_tiles, rank_tiles, wts, counts = _router(x_units, mod_units, nrm_ffn, wr, br, tri)
    block_e, n_used, pad_end, padded, dest_tiles = _routing_tables(eid_tiles, rank_tiles, counts[:, 0])
    n_slots = block_e.shape[0] * EXPERT_BLOCK
    xs = _dispatch(pad_end, padded, dest_tiles, h, n_slots)
    return dest_tiles, wts.T, _experts(block_e, n_used, xs, wg, wu, wd)


def _dest_per_step(dest_tiles, tb):
    n_tiles = dest_tiles.shape[0]
    per = MOE_TILE // tb
    d = dest_tiles.reshape(n_tiles, TOPK, per, tb).transpose(0, 2, 1, 3)
    return d.reshape(n_tiles * per, 1, TOPK * tb)


def kernel(x_prompt, x_sample, c_prompt, c_sample, state_hgrn, state_gla, w_ada, b_ada, norm_mix,
           norm_ffn, w_in, hg_lb, hg_onorm, w_gk2, b_gk, gla_onorm, w_br_a, w_br_b, w_out, w_rg, b_rg,
           w_re, b_re, w_e_gate, w_e_up, w_e_down, norm_final):
    bp = x_prompt.shape[0]
    bs = x_sample.shape[0]
    mod = _ada_mod(jnp.concatenate([c_prompt, c_sample], axis=0), w_ada, b_ada)
    mod = mod.reshape(DEPTH, bp + bs, 6, D_MODEL)

    glr0 = C_GOG + GLA_W
    win_r = jnp.concatenate(
        [w_in[:, :, :glr0], w_in[:, :, glr0 + GLA_GATE_RANK:], w_in[:, :, glr0:glr0 + GLA_GATE_RANK],
         jnp.zeros((DEPTH, D_MODEL, LANES - GLA_GATE_RANK), F32)], axis=2).astype(BF16)
    wgk2_p = jnp.concatenate(
        [w_gk2, jnp.zeros((DEPTH, LANES - GLA_GATE_RANK, GLA_KW), F32)], axis=1).astype(BF16)
    wa_b = w_br_a.astype(BF16)
    wb_b = w_br_b.astype(BF16)
    wo_b = w_out.astype(BF16)
    def plan_consts(tb):
        r = np.arange(tb)
        chunk_tril = (r[:, None] // CHUNK == r[None, :] // CHUNK) & (r[None, :] <= r[:, None])
        return [jnp.asarray(chunk_tril, BF16), jnp.asarray(_slab_mask(), F32),
                jnp.asarray(SAFE_PLAN.segment_sum_matrix(), BF16), jnp.asarray(SAFE_PLAN.masks(), F32)]
    zpad = jnp.zeros((DEPTH, 8 - N_GROUPS, D_MODEL), F32)
    wr = jnp.concatenate([jnp.swapaxes(w_rg, 1, 2), zpad, jnp.swapaxes(w_re, 1, 2)], axis=1)
    br = jnp.concatenate([b_rg, jnp.zeros((DEPTH, 8 - N_GROUPS), F32), b_re], axis=1)[:, :, None]
    nrm_f = norm_final.reshape(1, D_MODEL)
    assign = np.arange(TILE_ASSIGN)
    tri = jnp.asarray(assign[:, None] < assign[None, :], BF16)

    def run(x, mod_g, shg, sgla, tb):
        bsz = x.shape[0]
        new_hg, new_gla = [], []
        pending = None
        for l in range(DEPTH):
            x, s1, s2 = _mixer(
                x, mod_g[l], norm_mix[l:l + 1], win_r[l], hg_lb, wgk2_p[l], b_gk[l:l + 1],
                hg_onorm[l:l + 1], gla_onorm[l:l + 1], wa_b[l], wb_b[l], wo_b[l],
                shg[l], sgla[l].reshape(bsz, GLA_KW // LANES, LANES, HEAD_DV), plan_consts(tb),
                layer=l, tb=tb, pending_moe=pending)
            new_hg.append(s1)
            new_gla.append(s2.reshape(bsz, GLA_HEADS, GLA_DK, HEAD_DV))
            dest_tiles, wts_col, y_slots = _moe_experts(
                x, mod_g[l], norm_ffn[l:l + 1], wr[l], br[l], tri, w_e_gate[l], w_e_up[l], w_e_down[l])
            pending = (_dest_per_step(dest_tiles, tb), wts_col, mod_g[l], y_slots)
        x_units, mod_units = _moe_units(x, mod_g[DEPTH - 1])
        y = _combine(dest_tiles, x_units, mod_units, wts_col, nrm_f, y_slots, final_norm=True)
        return y.reshape(x.shape), jnp.stack(new_hg), jnp.stack(new_gla)

    zeros_hg = jnp.zeros((DEPTH, bp, HG_HEADS, HG_DK, HEAD_DV), F32)
    zeros_gla = jnp.zeros((DEPTH, bp, GLA_HEADS, GLA_DK, HEAD_DV), F32)
    y_p, hg_p, gla_p = run(x_prompt, mod[:, :bp], zeros_hg, zeros_gla, 256)
    y_s, hg_s, gla_s = run(x_sample, mod[:, bp:], state_hgrn, state_gla, CHUNK)
    return (y_p, y_s, hg_p, gla_p, hg_s, gla_s)
```

```python
import functools

import numpy as np
import jax
import jax.numpy as jnp
from jax import lax
from jax.experimental import pallas as pl
from jax.experimental.pallas import tpu as pltpu

F32 = jnp.float32
BF16 = jnp.bfloat16

D_MODEL = 1024
DEPTH = 2
CHUNK = 64
NORM_EPS = 1e-6
LOG_FLOOR = 1e-30
HG_HEADS = 4
HG_DK = 128
HEAD_DV = 128
HG_KW = HG_HEADS * HG_DK
HG_W = HG_HEADS * HEAD_DV
GLA_HEADS = 4
GLA_DK = 64
GLA_KW = GLA_HEADS * GLA_DK
GLA_W = GLA_HEADS * HEAD_DV
GLA_GATE_RANK = 16
GLA_GATE_NORM = 16.0
N_GROUPS = 4
EXPERTS_PER_GROUP = 8
N_EXPERTS = N_GROUPS * EXPERTS_PER_GROUP
TOPK = 2
D_EXPERT = 512

LANES = 128
VMEM_LIMIT = 56 * 1024 * 1024

C_HQ = 0
C_HF = C_HQ + HG_KW
C_HI = C_HF + HG_KW
C_HOG = C_HI + HG_W
C_GQ = C_HOG + HG_W
C_GK = C_GQ + GLA_KW
C_GV = C_GK + GLA_KW
C_GOG = C_GV + GLA_W
C_GA = C_GOG + GLA_W
C_GB = C_GA + D_MODEL
C_GLR = C_GB + D_MODEL
IN_COLS_PAD = C_GLR + LANES
MXU_WIDTH = 256
PROJ_TILE = 4 * MXU_WIDTH

class _ScorePlan:
    def __init__(self, levels, adjacent, diag_block):
        self.levels = levels
        self.adjacent = adjacent
        self.diag_block = diag_block
        self.cum_rows = (2 * len(levels) + 1) * CHUNK
        self.n_masks = len(levels) + 1 + int(adjacent)

    def segment_sum_matrix(self):
        t = np.arange(CHUNK)[:, None]
        r = np.arange(CHUNK)[None, :]
        rows = []
        for m in self.levels:
            same = (t // m) == (r // m)
            rows.append(same & (r <= t))
            rows.append(same & (r > t))
        rows.append(r <= t)
        return np.concatenate(rows, axis=0).astype(np.float32)

    def masks(self):
        t = np.arange(CHUNK)[:, None]
        s = np.arange(CHUNK)[None, :]
        masks = [((t // self.diag_block) == (s // self.diag_block)) & (s <= t)]
        for m in self.levels + ((1,) if self.adjacent else ()):
            masks.append(((t // (2 * m)) == (s // (2 * m))) & ((t // m) % 2 == 1) & ((s // m) % 2 == 0))
        return np.stack(masks).astype(np.float32)


SAFE_PLAN = _ScorePlan((32, 16, 8, 4, 2), True, 1)


def _dot(a, b):
    return jnp.dot(a, b, preferred_element_type=F32)


def _dot_nt(a, b):
    return lax.dot_general(a, b, (((1,), (1,)), ((), ())), preferred_element_type=F32)


def _sigmoid(x):
    return 1.0 / (1.0 + jnp.exp(-x))


def _silu(x):
    return x * _sigmoid(x)


def _rms_mod(x, gain, scale, shift):
    y = x * lax.rsqrt(jnp.mean(x * x, axis=-1, keepdims=True) + NORM_EPS)
    return y * gain * (1.0 + scale) + shift


def _ada_kernel(c_ref, w_ref, b_ref, o_ref):
    c = c_ref[...]
    o_ref[0] = jnp.dot(_silu(c), w_ref[0], preferred_element_type=F32,
                       precision=lax.Precision.HIGHEST) + b_ref[0]


def _ada_mod(c_all, w_ada, b_ada):
    nb = c_all.shape[0]
    tn = 512
    return pl.pallas_call(
        _ada_kernel,
        grid=(DEPTH, 6 * D_MODEL // tn),
        in_specs=[
            pl.BlockSpec((nb, D_MODEL), lambda l, j: (0, 0)),
            pl.BlockSpec((1, D_MODEL, tn), lambda l, j: (l, 0, j)),
            pl.BlockSpec((1, 1, tn), lambda l, j: (l, 0, j)),
        ],
        out_specs=pl.BlockSpec((1, nb, tn), lambda l, j: (l, 0, j)),
        out_shape=jax.ShapeDtypeStruct((DEPTH, nb, 6 * D_MODEL), F32),
        name="ada_mod",
    )(c_all, w_ada, b_ada.reshape(DEPTH, 1, 6 * D_MODEL))


def _chunk_attention(q, k, v, g, states, mall_ref, mask_ref, heads_per_tile, plan):
    w = q.shape[1]
    n_tiles = w // LANES
    g_hi = g.astype(BF16)
    r1 = g - g_hi.astype(F32)
    g_mid = r1.astype(BF16)
    g_lo = (r1 - g_mid.astype(F32)).astype(BF16)
    mall = mall_ref[...]
    cums = _dot(mall, g_hi) + _dot(mall, g_mid) + _dot(mall, g_lo)
    b = cums[plan.cum_rows - CHUNK:plan.cum_rows]
    level_q = []
    level_k = []
    for i in range(len(plan.levels)):
        level_q.append(q * jnp.exp(cums[2 * i * CHUNK:(2 * i + 1) * CHUNK]))
        level_k.append(k * jnp.exp(cums[(2 * i + 1) * CHUNK:(2 * i + 2) * CHUNK]))
    if plan.diag_block == 1:
        qs = [q]
        ks = [k]
    else:
        i = plan.levels.index(plan.diag_block)
        qs = [level_q[i]]
        ks = [k * jnp.exp(-cums[2 * i * CHUNK:(2 * i + 1) * CHUNK])]
    qs += level_q
    ks += level_k
    if plan.adjacent:
        qs.append(q * jnp.exp(g))
        ks.append(k)
    b_last = b[CHUNK - 1:CHUNK]
    q_in = q * jnp.exp(b)
    k_out = k * jnp.exp(b_last - b)
    e_last = jnp.exp(b_last)

    dk = LANES // heads_per_tile
    lane = lax.broadcasted_iota(jnp.int32, (CHUNK, LANES), 1)
    row = lax.broadcasted_iota(jnp.int32, (LANES, HEAD_DV), 0)
    outs = []
    new_states = []
    for ti in range(n_tiles):
        sl = slice(ti * LANES, (ti + 1) * LANES)
        ks_t =[kk[:, sl].astype(BF16) for kk in ks]
        k_out_t = k_out[:, sl].T.astype(BF16)
        e_col = jnp.broadcast_to(e_last[:, sl], (LANES, LANES)).T
        s_old = states[ti]
        s_old_b = s_old.astype(BF16)
        upd = None
        for j in range(heads_per_tile):
            head = ti * heads_per_tile + j
            if heads_per_tile == 1:
                sel = lambda a: a
            else:
                in_head = (lane // dk) == j
                sel = lambda a, in_head=in_head: jnp.where(in_head, a, 0.0)
            sc = jnp.zeros((CHUNK, CHUNK), F32)
            for i in range(plan.n_masks):
                sc = sc + _dot_nt(sel(qs[i][:, sl]).astype(BF16), ks_t[i]) * mask_ref[i]
            vh = v[:, head * HEAD_DV:(head + 1) * HEAD_DV].astype(BF16)
            o = _dot(sc.astype(BF16), vh) + _dot(sel(q_in[:, sl]).astype(BF16), s_old_b)
            outs.append(o)
            u = _dot(k_out_t, vh)
            upd = u if upd is None else jnp.where((row // dk) == j, u, upd)
        new_states.append(e_col * s_old + upd)
    return jnp.concatenate(outs, axis=1), new_states


FAST_BLOCK = 16
N_SUB = CHUNK // FAST_BLOCK
SLAB_ROWS = FAST_BLOCK * (N_SUB * (N_SUB - 1) // 2) + CHUNK
FAST_BLOCK_DECAY_LIMIT = 60.0


def _slab_mask():
    t = np.arange(CHUNK)[:, None]
    cols = []
    for i in range(1, N_SUB):
        cols.append(np.broadcast_to(t // FAST_BLOCK == i, (CHUNK, i * FAST_BLOCK)))
    s = np.arange(CHUNK)[None, :]
    cols.append((t // FAST_BLOCK == s // FAST_BLOCK) & (s <= t))
    return np.concatenate(cols, axis=1).astype(np.float32)


def _block_attention_fast(q, k, v, g, states, tril_ref, slab_mask_ref, heads_per_tile):
    rows, w = q.shape
    n_chunks = rows // CHUNK
    n_tiles = w // LANES
    dk = LANES // heads_per_tile
    n_heads = n_tiles * heads_per_tile

    g_hi = g.astype(BF16)
    r1 = g - g_hi.astype(F32)
    g_mid = r1.astype(BF16)
    g_lo = (r1 - g_mid.astype(F32)).astype(BF16)
    tril = tril_ref[...]
    b = _dot(tril, g_hi) + _dot(tril, g_mid) + _dot(tril, g_lo)

    def end_row(c, i):
        r = c * CHUNK + (i + 1) * FAST_BLOCK
        return b[r - 1:r]

    def per_block(row_of):
        return jnp.concatenate([jnp.broadcast_to(row_of(c, i), (FAST_BLOCK, w))
                                for c in range(n_chunks) for i in range(N_SUB)], axis=0)

    zero = jnp.zeros((1, w), F32)
    b_start = per_block(lambda c, i: zero if i == 0 else end_row(c, i - 1))
    b_end = per_block(end_row)
    q_blk = q * jnp.exp(b - b_start)
    k_diag = k * jnp.exp(b_start - b)
    k_end = k * jnp.exp(b_end - b)
    q_in = q_blk * jnp.exp(b_start)
    k_out = k_end * jnp.exp(per_block(lambda c, i: end_row(c, N_SUB - 1)) - b_end)

    lane = lax.broadcasted_iota(jnp.int32, (CHUNK, LANES), 1)
    row = lax.broadcasted_iota(jnp.int32, (LANES, HEAD_DV), 0)
    slab_mask = slab_mask_ref[...]

    def sel(a, j):
        return a if heads_per_tile == 1 else jnp.where((lane // dk) == j, a, 0.0)

    v_b = v.astype(BF16)

    scores = {}
    for c in range(n_chunks):
        r0 = c * CHUNK
        slabs = []
        for i in range(1, N_SUB):
            for jb in range(i):
                blk = k_end[r0 + jb * FAST_BLOCK:r0 + (jb + 1) * FAST_BLOCK]
                slabs.append(blk if jb == i - 1 else blk * jnp.exp(end_row(c, i - 1) - end_row(c, jb)))
        slabs.append(k_diag[r0:r0 + CHUNK])
        k_slab = jnp.concatenate(slabs, axis=0).astype(BF16)
        for ti in range(n_tiles):
            sl = slice(ti * LANES, (ti + 1) * LANES)
            for j in range(heads_per_tile):
                qh = sel(q_blk[r0:r0 + CHUNK, sl], j).astype(BF16)
                scores[c, ti * heads_per_tile + j] = (_dot_nt(qh, k_slab[:, sl]) * slab_mask).astype(BF16)

    entering = [list(states)]
    for c in range(n_chunks):
        r0 = c * CHUNK
        nxt = []
        for ti in range(n_tiles):
            sl = slice(ti * LANES, (ti + 1) * LANES)
            k_out_t = k_out[r0:r0 + CHUNK, sl].T.astype(BF16)
            upd = None
            for j in range(heads_per_tile):
                head = ti * heads_per_tile + j
                u = _dot(k_out_t, v_b[r0:r0 + CHUNK, head * HEAD_DV:(head + 1) * HEAD_DV])
                upd = u if upd is None else jnp.where((row // dk) == j, u, upd)
            e_col = jnp.broadcast_to(jnp.exp(end_row(c, N_SUB - 1)[:, sl]), (LANES, LANES)).T
            nxt.append(e_col * entering[c][ti] + upd)
        entering.append(nxt)

    out_rows = []
    for c in range(n_chunks):
        r0 = c * CHUNK
        outs = []
        for head in range(n_heads):
            ti, j = divmod(head, heads_per_tile)
            sl = slice(ti * LANES, (ti + 1) * LANES)
            vh = v_b[r0:r0 + CHUNK, head * HEAD_DV:(head + 1) * HEAD_DV]
            v_slab = jnp.concatenate([vh[:i * FAST_BLOCK] for i in range(1, N_SUB)] + [vh], axis=0)
            outs.append(_dot(scores[c, head], v_slab)
                        + _dot(sel(q_in[r0:r0 + CHUNK, sl], j).astype(BF16), entering[c][ti].astype(BF16)))
        out_rows.append(jnp.concatenate(outs, axis=1))
    return jnp.concatenate(out_rows, axis=0), entering[n_chunks]


def _head_norm_gate(o, gain, gate):
    outs = []
    for h in range(o.shape[1] // HEAD_DV):
        sl = slice(h * HEAD_DV, (h + 1) * HEAD_DV)
        oh = o[:, sl]
        oh = oh * lax.rsqrt(jnp.mean(oh * oh, axis=-1, keepdims=True) + NORM_EPS) * gain
        outs.append(oh * _silu(gate[:, sl]))
    return jnp.concatenate(outs, axis=1)


def _mixer_kernel(x_ref, mod_ref, nrm_ref, win_ref, lb_ref, wgk2_ref, bgk_ref, hgn_ref, glan_ref,
                  wa_ref, wb_ref, wo_ref, shg0_ref, sgla0_ref,
                  tril_ref, slab_mask_ref, mall_safe_ref, mask_safe_ref, *rest, layer, tb, pending_moe):
    if pending_moe:
        dst_cur_ref, dst_nxt_ref, wts_ref, modp_ref, y_hbm = rest[:5]
        rest = rest[5:]
    xo_ref, shg_o_ref, sgla_o_ref, p_scr, k_scr, lg_scr, shg_scr, sgla_scr = rest[:8]
    j = pl.program_id(1)

    @pl.when(j == 0)
    def _():
        shg_scr[...] = shg0_ref[0]
        sgla_scr[...] = sgla0_ref[0]

    x = x_ref[0]
    n_prefetch = 0
    if pending_moe:
        cbuf, csem = rest[8:]
        n_prefetch = TOPK * tb
        step = pl.program_id(0) * pl.num_programs(1) + j
        last = pl.num_programs(0) * pl.num_programs(1) - 1
        slot = step % 2

        @pl.when(step == 0)
        def _():
            _start_row_gather(dst_cur_ref, n_prefetch, y_hbm, cbuf.at[0], csem.at[0])

        _wait_row_gather(n_prefetch, y_hbm, cbuf.at[slot], csem.at[slot])
        x = x + modp_ref[0, 5:6, :] * (wts_ref[:, 0:1] * cbuf[slot, 0:tb, :]
                                       + wts_ref[:, 1:2] * cbuf[slot, tb:n_prefetch, :])

    def prefetch_rows(r0, r1):
        for r in range(r0, r1):
            row = dst_nxt_ref[0, 0, r]
            pltpu.make_async_copy(y_hbm.at[pl.ds(row, 1)], cbuf.at[1 - slot, pl.ds(r, 1)],
                                  csem.at[1 - slot]).start()

    sh1 = mod_ref[0, 0:1, :]
    sc1 = mod_ref[0, 1:2, :]
    g1 = mod_ref[0, 2:3, :]
    hb = _rms_mod(x, nrm_ref[...], sc1, sh1).astype(BF16)
    col_tiles = list(range(0, IN_COLS_PAD, PROJ_TILE))
    for i, c in enumerate(col_tiles):
        c1 = min(c + PROJ_TILE, IN_COLS_PAD)
        p_scr[:, c:c1] = _dot(hb, win_ref[:, c:c1])
        if pending_moe:
            prefetch_rows(n_prefetch * i // len(col_tiles), n_prefetch * (i + 1) // len(col_tiles))

    lb_all = lb_ref[...]
    lb_max = jnp.max(lb_all, axis=0, keepdims=True)
    lb_exp = jnp.exp(lb_all - lb_max)
    sm = lb_exp / jnp.sum(lb_exp, axis=0, keepdims=True)
    lbl = jnp.clip(jnp.sum(sm[0:layer + 1], axis=0, keepdims=True) - sm[0:1], 0.0, 1.0)

    p_scr[:, C_HQ:C_HQ + HG_KW] = _silu(p_scr[:, C_HQ:C_HQ + HG_KW]) * (HG_DK ** -0.5)
    z = p_scr[:, C_HF:C_HF + HG_KW]
    f = lbl + (1.0 - lbl) * _sigmoid(z)
    p_scr[:, C_HF:C_HF + HG_KW] = jnp.log(jnp.maximum(f, LOG_FLOOR))
    k_scr[...] = (1.0 - lbl) * _sigmoid(-z)
    glr = p_scr[:, C_GLR:C_GLR + LANES].astype(BF16)
    gate = _dot(glr, wgk2_ref[...]) + bgk_ref[...]
    lg_scr[...] = (jnp.minimum(gate, 0.0) - jnp.log1p(jnp.exp(-jnp.abs(gate)))) * (1.0 / GLA_GATE_NORM)
    p_scr[:, C_GQ:C_GQ + GLA_KW] = p_scr[:, C_GQ:C_GQ + GLA_KW] * (GLA_DK ** -0.5)

    n_hg_tiles = HG_KW // LANES
    n_gla_tiles = GLA_KW // LANES

    def one_chunk(rows, states, attend):
        o_hg, st_hg = attend(p_scr[rows, C_HQ:C_HQ + HG_KW], k_scr[rows, :],
                             p_scr[rows, C_HI:C_HI + HG_W], p_scr[rows, C_HF:C_HF + HG_KW],
                             states[:n_hg_tiles], 1)
        p_scr[rows, C_HI:C_HI + HG_W] = o_hg
        o_gla, st_gla = attend(p_scr[rows, C_GQ:C_GQ + GLA_KW], p_scr[rows, C_GK:C_GK + GLA_KW],
                               p_scr[rows, C_GV:C_GV + GLA_W], lg_scr[rows, :],
                               states[n_hg_tiles:], 2)
        p_scr[rows, C_GV:C_GV + GLA_W] = o_gla
        return st_hg + st_gla

    def attend_safe(q, k, v, g, states, heads_per_tile):
        return _chunk_attention(q, k, v, g, states, mall_safe_ref, mask_safe_ref, heads_per_tile, SAFE_PLAN)

    def load_states():
        return [shg_scr[t] for t in range(n_hg_tiles)] + [sgla_scr[t] for t in range(n_gla_tiles)]

    def store_states(states):
        for t in range(n_hg_tiles):
            shg_scr[t] = states[t]
        for t in range(n_gla_tiles):
            sgla_scr[t] = states[n_hg_tiles + t]

    def run_block_fast():
        states = load_states()
        o_hg, st_hg = _block_attention_fast(
            p_scr[:, C_HQ:C_HQ + HG_KW], k_scr[...], p_scr[:, C_HI:C_HI + HG_W],
            p_scr[:, C_HF:C_HF + HG_KW], states[:n_hg_tiles], tril_ref, slab_mask_ref, 1)
        p_scr[:, C_HI:C_HI + HG_W] = o_hg
        o_gla, st_gla = _block_attention_fast(
            p_scr[:, C_GQ:C_GQ + GLA_KW], p_scr[:, C_GK:C_GK + GLA_KW], p_scr[:, C_GV:C_GV + GLA_W],
            lg_scr[...], states[n_hg_tiles:], tril_ref, slab_mask_ref, 2)
        p_scr[:, C_GV:C_GV + GLA_W] = o_gla
        store_states(st_hg + st_gla)

    def run_chunks_safe():
        def chunk_body(ci, carry):
            rows = pl.ds(pl.multiple_of(ci * CHUNK, CHUNK), CHUNK)
            store_states(one_chunk(rows, load_states(), attend_safe))
            return carry

        lax.fori_loop(0, tb // CHUNK, chunk_body, 0)

    blk = FAST_BLOCK
    min_hg = jnp.min(jnp.sum(p_scr[:, C_HF:C_HF + HG_KW].reshape(tb // blk, blk, HG_KW), axis=1))
    min_gla = jnp.min(jnp.sum(lg_scr[...].reshape(tb // blk, blk, GLA_KW), axis=1))
    bounded = jnp.minimum(min_hg, min_gla) >= -FAST_BLOCK_DECAY_LIMIT

    @pl.when(bounded)
    def _():
        run_block_fast()

    @pl.when(jnp.logical_not(bounded))
    def _():
        run_chunks_safe()

    o_hg = _head_norm_gate(p_scr[:, C_HI:C_HI + HG_W], hgn_ref[...], p_scr[:, C_HOG:C_HOG + HG_W])
    o_gla = _head_norm_gate(p_scr[:, C_GV:C_GV + GLA_W], glan_ref[...], p_scr[:, C_GOG:C_GOG + GLA_W])
    ya = _dot(o_hg.astype(BF16), wa_ref[...])
    yb = _dot(o_gla.astype(BF16), wb_ref[...])
    merged = (_sigmoid(p_scr[:, C_GA:C_GA + D_MODEL]) * ya
              + _sigmoid(p_scr[:, C_GB:C_GB + D_MODEL]) * yb)
    m = _dot(merged.astype(BF16), wo_ref[...])
    xo_ref[0] = x + g1 * m

    @pl.when(j == pl.num_programs(1) - 1)
    def _():
        shg_o_ref[0] = shg_scr[...]
        sgla_o_ref[0] = sgla_scr[...]

    if pending_moe:
        @pl.when(step == last)
        def _():
            _wait_row_gather(n_prefetch, y_hbm, cbuf.at[1 - slot], csem.at[1 - slot])


def _const_spec(shape):
    nd = len(shape)
    return pl.BlockSpec(shape, lambda b, j, nd=nd: (0,) * nd, pipeline_mode=pl.Buffered(1))


def _mixer(x, mod, nrm, win, hg_lb, wgk2, bgk, hgn, glan, wa, wb, wo, shg0, sgla0, plan_consts,
           *, layer, tb, pending_moe=None):
    bsz, seq, _ = x.shape
    nj = seq // tb
    kern = functools.partial(_mixer_kernel, layer=layer, tb=tb, pending_moe=pending_moe is not None)
    n_gla_tiles = GLA_KW // LANES
    extra_specs, extra_args, extra_scratch = [], [], []
    if pending_moe is not None:
        dest_steps, wts_col, mod_prev, y_slots = pending_moe
        n_steps = bsz * nj
        extra_specs = [
            pl.BlockSpec((1, 1, TOPK * tb), lambda b, j: (b * nj + j, 0, 0), memory_space=pltpu.SMEM),
            pl.BlockSpec((1, 1, TOPK * tb), lambda b, j: (jnp.minimum(b * nj + j + 1, n_steps - 1), 0, 0),
                         memory_space=pltpu.SMEM),
            pl.BlockSpec((tb, TOPK), lambda b, j: (b * nj + j, 0)),
            pl.BlockSpec((1, 6, D_MODEL), lambda b, j: (b, 0, 0)),
            pl.BlockSpec(memory_space=pl.ANY),
        ]
        extra_args = [dest_steps, dest_steps, wts_col, mod_prev, y_slots]
        extra_scratch = [pltpu.VMEM((2, TOPK * tb, D_MODEL), F32), pltpu.SemaphoreType.DMA((2,))]
    return pl.pallas_call(
        kern,
        grid=(bsz, nj),
        in_specs=[
            pl.BlockSpec((1, tb, D_MODEL), lambda b, j: (b, j, 0)),
            pl.BlockSpec((1, 6, D_MODEL), lambda b, j: (b, 0, 0)),
            _const_spec((1, D_MODEL)),
            _const_spec((D_MODEL, IN_COLS_PAD)),
            _const_spec((DEPTH, HG_KW)),
            _const_spec((LANES, GLA_KW)),
            _const_spec((1, GLA_KW)),
            _const_spec((1, HEAD_DV)),
            _const_spec((1, HEAD_DV)),
            _const_spec((HG_W, D_MODEL)),
            _const_spec((GLA_W, D_MODEL)),
            _const_spec((D_MODEL, D_MODEL)),
            pl.BlockSpec((1, HG_HEADS, HG_DK, HEAD_DV), lambda b, j: (b, 0, 0, 0)),
            pl.BlockSpec((1, n_gla_tiles, LANES, HEAD_DV), lambda b, j: (b, 0, 0, 0)),
            _const_spec((tb, tb)),
            _const_spec((CHUNK, SLAB_ROWS)),
            _const_spec((SAFE_PLAN.cum_rows, CHUNK)),
            _const_spec((SAFE_PLAN.n_masks, CHUNK, CHUNK)),
        ] + extra_specs,
        out_specs=[
            pl.BlockSpec((1, tb, D_MODEL), lambda b, j: (b, j, 0)),
            pl.BlockSpec((1, HG_HEADS, HG_DK, HEAD_DV), lambda b, j: (b, 0, 0, 0)),
            pl.BlockSpec((1, n_gla_tiles, LANES, HEAD_DV), lambda b, j: (b, 0, 0, 0)),
        ],
        out_shape=[
            jax.ShapeDtypeStruct((bsz, seq, D_MODEL), F32),
            jax.ShapeDtypeStruct((bsz, HG_HEADS, HG_DK, HEAD_DV), F32),
            jax.ShapeDtypeStruct((bsz, n_gla_tiles, LANES, HEAD_DV), F32),
        ],
        scratch_shapes=[
            pltpu.VMEM((tb, IN_COLS_PAD), F32),
            pltpu.VMEM((tb, HG_KW), F32),
            pltpu.VMEM((tb, GLA_KW), F32),
            pltpu.VMEM((HG_HEADS, HG_DK, HEAD_DV), F32),
            pltpu.VMEM((n_gla_tiles, LANES, HEAD_DV), F32),
        ] + extra_scratch,
        compiler_params=pltpu.CompilerParams(
            dimension_semantics=("arbitrary", "arbitrary"), vmem_limit_bytes=VMEM_LIMIT),
        name=f"mixer_l{layer}",
    )(x, mod, nrm, win, hg_lb, wgk2, bgk, hgn, glan, wa, wb, wo, shg0, sgla0, *plan_consts, *extra_args)


ROUTER_ROWS = 8 + N_EXPERTS
MOE_TILE = 512
TILE_ASSIGN = TOPK * MOE_TILE
EXPERT_BLOCK = 512


def _first_argmax_rows(vals, n):
    ridx = lax.broadcasted_iota(jnp.int32, vals.shape, 0)
    vmax = jnp.max(vals, axis=0, keepdims=True)
    imax = jnp.min(jnp.where(vals == vmax, ridx, n), axis=0, keepdims=True)
    return vmax, imax


def _router_kernel(x_ref, mod_ref, nrm_ref, wr_ref, br_ref, tri_ref,
                   h_ref, eid_ref, rank_ref, wts_ref, cnt_ref, run_scr):
    @pl.when(pl.program_id(0) == 0)
    def _():
        run_scr[...] = jnp.zeros_like(run_scr)

    u, lt, _ = x_ref.shape
    x = x_ref[...]
    sh2 = mod_ref[:, 3:4, :]
    sc2 = mod_ref[:, 4:5, :]
    h = _rms_mod(x, nrm_ref[...].reshape(1, 1, D_MODEL), sc2, sh2).reshape(u * lt, D_MODEL)
    h_ref[...] = h
    logits = lax.dot_general(wr_ref[...], h, (((1,), (1,)), ((), ())), preferred_element_type=F32,
                             precision=lax.Precision.HIGHEST) + br_ref[...]
    gl = logits[0:N_GROUPS]
    gmax, gi = _first_argmax_rows(gl, N_GROUPS)
    gp = 1.0 / jnp.sum(jnp.exp(gl - gmax), axis=0, keepdims=True)
    le = logits[8:8 + EXPERTS_PER_GROUP]
    for g in range(1, N_GROUPS):
        le = jnp.where(gi == g, logits[8 + g * EXPERTS_PER_GROUP:8 + (g + 1) * EXPERTS_PER_GROUP], le)
    pe = jnp.exp(le - jnp.max(le, axis=0, keepdims=True))
    pe = pe / jnp.sum(pe, axis=0, keepdims=True)
    v1, i1 = _first_argmax_rows(pe, EXPERTS_PER_GROUP)
    ridx = lax.broadcasted_iota(jnp.int32, pe.shape, 0)
    v2, i2 = _first_argmax_rows(jnp.where(ridx == i1, -1.0, pe), EXPERTS_PER_GROUP)
    vsum = v1 + v2
    wts_ref[0:1, :] = gp * v1 / vsum
    wts_ref[1:2, :] = gp * v2 / vsum
    eflat = jnp.concatenate([gi * EXPERTS_PER_GROUP + i1, gi * EXPERTS_PER_GROUP + i2], axis=1)
    eid_ref[0] = eflat
    onehot = (eflat == lax.broadcasted_iota(jnp.int32, (N_EXPERTS, TILE_ASSIGN), 0)).astype(F32)
    before = _dot(onehot.astype(BF16), tri_ref[...]) + run_scr[...]
    rank_ref[0] = jnp.sum(onehot * before, axis=0, keepdims=True).astype(jnp.int32)
    run_scr[...] = run_scr[...] + jnp.sum(onehot, axis=1, keepdims=True)
    cnt_ref[...] = run_scr[...].astype(jnp.int32)


def _router(x_units, mod_units, nrm, wr, br, tri):
    n_units, lt, _ = x_units.shape
    u = MOE_TILE // lt
    n_tiles = n_units // u
    return pl.pallas_call(
        _router_kernel,
        grid=(n_tiles,),
        in_specs=[
            pl.BlockSpec((u, lt, D_MODEL), lambda i: (i, 0, 0)),
            pl.BlockSpec((u, 6, D_MODEL), lambda i: (i, 0, 0)),
            pl.BlockSpec((1, D_MODEL), lambda i: (0, 0)),
            pl.BlockSpec((ROUTER_ROWS, D_MODEL), lambda i: (0, 0)),
            pl.BlockSpec((ROUTER_ROWS, 1), lambda i: (0, 0)),
            pl.BlockSpec((TILE_ASSIGN, TILE_ASSIGN), lambda i: (0, 0)),
        ],
        out_specs=[
            pl.BlockSpec((MOE_TILE, D_MODEL), lambda i: (i, 0)),
            pl.BlockSpec((1, 1, TILE_ASSIGN), lambda i: (i, 0, 0)),
            pl.BlockSpec((1, 1, TILE_ASSIGN), lambda i: (i, 0, 0)),
            pl.BlockSpec((TOPK, MOE_TILE), lambda i: (0, i)),
            pl.BlockSpec((N_EXPERTS, 1), lambda i: (0, 0)),
        ],
        out_shape=[
            jax.ShapeDtypeStruct((n_tiles * MOE_TILE, D_MODEL), F32),
            jax.ShapeDtypeStruct((n_tiles, 1, TILE_ASSIGN), jnp.int32),
            jax.ShapeDtypeStruct((n_tiles, 1, TILE_ASSIGN), jnp.int32),
            jax.ShapeDtypeStruct((TOPK, n_tiles * MOE_TILE), F32),
            jax.ShapeDtypeStruct((N_EXPERTS, 1), jnp.int32),
        ],
        scratch_shapes=[pltpu.VMEM((N_EXPERTS, 1), F32)],
        compiler_params=pltpu.CompilerParams(dimension_semantics=("arbitrary",)),
        name="moe_router",
    )(x_units, mod_units, nrm, wr, br, tri)


def _start_row_gather(idx_ref, n_rows, src_hbm, dst, sem):
    def body(r, carry):
        row = idx_ref[0, 0, r]
        pltpu.make_async_copy(src_hbm.at[pl.ds(row, 1)], dst.at[pl.ds(r, 1)], sem).start()
        return carry
    lax.fori_loop(0, n_rows, body, 0, unroll=8)


def _wait_row_gather(n_rows, src_hbm, dst, sem):
    pltpu.make_async_copy(src_hbm.at[pl.ds(0, n_rows)], dst, sem).wait()


def _dispatch_kernel(pend_ref, padded_ref, dest_ref, h_ref, xs_hbm, zbuf, sem):
    n_blocks = xs_hbm.shape[0] // EXPERT_BLOCK

    def zero_block(first_row):
        return pltpu.make_async_copy(
            zbuf, xs_hbm.at[pl.ds(pl.multiple_of(first_row, EXPERT_BLOCK), EXPERT_BLOCK)], sem.at[0])

    @pl.when(pl.program_id(0) == 0)
    def _():
        zbuf[...] = jnp.zeros_like(zbuf)
        n_used = pend_ref[N_EXPERTS - 1] // EXPERT_BLOCK
        for e in range(N_EXPERTS):
            @pl.when(padded_ref[e] > 0)
            def _():
                zero_block(pend_ref[e] - EXPERT_BLOCK).start()
        lax.fori_loop(n_used, n_blocks, lambda b, c: (zero_block(b * EXPERT_BLOCK).start(), c)[1], 0)
        for e in range(N_EXPERTS):
            @pl.when(padded_ref[e] > 0)
            def _():
                zero_block(pend_ref[e] - EXPERT_BLOCK).wait()
        lax.fori_loop(n_used, n_blocks, lambda b, c: (zero_block(b * EXPERT_BLOCK).wait(), c)[1], 0)

    def body(t, carry):
        for k in range(TOPK):
            slot = dest_ref[0, 0, k * MOE_TILE + t]
            pltpu.make_async_copy(h_ref.at[pl.ds(t, 1)], xs_hbm.at[pl.ds(slot, 1)], sem.at[1]).start()
        return carry
    lax.fori_loop(0, MOE_TILE, body, 0, unroll=8)
    for k in range(TOPK):
        pltpu.make_async_copy(h_ref, xs_hbm.at[pl.ds(0, MOE_TILE)], sem.at[1]).wait()


def _dispatch(pad_end, padded, dest_tiles, h, n_slots):
    n_tiles = dest_tiles.shape[0]
    grid_spec = pltpu.PrefetchScalarGridSpec(
        num_scalar_prefetch=2,
        grid=(n_tiles,),
        in_specs=[
            pl.BlockSpec((1, 1, TILE_ASSIGN), lambda i, pe, pd: (i, 0, 0), memory_space=pltpu.SMEM),
            pl.BlockSpec((MOE_TILE, D_MODEL), lambda i, pe, pd: (i, 0)),
        ],
        out_specs=pl.BlockSpec(memory_space=pl.ANY),
        scratch_shapes=[pltpu.VMEM((EXPERT_BLOCK, D_MODEL), F32), pltpu.SemaphoreType.DMA((2,))],
    )
    return pl.pallas_call(
        _dispatch_kernel,
        grid_spec=grid_spec,
        out_shape=jax.ShapeDtypeStruct((n_slots, D_MODEL), F32),
        compiler_params=pltpu.CompilerParams(dimension_semantics=("arbitrary",)),
        name="moe_dispatch",
    )(pad_end, padded, dest_tiles, h)


def _experts_kernel(be_ref, nused_ref, x_ref, wg_ref, wu_ref, wd_ref, o_ref):
    @pl.when(pl.program_id(0) < nused_ref[0])
    def _():
        xb = x_ref[...].astype(BF16)
        a = _silu(_dot(xb, wg_ref[0, 0])) * _dot(xb, wu_ref[0, 0])
        o_ref[...] = _dot(a.astype(BF16), wd_ref[0, 0])

    @pl.when(pl.program_id(0) >= nused_ref[0])
    def _():
        o_ref[...] = jnp.zeros_like(o_ref)


def _experts(block_e, n_used, xs, wg, wu, wd, layer):
    n_blocks = xs.shape[0] // EXPERT_BLOCK

    def row_block(i, be, nu):
        return (jnp.minimum(i, nu[0] - 1), 0)

    def expert_block(i, be, nu):
        return (layer, be[jnp.minimum(i, nu[0] - 1)], 0, 0)

    grid_spec = pltpu.PrefetchScalarGridSpec(
        num_scalar_prefetch=2,
        grid=(n_blocks,),
        in_specs=[
            pl.BlockSpec((EXPERT_BLOCK, D_MODEL), row_block),
            pl.BlockSpec((1, 1, D_MODEL, D_EXPERT), expert_block),
            pl.BlockSpec((1, 1, D_MODEL, D_EXPERT), expert_block),
            pl.BlockSpec((1, 1, D_EXPERT, D_MODEL), expert_block),
        ],
        out_specs=pl.BlockSpec((EXPERT_BLOCK, D_MODEL), lambda i, be, nu: (i, 0)),
    )
    return pl.pallas_call(
        _experts_kernel,
        grid_spec=grid_spec,
        out_shape=jax.ShapeDtypeStruct(xs.shape, F32),
        compiler_params=pltpu.CompilerParams(
            dimension_semantics=("arbitrary",), vmem_limit_bytes=VMEM_LIMIT),
        name="moe_experts",
    )(block_e, n_used, xs, wg, wu, wd)


def _combine_kernel(dst_cur_ref, dst_nxt_ref, x_ref, mod_ref, wts_ref, nrm_ref, y_hbm, o_ref, buf, sem,
                    *, final_norm):
    i = pl.program_id(0)
    n = pl.num_programs(0)
    slot = i % 2

    @pl.when(i == 0)
    def _():
        _start_row_gather(dst_cur_ref, TILE_ASSIGN, y_hbm, buf.at[0], sem.at[0])

    _wait_row_gather(TILE_ASSIGN, y_hbm, buf.at[slot], sem.at[slot])
    u, lt, _ = x_ref.shape
    n_pieces = MOE_TILE // CHUNK
    per_piece = TILE_ASSIGN // n_pieces
    for c in range(n_pieces):
        unit, r0 = divmod(c * CHUNK, lt)
        t0 = c * CHUNK
        y = (wts_ref[t0:t0 + CHUNK, 0:1] * buf[slot, t0:t0 + CHUNK, :]
             + wts_ref[t0:t0 + CHUNK, 1:2] * buf[slot, MOE_TILE + t0:MOE_TILE + t0 + CHUNK, :])
        out = x_ref[unit, r0:r0 + CHUNK, :] + mod_ref[unit, 5:6, :] * y
        if final_norm:
            out = out * lax.rsqrt(jnp.mean(out * out, axis=-1, keepdims=True) + NORM_EPS) * nrm_ref[...]
        o_ref[unit, r0:r0 + CHUNK, :] = out
        for r in range(c * per_piece, (c + 1) * per_piece):
            row = dst_nxt_ref[0, 0, r]
            pltpu.make_async_copy(y_hbm.at[pl.ds(row, 1)], buf.at[1 - slot, pl.ds(r, 1)],
                                  sem.at[1 - slot]).start()

    @pl.when(i == n - 1)
    def _():
        _wait_row_gather(TILE_ASSIGN, y_hbm, buf.at[1 - slot], sem.at[1 - slot])


def _combine(dest_tiles, x_units, mod_units, wts_col, nrm, y_slots, *, final_norm):
    n_units, lt, _ = x_units.shape
    u = MOE_TILE // lt
    n_tiles = n_units // u
    return pl.pallas_call(
        functools.partial(_combine_kernel, final_norm=final_norm),
        grid=(n_tiles,),
        in_specs=[
            pl.BlockSpec((1, 1, TILE_ASSIGN), lambda i: (i, 0, 0), memory_space=pltpu.SMEM),
            pl.BlockSpec((1, 1, TILE_ASSIGN), lambda i: (jnp.minimum(i + 1, n_tiles - 1), 0, 0),
                         memory_space=pltpu.SMEM),
            pl.BlockSpec((u, lt, D_MODEL), lambda i: (i, 0, 0)),
            pl.BlockSpec((u, 6, D_MODEL), lambda i: (i, 0, 0)),
            pl.BlockSpec((MOE_TILE, TOPK), lambda i: (i, 0)),
            pl.BlockSpec((1, D_MODEL), lambda i: (0, 0)),
            pl.BlockSpec(memory_space=pl.ANY),
        ],
        out_specs=pl.BlockSpec((u, lt, D_MODEL), lambda i: (i, 0, 0)),
        out_shape=jax.ShapeDtypeStruct(x_units.shape, F32),
        scratch_shapes=[pltpu.VMEM((2, TILE_ASSIGN, D_MODEL), F32), pltpu.SemaphoreType.DMA((2,))],
        compiler_params=pltpu.CompilerParams(
            dimension_semantics=("arbitrary",), vmem_limit_bytes=VMEM_LIMIT),
        name="moe_combine",
    )(dest_tiles, dest_tiles, x_units, mod_units, wts_col, nrm, y_slots)


def _routing_tables(eid_tiles, rank_tiles, counts):
    n_blocks = eid_tiles.size // EXPERT_BLOCK + N_EXPERTS
    padded = (counts + EXPERT_BLOCK - 1) // EXPERT_BLOCK * EXPERT_BLOCK
    pad_end = jnp.cumsum(padded).astype(jnp.int32)
    pad_start = pad_end - padded
    block_start = jnp.arange(n_blocks, dtype=jnp.int32)[:, None] * EXPERT_BLOCK
    block_e = jnp.minimum(jnp.sum((block_start >= pad_end[None, :]).astype(jnp.int32), axis=1),
                          N_EXPERTS - 1).astype(jnp.int32)
    n_used = pad_end[-1:] // EXPERT_BLOCK
    experts = jnp.arange(N_EXPERTS, dtype=jnp.int32)
    first_slot = jnp.sum(jnp.where(eid_tiles[..., None] == experts, pad_start, 0), axis=-1)
    return block_e, n_used, pad_end, padded, first_slot + rank_tiles


def _moe_units(x, mod_l):
    bsz, seq, _ = x.shape
    lt = min(seq, MOE_TILE)
    per = seq // lt
    x_units = x.reshape(bsz * seq // lt, lt, D_MODEL)
    mod_units = jnp.repeat(mod_l, per, axis=0) if per > 1 else mod_l
    return x_units, mod_units


def _moe_experts(x, mod_l, nrm_ffn, wr, br, tri, wg, wu, wd, layer):
    x_units, mod_units = _moe_units(x, mod_l)
    h, eid_tiles, rank_tiles, wts, counts = _router(x_units, mod_units, nrm_ffn, wr, br, tri)
    block_e, n_used, pad_end, padded, dest_tiles = _routing_tables(eid_tiles, rank_tiles, counts[:, 0])
    n_slots = block_e.shape[0] * EXPERT_BLOCK
    xs = _dispatch(pad_end, padded, dest_tiles, h, n_slots)
    return dest_tiles, wts.T, _experts(block_e, n_used, xs, wg, wu, wd, layer)


def _dest_per_step(dest_tiles, tb):
    n_tiles = dest_tiles.shape[0]
    per = MOE_TILE // tb
    d = dest_tiles.reshape(n_tiles, TOPK, per, tb).transpose(0, 2, 1, 3)
    return d.reshape(n_tiles * per, 1, TOPK * tb)


def kernel(x_prompt, x_sample, c_prompt, c_sample, state_hgrn, state_gla, w_ada, b_ada, norm_mix,
           norm_ffn, w_in, hg_lb, hg_onorm, w_gk2, b_gk, gla_onorm, w_br_a, w_br_b, w_out, w_rg, b_rg,
           w_re, b_re, w_e_gate, w_e_up, w_e_down, norm_final):
    bp = x_prompt.shape[0]
    bs = x_sample.shape[0]
    mod = _ada_mod(jnp.concatenate([c_prompt, c_sample], axis=0), w_ada, b_ada)
    mod = mod.reshape(DEPTH, bp + bs, 6, D_MODEL)

    glr0 = C_GOG + GLA_W
    win_r = jnp.concatenate(
        [w_in[:, :, :glr0], w_in[:, :, glr0 + GLA_GATE_RANK:], w_in[:, :, glr0:glr0 + GLA_GATE_RANK],
         jnp.zeros((DEPTH, D_MODEL, LANES - GLA_GATE_RANK), F32)], axis=2).astype(BF16)
    wgk2_p = jnp.concatenate(
        [w_gk2, jnp.zeros((DEPTH, LANES - GLA_GATE_RANK, GLA_KW), F32)], axis=1).astype(BF16)
    wa_b = w_br_a.astype(BF16)
    wb_b = w_br_b.astype(BF16)
    wo_b = w_out.astype(BF16)
    def plan_consts(tb):
        r = np.arange(tb)
        chunk_tril = (r[:, None] // CHUNK == r[None, :] // CHUNK) & (r[None, :] <= r[:, None])
        return [jnp.asarray(chunk_tril, BF16), jnp.asarray(_slab_mask(), F32),
                jnp.asarray(SAFE_PLAN.segment_sum_matrix(), BF16), jnp.asarray(SAFE_PLAN.masks(), F32)]
    zpad = jnp.zeros((DEPTH, 8 - N_GROUPS, D_MODEL), F32)
    wr = jnp.concatenate([jnp.swapaxes(w_rg, 1, 2), zpad, jnp.swapaxes(w_re, 1, 2)], axis=1)
    br = jnp.concatenate([b_rg, jnp.zeros((DEPTH, 8 - N_GROUPS), F32), b_re], axis=1)[:, :, None]
    wg_b = w_e_gate.astype(BF16)
    wu_b = w_e_up.astype(BF16)
    wd_b = w_e_down.astype(BF16)
    nrm_f = norm_final.reshape(1, D_MODEL)
    assign = np.arange(TILE_ASSIGN)
    tri = jnp.asarray(assign[:, None] < assign[None, :], BF16)

    def run(x, mod_g, shg, sgla, tb):
        bsz = x.shape[0]
        new_hg, new_gla = [], []
        pending = None
        for l in range(DEPTH):
            x, s1, s2 = _mixer(
                x, mod_g[l], norm_mix[l:l + 1], win_r[l], hg_lb, wgk2_p[l], b_gk[l:l + 1],
                hg_onorm[l:l + 1], gla_onorm[l:l + 1], wa_b[l], wb_b[l], wo_b[l],
                shg[l], sgla[l].reshape(bsz, GLA_KW // LANES, LANES, HEAD_DV), plan_consts(tb),
                layer=l, tb=tb, pending_moe=pending)
            new_hg.append(s1)
            new_gla.append(s2.reshape(bsz, GLA_HEADS, GLA_DK, HEAD_DV))
            dest_tiles, wts_col, y_slots = _moe_experts(
                x, mod_g[l], norm_ffn[l:l + 1], wr[l], br[l], tri, wg_b, wu_b, wd_b, l)
            pending = (_dest_per_step(dest_tiles, tb), wts_col, mod_g[l], y_slots)
        x_units, mod_units = _moe_units(x, mod_g[DEPTH - 1])
        y = _combine(dest_tiles, x_units, mod_units, wts_col, nrm_f, y_slots, final_norm=True)
        return y.reshape(x.shape), jnp.stack(new_hg), jnp.stack(new_gla)

    zeros_hg = jnp.zeros((DEPTH, bp, HG_HEADS, HG_DK, HEAD_DV), F32)
    zeros_gla = jnp.zeros((DEPTH, bp, GLA_HEADS, GLA_DK, HEAD_DV), F32)
    y_p, hg_p, gla_p = run(x_prompt, mod[:, :bp], zeros_hg, zeros_gla, 256)
    y_s, hg_s, gla_s = run(x_sample, mod[:, bp:], state_hgrn, state_gla, CHUNK)
    return (y_p, y_s, hg_p, gla_p, hg_s, gla_s)
```

```python
import functools

import numpy as np
import jax
import jax.numpy as jnp
from jax import lax
from jax.experimental import pallas as pl
from jax.experimental.pallas import tpu as pltpu

F32 = jnp.float32
BF16 = jnp.bfloat16

D_MODEL = 1024
DEPTH = 2
CHUNK = 64
NORM_EPS = 1e-6
LOG_FLOOR = 1e-30
HG_HEADS = 4
HG_DK = 128
HEAD_DV = 128
HG_KW = HG_HEADS * HG_DK
HG_W = HG_HEADS * HEAD_DV
GLA_HEADS = 4
GLA_DK = 64
GLA_KW = GLA_HEADS * GLA_DK
GLA_W = GLA_HEADS * HEAD_DV
GLA_GATE_RANK = 16
GLA_GATE_NORM = 16.0
N_GROUPS = 4
EXPERTS_PER_GROUP = 8
N_EXPERTS = N_GROUPS * EXPERTS_PER_GROUP
TOPK = 2
D_EXPERT = 512

LANES = 128
VMEM_LIMIT = 56 * 1024 * 1024

C_HQ = 0
C_HF = C_HQ + HG_KW
C_HI = C_HF + HG_KW
C_HOG = C_HI + HG_W
C_GQ = C_HOG + HG_W
C_GK = C_GQ + GLA_KW
C_GV = C_GK + GLA_KW
C_GOG = C_GV + GLA_W
C_GA = C_GOG + GLA_W
C_GB = C_GA + D_MODEL
C_GLR = C_GB + D_MODEL
IN_COLS_PAD = C_GLR + LANES
MXU_WIDTH = 256
PROJ_TILE = 4 * MXU_WIDTH

class _ScorePlan:
    def __init__(self, levels, adjacent, diag_block):
        self.levels = levels
        self.adjacent = adjacent
        self.diag_block = diag_block
        self.cum_rows = (2 * len(levels) + 1) * CHUNK
        self.n_masks = len(levels) + 1 + int(adjacent)

    def segment_sum_matrix(self):
        t = np.arange(CHUNK)[:, None]
        r = np.arange(CHUNK)[None, :]
        rows = []
        for m in self.levels:
            same = (t // m) == (r // m)
            rows.append(same & (r <= t))
            rows.append(same & (r > t))
        rows.append(r <= t)
        return np.concatenate(rows, axis=0).astype(np.float32)

    def masks(self):
        t = np.arange(CHUNK)[:, None]
        s = np.arange(CHUNK)[None, :]
        masks = [((t // self.diag_block) == (s // self.diag_block)) & (s <= t)]
        for m in self.levels + ((1,) if self.adjacent else ()):
            masks.append(((t // (2 * m)) == (s // (2 * m))) & ((t // m) % 2 == 1) & ((s // m) % 2 == 0))
        return np.stack(masks).astype(np.float32)


SAFE_PLAN = _ScorePlan((32, 16, 8, 4, 2), True, 1)


def _dot(a, b):
    return jnp.dot(a, b, preferred_element_type=F32)


def _dot_nt(a, b):
    return lax.dot_general(a, b, (((1,), (1,)), ((), ())), preferred_element_type=F32)


def _sigmoid(x):
    return 1.0 / (1.0 + jnp.exp(-x))


def _silu(x):
    return x * _sigmoid(x)


def _rms_mod(x, gain, scale, shift):
    y = x * lax.rsqrt(jnp.mean(x * x, axis=-1, keepdims=True) + NORM_EPS)
    return y * gain * (1.0 + scale) + shift


def _ada_kernel(c_ref, w_ref, b_ref, o_ref):
    c = c_ref[...]
    o_ref[0] = jnp.dot(_silu(c), w_ref[0], preferred_element_type=F32,
                       precision=lax.Precision.HIGHEST) + b_ref[0]


def _ada_mod(c_all, w_ada, b_ada):
    nb = c_all.shape[0]
    tn = 512
    return pl.pallas_call(
        _ada_kernel,
        grid=(DEPTH, 6 * D_MODEL // tn),
        in_specs=[
            pl.BlockSpec((nb, D_MODEL), lambda l, j: (0, 0)),
            pl.BlockSpec((1, D_MODEL, tn), lambda l, j: (l, 0, j)),
            pl.BlockSpec((1, 1, tn), lambda l, j: (l, 0, j)),
        ],
        out_specs=pl.BlockSpec((1, nb, tn), lambda l, j: (l, 0, j)),
        out_shape=jax.ShapeDtypeStruct((DEPTH, nb, 6 * D_MODEL), F32),
        name="ada_mod",
    )(c_all, w_ada, b_ada.reshape(DEPTH, 1, 6 * D_MODEL))


def _chunk_attention(q, k, v, g, states, mall_ref, mask_ref, heads_per_tile, plan):
    w = q.shape[1]
    n_tiles = w // LANES
    g_hi = g.astype(BF16)
    r1 = g - g_hi.astype(F32)
    g_mid = r1.astype(BF16)
    g_lo = (r1 - g_mid.astype(F32)).astype(BF16)
    mall = mall_ref[...]
    cums = _dot(mall, g_hi) + _dot(mall, g_mid) + _dot(mall, g_lo)
    b = cums[plan.cum_rows - CHUNK:plan.cum_rows]
    level_q = []
    level_k = []
    for i in range(len(plan.levels)):
        level_q.append(q * jnp.exp(cums[2 * i * CHUNK:(2 * i + 1) * CHUNK]))
        level_k.append(k * jnp.exp(cums[(2 * i + 1) * CHUNK:(2 * i + 2) * CHUNK]))
    if plan.diag_block == 1:
        qs = [q]
        ks = [k]
    else:
        i = plan.levels.index(plan.diag_block)
        qs = [level_q[i]]
        ks = [k * jnp.exp(-cums[2 * i * CHUNK:(2 * i + 1) * CHUNK])]
    qs += level_q
    ks += level_k
    if plan.adjacent:
        qs.append(q * jnp.exp(g))
        ks.append(k)
    b_last = b[CHUNK - 1:CHUNK]
    q_in = q * jnp.exp(b)
    k_out = k * jnp.exp(b_last - b)
    e_last = jnp.exp(b_last)

    dk = LANES // heads_per_tile
    lane = lax.broadcasted_iota(jnp.int32, (CHUNK, LANES), 1)
    row = lax.broadcasted_iota(jnp.int32, (LANES, HEAD_DV), 0)
    outs = []
    new_states = []
    for ti in range(n_tiles):
        sl = slice(ti * LANES, (ti + 1) * LANES)
        ks_t =[kk[:, sl].astype(BF16) for kk in ks]
        k_out_t = k_out[:, sl].T.astype(BF16)
        e_col = jnp.broadcast_to(e_last[:, sl], (LANES, LANES)).T
        s_old = states[ti]
        s_old_b = s_old.astype(BF16)
        upd = None
        for j in range(heads_per_tile):
            head = ti * heads_per_tile + j
            if heads_per_tile == 1:
                sel = lambda a: a
            else:
                in_head = (lane // dk) == j
                sel = lambda a, in_head=in_head: jnp.where(in_head, a, 0.0)
            sc = jnp.zeros((CHUNK, CHUNK), F32)
            for i in range(plan.n_masks):
                sc = sc + _dot_nt(sel(qs[i][:, sl]).astype(BF16), ks_t[i]) * mask_ref[i]
            vh = v[:, head * HEAD_DV:(head + 1) * HEAD_DV].astype(BF16)
            o = _dot(sc.astype(BF16), vh) + _dot(sel(q_in[:, sl]).astype(BF16), s_old_b)
            outs.append(o)
            u = _dot(k_out_t, vh)
            upd = u if upd is None else jnp.where((row // dk) == j, u, upd)
        new_states.append(e_col * s_old + upd)
    return jnp.concatenate(outs, axis=1), new_states


FAST_BLOCK = 16
N_SUB = CHUNK // FAST_BLOCK
SLAB_ROWS = FAST_BLOCK * (N_SUB * (N_SUB - 1) // 2) + CHUNK
FAST_BLOCK_DECAY_LIMIT = 60.0


def _slab_mask():
    t = np.arange(CHUNK)[:, None]
    cols = []
    for i in range(1, N_SUB):
        cols.append(np.broadcast_to(t // FAST_BLOCK == i, (CHUNK, i * FAST_BLOCK)))
    s = np.arange(CHUNK)[None, :]
    cols.append((t // FAST_BLOCK == s // FAST_BLOCK) & (s <= t))
    return np.concatenate(cols, axis=1).astype(np.float32)


def _block_attention_fast(q, k, v, g, states, tril_ref, slab_mask_ref, heads_per_tile):
    rows, w = q.shape
    n_chunks = rows // CHUNK
    n_tiles = w // LANES
    dk = LANES // heads_per_tile
    n_heads = n_tiles * heads_per_tile

    g_hi = g.astype(BF16)
    r1 = g - g_hi.astype(F32)
    g_mid = r1.astype(BF16)
    g_lo = (r1 - g_mid.astype(F32)).astype(BF16)
    tril = tril_ref[...]
    b = _dot(tril, g_hi) + _dot(tril, g_mid) + _dot(tril, g_lo)

    def end_row(c, i):
        r = c * CHUNK + (i + 1) * FAST_BLOCK
        return b[r - 1:r]

    def per_block(row_of):
        return jnp.concatenate([jnp.broadcast_to(row_of(c, i), (FAST_BLOCK, w))
                                for c in range(n_chunks) for i in range(N_SUB)], axis=0)

    zero = jnp.zeros((1, w), F32)
    b_start = per_block(lambda c, i: zero if i == 0 else end_row(c, i - 1))
    b_end = per_block(end_row)
    q_blk = q * jnp.exp(b - b_start)
    k_diag = k * jnp.exp(b_start - b)
    k_end = k * jnp.exp(b_end - b)
    q_in = q_blk * jnp.exp(b_start)
    k_out = k_end * jnp.exp(per_block(lambda c, i: end_row(c, N_SUB - 1)) - b_end)

    lane = lax.broadcasted_iota(jnp.int32, (CHUNK, LANES), 1)
    row = lax.broadcasted_iota(jnp.int32, (LANES, HEAD_DV), 0)
    slab_mask = slab_mask_ref[...]

    def sel(a, j):
        return a if heads_per_tile == 1 else jnp.where((lane // dk) == j, a, 0.0)

    v_b = v.astype(BF16)

    scores = {}
    for c in range(n_chunks):
        r0 = c * CHUNK
        slabs = []
        for i in range(1, N_SUB):
            for jb in range(i):
                blk = k_end[r0 + jb * FAST_BLOCK:r0 + (jb + 1) * FAST_BLOCK]
                slabs.append(blk if jb == i - 1 else blk * jnp.exp(end_row(c, i - 1) - end_row(c, jb)))
        slabs.append(k_diag[r0:r0 + CHUNK])
        k_slab = jnp.concatenate(slabs, axis=0).astype(BF16)
        for ti in range(n_tiles):
            sl = slice(ti * LANES, (ti + 1) * LANES)
            for j in range(heads_per_tile):
                qh = sel(q_blk[r0:r0 + CHUNK, sl], j).astype(BF16)
                scores[c, ti * heads_per_tile + j] = (_dot_nt(qh, k_slab[:, sl]) * slab_mask).astype(BF16)

    entering = [list(states)]
    for c in range(n_chunks):
        r0 = c * CHUNK
        nxt = []
        for ti in range(n_tiles):
            sl = slice(ti * LANES, (ti + 1) * LANES)
            k_out_t = k_out[r0:r0 + CHUNK, sl].T.astype(BF16)
            upd = None
            for j in range(heads_per_tile):
                head = ti * heads_per_tile + j
                u = _dot(k_out_t, v_b[r0:r0 + CHUNK, head * HEAD_DV:(head + 1) * HEAD_DV])
                upd = u if upd is None else jnp.where((row // dk) == j, u, upd)
            e_col = jnp.broadcast_to(jnp.exp(end_row(c, N_SUB - 1)[:, sl]), (LANES, LANES)).T
            nxt.append(e_col * entering[c][ti] + upd)
        entering.append(nxt)

    out_rows = []
    for c in range(n_chunks):
        r0 = c * CHUNK
        outs = []
        for head in range(n_heads):
            ti, j = divmod(head, heads_per_tile)
            sl = slice(ti * LANES, (ti + 1) * LANES)
            vh = v_b[r0:r0 + CHUNK, head * HEAD_DV:(head + 1) * HEAD_DV]
            v_slab = jnp.concatenate([vh[:i * FAST_BLOCK] for i in range(1, N_SUB)] + [vh], axis=0)
            outs.append(_dot(scores[c, head], v_slab)
                        + _dot(sel(q_in[r0:r0 + CHUNK, sl], j).astype(BF16), entering[c][ti].astype(BF16)))
        out_rows.append(jnp.concatenate(outs, axis=1))
    return jnp.concatenate(out_rows, axis=0), entering[n_chunks]


def _head_norm_gate(o, gain, gate):
    outs = []
    for h in range(o.shape[1] // HEAD_DV):
        sl = slice(h * HEAD_DV, (h + 1) * HEAD_DV)
        oh = o[:, sl]
        oh = oh * lax.rsqrt(jnp.mean(oh * oh, axis=-1, keepdims=True) + NORM_EPS) * gain
        outs.append(oh * _silu(gate[:, sl]))
    return jnp.concatenate(outs, axis=1)


def _mixer_kernel(x_ref, mod_ref, nrm_ref, win_ref, lb_ref, wgk2_ref, bgk_ref, hgn_ref, glan_ref,
                  wa_ref, wb_ref, wo_ref, shg0_ref, sgla0_ref,
                  tril_ref, slab_mask_ref, mall_safe_ref, mask_safe_ref, *rest, layer, tb, pending_moe):
    if pending_moe:
        dst_cur_ref, dst_nxt_ref, wts_ref, modp_ref, y_hbm = rest[:5]
        rest = rest[5:]
    xo_ref, shg_o_ref, sgla_o_ref, p_scr, k_scr, lg_scr, shg_scr, sgla_scr = rest[:8]
    j = pl.program_id(1)

    @pl.when(j == 0)
    def _():
        shg_scr[...] = shg0_ref[0]
        sgla_scr[...] = sgla0_ref[0]

    x = x_ref[0]
    n_prefetch = 0
    if pending_moe:
        cbuf, csem = rest[8:]
        n_prefetch = TOPK * tb
        step = pl.program_id(0) * pl.num_programs(1) + j
        last = pl.num_programs(0) * pl.num_programs(1) - 1
        slot = step % 2

        @pl.when(step == 0)
        def _():
            _start_row_gather(dst_cur_ref, n_prefetch, y_hbm, cbuf.at[0], csem.at[0])

        _wait_row_gather(n_prefetch, y_hbm, cbuf.at[slot], csem.at[slot])
        x = x + modp_ref[0, 5:6, :] * (wts_ref[:, 0:1] * cbuf[slot, 0:tb, :]
                                       + wts_ref[:, 1:2] * cbuf[slot, tb:n_prefetch, :])

    def prefetch_rows(r0, r1):
        for r in range(r0, r1):
            row = dst_nxt_ref[0, 0, r]
            pltpu.make_async_copy(y_hbm.at[pl.ds(row, 1)], cbuf.at[1 - slot, pl.ds(r, 1)],
                                  csem.at[1 - slot]).start()

    sh1 = mod_ref[0, 0:1, :]
    sc1 = mod_ref[0, 1:2, :]
    g1 = mod_ref[0, 2:3, :]
    hb = _rms_mod(x, nrm_ref[...], sc1, sh1).astype(BF16)
    col_tiles = list(range(0, IN_COLS_PAD, PROJ_TILE))
    for i, c in enumerate(col_tiles):
        c1 = min(c + PROJ_TILE, IN_COLS_PAD)
        p_scr[:, c:c1] = _dot(hb, win_ref[:, c:c1])
        if pending_moe:
            prefetch_rows(n_prefetch * i // len(col_tiles), n_prefetch * (i + 1) // len(col_tiles))

    lb_all = lb_ref[...]
    lb_max = jnp.max(lb_all, axis=0, keepdims=True)
    lb_exp = jnp.exp(lb_all - lb_max)
    sm = lb_exp / jnp.sum(lb_exp, axis=0, keepdims=True)
    lbl = jnp.clip(jnp.sum(sm[0:layer + 1], axis=0, keepdims=True) - sm[0:1], 0.0, 1.0)

    p_scr[:, C_HQ:C_HQ + HG_KW] = _silu(p_scr[:, C_HQ:C_HQ + HG_KW]) * (HG_DK ** -0.5)
    z = p_scr[:, C_HF:C_HF + HG_KW]
    f = lbl + (1.0 - lbl) * _sigmoid(z)
    p_scr[:, C_HF:C_HF + HG_KW] = jnp.log(jnp.maximum(f, LOG_FLOOR))
    k_scr[...] = (1.0 - lbl) * _sigmoid(-z)
    glr = p_scr[:, C_GLR:C_GLR + LANES].astype(BF16)
    gate = _dot(glr, wgk2_ref[...]) + bgk_ref[...]
    lg_scr[...] = (jnp.minimum(gate, 0.0) - jnp.log1p(jnp.exp(-jnp.abs(gate)))) * (1.0 / GLA_GATE_NORM)
    p_scr[:, C_GQ:C_GQ + GLA_KW] = p_scr[:, C_GQ:C_GQ + GLA_KW] * (GLA_DK ** -0.5)

    n_hg_tiles = HG_KW // LANES
    n_gla_tiles = GLA_KW // LANES

    def one_chunk(rows, states, attend):
        o_hg, st_hg = attend(p_scr[rows, C_HQ:C_HQ + HG_KW], k_scr[rows, :],
                             p_scr[rows, C_HI:C_HI + HG_W], p_scr[rows, C_HF:C_HF + HG_KW],
                             states[:n_hg_tiles], 1)
        p_scr[rows, C_HI:C_HI + HG_W] = o_hg
        o_gla, st_gla = attend(p_scr[rows, C_GQ:C_GQ + GLA_KW], p_scr[rows, C_GK:C_GK + GLA_KW],
                               p_scr[rows, C_GV:C_GV + GLA_W], lg_scr[rows, :],
                               states[n_hg_tiles:], 2)
        p_scr[rows, C_GV:C_GV + GLA_W] = o_gla
        return st_hg + st_gla

    def attend_safe(q, k, v, g, states, heads_per_tile):
        return _chunk_attention(q, k, v, g, states, mall_safe_ref, mask_safe_ref, heads_per_tile, SAFE_PLAN)

    def load_states():
        return [shg_scr[t] for t in range(n_hg_tiles)] + [sgla_scr[t] for t in range(n_gla_tiles)]

    def store_states(states):
        for t in range(n_hg_tiles):
            shg_scr[t] = states[t]
        for t in range(n_gla_tiles):
            sgla_scr[t] = states[n_hg_tiles + t]

    def run_block_fast():
        states = load_states()
        o_hg, st_hg = _block_attention_fast(
            p_scr[:, C_HQ:C_HQ + HG_KW], k_scr[...], p_scr[:, C_HI:C_HI + HG_W],
            p_scr[:, C_HF:C_HF + HG_KW], states[:n_hg_tiles], tril_ref, slab_mask_ref, 1)
        p_scr[:, C_HI:C_HI + HG_W] = o_hg
        o_gla, st_gla = _block_attention_fast(
            p_scr[:, C_GQ:C_GQ + GLA_KW], p_scr[:, C_GK:C_GK + GLA_KW], p_scr[:, C_GV:C_GV + GLA_W],
            lg_scr[...], states[n_hg_tiles:], tril_ref, slab_mask_ref, 2)
        p_scr[:, C_GV:C_GV + GLA_W] = o_gla
        store_states(st_hg + st_gla)

    def run_chunks_safe():
        def chunk_body(ci, carry):
            rows = pl.ds(pl.multiple_of(ci * CHUNK, CHUNK), CHUNK)
            store_states(one_chunk(rows, load_states(), attend_safe))
            return carry

        lax.fori_loop(0, tb // CHUNK, chunk_body, 0)

    blk = FAST_BLOCK
    min_hg = jnp.min(jnp.sum(p_scr[:, C_HF:C_HF + HG_KW].reshape(tb // blk, blk, HG_KW), axis=1))
    min_gla = jnp.min(jnp.sum(lg_scr[...].reshape(tb // blk, blk, GLA_KW), axis=1))
    bounded = jnp.minimum(min_hg, min_gla) >= -FAST_BLOCK_DECAY_LIMIT

    @pl.when(bounded)
    def _():
        run_block_fast()

    @pl.when(jnp.logical_not(bounded))
    def _():
        run_chunks_safe()

    o_hg = _head_norm_gate(p_scr[:, C_HI:C_HI + HG_W], hgn_ref[...], p_scr[:, C_HOG:C_HOG + HG_W])
    o_gla = _head_norm_gate(p_scr[:, C_GV:C_GV + GLA_W], glan_ref[...], p_scr[:, C_GOG:C_GOG + GLA_W])
    ya = _dot(o_hg.astype(BF16), wa_ref[...])
    yb = _dot(o_gla.astype(BF16), wb_ref[...])
    merged = (_sigmoid(p_scr[:, C_GA:C_GA + D_MODEL]) * ya
              + _sigmoid(p_scr[:, C_GB:C_GB + D_MODEL]) * yb)
    m = _dot(merged.astype(BF16), wo_ref[...])
    xo_ref[0] = x + g1 * m

    @pl.when(j == pl.num_programs(1) - 1)
    def _():
        shg_o_ref[0] = shg_scr[...]
        sgla_o_ref[0] = sgla_scr[...]

    if pending_moe:
        @pl.when(step == last)
        def _():
            _wait_row_gather(n_prefetch, y_hbm, cbuf.at[1 - slot], csem.at[1 - slot])


def _const_spec(shape):
    nd = len(shape)
    return pl.BlockSpec(shape, lambda b, j, nd=nd: (0,) * nd, pipeline_mode=pl.Buffered(1))


def _mixer(x, mod, nrm, win, hg_lb, wgk2, bgk, hgn, glan, wa, wb, wo, shg0, sgla0, plan_consts,
           *, layer, tb, pending_moe=None):
    bsz, seq, _ = x.shape
    nj = seq // tb
    kern = functools.partial(_mixer_kernel, layer=layer, tb=tb, pending_moe=pending_moe is not None)
    n_gla_tiles = GLA_KW // LANES
    extra_specs, extra_args, extra_scratch = [], [], []
    if pending_moe is not None:
        dest_steps, wts_col, mod_prev, y_slots = pending_moe
        n_steps = bsz * nj
        extra_specs = [
            pl.BlockSpec((1, 1, TOPK * tb), lambda b, j: (b * nj + j, 0, 0), memory_space=pltpu.SMEM),
            pl.BlockSpec((1, 1, TOPK * tb), lambda b, j: (jnp.minimum(b * nj + j + 1, n_steps - 1), 0, 0),
                         memory_space=pltpu.SMEM),
            pl.BlockSpec((tb, TOPK), lambda b, j: (b * nj + j, 0)),
            pl.BlockSpec((1, 6, D_MODEL), lambda b, j: (b, 0, 0)),
            pl.BlockSpec(memory_space=pl.ANY),
        ]
        extra_args = [dest_steps, dest_steps, wts_col, mod_prev, y_slots]
        extra_scratch = [pltpu.VMEM((2, TOPK * tb, D_MODEL), F32), pltpu.SemaphoreType.DMA((2,))]
    return pl.pallas_call(
        kern,
        grid=(bsz, nj),
        in_specs=[
            pl.BlockSpec((1, tb, D_MODEL), lambda b, j: (b, j, 0)),
            pl.BlockSpec((1, 6, D_MODEL), lambda b, j: (b, 0, 0)),
            _const_spec((1, D_MODEL)),
            _const_spec((D_MODEL, IN_COLS_PAD)),
            _const_spec((DEPTH, HG_KW)),
            _const_spec((LANES, GLA_KW)),
            _const_spec((1, GLA_KW)),
            _const_spec((1, HEAD_DV)),
            _const_spec((1, HEAD_DV)),
            _const_spec((HG_W, D_MODEL)),
            _const_spec((GLA_W, D_MODEL)),
            _const_spec((D_MODEL, D_MODEL)),
            pl.BlockSpec((1, HG_HEADS, HG_DK, HEAD_DV), lambda b, j: (b, 0, 0, 0)),
            pl.BlockSpec((1, n_gla_tiles, LANES, HEAD_DV), lambda b, j: (b, 0, 0, 0)),
            _const_spec((tb, tb)),
            _const_spec((CHUNK, SLAB_ROWS)),
            _const_spec((SAFE_PLAN.cum_rows, CHUNK)),
            _const_spec((SAFE_PLAN.n_masks, CHUNK, CHUNK)),
        ] + extra_specs,
        out_specs=[
            pl.BlockSpec((1, tb, D_MODEL), lambda b, j: (b, j, 0)),
            pl.BlockSpec((1, HG_HEADS, HG_DK, HEAD_DV), lambda b, j: (b, 0, 0, 0)),
            pl.BlockSpec((1, n_gla_tiles, LANES, HEAD_DV), lambda b, j: (b, 0, 0, 0)),
        ],
        out_shape=[
            jax.ShapeDtypeStruct((bsz, seq, D_MODEL), F32),
            jax.ShapeDtypeStruct((bsz, HG_HEADS, HG_DK, HEAD_DV), F32),
            jax.ShapeDtypeStruct((bsz, n_gla_tiles, LANES, HEAD_DV), F32),
        ],
        scratch_shapes=[
            pltpu.VMEM((tb, IN_COLS_PAD), F32),
            pltpu.VMEM((tb, HG_KW), F32),
            pltpu.VMEM((tb, GLA_KW), F32),
            pltpu.VMEM((HG_HEADS, HG_DK, HEAD_DV), F32),
            pltpu.VMEM((n_gla_tiles, LANES, HEAD_DV), F32),
        ] + extra_scratch,
        compiler_params=pltpu.CompilerParams(
            dimension_semantics=("arbitrary", "arbitrary"), vmem_limit_bytes=VMEM_LIMIT),
        name=f"mixer_l{layer}",
    )(x, mod, nrm, win, hg_lb, wgk2, bgk, hgn, glan, wa, wb, wo, shg0, sgla0, *plan_consts, *extra_args)


ROUTER_ROWS = 48
MOE_TILE = 512
TILE_ASSIGN = TOPK * MOE_TILE
EXPERT_BLOCK = 512


def _first_argmax_rows(vals, n):
    ridx = lax.broadcasted_iota(jnp.int32, vals.shape, 0)
    vmax = jnp.max(vals, axis=0, keepdims=True)
    imax = jnp.min(jnp.where(vals == vmax, ridx, n), axis=0, keepdims=True)
    return vmax, imax


def _router_kernel(x_ref, mod_ref, nrm_ref, wr_ref, br_ref, tri_ref,
                   h_ref, eid_ref, rank_ref, wts_ref, cnt_ref, run_scr):
    @pl.when(pl.program_id(0) == 0)
    def _():
        run_scr[...] = jnp.zeros_like(run_scr)

    u, lt, _ = x_ref.shape
    x = x_ref[...]
    sh2 = mod_ref[:, 3:4, :]
    sc2 = mod_ref[:, 4:5, :]
    h = _rms_mod(x, nrm_ref[...].reshape(1, 1, D_MODEL), sc2, sh2).reshape(u * lt, D_MODEL)
    h_ref[...] = h
    h_hi = h.astype(BF16)
    h_lo = (h - h_hi.astype(F32)).astype(BF16)
    w = wr_ref[...]
    w_hi = w.astype(BF16)
    w_lo = (w - w_hi.astype(F32)).astype(BF16)
    p_hi = _dot_nt(jnp.concatenate([w_hi, w_lo], axis=0), h_hi)
    logits = p_hi[:ROUTER_ROWS] + p_hi[ROUTER_ROWS:] + _dot_nt(w_hi, h_lo) + br_ref[...]
    gl = logits[0:N_GROUPS]
    gmax, gi = _first_argmax_rows(gl, N_GROUPS)
    gp = 1.0 / jnp.sum(jnp.exp(gl - gmax), axis=0, keepdims=True)
    le = logits[8:8 + EXPERTS_PER_GROUP]
    for g in range(1, N_GROUPS):
        le = jnp.where(gi == g, logits[8 + g * EXPERTS_PER_GROUP:8 + (g + 1) * EXPERTS_PER_GROUP], le)
    pe = jnp.exp(le - jnp.max(le, axis=0, keepdims=True))
    pe = pe / jnp.sum(pe, axis=0, keepdims=True)
    v1, i1 = _first_argmax_rows(pe, EXPERTS_PER_GROUP)
    ridx = lax.broadcasted_iota(jnp.int32, pe.shape, 0)
    v2, i2 = _first_argmax_rows(jnp.where(ridx == i1, -1.0, pe), EXPERTS_PER_GROUP)
    vsum = v1 + v2
    wts_ref[0:1, :] = gp * v1 / vsum
    wts_ref[1:2, :] = gp * v2 / vsum
    eflat = jnp.concatenate([gi * EXPERTS_PER_GROUP + i1, gi * EXPERTS_PER_GROUP + i2], axis=1)
    eid_ref[0] = eflat
    onehot = (eflat == lax.broadcasted_iota(jnp.int32, (N_EXPERTS, TILE_ASSIGN), 0)).astype(F32)
    before = _dot(onehot.astype(BF16), tri_ref[...]) + run_scr[...]
    rank_ref[0] = jnp.sum(onehot * before, axis=0, keepdims=True).astype(jnp.int32)
    run_scr[...] = run_scr[...] + jnp.sum(onehot, axis=1, keepdims=True)
    cnt_ref[...] = run_scr[...].astype(jnp.int32)


def _router(x_units, mod_units, nrm, wr, br, tri):
    n_units, lt, _ = x_units.shape
    u = MOE_TILE // lt
    n_tiles = n_units // u
    return pl.pallas_call(
        _router_kernel,
        grid=(n_tiles,),
        in_specs=[
            pl.BlockSpec((u, lt, D_MODEL), lambda i: (i, 0, 0)),
            pl.BlockSpec((u, 6, D_MODEL), lambda i: (i, 0, 0)),
            pl.BlockSpec((1, D_MODEL), lambda i: (0, 0)),
            pl.BlockSpec((ROUTER_ROWS, D_MODEL), lambda i: (0, 0)),
            pl.BlockSpec((ROUTER_ROWS, 1), lambda i: (0, 0)),
            pl.BlockSpec((TILE_ASSIGN, TILE_ASSIGN), lambda i: (0, 0)),
        ],
        out_specs=[
            pl.BlockSpec((MOE_TILE, D_MODEL), lambda i: (i, 0)),
            pl.BlockSpec((1, 1, TILE_ASSIGN), lambda i: (i, 0, 0)),
            pl.BlockSpec((1, 1, TILE_ASSIGN), lambda i: (i, 0, 0)),
            pl.BlockSpec((TOPK, MOE_TILE), lambda i: (0, i)),
            pl.BlockSpec((N_EXPERTS, 1), lambda i: (0, 0)),
        ],
        out_shape=[
            jax.ShapeDtypeStruct((n_tiles * MOE_TILE, D_MODEL), F32),
            jax.ShapeDtypeStruct((n_tiles, 1, TILE_ASSIGN), jnp.int32),
            jax.ShapeDtypeStruct((n_tiles, 1, TILE_ASSIGN), jnp.int32),
            jax.ShapeDtypeStruct((TOPK, n_tiles * MOE_TILE), F32),
            jax.ShapeDtypeStruct((N_EXPERTS, 1), jnp.int32),
        ],
        scratch_shapes=[pltpu.VMEM((N_EXPERTS, 1), F32)],
        compiler_params=pltpu.CompilerParams(dimension_semantics=("arbitrary",)),
        name="moe_router",
    )(x_units, mod_units, nrm, wr, br, tri)


def _start_row_gather(idx_ref, n_rows, src_hbm, dst, sem):
    def body(r, carry):
        row = idx_ref[0, 0, r]
        pltpu.make_async_copy(src_hbm.at[pl.ds(row, 1)], dst.at[pl.ds(r, 1)], sem).start()
        return carry
    lax.fori_loop(0, n_rows, body, 0, unroll=8)


def _wait_row_gather(n_rows, src_hbm, dst, sem):
    pltpu.make_async_copy(src_hbm.at[pl.ds(0, n_rows)], dst, sem).wait()


def _dispatch_kernel(pend_ref, padded_ref, dest_ref, h_ref, xs_hbm, zbuf, sem):
    n_blocks = xs_hbm.shape[0] // EXPERT_BLOCK

    def zero_block(first_row):
        return pltpu.make_async_copy(
            zbuf, xs_hbm.at[pl.ds(pl.multiple_of(first_row, EXPERT_BLOCK), EXPERT_BLOCK)], sem.at[0])

    @pl.when(pl.program_id(0) == 0)
    def _():
        zbuf[...] = jnp.zeros_like(zbuf)
        n_used = pend_ref[N_EXPERTS - 1] // EXPERT_BLOCK
        for e in range(N_EXPERTS):
            @pl.when(padded_ref[e] > 0)
            def _():
                zero_block(pend_ref[e] - EXPERT_BLOCK).start()
        lax.fori_loop(n_used, n_blocks, lambda b, c: (zero_block(b * EXPERT_BLOCK).start(), c)[1], 0)
        for e in range(N_EXPERTS):
            @pl.when(padded_ref[e] > 0)
            def _():
                zero_block(pend_ref[e] - EXPERT_BLOCK).wait()
        lax.fori_loop(n_used, n_blocks, lambda b, c: (zero_block(b * EXPERT_BLOCK).wait(), c)[1], 0)

    def body(t, carry):
        for k in range(TOPK):
            slot = dest_ref[0, 0, k * MOE_TILE + t]
            pltpu.make_async_copy(h_ref.at[pl.ds(t, 1)], xs_hbm.at[pl.ds(slot, 1)], sem.at[1]).start()
        return carry
    lax.fori_loop(0, MOE_TILE, body, 0, unroll=8)
    for k in range(TOPK):
        pltpu.make_async_copy(h_ref, xs_hbm.at[pl.ds(0, MOE_TILE)], sem.at[1]).wait()


def _dispatch(pad_end, padded, dest_tiles, h, n_slots):
    n_tiles = dest_tiles.shape[0]
    grid_spec = pltpu.PrefetchScalarGridSpec(
        num_scalar_prefetch=2,
        grid=(n_tiles,),
        in_specs=[
            pl.BlockSpec((1, 1, TILE_ASSIGN), lambda i, pe, pd: (i, 0, 0), memory_space=pltpu.SMEM),
            pl.BlockSpec((MOE_TILE, D_MODEL), lambda i, pe, pd: (i, 0)),
        ],
        out_specs=pl.BlockSpec(memory_space=pl.ANY),
        scratch_shapes=[pltpu.VMEM((EXPERT_BLOCK, D_MODEL), F32), pltpu.SemaphoreType.DMA((2,))],
    )
    return pl.pallas_call(
        _dispatch_kernel,
        grid_spec=grid_spec,
        out_shape=jax.ShapeDtypeStruct((n_slots, D_MODEL), F32),
        compiler_params=pltpu.CompilerParams(dimension_semantics=("arbitrary",)),
        name="moe_dispatch",
    )(pad_end, padded, dest_tiles, h)


def _experts_kernel(be_ref, nused_ref, x_ref, wg_ref, wu_ref, wd_ref, o_ref):
    @pl.when(pl.program_id(0) < nused_ref[0])
    def _():
        xb = x_ref[...].astype(BF16)
        a = _silu(_dot(xb, wg_ref[0, 0])) * _dot(xb, wu_ref[0, 0])
        o_ref[...] = _dot(a.astype(BF16), wd_ref[0, 0])

    @pl.when(pl.program_id(0) >= nused_ref[0])
    def _():
        o_ref[...] = jnp.zeros_like(o_ref)


def _experts(block_e, n_used, xs, wg, wu, wd, layer):
    n_blocks = xs.shape[0] // EXPERT_BLOCK

    def row_block(i, be, nu):
        return (jnp.minimum(i, nu[0] - 1), 0)

    def expert_block(i, be, nu):
        return (layer, be[jnp.minimum(i, nu[0] - 1)], 0, 0)

    grid_spec = pltpu.PrefetchScalarGridSpec(
        num_scalar_prefetch=2,
        grid=(n_blocks,),
        in_specs=[
            pl.BlockSpec((EXPERT_BLOCK, D_MODEL), row_block),
            pl.BlockSpec((1, 1, D_MODEL, D_EXPERT), expert_block),
            pl.BlockSpec((1, 1, D_MODEL, D_EXPERT), expert_block),
            pl.BlockSpec((1, 1, D_EXPERT, D_MODEL), expert_block),
        ],
        out_specs=pl.BlockSpec((EXPERT_BLOCK, D_MODEL), lambda i, be, nu: (i, 0)),
    )
    return pl.pallas_call(
        _experts_kernel,
        grid_spec=grid_spec,
        out_shape=jax.ShapeDtypeStruct(xs.shape, F32),
        compiler_params=pltpu.CompilerParams(
            dimension_semantics=("arbitrary",), vmem_limit_bytes=VMEM_LIMIT),
        name="moe_experts",
    )(block_e, n_used, xs, wg, wu, wd)


def _combine_kernel(dst_cur_ref, dst_nxt_ref, x_ref, mod_ref, wts_ref, nrm_ref, y_hbm, o_ref, buf, sem,
                    *, final_norm):
    i = pl.program_id(0)
    n = pl.num_programs(0)
    slot = i % 2

    @pl.when(i == 0)
    def _():
        _start_row_gather(dst_cur_ref, TILE_ASSIGN, y_hbm, buf.at[0], sem.at[0])

    _wait_row_gather(TILE_ASSIGN, y_hbm, buf.at[slot], sem.at[slot])
    u, lt, _ = x_ref.shape
    n_pieces = MOE_TILE // CHUNK
    per_piece = TILE_ASSIGN // n_pieces
    for c in range(n_pieces):
        unit, r0 = divmod(c * CHUNK, lt)
        t0 = c * CHUNK
        y = (wts_ref[t0:t0 + CHUNK, 0:1] * buf[slot, t0:t0 + CHUNK, :]
             + wts_ref[t0:t0 + CHUNK, 1:2] * buf[slot, MOE_TILE + t0:MOE_TILE + t0 + CHUNK, :])
        out = x_ref[unit, r0:r0 + CHUNK, :] + mod_ref[unit, 5:6, :] * y
        if final_norm:
            out = out * lax.rsqrt(jnp.mean(out * out, axis=-1, keepdims=True) + NORM_EPS) * nrm_ref[...]
        o_ref[unit, r0:r0 + CHUNK, :] = out
        for r in range(c * per_piece, (c + 1) * per_piece):
            row = dst_nxt_ref[0, 0, r]
            pltpu.make_async_copy(y_hbm.at[pl.ds(row, 1)], buf.at[1 - slot, pl.ds(r, 1)],
                                  sem.at[1 - slot]).start()

    @pl.when(i == n - 1)
    def _():
        _wait_row_gather(TILE_ASSIGN, y_hbm, buf.at[1 - slot], sem.at[1 - slot])


def _combine(dest_tiles, x_units, mod_units, wts_col, nrm, y_slots, *, final_norm):
    n_units, lt, _ = x_units.shape
    u = MOE_TILE // lt
    n_tiles = n_units // u
    return pl.pallas_call(
        functools.partial(_combine_kernel, final_norm=final_norm),
        grid=(n_tiles,),
        in_specs=[
            pl.BlockSpec((1, 1, TILE_ASSIGN), lambda i: (i, 0, 0), memory_space=pltpu.SMEM),
            pl.BlockSpec((1, 1, TILE_ASSIGN), lambda i: (jnp.minimum(i + 1, n_tiles - 1), 0, 0),
                         memory_space=pltpu.SMEM),
            pl.BlockSpec((u, lt, D_MODEL), lambda i: (i, 0, 0)),
            pl.BlockSpec((u, 6, D_MODEL), lambda i: (i, 0, 0)),
            pl.BlockSpec((MOE_TILE, TOPK), lambda i: (i, 0)),
            pl.BlockSpec((1, D_MODEL), lambda i: (0, 0)),
            pl.BlockSpec(memory_space=pl.ANY),
        ],
        out_specs=pl.BlockSpec((u, lt, D_MODEL), lambda i: (i, 0, 0)),
        out_shape=jax.ShapeDtypeStruct(x_units.shape, F32),
        scratch_shapes=[pltpu.VMEM((2, TILE_ASSIGN, D_MODEL), F32), pltpu.SemaphoreType.DMA((2,))],
        compiler_params=pltpu.CompilerParams(
            dimension_semantics=("arbitrary",), vmem_limit_bytes=VMEM_LIMIT),
        name="moe_combine",
    )(dest_tiles, dest_tiles, x_units, mod_units, wts_col, nrm, y_slots)


def _routing_tables(eid_tiles, rank_tiles, counts):
    n_blocks = eid_tiles.size // EXPERT_BLOCK + N_EXPERTS
    padded = (counts + EXPERT_BLOCK - 1) // EXPERT_BLOCK * EXPERT_BLOCK
    pad_end = jnp.cumsum(padded).astype(jnp.int32)
    pad_start = pad_end - padded
    block_start = jnp.arange(n_blocks, dtype=jnp.int32)[:, None] * EXPERT_BLOCK
    block_e = jnp.minimum(jnp.sum((block_start >= pad_end[None, :]).astype(jnp.int32), axis=1),
                          N_EXPERTS - 1).astype(jnp.int32)
    n_used = pad_end[-1:] // EXPERT_BLOCK
    experts = jnp.arange(N_EXPERTS, dtype=jnp.int32)
    first_slot = jnp.sum(jnp.where(eid_tiles[..., None] == experts, pad_start, 0), axis=-1)
    return block_e, n_used, pad_end, padded, first_slot + rank_tiles


def _moe_units(x, mod_l):
    bsz, seq, _ = x.shape
    lt = min(seq, MOE_TILE)
    per = seq // lt
    x_units = x.reshape(bsz * seq // lt, lt, D_MODEL)
    mod_units = jnp.repeat(mod_l, per, axis=0) if per > 1 else mod_l
    return x_units, mod_units


def _moe_experts(x, mod_l, nrm_ffn, wr, br, tri, wg, wu, wd, layer):
    x_units, mod_units = _moe_units(x, mod_l)
    h, eid_tiles, rank_tiles, wts, counts = _router(x_units, mod_units, nrm_ffn, wr, br, tri)
    block_e, n_used, pad_end, padded, dest_tiles = _routing_tables(eid_tiles, rank_tiles, counts[:, 0])
    n_slots = block_e.shape[0] * EXPERT_BLOCK
    xs = _dispatch(pad_end, padded, dest_tiles, h, n_slots)
    return dest_tiles, wts.T, _experts(block_e, n_used, xs, wg, wu, wd, layer)


def _dest_per_step(dest_tiles, tb):
    n_tiles = dest_tiles.shape[0]
    per = MOE_TILE // tb
    d = dest_tiles.reshape(n_tiles, TOPK, per, tb).transpose(0, 2, 1, 3)
    return d.reshape(n_tiles * per, 1, TOPK * tb)


def kernel(x_prompt, x_sample, c_prompt, c_sample, state_hgrn, state_gla, w_ada, b_ada, norm_mix,
           norm_ffn, w_in, hg_lb, hg_onorm, w_gk2, b_gk, gla_onorm, w_br_a, w_br_b, w_out, w_rg, b_rg,
           w_re, b_re, w_e_gate, w_e_up, w_e_down, norm_final):
    bp = x_prompt.shape[0]
    bs = x_sample.shape[0]
    mod = _ada_mod(jnp.concatenate([c_prompt, c_sample], axis=0), w_ada, b_ada)
    mod = mod.reshape(DEPTH, bp + bs, 6, D_MODEL)

    glr0 = C_GOG + GLA_W
    win_r = jnp.concatenate(
        [w_in[:, :, :glr0], w_in[:, :, glr0 + GLA_GATE_RANK:], w_in[:, :, glr0:glr0 + GLA_GATE_RANK],
         jnp.zeros((DEPTH, D_MODEL, LANES - GLA_GATE_RANK), F32)], axis=2).astype(BF16)
    wgk2_p = jnp.concatenate(
        [w_gk2, jnp.zeros((DEPTH, LANES - GLA_GATE_RANK, GLA_KW), F32)], axis=1).astype(BF16)
    wa_b = w_br_a.astype(BF16)
    wb_b = w_br_b.astype(BF16)
    wo_b = w_out.astype(BF16)
    def plan_consts(tb):
        r = np.arange(tb)
        chunk_tril = (r[:, None] // CHUNK == r[None, :] // CHUNK) & (r[None, :] <= r[:, None])
        return [jnp.asarray(chunk_tril, BF16), jnp.asarray(_slab_mask(), F32),
                jnp.asarray(SAFE_PLAN.segment_sum_matrix(), BF16), jnp.asarray(SAFE_PLAN.masks(), F32)]
    zpad = jnp.zeros((DEPTH, 8 - N_GROUPS, D_MODEL), F32)
    ztail = jnp.zeros((DEPTH, ROUTER_ROWS - 8 - N_EXPERTS, D_MODEL), F32)
    wr = jnp.concatenate([jnp.swapaxes(w_rg, 1, 2), zpad, jnp.swapaxes(w_re, 1, 2), ztail], axis=1)
    br = jnp.concatenate([b_rg, jnp.zeros((DEPTH, 8 - N_GROUPS), F32), b_re,
                          jnp.zeros((DEPTH, ROUTER_ROWS - 8 - N_EXPERTS), F32)], axis=1)[:, :, None]
    wg_b = w_e_gate.astype(BF16)
    wu_b = w_e_up.astype(BF16)
    wd_b = w_e_down.astype(BF16)
    nrm_f = norm_final.reshape(1, D_MODEL)
    assign = np.arange(TILE_ASSIGN)
    tri = jnp.asarray(assign[:, None] < assign[None, :], BF16)

    def run(x, mod_g, shg, sgla, tb):
        bsz = x.shape[0]
        new_hg, new_gla = [], []
        pending = None
        for l in range(DEPTH):
            x, s1, s2 = _mixer(
                x, mod_g[l], norm_mix[l:l + 1], win_r[l], hg_lb, wgk2_p[l], b_gk[l:l + 1],
                hg_onorm[l:l + 1], gla_onorm[l:l + 1], wa_b[l], wb_b[l], wo_b[l],
                shg[l], sgla[l].reshape(bsz, GLA_KW // LANES, LANES, HEAD_DV), plan_consts(tb),
                layer=l, tb=tb, pending_moe=pending)
            new_hg.append(s1)
            new_gla.append(s2.reshape(bsz, GLA_HEADS, GLA_DK, HEAD_DV))
            dest_tiles, wts_col, y_slots = _moe_experts(
                x, mod_g[l], norm_ffn[l:l + 1], wr[l], br[l], tri, wg_b, wu_b, wd_b, l)
            pending = (_dest_per_step(dest_tiles, tb), wts_col, mod_g[l], y_slots)
        x_units, mod_units = _moe_units(x, mod_g[DEPTH - 1])
        y = _combine(dest_tiles, x_units, mod_units, wts_col, nrm_f, y_slots, final_norm=True)
        return y.reshape(x.shape), jnp.stack(new_hg), jnp.stack(new_gla)

    zeros_hg = jnp.zeros((DEPTH, bp, HG_HEADS, HG_DK, HEAD_DV), F32)
    zeros_gla = jnp.zeros((DEPTH, bp, GLA_HEADS, GLA_DK, HEAD_DV), F32)
    y_p, hg_p, gla_p = run(x_prompt, mod[:, :bp], zeros_hg, zeros_gla, 256)
    y_s, hg_s, gla_s = run(x_sample, mod[:, bp:], state_hgrn, state_gla, CHUNK)
    return (y_p, y_s, hg_p, gla_p, hg_s, gla_s)
```

```python
import functools

import numpy as np
import jax
import jax.numpy as jnp
from jax import lax
from jax.experimental import pallas as pl
from jax.experimental.pallas import tpu as pltpu

F32 = jnp.float32
BF16 = jnp.bfloat16

D_MODEL = 1024
DEPTH = 2
CHUNK = 64
NORM_EPS = 1e-6
LOG_FLOOR = 1e-30
HG_HEADS = 4
HG_DK = 128
HEAD_DV = 128
HG_KW = HG_HEADS * HG_DK
HG_W = HG_HEADS * HEAD_DV
GLA_HEADS = 4
GLA_DK = 64
GLA_KW = GLA_HEADS * GLA_DK
GLA_W = GLA_HEADS * HEAD_DV
GLA_GATE_RANK = 16
GLA_GATE_NORM = 16.0
N_GROUPS = 4
EXPERTS_PER_GROUP = 8
N_EXPERTS = N_GROUPS * EXPERTS_PER_GROUP
TOPK = 2
D_EXPERT = 512

LANES = 128
VMEM_LIMIT = 56 * 1024 * 1024

C_HQ = 0
C_HF = C_HQ + HG_KW
C_HI = C_HF + HG_KW
C_HOG = C_HI + HG_W
C_GQ = C_HOG + HG_W
C_GK = C_GQ + GLA_KW
C_GV = C_GK + GLA_KW
C_GOG = C_GV + GLA_W
C_GA = C_GOG + GLA_W
C_GB = C_GA + D_MODEL
C_GLR = C_GB + D_MODEL
IN_COLS_PAD = C_GLR + LANES
MXU_WIDTH = 256
PROJ_TILE = 4 * MXU_WIDTH


class _ScorePlan:
    def __init__(self, levels, adjacent, diag_block):
        self.levels = levels
        self.adjacent = adjacent
        self.diag_block = diag_block
        self.cum_rows = (2 * len(levels) + 1) * CHUNK
        self.n_masks = len(levels) + 1 + int(adjacent)

    def segment_sum_matrix(self):
        t = np.arange(CHUNK)[:, None]
        r = np.arange(CHUNK)[None, :]
        rows = []
        for m in self.levels:
            same = (t // m) == (r // m)
            rows.append(same & (r <= t))
            rows.append(same & (r > t))
        rows.append(r <= t)
        return np.concatenate(rows, axis=0).astype(np.float32)

    def masks(self):
        t = np.arange(CHUNK)[:, None]
        s = np.arange(CHUNK)[None, :]
        masks = [((t // self.diag_block) == (s // self.diag_block)) & (s <= t)]
        for m in self.levels + ((1,) if self.adjacent else ()):
            masks.append(((t // (2 * m)) == (s // (2 * m))) & ((t // m) % 2 == 1) & ((s // m) % 2 == 0))
        return np.stack(masks).astype(np.float32)


SAFE_PLAN = _ScorePlan((32, 16, 8, 4, 2), True, 1)


def _dot(a, b):
    return jnp.dot(a, b, preferred_element_type=F32)


def _dot_nt(a, b):
    return lax.dot_general(a, b, (((1,), (1,)), ((), ())), preferred_element_type=F32)


def _sigmoid(x):
    return 1.0 / (1.0 + jnp.exp(-x))


def _silu(x):
    return x * _sigmoid(x)


def _rms_mod(x, gain, scale, shift):
    y = x * lax.rsqrt(jnp.mean(x * x, axis=-1, keepdims=True) + NORM_EPS)
    return y * gain * (1.0 + scale) + shift


def _ada_kernel(c_ref, w_ref, b_ref, o_ref):
    c = c_ref[...]
    o_ref[0] = jnp.dot(_silu(c), w_ref[0], preferred_element_type=F32,
                       precision=lax.Precision.HIGHEST) + b_ref[0]


def _ada_mod(c_all, w_ada, b_ada):
    nb = c_all.shape[0]
    tn = 512
    return pl.pallas_call(
        _ada_kernel,
        grid=(DEPTH, 6 * D_MODEL // tn),
        in_specs=[
            pl.BlockSpec((nb, D_MODEL), lambda l, j: (0, 0)),
            pl.BlockSpec((1, D_MODEL, tn), lambda l, j: (l, 0, j)),
            pl.BlockSpec((1, 1, tn), lambda l, j: (l, 0, j)),
        ],
        out_specs=pl.BlockSpec((1, nb, tn), lambda l, j: (l, 0, j)),
        out_shape=jax.ShapeDtypeStruct((DEPTH, nb, 6 * D_MODEL), F32),
        name="ada_mod",
    )(c_all, w_ada, b_ada.reshape(DEPTH, 1, 6 * D_MODEL))


def _chunk_attention(q, k, v, g, states, mall_ref, mask_ref, heads_per_tile, plan):
    w = q.shape[1]
    n_tiles = w // LANES
    g_hi = g.astype(BF16)
    r1 = g - g_hi.astype(F32)
    g_mid = r1.astype(BF16)
    g_lo = (r1 - g_mid.astype(F32)).astype(BF16)
    mall = mall_ref[...]
    cums = _dot(mall, g_hi) + _dot(mall, g_mid) + _dot(mall, g_lo)
    b = cums[plan.cum_rows - CHUNK:plan.cum_rows]
    level_q = []
    level_k = []
    for i in range(len(plan.levels)):
        level_q.append(q * jnp.exp(cums[2 * i * CHUNK:(2 * i + 1) * CHUNK]))
        level_k.append(k * jnp.exp(cums[(2 * i + 1) * CHUNK:(2 * i + 2) * CHUNK]))
    if plan.diag_block == 1:
        qs = [q]
        ks = [k]
    else:
        i = plan.levels.index(plan.diag_block)
        qs = [level_q[i]]
        ks = [k * jnp.exp(-cums[2 * i * CHUNK:(2 * i + 1) * CHUNK])]
    qs += level_q
    ks += level_k
    if plan.adjacent:
        qs.append(q * jnp.exp(g))
        ks.append(k)
    b_last = b[CHUNK - 1:CHUNK]
    q_in = q * jnp.exp(b)
    k_out = k * jnp.exp(b_last - b)
    e_last = jnp.exp(b_last)

    dk = LANES // heads_per_tile
    lane = lax.broadcasted_iota(jnp.int32, (CHUNK, LANES), 1)
    row = lax.broadcasted_iota(jnp.int32, (LANES, HEAD_DV), 0)
    outs = []
    new_states = []
    for ti in range(n_tiles):
        sl = slice(ti * LANES, (ti + 1) * LANES)
        ks_t = [kk[:, sl].astype(BF16) for kk in ks]
        k_out_t = k_out[:, sl].T.astype(BF16)
        e_col = jnp.broadcast_to(e_last[:, sl], (LANES, LANES)).T
        s_old = states[ti]
        s_old_b = s_old.astype(BF16)
        upd = None
        for j in range(heads_per_tile):
            head = ti * heads_per_tile + j
            if heads_per_tile == 1:
                sel = lambda a: a
            else:
                in_head = (lane // dk) == j
                sel = lambda a, in_head=in_head: jnp.where(in_head, a, 0.0)
            sc = jnp.zeros((CHUNK, CHUNK), F32)
            for i in range(plan.n_masks):
                sc = sc + _dot_nt(sel(qs[i][:, sl]).astype(BF16), ks_t[i]) * mask_ref[i]
            vh = v[:, head * HEAD_DV:(head + 1) * HEAD_DV].astype(BF16)
            o = _dot(sc.astype(BF16), vh) + _dot(sel(q_in[:, sl]).astype(BF16), s_old_b)
            outs.append(o)
            u = _dot(k_out_t, vh)
            upd = u if upd is None else jnp.where((row // dk) == j, u, upd)
        new_states.append(e_col * s_old + upd)
    return jnp.concatenate(outs, axis=1), new_states


FAST_BLOCK = 16
N_SUB = CHUNK // FAST_BLOCK
SLAB_ROWS = FAST_BLOCK * (N_SUB * (N_SUB - 1) // 2) + CHUNK
FAST_BLOCK_DECAY_LIMIT = 60.0


def _slab_mask():
    t = np.arange(CHUNK)[:, None]
    cols = []
    for i in range(1, N_SUB):
        cols.append(np.broadcast_to(t // FAST_BLOCK == i, (CHUNK, i * FAST_BLOCK)))
    s = np.arange(CHUNK)[None, :]
    cols.append((t // FAST_BLOCK == s // FAST_BLOCK) & (s <= t))
    return np.concatenate(cols, axis=1).astype(np.float32)


def _block_attention_fast(q, k, v, g, states, tril_ref, slab_mask_ref, heads_per_tile, carry=True):
    rows, w = q.shape
    n_chunks = rows // CHUNK
    n_tiles = w // LANES
    dk = LANES // heads_per_tile
    n_heads = n_tiles * heads_per_tile

    g_hi = g.astype(BF16)
    r1 = g - g_hi.astype(F32)
    g_mid = r1.astype(BF16)
    g_lo = (r1 - g_mid.astype(F32)).astype(BF16)
    tril = tril_ref[...]
    b = _dot(tril, g_hi) + _dot(tril, g_mid) + _dot(tril, g_lo)

    def end_row(c, i):
        r = c * CHUNK + (i + 1) * FAST_BLOCK
        return b[r - 1:r]

    def per_block(row_of):
        return jnp.concatenate([jnp.broadcast_to(row_of(c, i), (FAST_BLOCK, w))
                                for c in range(n_chunks) for i in range(N_SUB)], axis=0)

    zero = jnp.zeros((1, w), F32)
    b_start = per_block(lambda c, i: zero if i == 0 else end_row(c, i - 1))
    b_end = per_block(end_row)
    q_blk = q * jnp.exp(b - b_start)
    k_diag = k * jnp.exp(b_start - b)
    k_end = k * jnp.exp(b_end - b)
    q_in = q_blk * jnp.exp(b_start)
    k_out = k_end * jnp.exp(per_block(lambda c, i: end_row(c, N_SUB - 1)) - b_end)

    lane = lax.broadcasted_iota(jnp.int32, (CHUNK, LANES), 1)
    row = lax.broadcasted_iota(jnp.int32, (LANES, HEAD_DV), 0)
    slab_mask = slab_mask_ref[...]

    def sel(a, j):
        return a if heads_per_tile == 1 else jnp.where((lane // dk) == j, a, 0.0)

    v_b = v.astype(BF16)

    scores = {}
    for c in range(n_chunks):
        r0 = c * CHUNK
        slabs = []
        for i in range(1, N_SUB):
            for jb in range(i):
                blk = k_end[r0 + jb * FAST_BLOCK:r0 + (jb + 1) * FAST_BLOCK]
                slabs.append(blk if jb == i - 1 else blk * jnp.exp(end_row(c, i - 1) - end_row(c, jb)))
        slabs.append(k_diag[r0:r0 + CHUNK])
        k_slab = jnp.concatenate(slabs, axis=0).astype(BF16)
        for ti in range(n_tiles):
            sl = slice(ti * LANES, (ti + 1) * LANES)
            for j in range(heads_per_tile):
                qh = sel(q_blk[r0:r0 + CHUNK, sl], j).astype(BF16)
                scores[c, ti * heads_per_tile + j] = (_dot_nt(qh, k_slab[:, sl]) * slab_mask).astype(BF16)

    entering = [list(states)] if carry else [list(st) for st in states]
    leaving = []
    for c in range(n_chunks):
        r0 = c * CHUNK
        nxt = []
        for ti in range(n_tiles):
            sl = slice(ti * LANES, (ti + 1) * LANES)
            k_out_t = k_out[r0:r0 + CHUNK, sl].T.astype(BF16)
            upd = None
            for j in range(heads_per_tile):
                head = ti * heads_per_tile + j
                u = _dot(k_out_t, v_b[r0:r0 + CHUNK, head * HEAD_DV:(head + 1) * HEAD_DV])
                upd = u if upd is None else jnp.where((row // dk) == j, u, upd)
            e_col = jnp.broadcast_to(jnp.exp(end_row(c, N_SUB - 1)[:, sl]), (LANES, LANES)).T
            nxt.append(e_col * entering[c][ti] + upd)
        leaving.append(nxt)
        if carry:
            entering.append(nxt)

    out_rows = []
    for c in range(n_chunks):
        r0 = c * CHUNK
        outs = []
        for head in range(n_heads):
            ti, j = divmod(head, heads_per_tile)
            sl = slice(ti * LANES, (ti + 1) * LANES)
            vh = v_b[r0:r0 + CHUNK, head * HEAD_DV:(head + 1) * HEAD_DV]
            v_slab = jnp.concatenate([vh[:i * FAST_BLOCK] for i in range(1, N_SUB)] + [vh], axis=0)
            outs.append(_dot(scores[c, head], v_slab)
                        + _dot(sel(q_in[r0:r0 + CHUNK, sl], j).astype(BF16), entering[c][ti].astype(BF16)))
        out_rows.append(jnp.concatenate(outs, axis=1))
    return jnp.concatenate(out_rows, axis=0), (leaving[-1] if carry else leaving)


def _head_norm_gate(o, gain, gate):
    outs = []
    for h in range(o.shape[1] // HEAD_DV):
        sl = slice(h * HEAD_DV, (h + 1) * HEAD_DV)
        oh = o[:, sl]
        oh = oh * lax.rsqrt(jnp.mean(oh * oh, axis=-1, keepdims=True) + NORM_EPS) * gain
        outs.append(oh * _silu(gate[:, sl]))
    return jnp.concatenate(outs, axis=1)


def _mixer_kernel(x_ref, mod_ref, nrm_ref, win_ref, lb_ref, wgk2_ref, bgk_ref, hgn_ref, glan_ref,
                  wa_ref, wb_ref, wo_ref, shg0_ref, sgla0_ref,
                  tril_ref, slab_mask_ref, mall_safe_ref, mask_safe_ref, *rest, layer, tb, pending_moe):
    if pending_moe:
        dst_cur_ref, dst_nxt_ref, wts_ref, modp_ref, y_hbm = rest[:5]
        rest = rest[5:]
    xo_ref, shg_o_ref, sgla_o_ref, p_scr, k_scr, lg_scr, shg_scr, sgla_scr = rest[:8]
    j = pl.program_id(1)
    nb = x_ref.shape[0]
    rows_all = nb * tb

    @pl.when(j == 0)
    def _():
        shg_scr[...] = shg0_ref[...].reshape(shg_scr.shape)
        sgla_scr[...] = sgla0_ref[...].reshape(sgla_scr.shape)

    def per_row(ref, i):
        if nb == 1:
            return ref[0, i:i + 1, :]
        return jnp.broadcast_to(ref[:, i:i + 1, :], (nb, tb, D_MODEL)).reshape(rows_all, D_MODEL)

    x = x_ref[...].reshape(rows_all, D_MODEL)
    n_prefetch = 0
    if pending_moe:
        cbuf, csem = rest[8:]
        n_prefetch = TOPK * rows_all
        step = pl.program_id(0) * pl.num_programs(1) + j
        last = pl.num_programs(0) * pl.num_programs(1) - 1
        slot = step % 2

        @pl.when(step == 0)
        def _():
            _start_row_gather(dst_cur_ref, n_prefetch, y_hbm, cbuf.at[0], csem.at[0])

        _wait_row_gather(n_prefetch, y_hbm, cbuf.at[slot], csem.at[slot])
        x = x + per_row(modp_ref, 5) * (wts_ref[:, 0:1] * cbuf[slot, 0:rows_all, :]
                                        + wts_ref[:, 1:2] * cbuf[slot, rows_all:n_prefetch, :])

    def prefetch_rows(r0, r1):
        for r in range(r0, r1):
            row = dst_nxt_ref[0, 0, r]
            pltpu.make_async_copy(y_hbm.at[pl.ds(row, 1)], cbuf.at[1 - slot, pl.ds(r, 1)],
                                  csem.at[1 - slot]).start()

    sh1 = per_row(mod_ref, 0)
    sc1 = per_row(mod_ref, 1)
    g1 = per_row(mod_ref, 2)
    hb = _rms_mod(x, nrm_ref[...], sc1, sh1).astype(BF16)
    col_tiles = list(range(0, IN_COLS_PAD, PROJ_TILE))
    for i, c in enumerate(col_tiles):
        c1 = min(c + PROJ_TILE, IN_COLS_PAD)
        p_scr[:, c:c1] = _dot(hb, win_ref[:, c:c1])
        if pending_moe:
            prefetch_rows(n_prefetch * i // len(col_tiles), n_prefetch * (i + 1) // len(col_tiles))

    lb_all = lb_ref[...]
    lb_max = jnp.max(lb_all, axis=0, keepdims=True)
    lb_exp = jnp.exp(lb_all - lb_max)
    sm = lb_exp / jnp.sum(lb_exp, axis=0, keepdims=True)
    lbl = jnp.clip(jnp.sum(sm[0:layer + 1], axis=0, keepdims=True) - sm[0:1], 0.0, 1.0)

    p_scr[:, C_HQ:C_HQ + HG_KW] = _silu(p_scr[:, C_HQ:C_HQ + HG_KW]) * (HG_DK ** -0.5)
    z = p_scr[:, C_HF:C_HF + HG_KW]
    f = lbl + (1.0 - lbl) * _sigmoid(z)
    p_scr[:, C_HF:C_HF + HG_KW] = jnp.log(jnp.maximum(f, LOG_FLOOR))
    k_scr[...] = (1.0 - lbl) * _sigmoid(-z)
    glr = p_scr[:, C_GLR:C_GLR + LANES].astype(BF16)
    gate = _dot(glr, wgk2_ref[...]) + bgk_ref[...]
    lg_scr[...] = (jnp.minimum(gate, 0.0) - jnp.log1p(jnp.exp(-jnp.abs(gate)))) * (1.0 / GLA_GATE_NORM)
    p_scr[:, C_GQ:C_GQ + GLA_KW] = p_scr[:, C_GQ:C_GQ + GLA_KW] * (GLA_DK ** -0.5)

    n_hg_tiles = HG_KW // LANES
    n_gla_tiles = GLA_KW // LANES

    def one_chunk(rows, states, attend):
        o_hg, st_hg = attend(p_scr[rows, C_HQ:C_HQ + HG_KW], k_scr[rows, :],
                             p_scr[rows, C_HI:C_HI + HG_W], p_scr[rows, C_HF:C_HF + HG_KW],
                             states[:n_hg_tiles], 1)
        p_scr[rows, C_HI:C_HI + HG_W] = o_hg
        o_gla, st_gla = attend(p_scr[rows, C_GQ:C_GQ + GLA_KW], p_scr[rows, C_GK:C_GK + GLA_KW],
                               p_scr[rows, C_GV:C_GV + GLA_W], lg_scr[rows, :],
                               states[n_hg_tiles:], 2)
        p_scr[rows, C_GV:C_GV + GLA_W] = o_gla
        return st_hg + st_gla

    def attend_safe(q, k, v, g, states, heads_per_tile):
        return _chunk_attention(q, k, v, g, states, mall_safe_ref, mask_safe_ref, heads_per_tile, SAFE_PLAN)

    def load_states(seq=0):
        return ([shg_scr[seq * n_hg_tiles + t] for t in range(n_hg_tiles)]
                + [sgla_scr[seq * n_gla_tiles + t] for t in range(n_gla_tiles)])

    def store_states(states, seq=0):
        for t in range(n_hg_tiles):
            shg_scr[seq * n_hg_tiles + t] = states[t]
        for t in range(n_gla_tiles):
            sgla_scr[seq * n_gla_tiles + t] = states[n_hg_tiles + t]

    def run_block_fast():
        carry = nb == 1
        seqs = [load_states(s) for s in range(nb)]
        hg_in = seqs[0][:n_hg_tiles] if carry else [st[:n_hg_tiles] for st in seqs]
        gla_in = seqs[0][n_hg_tiles:] if carry else [st[n_hg_tiles:] for st in seqs]
        o_hg, st_hg = _block_attention_fast(
            p_scr[:, C_HQ:C_HQ + HG_KW], k_scr[...], p_scr[:, C_HI:C_HI + HG_W],
            p_scr[:, C_HF:C_HF + HG_KW], hg_in, tril_ref, slab_mask_ref, 1, carry)
        p_scr[:, C_HI:C_HI + HG_W] = o_hg
        o_gla, st_gla = _block_attention_fast(
            p_scr[:, C_GQ:C_GQ + GLA_KW], p_scr[:, C_GK:C_GK + GLA_KW], p_scr[:, C_GV:C_GV + GLA_W],
            lg_scr[...], gla_in, tril_ref, slab_mask_ref, 2, carry)
        p_scr[:, C_GV:C_GV + GLA_W] = o_gla
        if carry:
            store_states(st_hg + st_gla)
        else:
            for s in range(nb):
                store_states(st_hg[s] + st_gla[s], s)

    def run_chunks_safe():
        def chunk_body(ci, carry):
            rows = pl.ds(pl.multiple_of(ci * CHUNK, CHUNK), CHUNK)
            seq = 0 if nb == 1 else ci
            store_states(one_chunk(rows, load_states(seq), attend_safe), seq)
            return carry

        lax.fori_loop(0, rows_all // CHUNK, chunk_body, 0)

    blk = FAST_BLOCK
    min_hg = jnp.min(jnp.sum(p_scr[:, C_HF:C_HF + HG_KW].reshape(rows_all // blk, blk, HG_KW), axis=1))
    min_gla = jnp.min(jnp.sum(lg_scr[...].reshape(rows_all // blk, blk, GLA_KW), axis=1))
    bounded = jnp.minimum(min_hg, min_gla) >= -FAST_BLOCK_DECAY_LIMIT

    @pl.when(bounded)
    def _():
        run_block_fast()

    @pl.when(jnp.logical_not(bounded))
    def _():
        run_chunks_safe()

    o_hg = _head_norm_gate(p_scr[:, C_HI:C_HI + HG_W], hgn_ref[...], p_scr[:, C_HOG:C_HOG + HG_W])
    o_gla = _head_norm_gate(p_scr[:, C_GV:C_GV + GLA_W], glan_ref[...], p_scr[:, C_GOG:C_GOG + GLA_W])
    ya = _dot(o_hg.astype(BF16), wa_ref[...])
    yb = _dot(o_gla.astype(BF16), wb_ref[...])
    merged = (_sigmoid(p_scr[:, C_GA:C_GA + D_MODEL]) * ya
              + _sigmoid(p_scr[:, C_GB:C_GB + D_MODEL]) * yb)
    m = _dot(merged.astype(BF16), wo_ref[...])
    xo_ref[...] = (x + g1 * m).reshape(nb, tb, D_MODEL)

    @pl.when(j == pl.num_programs(1) - 1)
    def _():
        shg_o_ref[...] = shg_scr[...].reshape(shg_o_ref.shape)
        sgla_o_ref[...] = sgla_scr[...].reshape(sgla_o_ref.shape)

    if pending_moe:
        @pl.when(step == last)
        def _():
            _wait_row_gather(n_prefetch, y_hbm, cbuf.at[1 - slot], csem.at[1 - slot])


def _const_spec(shape):
    nd = len(shape)
    return pl.BlockSpec(shape, lambda b, j, nd=nd: (0,) * nd, pipeline_mode=pl.Buffered(1))


def _mixer(x, mod, nrm, win, hg_lb, wgk2, bgk, hgn, glan, wa, wb, wo, shg0, sgla0, plan_consts,
           *, layer, tb, nb=1, pending_moe=None):
    bsz, seq, _ = x.shape
    assert nb == 1 or seq == tb == CHUNK
    nj = seq // tb
    rows = nb * tb
    kern = functools.partial(_mixer_kernel, layer=layer, tb=tb, pending_moe=pending_moe is not None)
    n_gla_tiles = GLA_KW // LANES
    extra_specs, extra_args, extra_scratch = [], [], []
    if pending_moe is not None:
        dest_steps, wts_col, mod_prev, y_slots = pending_moe
        n_steps = bsz // nb * nj
        extra_specs = [
            pl.BlockSpec((1, 1, TOPK * rows), lambda b, j: (b * nj + j, 0, 0), memory_space=pltpu.SMEM),
            pl.BlockSpec((1, 1, TOPK * rows), lambda b, j: (jnp.minimum(b * nj + j + 1, n_steps - 1), 0, 0),
                         memory_space=pltpu.SMEM),
            pl.BlockSpec((rows, TOPK), lambda b, j: (b * nj + j, 0)),
            pl.BlockSpec((nb, 6, D_MODEL), lambda b, j: (b, 0, 0)),
            pl.BlockSpec(memory_space=pl.ANY),
        ]
        extra_args = [dest_steps, dest_steps, wts_col, mod_prev, y_slots]
        extra_scratch = [pltpu.VMEM((2, TOPK * rows, D_MODEL), F32), pltpu.SemaphoreType.DMA((2,))]
    return pl.pallas_call(
        kern,
        grid=(bsz // nb, nj),
        in_specs=[
            pl.BlockSpec((nb, tb, D_MODEL), lambda b, j: (b, j, 0)),
            pl.BlockSpec((nb, 6, D_MODEL), lambda b, j: (b, 0, 0)),
            _const_spec((1, D_MODEL)),
            _const_spec((D_MODEL, IN_COLS_PAD)),
            _const_spec((DEPTH, HG_KW)),
            _const_spec((LANES, GLA_KW)),
            _const_spec((1, GLA_KW)),
            _const_spec((1, HEAD_DV)),
            _const_spec((1, HEAD_DV)),
            _const_spec((HG_W, D_MODEL)),
            _const_spec((GLA_W, D_MODEL)),
            _const_spec((D_MODEL, D_MODEL)),
            pl.BlockSpec((nb, HG_HEADS, HG_DK, HEAD_DV), lambda b, j: (b, 0, 0, 0)),
            pl.BlockSpec((nb, n_gla_tiles, LANES, HEAD_DV), lambda b, j: (b, 0, 0, 0)),
            _const_spec((rows, rows)),
            _const_spec((CHUNK, SLAB_ROWS)),
            _const_spec((SAFE_PLAN.cum_rows, CHUNK)),
            _const_spec((SAFE_PLAN.n_masks, CHUNK, CHUNK)),
        ] + extra_specs,
        out_specs=[
            pl.BlockSpec((nb, tb, D_MODEL), lambda b, j: (b, j, 0)),
            pl.BlockSpec((nb, HG_HEADS, HG_DK, HEAD_DV), lambda b, j: (b, 0, 0, 0)),
            pl.BlockSpec((nb, n_gla_tiles, LANES, HEAD_DV), lambda b, j: (b, 0, 0, 0)),
        ],
        out_shape=[
            jax.ShapeDtypeStruct((bsz, seq, D_MODEL), F32),
            jax.ShapeDtypeStruct((bsz, HG_HEADS, HG_DK, HEAD_DV), F32),
            jax.ShapeDtypeStruct((bsz, n_gla_tiles, LANES, HEAD_DV), F32),
        ],
        scratch_shapes=[
            pltpu.VMEM((rows, IN_COLS_PAD), F32),
            pltpu.VMEM((rows, HG_KW), F32),
            pltpu.VMEM((rows, GLA_KW), F32),
            pltpu.VMEM((nb * HG_HEADS, HG_DK, HEAD_DV), F32),
            pltpu.VMEM((nb * n_gla_tiles, LANES, HEAD_DV), F32),
        ] + extra_scratch,
        compiler_params=pltpu.CompilerParams(
            dimension_semantics=("arbitrary", "arbitrary"), vmem_limit_bytes=VMEM_LIMIT),
        name=f"mixer_l{layer}",
    )(x, mod, nrm, win, hg_lb, wgk2, bgk, hgn, glan, wa, wb, wo, shg0, sgla0, *plan_consts, *extra_args)


ROUTER_ROWS = 48
MOE_TILE = 512
TILE_ASSIGN = TOPK * MOE_TILE
MAX_EXPERT_BLOCK = 512


def _expert_block_rows(n_assign):
    return max(LANES, min(MAX_EXPERT_BLOCK, n_assign // N_EXPERTS // 2))


def _first_argmax_rows(vals, n):
    ridx = lax.broadcasted_iota(jnp.int32, vals.shape, 0)
    vmax = jnp.max(vals, axis=0, keepdims=True)
    imax = jnp.min(jnp.where(vals == vmax, ridx, n), axis=0, keepdims=True)
    return vmax, imax


def _router_kernel(x_ref, mod_ref, nrm_ref, wr_ref, br_ref, tri_ref,
                   h_ref, eid_ref, rank_ref, wts_ref, cnt_ref, run_scr):
    @pl.when(pl.program_id(0) == 0)
    def _():
        run_scr[...] = jnp.zeros_like(run_scr)

    u, lt, _ = x_ref.shape
    x = x_ref[...]
    sh2 = mod_ref[:, 3:4, :]
    sc2 = mod_ref[:, 4:5, :]
    h = _rms_mod(x, nrm_ref[...].reshape(1, 1, D_MODEL), sc2, sh2).reshape(u * lt, D_MODEL)
    h_ref[...] = h
    h_hi = h.astype(BF16)
    h_lo = (h - h_hi.astype(F32)).astype(BF16)
    w = wr_ref[...]
    w_hi = w.astype(BF16)
    w_lo = (w - w_hi.astype(F32)).astype(BF16)
    p_hi = _dot_nt(jnp.concatenate([w_hi, w_lo], axis=0), h_hi)
    logits = p_hi[:ROUTER_ROWS] + p_hi[ROUTER_ROWS:] + _dot_nt(w_hi, h_lo) + br_ref[...]
    gl = logits[0:N_GROUPS]
    gmax, gi = _first_argmax_rows(gl, N_GROUPS)
    gp = 1.0 / jnp.sum(jnp.exp(gl - gmax), axis=0, keepdims=True)
    le = logits[8:8 + EXPERTS_PER_GROUP]
    for g in range(1, N_GROUPS):
        le = jnp.where(gi == g, logits[8 + g * EXPERTS_PER_GROUP:8 + (g + 1) * EXPERTS_PER_GROUP], le)
    pe = jnp.exp(le - jnp.max(le, axis=0, keepdims=True))
    pe = pe / jnp.sum(pe, axis=0, keepdims=True)
    v1, i1 = _first_argmax_rows(pe, EXPERTS_PER_GROUP)
    ridx = lax.broadcasted_iota(jnp.int32, pe.shape, 0)
    v2, i2 = _first_argmax_rows(jnp.where(ridx == i1, -1.0, pe), EXPERTS_PER_GROUP)
    vsum = v1 + v2
    wts_ref[0:1, :] = gp * v1 / vsum
    wts_ref[1:2, :] = gp * v2 / vsum
    eflat = jnp.concatenate([gi * EXPERTS_PER_GROUP + i1, gi * EXPERTS_PER_GROUP + i2], axis=1)
    eid_ref[0] = eflat
    onehot = (eflat == lax.broadcasted_iota(jnp.int32, (N_EXPERTS, TILE_ASSIGN), 0)).astype(F32)
    before = _dot(onehot.astype(BF16), tri_ref[...]) + run_scr[...]
    rank_ref[0] = jnp.sum(onehot * before, axis=0, keepdims=True).astype(jnp.int32)
    run_scr[...] = run_scr[...] + jnp.sum(onehot, axis=1, keepdims=True)
    cnt_ref[...] = run_scr[...].astype(jnp.int32)


def _router(x_units, mod_units, nrm, wr, br, tri):
    n_units, lt, _ = x_units.shape
    u = MOE_TILE // lt
    n_tiles = n_units // u
    return pl.pallas_call(
        _router_kernel,
        grid=(n_tiles,),
        in_specs=[
            pl.BlockSpec((u, lt, D_MODEL), lambda i: (i, 0, 0)),
            pl.BlockSpec((u, 6, D_MODEL), lambda i: (i, 0, 0)),
            pl.BlockSpec((1, D_MODEL), lambda i: (0, 0)),
            pl.BlockSpec((ROUTER_ROWS, D_MODEL), lambda i: (0, 0)),
            pl.BlockSpec((ROUTER_ROWS, 1), lambda i: (0, 0)),
            pl.BlockSpec((TILE_ASSIGN, TILE_ASSIGN), lambda i: (0, 0)),
        ],
        out_specs=[
            pl.BlockSpec((MOE_TILE, D_MODEL), lambda i: (i, 0)),
            pl.BlockSpec((1, 1, TILE_ASSIGN), lambda i: (i, 0, 0)),
            pl.BlockSpec((1, 1, TILE_ASSIGN), lambda i: (i, 0, 0)),
            pl.BlockSpec((TOPK, MOE_TILE), lambda i: (0, i)),
            pl.BlockSpec((N_EXPERTS, 1), lambda i: (0, 0)),
        ],
        out_shape=[
            jax.ShapeDtypeStruct((n_tiles * MOE_TILE, D_MODEL), F32),
            jax.ShapeDtypeStruct((n_tiles, 1, TILE_ASSIGN), jnp.int32),
            jax.ShapeDtypeStruct((n_tiles, 1, TILE_ASSIGN), jnp.int32),
            jax.ShapeDtypeStruct((TOPK, n_tiles * MOE_TILE), F32),
            jax.ShapeDtypeStruct((N_EXPERTS, 1), jnp.int32),
        ],
        scratch_shapes=[pltpu.VMEM((N_EXPERTS, 1), F32)],
        compiler_params=pltpu.CompilerParams(dimension_semantics=("arbitrary",)),
        name="moe_router",
    )(x_units, mod_units, nrm, wr, br, tri)


def _start_row_gather(idx_ref, n_rows, src_hbm, dst, sem):
    def body(r, carry):
        row = idx_ref[0, 0, r]
        pltpu.make_async_copy(src_hbm.at[pl.ds(row, 1)], dst.at[pl.ds(r, 1)], sem).start()
        return carry
    lax.fori_loop(0, n_rows, body, 0, unroll=8)


def _wait_row_gather(n_rows, src_hbm, dst, sem):
    pltpu.make_async_copy(src_hbm.at[pl.ds(0, n_rows)], dst, sem).wait()


def _dispatch_kernel(pend_ref, padded_ref, dest_ref, h_ref, xs_hbm, zbuf, sem):
    block_rows = zbuf.shape[0]
    n_blocks = xs_hbm.shape[0] // block_rows

    def zero_block(first_row):
        return pltpu.make_async_copy(
            zbuf, xs_hbm.at[pl.ds(pl.multiple_of(first_row, block_rows), block_rows)], sem.at[0])

    @pl.when(pl.program_id(0) == 0)
    def _():
        zbuf[...] = jnp.zeros_like(zbuf)
        n_used = pend_ref[N_EXPERTS - 1] // block_rows
        for e in range(N_EXPERTS):
            @pl.when(padded_ref[e] > 0)
            def _():
                zero_block(pend_ref[e] - block_rows).start()
        lax.fori_loop(n_used, n_blocks, lambda b, c: (zero_block(b * block_rows).start(), c)[1], 0)
        for e in range(N_EXPERTS):
            @pl.when(padded_ref[e] > 0)
            def _():
                zero_block(pend_ref[e] - block_rows).wait()
        lax.fori_loop(n_used, n_blocks, lambda b, c: (zero_block(b * block_rows).wait(), c)[1], 0)

    def body(t, carry):
        for k in range(TOPK):
            slot = dest_ref[0, 0, k * MOE_TILE + t]
            pltpu.make_async_copy(h_ref.at[pl.ds(t, 1)], xs_hbm.at[pl.ds(slot, 1)], sem.at[1]).start()
        return carry
    lax.fori_loop(0, MOE_TILE, body, 0, unroll=8)
    for k in range(TOPK):
        pltpu.make_async_copy(h_ref, xs_hbm.at[pl.ds(0, MOE_TILE)], sem.at[1]).wait()


def _dispatch(pad_end, padded, dest_tiles, h, n_slots, block_rows):
    n_tiles = dest_tiles.shape[0]
    grid_spec = pltpu.PrefetchScalarGridSpec(
        num_scalar_prefetch=2,
        grid=(n_tiles,),
        in_specs=[
            pl.BlockSpec((1, 1, TILE_ASSIGN), lambda i, pe, pd: (i, 0, 0), memory_space=pltpu.SMEM),
            pl.BlockSpec((MOE_TILE, D_MODEL), lambda i, pe, pd: (i, 0)),
        ],
        out_specs=pl.BlockSpec(memory_space=pl.ANY),
        scratch_shapes=[pltpu.VMEM((block_rows, D_MODEL), F32), pltpu.SemaphoreType.DMA((2,))],
    )
    return pl.pallas_call(
        _dispatch_kernel,
        grid_spec=grid_spec,
        out_shape=jax.ShapeDtypeStruct((n_slots, D_MODEL), F32),
        compiler_params=pltpu.CompilerParams(dimension_semantics=("arbitrary",)),
        name="moe_dispatch",
    )(pad_end, padded, dest_tiles, h)


def _experts_kernel(be_ref, nused_ref, x_ref, wg_ref, wu_ref, wd_ref, o_ref):
    @pl.when(pl.program_id(0) < nused_ref[0])
    def _():
        xb = x_ref[...].astype(BF16)
        a = _silu(_dot(xb, wg_ref[0, 0])) * _dot(xb, wu_ref[0, 0])
        o_ref[...] = _dot(a.astype(BF16), wd_ref[0, 0])

    @pl.when(pl.program_id(0) >= nused_ref[0])
    def _():
        o_ref[...] = jnp.zeros_like(o_ref)


def _experts(block_e, n_used, xs, wg, wu, wd, layer, block_rows):
    n_blocks = xs.shape[0] // block_rows

    def row_block(i, be, nu):
        return (jnp.minimum(i, nu[0] - 1), 0)

    def expert_block(i, be, nu):
        return (layer, be[jnp.minimum(i, nu[0] - 1)], 0, 0)

    grid_spec = pltpu.PrefetchScalarGridSpec(
        num_scalar_prefetch=2,
        grid=(n_blocks,),
        in_specs=[
            pl.BlockSpec((block_rows, D_MODEL), row_block),
            pl.BlockSpec((1, 1, D_MODEL, D_EXPERT), expert_block),
            pl.BlockSpec((1, 1, D_MODEL, D_EXPERT), expert_block),
            pl.BlockSpec((1, 1, D_EXPERT, D_MODEL), expert_block),
        ],
        out_specs=pl.BlockSpec((block_rows, D_MODEL), lambda i, be, nu: (i, 0)),
    )
    return pl.pallas_call(
        _experts_kernel,
        grid_spec=grid_spec,
        out_shape=jax.ShapeDtypeStruct(xs.shape, F32),
        compiler_params=pltpu.CompilerParams(
            dimension_semantics=("arbitrary",), vmem_limit_bytes=VMEM_LIMIT),
        name="moe_experts",
    )(block_e, n_used, xs, wg, wu, wd)


def _combine_kernel(dst_cur_ref, dst_nxt_ref, x_ref, mod_ref, wts_ref, nrm_ref, y_hbm, o_ref, buf, sem,
                    *, final_norm):
    i = pl.program_id(0)
    n = pl.num_programs(0)
    slot = i % 2

    @pl.when(i == 0)
    def _():
        _start_row_gather(dst_cur_ref, TILE_ASSIGN, y_hbm, buf.at[0], sem.at[0])

    _wait_row_gather(TILE_ASSIGN, y_hbm, buf.at[slot], sem.at[slot])
    u, lt, _ = x_ref.shape
    n_pieces = MOE_TILE // CHUNK
    per_piece = TILE_ASSIGN // n_pieces
    for c in range(n_pieces):
        unit, r0 = divmod(c * CHUNK, lt)
        t0 = c * CHUNK
        y = (wts_ref[t0:t0 + CHUNK, 0:1] * buf[slot, t0:t0 + CHUNK, :]
             + wts_ref[t0:t0 + CHUNK, 1:2] * buf[slot, MOE_TILE + t0:MOE_TILE + t0 + CHUNK, :])
        out = x_ref[unit, r0:r0 + CHUNK, :] + mod_ref[unit, 5:6, :] * y
        if final_norm:
            out = out * lax.rsqrt(jnp.mean(out * out, axis=-1, keepdims=True) + NORM_EPS) * nrm_ref[...]
        o_ref[unit, r0:r0 + CHUNK, :] = out
        for r in range(c * per_piece, (c + 1) * per_piece):
            row = dst_nxt_ref[0, 0, r]
            pltpu.make_async_copy(y_hbm.at[pl.ds(row, 1)], buf.at[1 - slot, pl.ds(r, 1)],
                                  sem.at[1 - slot]).start()

    @pl.when(i == n - 1)
    def _():
        _wait_row_gather(TILE_ASSIGN, y_hbm, buf.at[1 - slot], sem.at[1 - slot])


def _combine(dest_tiles, x_units, mod_units, wts_col, nrm, y_slots, *, final_norm):
    n_units, lt, _ = x_units.shape
    u = MOE_TILE // lt
    n_tiles = n_units // u
    return pl.pallas_call(
        functools.partial(_combine_kernel, final_norm=final_norm),
        grid=(n_tiles,),
        in_specs=[
            pl.BlockSpec((1, 1, TILE_ASSIGN), lambda i: (i, 0, 0), memory_space=pltpu.SMEM),
            pl.BlockSpec((1, 1, TILE_ASSIGN), lambda i: (jnp.minimum(i + 1, n_tiles - 1), 0, 0),
                         memory_space=pltpu.SMEM),
            pl.BlockSpec((u, lt, D_MODEL), lambda i: (i, 0, 0)),
            pl.BlockSpec((u, 6, D_MODEL), lambda i: (i, 0, 0)),
            pl.BlockSpec((MOE_TILE, TOPK), lambda i: (i, 0)),
            pl.BlockSpec((1, D_MODEL), lambda i: (0, 0)),
            pl.BlockSpec(memory_space=pl.ANY),
        ],
        out_specs=pl.BlockSpec((u, lt, D_MODEL), lambda i: (i, 0, 0)),
        out_shape=jax.ShapeDtypeStruct(x_units.shape, F32),
        scratch_shapes=[pltpu.VMEM((2, TILE_ASSIGN, D_MODEL), F32), pltpu.SemaphoreType.DMA((2,))],
        compiler_params=pltpu.CompilerParams(
            dimension_semantics=("arbitrary",), vmem_limit_bytes=VMEM_LIMIT),
        name="moe_combine",
    )(dest_tiles, dest_tiles, x_units, mod_units, wts_col, nrm, y_slots)


def _routing_tables(eid_tiles, rank_tiles, counts, block_rows):
    n_blocks = eid_tiles.size // block_rows + N_EXPERTS
    padded = (counts + block_rows - 1) // block_rows * block_rows
    pad_end = jnp.cumsum(padded).astype(jnp.int32)
    pad_start = pad_end - padded
    block_start = jnp.arange(n_blocks, dtype=jnp.int32)[:, None] * block_rows
    block_e = jnp.minimum(jnp.sum((block_start >= pad_end[None, :]).astype(jnp.int32), axis=1),
                          N_EXPERTS - 1).astype(jnp.int32)
    n_used = pad_end[-1:] // block_rows
    experts = jnp.arange(N_EXPERTS, dtype=jnp.int32)
    first_slot = jnp.sum(jnp.where(eid_tiles[..., None] == experts, pad_start, 0), axis=-1)
    return block_e, n_used, pad_end, padded, first_slot + rank_tiles


def _moe_units(x, mod_l):
    bsz, seq, _ = x.shape
    lt = min(seq, MOE_TILE)
    per = seq // lt
    x_units = x.reshape(bsz * seq // lt, lt, D_MODEL)
    mod_units = jnp.repeat(mod_l, per, axis=0) if per > 1 else mod_l
    return x_units, mod_units


def _moe_experts(x, mod_l, nrm_ffn, wr, br, tri, wg, wu, wd, layer):
    x_units, mod_units = _moe_units(x, mod_l)
    h, eid_tiles, rank_tiles, wts, counts = _router(x_units, mod_units, nrm_ffn, wr, br, tri)
    block_rows = _expert_block_rows(eid_tiles.size)
    block_e, n_used, pad_end, padded, dest_tiles = _routing_tables(
        eid_tiles, rank_tiles, counts[:, 0], block_rows)
    n_slots = block_e.shape[0] * block_rows
    xs = _dispatch(pad_end, padded, dest_tiles, h, n_slots, block_rows)
    return dest_tiles, wts.T, _experts(block_e, n_used, xs, wg, wu, wd, layer, block_rows)


def _dest_per_step(dest_tiles, tb):
    n_tiles = dest_tiles.shape[0]
    per = MOE_TILE // tb
    d = dest_tiles.reshape(n_tiles, TOPK, per, tb).transpose(0, 2, 1, 3)
    return d.reshape(n_tiles * per, 1, TOPK * tb)


def kernel(x_prompt, x_sample, c_prompt, c_sample, state_hgrn, state_gla, w_ada, b_ada, norm_mix,
           norm_ffn, w_in, hg_lb, hg_onorm, w_gk2, b_gk, gla_onorm, w_br_a, w_br_b, w_out, w_rg, b_rg,
           w_re, b_re, w_e_gate, w_e_up, w_e_down, norm_final):
    bp = x_prompt.shape[0]
    bs = x_sample.shape[0]
    mod = _ada_mod(jnp.concatenate([c_prompt, c_sample], axis=0), w_ada, b_ada)
    mod = mod.reshape(DEPTH, bp + bs, 6, D_MODEL)

    glr0 = C_GOG + GLA_W
    win_r = jnp.concatenate(
        [w_in[:, :, :glr0], w_in[:, :, glr0 + GLA_GATE_RANK:], w_in[:, :, glr0:glr0 + GLA_GATE_RANK],
         jnp.zeros((DEPTH, D_MODEL, LANES - GLA_GATE_RANK), F32)], axis=2).astype(BF16)
    wgk2_p = jnp.concatenate(
        [w_gk2, jnp.zeros((DEPTH, LANES - GLA_GATE_RANK, GLA_KW), F32)], axis=1).astype(BF16)
    wa_b = w_br_a.astype(BF16)
    wb_b = w_br_b.astype(BF16)
    wo_b = w_out.astype(BF16)

    def plan_consts(rows):
        r = np.arange(rows)
        chunk_tril = (r[:, None] // CHUNK == r[None, :] // CHUNK) & (r[None, :] <= r[:, None])
        return [jnp.asarray(chunk_tril, BF16), jnp.asarray(_slab_mask(), F32),
                jnp.asarray(SAFE_PLAN.segment_sum_matrix(), BF16), jnp.asarray(SAFE_PLAN.masks(), F32)]

    zpad = jnp.zeros((DEPTH, 8 - N_GROUPS, D_MODEL), F32)
    ztail = jnp.zeros((DEPTH, ROUTER_ROWS - 8 - N_EXPERTS, D_MODEL), F32)
    wr = jnp.concatenate([jnp.swapaxes(w_rg, 1, 2), zpad, jnp.swapaxes(w_re, 1, 2), ztail], axis=1)
    br = jnp.concatenate([b_rg, jnp.zeros((DEPTH, 8 - N_GROUPS), F32), b_re,
                          jnp.zeros((DEPTH, ROUTER_ROWS - 8 - N_EXPERTS), F32)], axis=1)[:, :, None]
    wg_b = w_e_gate.astype(BF16)
    wu_b = w_e_up.astype(BF16)
    wd_b = w_e_down.astype(BF16)
    nrm_f = norm_final.reshape(1, D_MODEL)
    assign = np.arange(TILE_ASSIGN)
    tri = jnp.asarray(assign[:, None] < assign[None, :], BF16)

    def run(x, mod_g, shg, sgla, tb, nb):
        bsz = x.shape[0]
        new_hg, new_gla = [], []
        pending = None
        for l in range(DEPTH):
            x, s1, s2 = _mixer(
                x, mod_g[l], norm_mix[l:l + 1], win_r[l], hg_lb, wgk2_p[l], b_gk[l:l + 1],
                hg_onorm[l:l + 1], gla_onorm[l:l + 1], wa_b[l], wb_b[l], wo_b[l],
                shg[l], sgla[l].reshape(bsz, GLA_KW // LANES, LANES, HEAD_DV), plan_consts(nb * tb),
                layer=l, tb=tb, nb=nb, pending_moe=pending)
            new_hg.append(s1)
            new_gla.append(s2.reshape(bsz, GLA_HEADS, GLA_DK, HEAD_DV))
            dest_tiles, wts_col, y_slots = _moe_experts(
                x, mod_g[l], norm_ffn[l:l + 1], wr[l], br[l], tri, wg_b, wu_b, wd_b, l)
            pending = (_dest_per_step(dest_tiles, nb * tb), wts_col, mod_g[l], y_slots)
        x_units, mod_units = _moe_units(x, mod_g[DEPTH - 1])
        y = _combine(dest_tiles, x_units, mod_units, wts_col, nrm_f, y_slots, final_norm=True)
        return y.reshape(x.shape), jnp.stack(new_hg), jnp.stack(new_gla)

    zeros_hg = jnp.zeros((DEPTH, bp, HG_HEADS, HG_DK, HEAD_DV), F32)
    zeros_gla = jnp.zeros((DEPTH, bp, GLA_HEADS, GLA_DK, HEAD_DV), F32)
    y_p, hg_p, gla_p = run(x_prompt, mod[:, :bp], zeros_hg, zeros_gla, 256, 1)
    y_s, hg_s, gla_s = run(x_sample, mod[:, bp:], state_hgrn, state_gla, CHUNK, 4)
    return (y_p, y_s, hg_p, gla_p, hg_s, gla_s)
```

```python
import functools

import numpy as np
import jax
import jax.numpy as jnp
from jax import lax
from jax.experimental import pallas as pl
from jax.experimental.pallas import tpu as pltpu

F32 = jnp.float32
BF16 = jnp.bfloat16

D_MODEL = 1024
DEPTH = 2
CHUNK = 64
NORM_EPS = 1e-6
LOG_FLOOR = 1e-30
HG_HEADS = 4
HG_DK = 128
HEAD_DV = 128
HG_KW = HG_HEADS * HG_DK
HG_W = HG_HEADS * HEAD_DV
GLA_HEADS = 4
GLA_DK = 64
GLA_KW = GLA_HEADS * GLA_DK
GLA_W = GLA_HEADS * HEAD_DV
GLA_GATE_RANK = 16
GLA_GATE_NORM = 16.0
N_GROUPS = 4
EXPERTS_PER_GROUP = 8
N_EXPERTS = N_GROUPS * EXPERTS_PER_GROUP
TOPK = 2
D_EXPERT = 512

LANES = 128
VMEM_LIMIT = 56 * 1024 * 1024

C_HQ = 0
C_HF = C_HQ + HG_KW
C_HI = C_HF + HG_KW
C_HOG = C_HI + HG_W
C_GQ = C_HOG + HG_W
C_GK = C_GQ + GLA_KW
C_GV = C_GK + GLA_KW
C_GOG = C_GV + GLA_W
C_GA = C_GOG + GLA_W
C_GB = C_GA + D_MODEL
C_GLR = C_GB + D_MODEL
IN_COLS_PAD = C_GLR + LANES
MXU_WIDTH = 256
PROJ_TILE = 4 * MXU_WIDTH


class _ScorePlan:
    def __init__(self, levels, adjacent, diag_block):
        self.levels = levels
        self.adjacent = adjacent
        self.diag_block = diag_block
        self.cum_rows = (2 * len(levels) + 1) * CHUNK
        self.n_masks = len(levels) + 1 + int(adjacent)

    def segment_sum_matrix(self):
        t = np.arange(CHUNK)[:, None]
        r = np.arange(CHUNK)[None, :]
        rows = []
        for m in self.levels:
            same = (t // m) == (r // m)
            rows.append(same & (r <= t))
            rows.append(same & (r > t))
        rows.append(r <= t)
        return np.concatenate(rows, axis=0).astype(np.float32)

    def masks(self):
        t = np.arange(CHUNK)[:, None]
        s = np.arange(CHUNK)[None, :]
        masks = [((t // self.diag_block) == (s // self.diag_block)) & (s <= t)]
        for m in self.levels + ((1,) if self.adjacent else ()):
            masks.append(((t // (2 * m)) == (s // (2 * m))) & ((t // m) % 2 == 1) & ((s // m) % 2 == 0))
        return np.stack(masks).astype(np.float32)


SAFE_PLAN = _ScorePlan((32, 16, 8, 4, 2), True, 1)


def _dot(a, b):
    return jnp.dot(a, b, preferred_element_type=F32)


def _dot_nt(a, b):
    return lax.dot_general(a, b, (((1,), (1,)), ((), ())), preferred_element_type=F32)


def _sigmoid(x):
    return 1.0 / (1.0 + jnp.exp(-x))


def _silu(x):
    return x * _sigmoid(x)


def _rms_mod(x, gain, scale, shift):
    y = x * lax.rsqrt(jnp.mean(x * x, axis=-1, keepdims=True) + NORM_EPS)
    return y * gain * (1.0 + scale) + shift


def _ada_kernel(c_ref, w_ref, b_ref, o_ref):
    c = c_ref[...]
    o_ref[0] = jnp.dot(_silu(c), w_ref[0], preferred_element_type=F32,
                       precision=lax.Precision.HIGHEST) + b_ref[0]


def _ada_mod(c_all, w_ada, b_ada):
    nb = c_all.shape[0]
    tn = 512
    return pl.pallas_call(
        _ada_kernel,
        grid=(DEPTH, 6 * D_MODEL // tn),
        in_specs=[
            pl.BlockSpec((nb, D_MODEL), lambda l, j: (0, 0)),
            pl.BlockSpec((1, D_MODEL, tn), lambda l, j: (l, 0, j)),
            pl.BlockSpec((1, 1, tn), lambda l, j: (l, 0, j)),
        ],
        out_specs=pl.BlockSpec((1, nb, tn), lambda l, j: (l, 0, j)),
        out_shape=jax.ShapeDtypeStruct((DEPTH, nb, 6 * D_MODEL), F32),
        name="ada_mod",
    )(c_all, w_ada, b_ada.reshape(DEPTH, 1, 6 * D_MODEL))


def _chunk_attention(q, k, v, g, states, mall_ref, mask_ref, heads_per_tile, plan):
    w = q.shape[1]
    n_tiles = w // LANES
    g_hi = g.astype(BF16)
    r1 = g - g_hi.astype(F32)
    g_mid = r1.astype(BF16)
    g_lo = (r1 - g_mid.astype(F32)).astype(BF16)
    mall = mall_ref[...]
    cums = _dot(mall, g_hi) + _dot(mall, g_mid) + _dot(mall, g_lo)
    b = cums[plan.cum_rows - CHUNK:plan.cum_rows]
    level_q = []
    level_k = []
    for i in range(len(plan.levels)):
        level_q.append(q * jnp.exp(cums[2 * i * CHUNK:(2 * i + 1) * CHUNK]))
        level_k.append(k * jnp.exp(cums[(2 * i + 1) * CHUNK:(2 * i + 2) * CHUNK]))
    if plan.diag_block == 1:
        qs = [q]
        ks = [k]
    else:
        i = plan.levels.index(plan.diag_block)
        qs = [level_q[i]]
        ks = [k * jnp.exp(-cums[2 * i * CHUNK:(2 * i + 1) * CHUNK])]
    qs += level_q
    ks += level_k
    if plan.adjacent:
        qs.append(q * jnp.exp(g))
        ks.append(k)
    b_last = b[CHUNK - 1:CHUNK]
    q_in = q * jnp.exp(b)
    k_out = k * jnp.exp(b_last - b)
    e_last = jnp.exp(b_last)

    dk = LANES // heads_per_tile
    lane = lax.broadcasted_iota(jnp.int32, (CHUNK, LANES), 1)
    row = lax.broadcasted_iota(jnp.int32, (LANES, HEAD_DV), 0)
    outs = []
    new_states = []
    for ti in range(n_tiles):
        sl = slice(ti * LANES, (ti + 1) * LANES)
        ks_t = [kk[:, sl].astype(BF16) for kk in ks]
        k_out_t = k_out[:, sl].T.astype(BF16)
        e_col = jnp.broadcast_to(e_last[:, sl], (LANES, LANES)).T
        s_old = states[ti]
        s_old_b = s_old.astype(BF16)
        upd = None
        for j in range(heads_per_tile):
            head = ti * heads_per_tile + j
            if heads_per_tile == 1:
                sel = lambda a: a
            else:
                in_head = (lane // dk) == j
                sel = lambda a, in_head=in_head: jnp.where(in_head, a, 0.0)
            sc = jnp.zeros((CHUNK, CHUNK), F32)
            for i in range(plan.n_masks):
                sc = sc + _dot_nt(sel(qs[i][:, sl]).astype(BF16), ks_t[i]) * mask_ref[i]
            vh = v[:, head * HEAD_DV:(head + 1) * HEAD_DV].astype(BF16)
            o = _dot(sc.astype(BF16), vh) + _dot(sel(q_in[:, sl]).astype(BF16), s_old_b)
            outs.append(o)
            u = _dot(k_out_t, vh)
            upd = u if upd is None else jnp.where((row // dk) == j, u, upd)
        new_states.append(e_col * s_old + upd)
    return jnp.concatenate(outs, axis=1), new_states


FAST_BLOCK = 16
N_SUB = CHUNK // FAST_BLOCK
SLAB_ROWS = FAST_BLOCK * (N_SUB * (N_SUB - 1) // 2) + CHUNK
FAST_BLOCK_DECAY_LIMIT = 60.0


def _slab_mask():
    t = np.arange(CHUNK)[:, None]
    cols = []
    for i in range(1, N_SUB):
        cols.append(np.broadcast_to(t // FAST_BLOCK == i, (CHUNK, i * FAST_BLOCK)))
    s = np.arange(CHUNK)[None, :]
    cols.append((t // FAST_BLOCK == s // FAST_BLOCK) & (s <= t))
    return np.concatenate(cols, axis=1).astype(np.float32)


def _block_attention_fast(q, k, v, g, states, tril_ref, slab_mask_ref, heads_per_tile, carry=True):
    rows, w = q.shape
    n_chunks = rows // CHUNK
    n_tiles = w // LANES
    dk = LANES // heads_per_tile
    n_heads = n_tiles * heads_per_tile

    g_hi = g.astype(BF16)
    r1 = g - g_hi.astype(F32)
    g_mid = r1.astype(BF16)
    g_lo = (r1 - g_mid.astype(F32)).astype(BF16)
    tril = tril_ref[...]
    b = _dot(tril, g_hi) + _dot(tril, g_mid) + _dot(tril, g_lo)

    def end_row(c, i):
        r = c * CHUNK + (i + 1) * FAST_BLOCK
        return b[r - 1:r]

    def per_block(row_of):
        return jnp.concatenate([jnp.broadcast_to(row_of(c, i), (FAST_BLOCK, w))
                                for c in range(n_chunks) for i in range(N_SUB)], axis=0)

    zero = jnp.zeros((1, w), F32)
    b_start = per_block(lambda c, i: zero if i == 0 else end_row(c, i - 1))
    b_end = per_block(end_row)
    q_blk = q * jnp.exp(b - b_start)
    k_diag = k * jnp.exp(b_start - b)
    k_end = k * jnp.exp(b_end - b)
    q_in = q_blk * jnp.exp(b_start)
    k_out = k_end * jnp.exp(per_block(lambda c, i: end_row(c, N_SUB - 1)) - b_end)

    lane = lax.broadcasted_iota(jnp.int32, (CHUNK, LANES), 1)
    row = lax.broadcasted_iota(jnp.int32, (LANES, HEAD_DV), 0)
    slab_mask = slab_mask_ref[...]

    def sel(a, j):
        return a if heads_per_tile == 1 else jnp.where((lane // dk) == j, a, 0.0)

    v_b = v.astype(BF16)

    scores = {}
    for c in range(n_chunks):
        r0 = c * CHUNK
        slabs = []
        for i in range(1, N_SUB):
            for jb in range(i):
                blk = k_end[r0 + jb * FAST_BLOCK:r0 + (jb + 1) * FAST_BLOCK]
                slabs.append(blk if jb == i - 1 else blk * jnp.exp(end_row(c, i - 1) - end_row(c, jb)))
        slabs.append(k_diag[r0:r0 + CHUNK])
        k_slab = jnp.concatenate(slabs, axis=0).astype(BF16)
        for ti in range(n_tiles):
            sl = slice(ti * LANES, (ti + 1) * LANES)
            for j in range(heads_per_tile):
                qh = sel(q_blk[r0:r0 + CHUNK, sl], j).astype(BF16)
                scores[c, ti * heads_per_tile + j] = (_dot_nt(qh, k_slab[:, sl]) * slab_mask).astype(BF16)

    entering = [list(states)] if carry else [list(st) for st in states]
    leaving = []
    for c in range(n_chunks):
        r0 = c * CHUNK
        nxt = []
        for ti in range(n_tiles):
            sl = slice(ti * LANES, (ti + 1) * LANES)
            k_out_t = k_out[r0:r0 + CHUNK, sl].T.astype(BF16)
            upd = None
            for j in range(heads_per_tile):
                head = ti * heads_per_tile + j
                u = _dot(k_out_t, v_b[r0:r0 + CHUNK, head * HEAD_DV:(head + 1) * HEAD_DV])
                upd = u if upd is None else jnp.where((row // dk) == j, u, upd)
            e_col = jnp.broadcast_to(jnp.exp(end_row(c, N_SUB - 1)[:, sl]), (LANES, LANES)).T
            nxt.append(e_col * entering[c][ti] + upd)
        leaving.append(nxt)
        if carry:
            entering.append(nxt)

    out_rows = []
    for c in range(n_chunks):
        r0 = c * CHUNK
        outs = []
        for head in range(n_heads):
            ti, j = divmod(head, heads_per_tile)
            sl = slice(ti * LANES, (ti + 1) * LANES)
            vh = v_b[r0:r0 + CHUNK, head * HEAD_DV:(head + 1) * HEAD_DV]
            v_slab = jnp.concatenate([vh[:i * FAST_BLOCK] for i in range(1, N_SUB)] + [vh], axis=0)
            outs.append(_dot(scores[c, head], v_slab)
                        + _dot(sel(q_in[r0:r0 + CHUNK, sl], j).astype(BF16), entering[c][ti].astype(BF16)))
        out_rows.append(jnp.concatenate(outs, axis=1))
    return jnp.concatenate(out_rows, axis=0), (leaving[-1] if carry else leaving)


def _head_norm_gate(o, gain, gate):
    outs = []
    for h in range(o.shape[1] // HEAD_DV):
        sl = slice(h * HEAD_DV, (h + 1) * HEAD_DV)
        oh = o[:, sl]
        oh = oh * lax.rsqrt(jnp.mean(oh * oh, axis=-1, keepdims=True) + NORM_EPS) * gain
        outs.append(oh * _silu(gate[:, sl]))
    return jnp.concatenate(outs, axis=1)


def _mixer_kernel(x_ref, mod_ref, nrm_ref, win_ref, lb_ref, wgk2_ref, bgk_ref, hgn_ref, glan_ref,
                  wa_ref, wb_ref, wo_ref, shg0_ref, sgla0_ref,
                  tril_ref, slab_mask_ref, mall_safe_ref, mask_safe_ref, *rest, layer, tb, pending_moe):
    if pending_moe:
        dst_cur_ref, dst_nxt_ref, wts_ref, modp_ref, y_hbm = rest[:5]
        rest = rest[5:]
    xo_ref, shg_o_ref, sgla_o_ref, p_scr, k_scr, lg_scr, shg_scr, sgla_scr = rest[:8]
    j = pl.program_id(1)
    nb = x_ref.shape[0]
    rows_all = nb * tb

    @pl.when(j == 0)
    def _():
        shg_scr[...] = shg0_ref[...].reshape(shg_scr.shape)
        sgla_scr[...] = sgla0_ref[...].reshape(sgla_scr.shape)

    def per_row(ref, i):
        if nb == 1:
            return ref[0, i:i + 1, :]
        return jnp.broadcast_to(ref[:, i:i + 1, :], (nb, tb, D_MODEL)).reshape(rows_all, D_MODEL)

    x = x_ref[...].reshape(rows_all, D_MODEL)
    n_prefetch = 0
    if pending_moe:
        cbuf, csem = rest[8:]
        n_prefetch = TOPK * rows_all
        step = pl.program_id(0) * pl.num_programs(1) + j
        last = pl.num_programs(0) * pl.num_programs(1) - 1
        slot = step % 2

        @pl.when(step == 0)
        def _():
            _start_row_gather(dst_cur_ref, n_prefetch, y_hbm, cbuf.at[0], csem.at[0])

        _wait_row_gather(n_prefetch, y_hbm, cbuf.at[slot], csem.at[slot])
        x = x + per_row(modp_ref, 5) * (wts_ref[:, 0:1] * cbuf[slot, 0:rows_all, :]
                                        + wts_ref[:, 1:2] * cbuf[slot, rows_all:n_prefetch, :])

    def prefetch_rows(r0, r1):
        for r in range(r0, r1):
            row = dst_nxt_ref[0, 0, r]
            pltpu.make_async_copy(y_hbm.at[pl.ds(row, 1)], cbuf.at[1 - slot, pl.ds(r, 1)],
                                  csem.at[1 - slot]).start()

    sh1 = per_row(mod_ref, 0)
    sc1 = per_row(mod_ref, 1)
    g1 = per_row(mod_ref, 2)
    hb = _rms_mod(x, nrm_ref[...], sc1, sh1).astype(BF16)
    col_tiles = list(range(0, IN_COLS_PAD, PROJ_TILE))
    for i, c in enumerate(col_tiles):
        c1 = min(c + PROJ_TILE, IN_COLS_PAD)
        p_scr[:, c:c1] = _dot(hb, win_ref[:, c:c1])
        if pending_moe:
            prefetch_rows(n_prefetch * i // len(col_tiles), n_prefetch * (i + 1) // len(col_tiles))

    lb_all = lb_ref[...]
    lb_max = jnp.max(lb_all, axis=0, keepdims=True)
    lb_exp = jnp.exp(lb_all - lb_max)
    sm = lb_exp / jnp.sum(lb_exp, axis=0, keepdims=True)
    lbl = jnp.clip(jnp.sum(sm[0:layer + 1], axis=0, keepdims=True) - sm[0:1], 0.0, 1.0)

    p_scr[:, C_HQ:C_HQ + HG_KW] = _silu(p_scr[:, C_HQ:C_HQ + HG_KW]) * (HG_DK ** -0.5)
    z = p_scr[:, C_HF:C_HF + HG_KW]
    f = lbl + (1.0 - lbl) * _sigmoid(z)
    p_scr[:, C_HF:C_HF + HG_KW] = jnp.log(jnp.maximum(f, LOG_FLOOR))
    k_scr[...] = (1.0 - lbl) * _sigmoid(-z)
    glr = p_scr[:, C_GLR:C_GLR + LANES].astype(BF16)
    gate = _dot(glr, wgk2_ref[...]) + bgk_ref[...]
    lg_scr[...] = (jnp.minimum(gate, 0.0) - jnp.log1p(jnp.exp(-jnp.abs(gate)))) * (1.0 / GLA_GATE_NORM)
    p_scr[:, C_GQ:C_GQ + GLA_KW] = p_scr[:, C_GQ:C_GQ + GLA_KW] * (GLA_DK ** -0.5)

    n_hg_tiles = HG_KW // LANES
    n_gla_tiles = GLA_KW // LANES

    def one_chunk(rows, states, attend):
        o_hg, st_hg = attend(p_scr[rows, C_HQ:C_HQ + HG_KW], k_scr[rows, :],
                             p_scr[rows, C_HI:C_HI + HG_W], p_scr[rows, C_HF:C_HF + HG_KW],
                             states[:n_hg_tiles], 1)
        p_scr[rows, C_HI:C_HI + HG_W] = o_hg
        o_gla, st_gla = attend(p_scr[rows, C_GQ:C_GQ + GLA_KW], p_scr[rows, C_GK:C_GK + GLA_KW],
                               p_scr[rows, C_GV:C_GV + GLA_W], lg_scr[rows, :],
                               states[n_hg_tiles:], 2)
        p_scr[rows, C_GV:C_GV + GLA_W] = o_gla
        return st_hg + st_gla

    def attend_safe(q, k, v, g, states, heads_per_tile):
        return _chunk_attention(q, k, v, g, states, mall_safe_ref, mask_safe_ref, heads_per_tile, SAFE_PLAN)

    def load_states(seq=0):
        return ([shg_scr[seq * n_hg_tiles + t] for t in range(n_hg_tiles)]
                + [sgla_scr[seq * n_gla_tiles + t] for t in range(n_gla_tiles)])

    def store_states(states, seq=0):
        for t in range(n_hg_tiles):
            shg_scr[seq * n_hg_tiles + t] = states[t]
        for t in range(n_gla_tiles):
            sgla_scr[seq * n_gla_tiles + t] = states[n_hg_tiles + t]

    def run_block_fast():
        carry = nb == 1
        seqs = [load_states(s) for s in range(nb)]
        hg_in = seqs[0][:n_hg_tiles] if carry else [st[:n_hg_tiles] for st in seqs]
        gla_in = seqs[0][n_hg_tiles:] if carry else [st[n_hg_tiles:] for st in seqs]
        o_hg, st_hg = _block_attention_fast(
            p_scr[:, C_HQ:C_HQ + HG_KW], k_scr[...], p_scr[:, C_HI:C_HI + HG_W],
            p_scr[:, C_HF:C_HF + HG_KW], hg_in, tril_ref, slab_mask_ref, 1, carry)
        p_scr[:, C_HI:C_HI + HG_W] = o_hg
        o_gla, st_gla = _block_attention_fast(
            p_scr[:, C_GQ:C_GQ + GLA_KW], p_scr[:, C_GK:C_GK + GLA_KW], p_scr[:, C_GV:C_GV + GLA_W],
            lg_scr[...], gla_in, tril_ref, slab_mask_ref, 2, carry)
        p_scr[:, C_GV:C_GV + GLA_W] = o_gla
        if carry:
            store_states(st_hg + st_gla)
        else:
            for s in range(nb):
                store_states(st_hg[s] + st_gla[s], s)

    def run_chunks_safe():
        def chunk_body(ci, carry):
            rows = pl.ds(pl.multiple_of(ci * CHUNK, CHUNK), CHUNK)
            seq = 0 if nb == 1 else ci
            store_states(one_chunk(rows, load_states(seq), attend_safe), seq)
            return carry

        lax.fori_loop(0, rows_all // CHUNK, chunk_body, 0)

    blk = FAST_BLOCK
    min_hg = jnp.min(jnp.sum(p_scr[:, C_HF:C_HF + HG_KW].reshape(rows_all // blk, blk, HG_KW), axis=1))
    min_gla = jnp.min(jnp.sum(lg_scr[...].reshape(rows_all // blk, blk, GLA_KW), axis=1))
    bounded = jnp.minimum(min_hg, min_gla) >= -FAST_BLOCK_DECAY_LIMIT

    @pl.when(bounded)
    def _():
        run_block_fast()

    @pl.when(jnp.logical_not(bounded))
    def _():
        run_chunks_safe()

    o_hg = _head_norm_gate(p_scr[:, C_HI:C_HI + HG_W], hgn_ref[...], p_scr[:, C_HOG:C_HOG + HG_W])
    o_gla = _head_norm_gate(p_scr[:, C_GV:C_GV + GLA_W], glan_ref[...], p_scr[:, C_GOG:C_GOG + GLA_W])
    ya = _dot(o_hg.astype(BF16), wa_ref[...])
    yb = _dot(o_gla.astype(BF16), wb_ref[...])
    merged = (_sigmoid(p_scr[:, C_GA:C_GA + D_MODEL]) * ya
              + _sigmoid(p_scr[:, C_GB:C_GB + D_MODEL]) * yb)
    m = _dot(merged.astype(BF16), wo_ref[...])
    xo_ref[...] = (x + g1 * m).reshape(nb, tb, D_MODEL)

    @pl.when(j == pl.num_programs(1) - 1)
    def _():
        shg_o_ref[...] = shg_scr[...].reshape(shg_o_ref.shape)
        sgla_o_ref[...] = sgla_scr[...].reshape(sgla_o_ref.shape)

    if pending_moe:
        @pl.when(step == last)
        def _():
            _wait_row_gather(n_prefetch, y_hbm, cbuf.at[1 - slot], csem.at[1 - slot])


def _const_spec(shape):
    nd = len(shape)
    return pl.BlockSpec(shape, lambda b, j, nd=nd: (0,) * nd, pipeline_mode=pl.Buffered(1))


def _mixer(x, mod, nrm, win, hg_lb, wgk2, bgk, hgn, glan, wa, wb, wo, shg0, sgla0, plan_consts,
           *, layer, tb, nb=1, pending_moe=None):
    bsz, seq, _ = x.shape
    assert nb == 1 or seq == tb == CHUNK
    nj = seq // tb
    rows = nb * tb
    kern = functools.partial(_mixer_kernel, layer=layer, tb=tb, pending_moe=pending_moe is not None)
    n_gla_tiles = GLA_KW // LANES
    extra_specs, extra_args, extra_scratch = [], [], []
    if pending_moe is not None:
        dest_steps, wts_col, mod_prev, y_slots = pending_moe
        n_steps = bsz // nb * nj
        extra_specs = [
            pl.BlockSpec((1, 1, TOPK * rows), lambda b, j: (b * nj + j, 0, 0), memory_space=pltpu.SMEM),
            pl.BlockSpec((1, 1, TOPK * rows), lambda b, j: (jnp.minimum(b * nj + j + 1, n_steps - 1), 0, 0),
                         memory_space=pltpu.SMEM),
            pl.BlockSpec((rows, TOPK), lambda b, j: (b * nj + j, 0)),
            pl.BlockSpec((nb, 6, D_MODEL), lambda b, j: (b, 0, 0)),
            pl.BlockSpec(memory_space=pl.ANY),
        ]
        extra_args = [dest_steps, dest_steps, wts_col, mod_prev, y_slots]
        extra_scratch = [pltpu.VMEM((2, TOPK * rows, D_MODEL), F32), pltpu.SemaphoreType.DMA((2,))]
    return pl.pallas_call(
        kern,
        grid=(bsz // nb, nj),
        in_specs=[
            pl.BlockSpec((nb, tb, D_MODEL), lambda b, j: (b, j, 0)),
            pl.BlockSpec((nb, 6, D_MODEL), lambda b, j: (b, 0, 0)),
            _const_spec((1, D_MODEL)),
            _const_spec((D_MODEL, IN_COLS_PAD)),
            _const_spec((DEPTH, HG_KW)),
            _const_spec((LANES, GLA_KW)),
            _const_spec((1, GLA_KW)),
            _const_spec((1, HEAD_DV)),
            _const_spec((1, HEAD_DV)),
            _const_spec((HG_W, D_MODEL)),
            _const_spec((GLA_W, D_MODEL)),
            _const_spec((D_MODEL, D_MODEL)),
            pl.BlockSpec((nb, HG_HEADS, HG_DK, HEAD_DV), lambda b, j: (b, 0, 0, 0)),
            pl.BlockSpec((nb, n_gla_tiles, LANES, HEAD_DV), lambda b, j: (b, 0, 0, 0)),
            _const_spec((rows, rows)),
            _const_spec((CHUNK, SLAB_ROWS)),
            _const_spec((SAFE_PLAN.cum_rows, CHUNK)),
            _const_spec((SAFE_PLAN.n_masks, CHUNK, CHUNK)),
        ] + extra_specs,
        out_specs=[
            pl.BlockSpec((nb, tb, D_MODEL), lambda b, j: (b, j, 0)),
            pl.BlockSpec((nb, HG_HEADS, HG_DK, HEAD_DV), lambda b, j: (b, 0, 0, 0)),
            pl.BlockSpec((nb, n_gla_tiles, LANES, HEAD_DV), lambda b, j: (b, 0, 0, 0)),
        ],
        out_shape=[
            jax.ShapeDtypeStruct((bsz, seq, D_MODEL), F32),
            jax.ShapeDtypeStruct((bsz, HG_HEADS, HG_DK, HEAD_DV), F32),
            jax.ShapeDtypeStruct((bsz, n_gla_tiles, LANES, HEAD_DV), F32),
        ],
        scratch_shapes=[
            pltpu.VMEM((rows, IN_COLS_PAD), F32),
            pltpu.VMEM((rows, HG_KW), F32),
            pltpu.VMEM((rows, GLA_KW), F32),
            pltpu.VMEM((nb * HG_HEADS, HG_DK, HEAD_DV), F32),
            pltpu.VMEM((nb * n_gla_tiles, LANES, HEAD_DV), F32),
        ] + extra_scratch,
        compiler_params=pltpu.CompilerParams(
            dimension_semantics=("arbitrary", "arbitrary"), vmem_limit_bytes=VMEM_LIMIT),
        name=f"mixer_l{layer}",
    )(x, mod, nrm, win, hg_lb, wgk2, bgk, hgn, glan, wa, wb, wo, shg0, sgla0, *plan_consts, *extra_args)


ROUTER_ROWS = 48
MOE_TILE = 512
TILE_ASSIGN = TOPK * MOE_TILE
MAX_EXPERT_BLOCK = 512


def _expert_block_rows(n_assign):
    return max(LANES, min(MAX_EXPERT_BLOCK, n_assign // N_EXPERTS // 2))


def _first_argmax_rows(vals, n):
    ridx = lax.broadcasted_iota(jnp.int32, vals.shape, 0)
    vmax = jnp.max(vals, axis=0, keepdims=True)
    imax = jnp.min(jnp.where(vals == vmax, ridx, n), axis=0, keepdims=True)
    return vmax, imax


def _router_kernel(x_ref, mod_ref, nrm_ref, wr_ref, br_ref, tri_ref,
                   h_ref, eid_ref, rank_ref, wts_ref, cnt_ref, run_scr):
    @pl.when(pl.program_id(0) == 0)
    def _():
        run_scr[...] = jnp.zeros_like(run_scr)

    u, lt, _ = x_ref.shape
    x = x_ref[...]
    sh2 = mod_ref[:, 3:4, :]
    sc2 = mod_ref[:, 4:5, :]
    h = _rms_mod(x, nrm_ref[...].reshape(1, 1, D_MODEL), sc2, sh2).reshape(u * lt, D_MODEL)
    h_ref[...] = h
    h_hi = h.astype(BF16)
    h_lo = (h - h_hi.astype(F32)).astype(BF16)
    w = wr_ref[...]
    w_hi = w.astype(BF16)
    w_lo = (w - w_hi.astype(F32)).astype(BF16)
    p_hi = _dot_nt(jnp.concatenate([w_hi, w_lo], axis=0), h_hi)
    logits = p_hi[:ROUTER_ROWS] + p_hi[ROUTER_ROWS:] + _dot_nt(w_hi, h_lo) + br_ref[...]
    gl = logits[0:N_GROUPS]
    gmax, gi = _first_argmax_rows(gl, N_GROUPS)
    gp = 1.0 / jnp.sum(jnp.exp(gl - gmax), axis=0, keepdims=True)
    le = logits[8:8 + EXPERTS_PER_GROUP]
    for g in range(1, N_GROUPS):
        le = jnp.where(gi == g, logits[8 + g * EXPERTS_PER_GROUP:8 + (g + 1) * EXPERTS_PER_GROUP], le)
    pe = jnp.exp(le - jnp.max(le, axis=0, keepdims=True))
    pe = pe / jnp.sum(pe, axis=0, keepdims=True)
    v1, i1 = _first_argmax_rows(pe, EXPERTS_PER_GROUP)
    ridx = lax.broadcasted_iota(jnp.int32, pe.shape, 0)
    v2, i2 = _first_argmax_rows(jnp.where(ridx == i1, -1.0, pe), EXPERTS_PER_GROUP)
    vsum = v1 + v2
    wts_ref[0:1, :] = gp * v1 / vsum
    wts_ref[1:2, :] = gp * v2 / vsum
    eflat = jnp.concatenate([gi * EXPERTS_PER_GROUP + i1, gi * EXPERTS_PER_GROUP + i2], axis=1)
    eid_ref[0] = eflat
    onehot = (eflat == lax.broadcasted_iota(jnp.int32, (N_EXPERTS, TILE_ASSIGN), 0)).astype(F32)
    before = _dot(onehot.astype(BF16), tri_ref[...]) + run_scr[...]
    rank_ref[0] = jnp.sum(onehot * before, axis=0, keepdims=True).astype(jnp.int32)
    run_scr[...] = run_scr[...] + jnp.sum(onehot, axis=1, keepdims=True)
    cnt_ref[...] = run_scr[...].astype(jnp.int32)


def _router(x_units, mod_units, nrm, wr, br, tri):
    n_units, lt, _ = x_units.shape
    u = MOE_TILE // lt
    n_tiles = n_units // u
    return pl.pallas_call(
        _router_kernel,
        grid=(n_tiles,),
        in_specs=[
            pl.BlockSpec((u, lt, D_MODEL), lambda i: (i, 0, 0)),
            pl.BlockSpec((u, 6, D_MODEL), lambda i: (i, 0, 0)),
            pl.BlockSpec((1, D_MODEL), lambda i: (0, 0)),
            pl.BlockSpec((ROUTER_ROWS, D_MODEL), lambda i: (0, 0)),
            pl.BlockSpec((ROUTER_ROWS, 1), lambda i: (0, 0)),
            pl.BlockSpec((TILE_ASSIGN, TILE_ASSIGN), lambda i: (0, 0)),
        ],
        out_specs=[
            pl.BlockSpec((MOE_TILE, D_MODEL), lambda i: (i, 0)),
            pl.BlockSpec((1, 1, TILE_ASSIGN), lambda i: (i, 0, 0)),
            pl.BlockSpec((1, 1, TILE_ASSIGN), lambda i: (i, 0, 0)),
            pl.BlockSpec((TOPK, MOE_TILE), lambda i: (0, i)),
            pl.BlockSpec((N_EXPERTS, 1), lambda i: (0, 0)),
        ],
        out_shape=[
            jax.ShapeDtypeStruct((n_tiles * MOE_TILE, D_MODEL), F32),
            jax.ShapeDtypeStruct((n_tiles, 1, TILE_ASSIGN), jnp.int32),
            jax.ShapeDtypeStruct((n_tiles, 1, TILE_ASSIGN), jnp.int32),
            jax.ShapeDtypeStruct((TOPK, n_tiles * MOE_TILE), F32),
            jax.ShapeDtypeStruct((N_EXPERTS, 1), jnp.int32),
        ],
        scratch_shapes=[pltpu.VMEM((N_EXPERTS, 1), F32)],
        compiler_params=pltpu.CompilerParams(dimension_semantics=("arbitrary",)),
        name="moe_router",
    )(x_units, mod_units, nrm, wr, br, tri)


def _start_row_gather(idx_ref, n_rows, src_hbm, dst, sem):
    def body(r, carry):
        row = idx_ref[0, 0, r]
        pltpu.make_async_copy(src_hbm.at[pl.ds(row, 1)], dst.at[pl.ds(r, 1)], sem).start()
        return carry
    lax.fori_loop(0, n_rows, body, 0, unroll=8)


def _wait_row_gather(n_rows, src_hbm, dst, sem):
    pltpu.make_async_copy(src_hbm.at[pl.ds(0, n_rows)], dst, sem).wait()


def _dispatch_kernel(pend_ref, padded_ref, dest_ref, h_ref, h_hbm, xs_hbm, zbuf, sem):
    block_rows = zbuf.shape[0]
    n_blocks = xs_hbm.shape[0] // block_rows

    def zero_block(first_row):
        return pltpu.make_async_copy(
            zbuf, xs_hbm.at[pl.ds(pl.multiple_of(first_row, block_rows), block_rows)], sem.at[0])

    @pl.when(pl.program_id(0) == 0)
    def _():
        zbuf[...] = jnp.zeros_like(zbuf)
        n_used = pend_ref[N_EXPERTS - 1] // block_rows
        for e in range(N_EXPERTS):
            @pl.when(padded_ref[e] > 0)
            def _():
                zero_block(pend_ref[e] - block_rows).start()
        lax.fori_loop(n_used, n_blocks, lambda b, c: (zero_block(b * block_rows).start(), c)[1], 0)
        for e in range(N_EXPERTS):
            @pl.when(padded_ref[e] > 0)
            def _():
                zero_block(pend_ref[e] - block_rows).wait()
        lax.fori_loop(n_used, n_blocks, lambda b, c: (zero_block(b * block_rows).wait(), c)[1], 0)

    row0 = pl.program_id(0) * MOE_TILE

    def body(t, carry):
        pltpu.make_async_copy(h_ref.at[pl.ds(t, 1)], xs_hbm.at[pl.ds(dest_ref[0, 0, t], 1)],
                              sem.at[1]).start()
        pltpu.make_async_copy(h_hbm.at[pl.ds(row0 + t, 1)],
                              xs_hbm.at[pl.ds(dest_ref[0, 0, MOE_TILE + t], 1)], sem.at[1]).start()
        return carry
    lax.fori_loop(0, MOE_TILE, body, 0, unroll=8)
    for k in range(TOPK):
        pltpu.make_async_copy(h_ref, xs_hbm.at[pl.ds(0, MOE_TILE)], sem.at[1]).wait()


def _dispatch(pad_end, padded, dest_tiles, h, n_slots, block_rows):
    n_tiles = dest_tiles.shape[0]
    grid_spec = pltpu.PrefetchScalarGridSpec(
        num_scalar_prefetch=2,
        grid=(n_tiles,),
        in_specs=[
            pl.BlockSpec((1, 1, TILE_ASSIGN), lambda i, pe, pd: (i, 0, 0), memory_space=pltpu.SMEM),
            pl.BlockSpec((MOE_TILE, D_MODEL), lambda i, pe, pd: (i, 0)),
            pl.BlockSpec(memory_space=pl.ANY),
        ],
        out_specs=pl.BlockSpec(memory_space=pl.ANY),
        scratch_shapes=[pltpu.VMEM((block_rows, D_MODEL), F32), pltpu.SemaphoreType.DMA((2,))],
    )
    return pl.pallas_call(
        _dispatch_kernel,
        grid_spec=grid_spec,
        out_shape=jax.ShapeDtypeStruct((n_slots, D_MODEL), F32),
        compiler_params=pltpu.CompilerParams(dimension_semantics=("arbitrary",)),
        name="moe_dispatch",
    )(pad_end, padded, dest_tiles, h, h)


def _experts_kernel(be_ref, nused_ref, x_ref, wg_ref, wu_ref, wd_ref, o_ref):
    @pl.when(pl.program_id(0) < nused_ref[0])
    def _():
        xb = x_ref[...].astype(BF16)
        a = _silu(_dot(xb, wg_ref[0, 0])) * _dot(xb, wu_ref[0, 0])
        o_ref[...] = _dot(a.astype(BF16), wd_ref[0, 0])

    @pl.when(pl.program_id(0) >= nused_ref[0])
    def _():
        o_ref[...] = jnp.zeros_like(o_ref)


def _experts(block_e, n_used, xs, wg, wu, wd, layer, block_rows):
    n_blocks = xs.shape[0] // block_rows

    def row_block(i, be, nu):
        return (jnp.minimum(i, nu[0] - 1), 0)

    def expert_block(i, be, nu):
        return (layer, be[jnp.minimum(i, nu[0] - 1)], 0, 0)

    grid_spec = pltpu.PrefetchScalarGridSpec(
        num_scalar_prefetch=2,
        grid=(n_blocks,),
        in_specs=[
            pl.BlockSpec((block_rows, D_MODEL), row_block),
            pl.BlockSpec((1, 1, D_MODEL, D_EXPERT), expert_block),
            pl.BlockSpec((1, 1, D_MODEL, D_EXPERT), expert_block),
            pl.BlockSpec((1, 1, D_EXPERT, D_MODEL), expert_block),
        ],
        out_specs=pl.BlockSpec((block_rows, D_MODEL), lambda i, be, nu: (i, 0)),
    )
    return pl.pallas_call(
        _experts_kernel,
        grid_spec=grid_spec,
        out_shape=jax.ShapeDtypeStruct(xs.shape, F32),
        compiler_params=pltpu.CompilerParams(
            dimension_semantics=("arbitrary",), vmem_limit_bytes=VMEM_LIMIT),
        name="moe_experts",
    )(block_e, n_used, xs, wg, wu, wd)


def _combine_kernel(dst_cur_ref, dst_nxt_ref, x_ref, mod_ref, wts_ref, nrm_ref, y_hbm, o_ref, buf, sem,
                    *, final_norm):
    i = pl.program_id(0)
    n = pl.num_programs(0)
    slot = i % 2

    @pl.when(i == 0)
    def _():
        _start_row_gather(dst_cur_ref, TILE_ASSIGN, y_hbm, buf.at[0], sem.at[0])

    _wait_row_gather(TILE_ASSIGN, y_hbm, buf.at[slot], sem.at[slot])
    u, lt, _ = x_ref.shape
    n_pieces = MOE_TILE // CHUNK
    per_piece = TILE_ASSIGN // n_pieces
    for c in range(n_pieces):
        unit, r0 = divmod(c * CHUNK, lt)
        t0 = c * CHUNK
        y = (wts_ref[t0:t0 + CHUNK, 0:1] * buf[slot, t0:t0 + CHUNK, :]
             + wts_ref[t0:t0 + CHUNK, 1:2] * buf[slot, MOE_TILE + t0:MOE_TILE + t0 + CHUNK, :])
        out = x_ref[unit, r0:r0 + CHUNK, :] + mod_ref[unit, 5:6, :] * y
        if final_norm:
            out = out * lax.rsqrt(jnp.mean(out * out, axis=-1, keepdims=True) + NORM_EPS) * nrm_ref[...]
        o_ref[unit, r0:r0 + CHUNK, :] = out
        for r in range(c * per_piece, (c + 1) * per_piece):
            row = dst_nxt_ref[0, 0, r]
            pltpu.make_async_copy(y_hbm.at[pl.ds(row, 1)], buf.at[1 - slot, pl.ds(r, 1)],
                                  sem.at[1 - slot]).start()

    @pl.when(i == n - 1)
    def _():
        _wait_row_gather(TILE_ASSIGN, y_hbm, buf.at[1 - slot], sem.at[1 - slot])


def _combine(dest_tiles, x_units, mod_units, wts_col, nrm, y_slots, *, final_norm):
    n_units, lt, _ = x_units.shape
    u = MOE_TILE // lt
    n_tiles = n_units // u
    return pl.pallas_call(
        functools.partial(_combine_kernel, final_norm=final_norm),
        grid=(n_tiles,),
        in_specs=[
            pl.BlockSpec((1, 1, TILE_ASSIGN), lambda i: (i, 0, 0), memory_space=pltpu.SMEM),
            pl.BlockSpec((1, 1, TILE_ASSIGN), lambda i: (jnp.minimum(i + 1, n_tiles - 1), 0, 0),
                         memory_space=pltpu.SMEM),
            pl.BlockSpec((u, lt, D_MODEL), lambda i: (i, 0, 0)),
            pl.BlockSpec((u, 6, D_MODEL), lambda i: (i, 0, 0)),
            pl.BlockSpec((MOE_TILE, TOPK), lambda i: (i, 0)),
            pl.BlockSpec((1, D_MODEL), lambda i: (0, 0)),
            pl.BlockSpec(memory_space=pl.ANY),
        ],
        out_specs=pl.BlockSpec((u, lt, D_MODEL), lambda i: (i, 0, 0)),
        out_shape=jax.ShapeDtypeStruct(x_units.shape, F32),
        scratch_shapes=[pltpu.VMEM((2, TILE_ASSIGN, D_MODEL), F32), pltpu.SemaphoreType.DMA((2,))],
        compiler_params=pltpu.CompilerParams(
            dimension_semantics=("arbitrary",), vmem_limit_bytes=VMEM_LIMIT),
        name="moe_combine",
    )(dest_tiles, dest_tiles, x_units, mod_units, wts_col, nrm, y_slots)


def _routing_tables(eid_tiles, rank_tiles, counts, block_rows):
    n_blocks = eid_tiles.size // block_rows + N_EXPERTS
    padded = (counts + block_rows - 1) // block_rows * block_rows
    pad_end = jnp.cumsum(padded).astype(jnp.int32)
    pad_start = pad_end - padded
    block_start = jnp.arange(n_blocks, dtype=jnp.int32)[:, None] * block_rows
    block_e = jnp.minimum(jnp.sum((block_start >= pad_end[None, :]).astype(jnp.int32), axis=1),
                          N_EXPERTS - 1).astype(jnp.int32)
    n_used = pad_end[-1:] // block_rows
    experts = jnp.arange(N_EXPERTS, dtype=jnp.int32)
    first_slot = jnp.sum(jnp.where(eid_tiles[..., None] == experts, pad_start, 0), axis=-1)
    return block_e, n_used, pad_end, padded, first_slot + rank_tiles


def _moe_units(x, mod_l):
    bsz, seq, _ = x.shape
    lt = min(seq, MOE_TILE)
    per = seq // lt
    x_units = x.reshape(bsz * seq // lt, lt, D_MODEL)
    mod_units = jnp.repeat(mod_l, per, axis=0) if per > 1 else mod_l
    return x_units, mod_units


def _moe_experts(x, mod_l, nrm_ffn, wr, br, tri, wg, wu, wd, layer):
    x_units, mod_units = _moe_units(x, mod_l)
    h, eid_tiles, rank_tiles, wts, counts = _router(x_units, mod_units, nrm_ffn, wr, br, tri)
    block_rows = _expert_block_rows(eid_tiles.size)
    block_e, n_used, pad_end, padded, dest_tiles = _routing_tables(
        eid_tiles, rank_tiles, counts[:, 0], block_rows)
    n_slots = block_e.shape[0] * block_rows
    xs = _dispatch(pad_end, padded, dest_tiles, h, n_slots, block_rows)
    return dest_tiles, wts.T, _experts(block_e, n_used, xs, wg, wu, wd, layer, block_rows)


def _dest_per_step(dest_tiles, tb):
    n_tiles = dest_tiles.shape[0]
    per = MOE_TILE // tb
    d = dest_tiles.reshape(n_tiles, TOPK, per, tb).transpose(0, 2, 1, 3)
    return d.reshape(n_tiles * per, 1, TOPK * tb)


def kernel(x_prompt, x_sample, c_prompt, c_sample, state_hgrn, state_gla, w_ada, b_ada, norm_mix,
           norm_ffn, w_in, hg_lb, hg_onorm, w_gk2, b_gk, gla_onorm, w_br_a, w_br_b, w_out, w_rg, b_rg,
           w_re, b_re, w_e_gate, w_e_up, w_e_down, norm_final):
    bp = x_prompt.shape[0]
    bs = x_sample.shape[0]
    mod = _ada_mod(jnp.concatenate([c_prompt, c_sample], axis=0), w_ada, b_ada)
    mod = mod.reshape(DEPTH, bp + bs, 6, D_MODEL)

    glr0 = C_GOG + GLA_W
    win_r = jnp.concatenate(
        [w_in[:, :, :glr0], w_in[:, :, glr0 + GLA_GATE_RANK:], w_in[:, :, glr0:glr0 + GLA_GATE_RANK],
         jnp.zeros((DEPTH, D_MODEL, LANES - GLA_GATE_RANK), F32)], axis=2).astype(BF16)
    wgk2_p = jnp.concatenate(
        [w_gk2, jnp.zeros((DEPTH, LANES - GLA_GATE_RANK, GLA_KW), F32)], axis=1).astype(BF16)
    wa_b = w_br_a.astype(BF16)
    wb_b = w_br_b.astype(BF16)
    wo_b = w_out.astype(BF16)

    def plan_consts(rows):
        r = np.arange(rows)
        chunk_tril = (r[:, None] // CHUNK == r[None, :] // CHUNK) & (r[None, :] <= r[:, None])
        return [jnp.asarray(chunk_tril, BF16), jnp.asarray(_slab_mask(), F32),
                jnp.asarray(SAFE_PLAN.segment_sum_matrix(), BF16), jnp.asarray(SAFE_PLAN.masks(), F32)]

    zpad = jnp.zeros((DEPTH, 8 - N_GROUPS, D_MODEL), F32)
    ztail = jnp.zeros((DEPTH, ROUTER_ROWS - 8 - N_EXPERTS, D_MODEL), F32)
    wr = jnp.concatenate([jnp.swapaxes(w_rg, 1, 2), zpad, jnp.swapaxes(w_re, 1, 2), ztail], axis=1)
    br = jnp.concatenate([b_rg, jnp.zeros((DEPTH, 8 - N_GROUPS), F32), b_re,
                          jnp.zeros((DEPTH, ROUTER_ROWS - 8 - N_EXPERTS), F32)], axis=1)[:, :, None]
    wg_b = w_e_gate.astype(BF16)
    wu_b = w_e_up.astype(BF16)
    wd_b = w_e_down.astype(BF16)
    nrm_f = norm_final.reshape(1, D_MODEL)
    assign = np.arange(TILE_ASSIGN)
    tri = jnp.asarray(assign[:, None] < assign[None, :], BF16)

    def run(x, mod_g, shg, sgla, tb, nb):
        bsz = x.shape[0]
        new_hg, new_gla = [], []
        pending = None
        for l in range(DEPTH):
            x, s1, s2 = _mixer(
                x, mod_g[l], norm_mix[l:l + 1], win_r[l], hg_lb, wgk2_p[l], b_gk[l:l + 1],
                hg_onorm[l:l + 1], gla_onorm[l:l + 1], wa_b[l], wb_b[l], wo_b[l],
                shg[l], sgla[l].reshape(bsz, GLA_KW // LANES, LANES, HEAD_DV), plan_consts(nb * tb),
                layer=l, tb=tb, nb=nb, pending_moe=pending)
            new_hg.append(s1)
            new_gla.append(s2.reshape(bsz, GLA_HEADS, GLA_DK, HEAD_DV))
            dest_tiles, wts_col, y_slots = _moe_experts(
                x, mod_g[l], norm_ffn[l:l + 1], wr[l], br[l], tri, wg_b, wu_b, wd_b, l)
            pending = (_dest_per_step(dest_tiles, nb * tb), wts_col, mod_g[l], y_slots)
        x_units, mod_units = _moe_units(x, mod_g[DEPTH - 1])
        y = _combine(dest_tiles, x_units, mod_units, wts_col, nrm_f, y_slots, final_norm=True)
        return y.reshape(x.shape), jnp.stack(new_hg), jnp.stack(new_gla)

    zeros_hg = jnp.zeros((DEPTH, bp, HG_HEADS, HG_DK, HEAD_DV), F32)
    zeros_gla = jnp.zeros((DEPTH, bp, GLA_HEADS, GLA_DK, HEAD_DV), F32)
    y_p, hg_p, gla_p = run(x_prompt, mod[:, :bp], zeros_hg, zeros_gla, 256, 1)
    y_s, hg_s, gla_s = run(x_sample, mod[:, bp:], state_hgrn, state_gla, CHUNK, 4)
    return (y_p, y_s, hg_p, gla_p, hg_s, gla_s)
```

```python
import functools

import numpy as np
import jax
import jax.numpy as jnp
from jax import lax
from jax.experimental import pallas as pl
from jax.experimental.pallas import tpu as pltpu

F32 = jnp.float32
BF16 = jnp.bfloat16

D_MODEL = 1024
DEPTH = 2
CHUNK = 64
NORM_EPS = 1e-6
LOG_FLOOR = 1e-30
HG_HEADS = 4
HG_DK = 128
HEAD_DV = 128
HG_KW = HG_HEADS * HG_DK
HG_W = HG_HEADS * HEAD_DV
GLA_HEADS = 4
GLA_DK = 64
GLA_KW = GLA_HEADS * GLA_DK
GLA_W = GLA_HEADS * HEAD_DV
GLA_GATE_RANK = 16
GLA_GATE_NORM = 16.0
N_GROUPS = 4
EXPERTS_PER_GROUP = 8
N_EXPERTS = N_GROUPS * EXPERTS_PER_GROUP
TOPK = 2
D_EXPERT = 512

LANES = 128
VMEM_LIMIT = 56 * 1024 * 1024

C_HQ = 0
C_HF = C_HQ + HG_KW
C_HI = C_HF + HG_KW
C_HOG = C_HI + HG_W
C_GQ = C_HOG + HG_W
C_GK = C_GQ + GLA_KW
C_GV = C_GK + GLA_KW
C_GOG = C_GV + GLA_W
C_GA = C_GOG + GLA_W
C_GB = C_GA + D_MODEL
C_GLR = C_GB + D_MODEL
IN_COLS_PAD = C_GLR + LANES
MXU_WIDTH = 256
PROJ_TILE = 4 * MXU_WIDTH


class _ScorePlan:
    def __init__(self, levels):
        self.levels = levels
        self.cum_rows = (2 * len(levels) + 1) * CHUNK
        self.n_masks = len(levels) + 2

    def segment_sum_matrix(self):
        t = np.arange(CHUNK)[:, None]
        r = np.arange(CHUNK)[None, :]
        rows = []
        for m in self.levels:
            same = (t // m) == (r // m)
            rows.append(same & (r <= t))
            rows.append(same & (r > t))
        rows.append(r <= t)
        return np.concatenate(rows, axis=0).astype(np.float32)

    def masks(self):
        t = np.arange(CHUNK)[:, None]
        s = np.arange(CHUNK)[None, :]
        masks = [t == s]
        for m in self.levels + (1,):
            masks.append(((t // (2 * m)) == (s // (2 * m))) & ((t // m) % 2 == 1) & ((s // m) % 2 == 0))
        return np.stack(masks).astype(np.float32)


SAFE_PLAN = _ScorePlan((32, 16, 8, 4, 2))


def _dot(a, b):
    return jnp.dot(a, b, preferred_element_type=F32)


def _dot_nt(a, b):
    return lax.dot_general(a, b, (((1,), (1,)), ((), ())), preferred_element_type=F32)


def _sigmoid(x):
    return 1.0 / (1.0 + jnp.exp(-x))


def _silu(x):
    return x * _sigmoid(x)


def _rms_mod(x, gain, scale, shift):
    y = x * lax.rsqrt(jnp.mean(x * x, axis=-1, keepdims=True) + NORM_EPS)
    return y * gain * (1.0 + scale) + shift


ADA_COL_TILE = 512


def _ada_kernel(c_ref, w_ref, b_ref, o_ref):
    c = c_ref[...]
    o_ref[0] = jnp.dot(_silu(c), w_ref[0], preferred_element_type=F32,
                       precision=lax.Precision.HIGHEST) + b_ref[0]


def _ada_mod(c_all, w_ada, b_ada):
    nb = c_all.shape[0]
    tn = ADA_COL_TILE
    return pl.pallas_call(
        _ada_kernel,
        grid=(DEPTH, 6 * D_MODEL // tn),
        in_specs=[
            pl.BlockSpec((nb, D_MODEL), lambda l, j: (0, 0)),
            pl.BlockSpec((1, D_MODEL, tn), lambda l, j: (l, 0, j)),
            pl.BlockSpec((1, 1, tn), lambda l, j: (l, 0, j)),
        ],
        out_specs=pl.BlockSpec((1, nb, tn), lambda l, j: (l, 0, j)),
        out_shape=jax.ShapeDtypeStruct((DEPTH, nb, 6 * D_MODEL), F32),
        name="ada_mod",
    )(c_all, w_ada, b_ada.reshape(DEPTH, 1, 6 * D_MODEL))


def _chunk_attention(q, k, v, g, states, mall_ref, mask_ref, heads_per_tile, plan):
    w = q.shape[1]
    n_tiles = w // LANES
    g_hi = g.astype(BF16)
    r1 = g - g_hi.astype(F32)
    g_mid = r1.astype(BF16)
    g_lo = (r1 - g_mid.astype(F32)).astype(BF16)
    mall = mall_ref[...]
    cums = _dot(mall, g_hi) + _dot(mall, g_mid) + _dot(mall, g_lo)
    b = cums[plan.cum_rows - CHUNK:plan.cum_rows]
    level_q = []
    level_k = []
    for i in range(len(plan.levels)):
        level_q.append(q * jnp.exp(cums[2 * i * CHUNK:(2 * i + 1) * CHUNK]))
        level_k.append(k * jnp.exp(cums[(2 * i + 1) * CHUNK:(2 * i + 2) * CHUNK]))
    qs = [q] + level_q + [q * jnp.exp(g)]
    ks = [k] + level_k + [k]
    b_last = b[CHUNK - 1:CHUNK]
    q_in = q * jnp.exp(b)
    k_out = k * jnp.exp(b_last - b)
    e_last = jnp.exp(b_last)

    dk = LANES // heads_per_tile
    lane = lax.broadcasted_iota(jnp.int32, (CHUNK, LANES), 1)
    row = lax.broadcasted_iota(jnp.int32, (LANES, HEAD_DV), 0)
    outs = []
    new_states = []
    for ti in range(n_tiles):
        sl = slice(ti * LANES, (ti + 1) * LANES)
        ks_t = [kk[:, sl].astype(BF16) for kk in ks]
        k_out_t = k_out[:, sl].T.astype(BF16)
        e_col = jnp.broadcast_to(e_last[:, sl], (LANES, LANES)).T
        s_old = states[ti]
        s_old_b = s_old.astype(BF16)
        upd = None
        for j in range(heads_per_tile):
            head = ti * heads_per_tile + j
            if heads_per_tile == 1:
                sel = lambda a: a
            else:
                in_head = (lane // dk) == j
                sel = lambda a, in_head=in_head: jnp.where(in_head, a, 0.0)
            sc = jnp.zeros((CHUNK, CHUNK), F32)
            for i in range(plan.n_masks):
                sc = sc + _dot_nt(sel(qs[i][:, sl]).astype(BF16), ks_t[i]) * mask_ref[i]
            vh = v[:, head * HEAD_DV:(head + 1) * HEAD_DV].astype(BF16)
            o = _dot(sc.astype(BF16), vh) + _dot(sel(q_in[:, sl]).astype(BF16), s_old_b)
            outs.append(o)
            u = _dot(k_out_t, vh)
            upd = u if upd is None else jnp.where((row // dk) == j, u, upd)
        new_states.append(e_col * s_old + upd)
    return jnp.concatenate(outs, axis=1), new_states


FAST_BLOCK = 16
N_SUB = CHUNK // FAST_BLOCK
SLAB_ROWS = FAST_BLOCK * (N_SUB * (N_SUB - 1) // 2) + CHUNK
FAST_BLOCK_DECAY_LIMIT = 60.0


def _slab_mask():
    t = np.arange(CHUNK)[:, None]
    cols = []
    for i in range(1, N_SUB):
        cols.append(np.broadcast_to(t // FAST_BLOCK == i, (CHUNK, i * FAST_BLOCK)))
    s = np.arange(CHUNK)[None, :]
    cols.append((t // FAST_BLOCK == s // FAST_BLOCK) & (s <= t))
    return np.concatenate(cols, axis=1).astype(np.float32)


def _block_attention_fast(q, k, v, g, states, tril_ref, slab_mask_ref, heads_per_tile, carry=True):
    rows, w = q.shape
    n_chunks = rows // CHUNK
    n_tiles = w // LANES
    dk = LANES // heads_per_tile
    n_heads = n_tiles * heads_per_tile

    g_hi = g.astype(BF16)
    r1 = g - g_hi.astype(F32)
    g_mid = r1.astype(BF16)
    g_lo = (r1 - g_mid.astype(F32)).astype(BF16)
    tril = tril_ref[...]
    b = _dot(tril, g_hi) + _dot(tril, g_mid) + _dot(tril, g_lo)

    def end_row(c, i):
        r = c * CHUNK + (i + 1) * FAST_BLOCK
        return b[r - 1:r]

    def per_block(row_of):
        return jnp.concatenate([jnp.broadcast_to(row_of(c, i), (FAST_BLOCK, w))
                                for c in range(n_chunks) for i in range(N_SUB)], axis=0)

    zero = jnp.zeros((1, w), F32)
    b_start = per_block(lambda c, i: zero if i == 0 else end_row(c, i - 1))
    b_end = per_block(end_row)
    q_blk = q * jnp.exp(b - b_start)
    k_diag = k * jnp.exp(b_start - b)
    k_end = k * jnp.exp(b_end - b)
    q_in = q_blk * jnp.exp(b_start)
    k_out = k_end * jnp.exp(per_block(lambda c, i: end_row(c, N_SUB - 1)) - b_end)

    lane = lax.broadcasted_iota(jnp.int32, (CHUNK, LANES), 1)
    row = lax.broadcasted_iota(jnp.int32, (LANES, HEAD_DV), 0)
    slab_mask = slab_mask_ref[...]

    def sel(a, j):
        return a if heads_per_tile == 1 else jnp.where((lane // dk) == j, a, 0.0)

    v_b = v.astype(BF16)

    scores = {}
    for c in range(n_chunks):
        r0 = c * CHUNK
        slabs = []
        for i in range(1, N_SUB):
            for jb in range(i):
                blk = k_end[r0 + jb * FAST_BLOCK:r0 + (jb + 1) * FAST_BLOCK]
                slabs.append(blk if jb == i - 1 else blk * jnp.exp(end_row(c, i - 1) - end_row(c, jb)))
        slabs.append(k_diag[r0:r0 + CHUNK])
        k_slab = jnp.concatenate(slabs, axis=0).astype(BF16)
        for ti in range(n_tiles):
            sl = slice(ti * LANES, (ti + 1) * LANES)
            for j in range(heads_per_tile):
                qh = sel(q_blk[r0:r0 + CHUNK, sl], j).astype(BF16)
                scores[c, ti * heads_per_tile + j] = (_dot_nt(qh, k_slab[:, sl]) * slab_mask).astype(BF16)

    entering = [list(states)] if carry else [list(st) for st in states]
    leaving = []
    for c in range(n_chunks):
        r0 = c * CHUNK
        nxt = []
        for ti in range(n_tiles):
            sl = slice(ti * LANES, (ti + 1) * LANES)
            k_out_t = k_out[r0:r0 + CHUNK, sl].T.astype(BF16)
            upd = None
            for j in range(heads_per_tile):
                head = ti * heads_per_tile + j
                u = _dot(k_out_t, v_b[r0:r0 + CHUNK, head * HEAD_DV:(head + 1) * HEAD_DV])
                upd = u if upd is None else jnp.where((row // dk) == j, u, upd)
            e_col = jnp.broadcast_to(jnp.exp(end_row(c, N_SUB - 1)[:, sl]), (LANES, LANES)).T
            nxt.append(e_col * entering[c][ti] + upd)
        leaving.append(nxt)
        if carry:
            entering.append(nxt)

    out_rows = []
    for c in range(n_chunks):
        r0 = c * CHUNK
        outs = []
        for head in range(n_heads):
            ti, j = divmod(head, heads_per_tile)
            sl = slice(ti * LANES, (ti + 1) * LANES)
            vh = v_b[r0:r0 + CHUNK, head * HEAD_DV:(head + 1) * HEAD_DV]
            v_slab = jnp.concatenate([vh[:i * FAST_BLOCK] for i in range(1, N_SUB)] + [vh], axis=0)
            outs.append(_dot(scores[c, head], v_slab)
                        + _dot(sel(q_in[r0:r0 + CHUNK, sl], j).astype(BF16), entering[c][ti].astype(BF16)))
        out_rows.append(jnp.concatenate(outs, axis=1))
    return jnp.concatenate(out_rows, axis=0), (leaving[-1] if carry else leaving)


def _head_norm_gate(o, gain, gate):
    outs = []
    for h in range(o.shape[1] // HEAD_DV):
        sl = slice(h * HEAD_DV, (h + 1) * HEAD_DV)
        oh = o[:, sl]
        oh = oh * lax.rsqrt(jnp.mean(oh * oh, axis=-1, keepdims=True) + NORM_EPS) * gain
        outs.append(oh * _silu(gate[:, sl]))
    return jnp.concatenate(outs, axis=1)


def _mixer_kernel(x_ref, mod_ref, nrm_ref, win_ref, lb_ref, wgk2_ref, bgk_ref, hgn_ref, glan_ref,
                  wa_ref, wb_ref, wo_ref, shg0_ref, sgla0_ref,
                  tril_ref, slab_mask_ref, mall_safe_ref, mask_safe_ref, *rest, layer, tb, pending_moe):
    if pending_moe:
        dst_cur_ref, dst_nxt_ref, wts_ref, modp_ref, y_hbm = rest[:5]
        rest = rest[5:]
    xo_ref, shg_o_ref, sgla_o_ref, p_scr, k_scr, lg_scr, shg_scr, sgla_scr = rest[:8]
    j = pl.program_id(1)
    nb = x_ref.shape[0]
    rows_all = nb * tb

    @pl.when(j == 0)
    def _():
        shg_scr[...] = shg0_ref[...].reshape(shg_scr.shape)
        sgla_scr[...] = sgla0_ref[...].reshape(sgla_scr.shape)

    def per_row(ref, i):
        if nb == 1:
            return ref[0, i:i + 1, :]
        return jnp.broadcast_to(ref[:, i:i + 1, :], (nb, tb, D_MODEL)).reshape(rows_all, D_MODEL)

    x = x_ref[...].reshape(rows_all, D_MODEL)
    n_prefetch = 0
    if pending_moe:
        cbuf, csem = rest[8:]
        n_prefetch = TOPK * rows_all
        step = pl.program_id(0) * pl.num_programs(1) + j
        last = pl.num_programs(0) * pl.num_programs(1) - 1
        slot = step % 2

        @pl.when(step == 0)
        def _():
            _start_row_gather(dst_cur_ref, n_prefetch, y_hbm, cbuf.at[0], csem.at[0])

        _wait_row_gather(n_prefetch, y_hbm, cbuf.at[slot], csem.at[slot])
        x = x + per_row(modp_ref, 5) * (wts_ref[:, 0:1] * cbuf[slot, 0:rows_all, :]
                                        + wts_ref[:, 1:2] * cbuf[slot, rows_all:n_prefetch, :])

    def prefetch_rows(r0, r1):
        for r in range(r0, r1):
            row = dst_nxt_ref[0, 0, r]
            pltpu.make_async_copy(y_hbm.at[pl.ds(row, 1)], cbuf.at[1 - slot, pl.ds(r, 1)],
                                  csem.at[1 - slot]).start()

    sh1 = per_row(mod_ref, 0)
    sc1 = per_row(mod_ref, 1)
    g1 = per_row(mod_ref, 2)
    hb = _rms_mod(x, nrm_ref[...], sc1, sh1).astype(BF16)
    col_tiles = list(range(0, IN_COLS_PAD, PROJ_TILE))
    for i, c in enumerate(col_tiles):
        c1 = min(c + PROJ_TILE, IN_COLS_PAD)
        p_scr[:, c:c1] = _dot(hb, win_ref[:, c:c1])
        if pending_moe:
            prefetch_rows(n_prefetch * i // len(col_tiles), n_prefetch * (i + 1) // len(col_tiles))

    lb_all = lb_ref[...]
    lb_max = jnp.max(lb_all, axis=0, keepdims=True)
    lb_exp = jnp.exp(lb_all - lb_max)
    sm = lb_exp / jnp.sum(lb_exp, axis=0, keepdims=True)
    lbl = jnp.clip(jnp.sum(sm[0:layer + 1], axis=0, keepdims=True) - sm[0:1], 0.0, 1.0)

    p_scr[:, C_HQ:C_HQ + HG_KW] = _silu(p_scr[:, C_HQ:C_HQ + HG_KW]) * (HG_DK ** -0.5)
    z = p_scr[:, C_HF:C_HF + HG_KW]
    f = lbl + (1.0 - lbl) * _sigmoid(z)
    p_scr[:, C_HF:C_HF + HG_KW] = jnp.log(jnp.maximum(f, LOG_FLOOR))
    k_scr[...] = (1.0 - lbl) * _sigmoid(-z)
    glr = p_scr[:, C_GLR:C_GLR + LANES].astype(BF16)
    gate = _dot(glr, wgk2_ref[...]) + bgk_ref[...]
    lg_scr[...] = (jnp.minimum(gate, 0.0) - jnp.log1p(jnp.exp(-jnp.abs(gate)))) * (1.0 / GLA_GATE_NORM)
    p_scr[:, C_GQ:C_GQ + GLA_KW] = p_scr[:, C_GQ:C_GQ + GLA_KW] * (GLA_DK ** -0.5)

    n_hg_tiles = HG_KW // LANES
    n_gla_tiles = GLA_KW // LANES

    def one_chunk(rows, states, attend):
        o_hg, st_hg = attend(p_scr[rows, C_HQ:C_HQ + HG_KW], k_scr[rows, :],
                             p_scr[rows, C_HI:C_HI + HG_W], p_scr[rows, C_HF:C_HF + HG_KW],
                             states[:n_hg_tiles], 1)
        p_scr[rows, C_HI:C_HI + HG_W] = o_hg
        o_gla, st_gla = attend(p_scr[rows, C_GQ:C_GQ + GLA_KW], p_scr[rows, C_GK:C_GK + GLA_KW],
                               p_scr[rows, C_GV:C_GV + GLA_W], lg_scr[rows, :],
                               states[n_hg_tiles:], 2)
        p_scr[rows, C_GV:C_GV + GLA_W] = o_gla
        return st_hg + st_gla

    def attend_safe(q, k, v, g, states, heads_per_tile):
        return _chunk_attention(q, k, v, g, states, mall_safe_ref, mask_safe_ref, heads_per_tile, SAFE_PLAN)

    def load_states(seq=0):
        return ([shg_scr[seq * n_hg_tiles + t] for t in range(n_hg_tiles)]
                + [sgla_scr[seq * n_gla_tiles + t] for t in range(n_gla_tiles)])

    def store_states(states, seq=0):
        for t in range(n_hg_tiles):
            shg_scr[seq * n_hg_tiles + t] = states[t]
        for t in range(n_gla_tiles):
            sgla_scr[seq * n_gla_tiles + t] = states[n_hg_tiles + t]

    def run_block_fast():
        carry = nb == 1
        seqs = [load_states(s) for s in range(nb)]
        hg_in = seqs[0][:n_hg_tiles] if carry else [st[:n_hg_tiles] for st in seqs]
        gla_in = seqs[0][n_hg_tiles:] if carry else [st[n_hg_tiles:] for st in seqs]
        o_hg, st_hg = _block_attention_fast(
            p_scr[:, C_HQ:C_HQ + HG_KW], k_scr[...], p_scr[:, C_HI:C_HI + HG_W],
            p_scr[:, C_HF:C_HF + HG_KW], hg_in, tril_ref, slab_mask_ref, 1, carry)
        p_scr[:, C_HI:C_HI + HG_W] = o_hg
        o_gla, st_gla = _block_attention_fast(
            p_scr[:, C_GQ:C_GQ + GLA_KW], p_scr[:, C_GK:C_GK + GLA_KW], p_scr[:, C_GV:C_GV + GLA_W],
            lg_scr[...], gla_in, tril_ref, slab_mask_ref, 2, carry)
        p_scr[:, C_GV:C_GV + GLA_W] = o_gla
        if carry:
            store_states(st_hg + st_gla)
        else:
            for s in range(nb):
                store_states(st_hg[s] + st_gla[s], s)

    def run_chunks_safe():
        def chunk_body(ci, carry):
            rows = pl.ds(pl.multiple_of(ci * CHUNK, CHUNK), CHUNK)
            seq = 0 if nb == 1 else ci
            store_states(one_chunk(rows, load_states(seq), attend_safe), seq)
            return carry

        lax.fori_loop(0, rows_all // CHUNK, chunk_body, 0)

    blk = FAST_BLOCK
    min_hg = jnp.min(jnp.sum(p_scr[:, C_HF:C_HF + HG_KW].reshape(rows_all // blk, blk, HG_KW), axis=1))
    min_gla = jnp.min(jnp.sum(lg_scr[...].reshape(rows_all // blk, blk, GLA_KW), axis=1))
    bounded = jnp.minimum(min_hg, min_gla) >= -FAST_BLOCK_DECAY_LIMIT

    @pl.when(bounded)
    def _():
        run_block_fast()

    @pl.when(jnp.logical_not(bounded))
    def _():
        run_chunks_safe()

    o_hg = _head_norm_gate(p_scr[:, C_HI:C_HI + HG_W], hgn_ref[...], p_scr[:, C_HOG:C_HOG + HG_W])
    o_gla = _head_norm_gate(p_scr[:, C_GV:C_GV + GLA_W], glan_ref[...], p_scr[:, C_GOG:C_GOG + GLA_W])
    ya = _dot(o_hg.astype(BF16), wa_ref[...])
    yb = _dot(o_gla.astype(BF16), wb_ref[...])
    merged = (_sigmoid(p_scr[:, C_GA:C_GA + D_MODEL]) * ya
              + _sigmoid(p_scr[:, C_GB:C_GB + D_MODEL]) * yb)
    m = _dot(merged.astype(BF16), wo_ref[...])
    xo_ref[...] = (x + g1 * m).reshape(nb, tb, D_MODEL)

    @pl.when(j == pl.num_programs(1) - 1)
    def _():
        shg_o_ref[...] = shg_scr[...].reshape(shg_o_ref.shape)
        sgla_o_ref[...] = sgla_scr[...].reshape(sgla_o_ref.shape)

    if pending_moe:
        @pl.when(step == last)
        def _():
            _wait_row_gather(n_prefetch, y_hbm, cbuf.at[1 - slot], csem.at[1 - slot])


def _const_spec(shape):
    nd = len(shape)
    return pl.BlockSpec(shape, lambda b, j, nd=nd: (0,) * nd, pipeline_mode=pl.Buffered(1))


def _mixer(x, mod, nrm, win, hg_lb, wgk2, bgk, hgn, glan, wa, wb, wo, shg0, sgla0, plan_consts,
           *, layer, tb, nb=1, pending_moe=None):
    bsz, seq, _ = x.shape
    assert nb == 1 or seq == tb == CHUNK
    nj = seq // tb
    rows = nb * tb
    kern = functools.partial(_mixer_kernel, layer=layer, tb=tb, pending_moe=pending_moe is not None)
    n_gla_tiles = GLA_KW // LANES
    extra_specs, extra_args, extra_scratch = [], [], []
    if pending_moe is not None:
        dest_steps, wts_col, mod_prev, y_slots = pending_moe
        n_steps = bsz // nb * nj
        extra_specs = [
            pl.BlockSpec((1, 1, TOPK * rows), lambda b, j: (b * nj + j, 0, 0), memory_space=pltpu.SMEM),
            pl.BlockSpec((1, 1, TOPK * rows), lambda b, j: (jnp.minimum(b * nj + j + 1, n_steps - 1), 0, 0),
                         memory_space=pltpu.SMEM),
            pl.BlockSpec((rows, TOPK), lambda b, j: (b * nj + j, 0)),
            pl.BlockSpec((nb, 6, D_MODEL), lambda b, j: (b, 0, 0)),
            pl.BlockSpec(memory_space=pl.ANY),
        ]
        extra_args = [dest_steps, dest_steps, wts_col, mod_prev, y_slots]
        extra_scratch = [pltpu.VMEM((2, TOPK * rows, D_MODEL), F32), pltpu.SemaphoreType.DMA((2,))]
    return pl.pallas_call(
        kern,
        grid=(bsz // nb, nj),
        in_specs=[
            pl.BlockSpec((nb, tb, D_MODEL), lambda b, j: (b, j, 0)),
            pl.BlockSpec((nb, 6, D_MODEL), lambda b, j: (b, 0, 0)),
            _const_spec((1, D_MODEL)),
            _const_spec((D_MODEL, IN_COLS_PAD)),
            _const_spec((DEPTH, HG_KW)),
            _const_spec((LANES, GLA_KW)),
            _const_spec((1, GLA_KW)),
            _const_spec((1, HEAD_DV)),
            _const_spec((1, HEAD_DV)),
            _const_spec((HG_W, D_MODEL)),
            _const_spec((GLA_W, D_MODEL)),
            _const_spec((D_MODEL, D_MODEL)),
            pl.BlockSpec((nb, HG_HEADS, HG_DK, HEAD_DV), lambda b, j: (b, 0, 0, 0)),
            pl.BlockSpec((nb, n_gla_tiles, LANES, HEAD_DV), lambda b, j: (b, 0, 0, 0)),
            _const_spec((rows, rows)),
            _const_spec((CHUNK, SLAB_ROWS)),
            _const_spec((SAFE_PLAN.cum_rows, CHUNK)),
            _const_spec((SAFE_PLAN.n_masks, CHUNK, CHUNK)),
        ] + extra_specs,
        out_specs=[
            pl.BlockSpec((nb, tb, D_MODEL), lambda b, j: (b, j, 0)),
            pl.BlockSpec((nb, HG_HEADS, HG_DK, HEAD_DV), lambda b, j: (b, 0, 0, 0)),
            pl.BlockSpec((nb, n_gla_tiles, LANES, HEAD_DV), lambda b, j: (b, 0, 0, 0)),
        ],
        out_shape=[
            jax.ShapeDtypeStruct((bsz, seq, D_MODEL), F32),
            jax.ShapeDtypeStruct((bsz, HG_HEADS, HG_DK, HEAD_DV), F32),
            jax.ShapeDtypeStruct((bsz, n_gla_tiles, LANES, HEAD_DV), F32),
        ],
        scratch_shapes=[
            pltpu.VMEM((rows, IN_COLS_PAD), F32),
            pltpu.VMEM((rows, HG_KW), F32),
            pltpu.VMEM((rows, GLA_KW), F32),
            pltpu.VMEM((nb * HG_HEADS, HG_DK, HEAD_DV), F32),
            pltpu.VMEM((nb * n_gla_tiles, LANES, HEAD_DV), F32),
        ] + extra_scratch,
        compiler_params=pltpu.CompilerParams(
            dimension_semantics=("arbitrary", "arbitrary"), vmem_limit_bytes=VMEM_LIMIT),
        name=f"mixer_l{layer}",
    )(x, mod, nrm, win, hg_lb, wgk2, bgk, hgn, glan, wa, wb, wo, shg0, sgla0, *plan_consts, *extra_args)


SUBLANES = 8
EXPERT_ROW0 = SUBLANES
ROUTER_ROWS = 48
MOE_TILE = 512
TILE_ASSIGN = TOPK * MOE_TILE
MAX_EXPERT_BLOCK = 512


def _expert_block_rows(n_assign):
    return max(LANES, min(MAX_EXPERT_BLOCK, n_assign // N_EXPERTS // 2))


def _first_argmax_rows(vals, n):
    ridx = lax.broadcasted_iota(jnp.int32, vals.shape, 0)
    vmax = jnp.max(vals, axis=0, keepdims=True)
    imax = jnp.min(jnp.where(vals == vmax, ridx, n), axis=0, keepdims=True)
    return vmax, imax


def _router_kernel(x_ref, mod_ref, nrm_ref, wr_ref, br_ref, tri_ref,
                   h_ref, eid_ref, rank_ref, wts_ref, cnt_ref, run_scr):
    @pl.when(pl.program_id(0) == 0)
    def _():
        run_scr[...] = jnp.zeros_like(run_scr)

    u, lt, _ = x_ref.shape
    x = x_ref[...]
    sh2 = mod_ref[:, 3:4, :]
    sc2 = mod_ref[:, 4:5, :]
    h = _rms_mod(x, nrm_ref[...].reshape(1, 1, D_MODEL), sc2, sh2).reshape(u * lt, D_MODEL)
    h_ref[...] = h
    h_hi = h.astype(BF16)
    h_lo = (h - h_hi.astype(F32)).astype(BF16)
    w = wr_ref[...]
    w_hi = w.astype(BF16)
    w_lo = (w - w_hi.astype(F32)).astype(BF16)
    p_hi = _dot_nt(jnp.concatenate([w_hi, w_lo], axis=0), h_hi)
    logits = p_hi[:ROUTER_ROWS] + p_hi[ROUTER_ROWS:] + _dot_nt(w_hi, h_lo) + br_ref[...]
    gl = logits[0:N_GROUPS]
    gmax, gi = _first_argmax_rows(gl, N_GROUPS)
    gp = 1.0 / jnp.sum(jnp.exp(gl - gmax), axis=0, keepdims=True)
    le = logits[EXPERT_ROW0:EXPERT_ROW0 + EXPERTS_PER_GROUP]
    for g in range(1, N_GROUPS):
        r0 = EXPERT_ROW0 + g * EXPERTS_PER_GROUP
        le = jnp.where(gi == g, logits[r0:r0 + EXPERTS_PER_GROUP], le)
    pe = jnp.exp(le - jnp.max(le, axis=0, keepdims=True))
    pe = pe / jnp.sum(pe, axis=0, keepdims=True)
    v1, i1 = _first_argmax_rows(pe, EXPERTS_PER_GROUP)
    ridx = lax.broadcasted_iota(jnp.int32, pe.shape, 0)
    v2, i2 = _first_argmax_rows(jnp.where(ridx == i1, -1.0, pe), EXPERTS_PER_GROUP)
    vsum = v1 + v2
    wts_ref[0:1, :] = gp * v1 / vsum
    wts_ref[1:2, :] = gp * v2 / vsum
    eflat = jnp.concatenate([gi * EXPERTS_PER_GROUP + i1, gi * EXPERTS_PER_GROUP + i2], axis=1)
    eid_ref[0] = eflat
    onehot = (eflat == lax.broadcasted_iota(jnp.int32, (N_EXPERTS, TILE_ASSIGN), 0)).astype(F32)
    before = _dot(onehot.astype(BF16), tri_ref[...]) + run_scr[...]
    rank_ref[0] = jnp.sum(onehot * before, axis=0, keepdims=True).astype(jnp.int32)
    run_scr[...] = run_scr[...] + jnp.sum(onehot, axis=1, keepdims=True)
    cnt_ref[...] = run_scr[...].astype(jnp.int32)


def _router(x_units, mod_units, nrm, wr, br, tri):
    n_units, lt, _ = x_units.shape
    u = MOE_TILE // lt
    n_tiles = n_units // u
    return pl.pallas_call(
        _router_kernel,
        grid=(n_tiles,),
        in_specs=[
            pl.BlockSpec((u, lt, D_MODEL), lambda i: (i, 0, 0)),
            pl.BlockSpec((u, 6, D_MODEL), lambda i: (i, 0, 0)),
            pl.BlockSpec((1, D_MODEL), lambda i: (0, 0)),
            pl.BlockSpec((ROUTER_ROWS, D_MODEL), lambda i: (0, 0)),
            pl.BlockSpec((ROUTER_ROWS, 1), lambda i: (0, 0)),
            pl.BlockSpec((TILE_ASSIGN, TILE_ASSIGN), lambda i: (0, 0)),
        ],
        out_specs=[
            pl.BlockSpec((MOE_TILE, D_MODEL), lambda i: (i, 0)),
            pl.BlockSpec((1, 1, TILE_ASSIGN), lambda i: (i, 0, 0)),
            pl.BlockSpec((1, 1, TILE_ASSIGN), lambda i: (i, 0, 0)),
            pl.BlockSpec((TOPK, MOE_TILE), lambda i: (0, i)),
            pl.BlockSpec((N_EXPERTS, 1), lambda i: (0, 0)),
        ],
        out_shape=[
            jax.ShapeDtypeStruct((n_tiles * MOE_TILE, D_MODEL), F32),
            jax.ShapeDtypeStruct((n_tiles, 1, TILE_ASSIGN), jnp.int32),
            jax.ShapeDtypeStruct((n_tiles, 1, TILE_ASSIGN), jnp.int32),
            jax.ShapeDtypeStruct((TOPK, n_tiles * MOE_TILE), F32),
            jax.ShapeDtypeStruct((N_EXPERTS, 1), jnp.int32),
        ],
        scratch_shapes=[pltpu.VMEM((N_EXPERTS, 1), F32)],
        compiler_params=pltpu.CompilerParams(dimension_semantics=("arbitrary",)),
        name="moe_router",
    )(x_units, mod_units, nrm, wr, br, tri)


def _start_row_gather(idx_ref, n_rows, src_hbm, dst, sem):
    def body(r, carry):
        row = idx_ref[0, 0, r]
        pltpu.make_async_copy(src_hbm.at[pl.ds(row, 1)], dst.at[pl.ds(r, 1)], sem).start()
        return carry
    lax.fori_loop(0, n_rows, body, 0, unroll=8)


def _wait_row_gather(n_rows, src_hbm, dst, sem):
    pltpu.make_async_copy(src_hbm.at[pl.ds(0, n_rows)], dst, sem).wait()


def _dispatch_kernel(pend_ref, padded_ref, dest_ref, h_ref, xs_hbm, zbuf, sem):
    block_rows = zbuf.shape[0]
    n_blocks = xs_hbm.shape[0] // block_rows

    def zero_block(first_row):
        return pltpu.make_async_copy(
            zbuf, xs_hbm.at[pl.ds(pl.multiple_of(first_row, block_rows), block_rows)], sem.at[0])

    @pl.when(pl.program_id(0) == 0)
    def _():
        zbuf[...] = jnp.zeros_like(zbuf)
        n_used = pend_ref[N_EXPERTS - 1] // block_rows
        for e in range(N_EXPERTS):
            @pl.when(padded_ref[e] > 0)
            def _():
                zero_block(pend_ref[e] - block_rows).start()
        lax.fori_loop(n_used, n_blocks, lambda b, c: (zero_block(b * block_rows).start(), c)[1], 0)
        for e in range(N_EXPERTS):
            @pl.when(padded_ref[e] > 0)
            def _():
                zero_block(pend_ref[e] - block_rows).wait()
        lax.fori_loop(n_used, n_blocks, lambda b, c: (zero_block(b * block_rows).wait(), c)[1], 0)

    def body(t, carry):
        for k in range(TOPK):
            slot = dest_ref[0, 0, k * MOE_TILE + t]
            pltpu.make_async_copy(h_ref.at[pl.ds(t, 1)], xs_hbm.at[pl.ds(slot, 1)], sem.at[1]).start()
        return carry
    lax.fori_loop(0, MOE_TILE, body, 0, unroll=8)
    for k in range(TOPK):
        pltpu.make_async_copy(h_ref, xs_hbm.at[pl.ds(0, MOE_TILE)], sem.at[1]).wait()


def _dispatch(pad_end, padded, dest_tiles, h, n_slots, block_rows):
    n_tiles = dest_tiles.shape[0]
    grid_spec = pltpu.PrefetchScalarGridSpec(
        num_scalar_prefetch=2,
        grid=(n_tiles,),
        in_specs=[
            pl.BlockSpec((1, 1, TILE_ASSIGN), lambda i, pe, pd: (i, 0, 0), memory_space=pltpu.SMEM),
            pl.BlockSpec((MOE_TILE, D_MODEL), lambda i, pe, pd: (i, 0)),
        ],
        out_specs=pl.BlockSpec(memory_space=pl.ANY),
        scratch_shapes=[pltpu.VMEM((block_rows, D_MODEL), F32), pltpu.SemaphoreType.DMA((2,))],
    )
    return pl.pallas_call(
        _dispatch_kernel,
        grid_spec=grid_spec,
        out_shape=jax.ShapeDtypeStruct((n_slots, D_MODEL), F32),
        compiler_params=pltpu.CompilerParams(dimension_semantics=("arbitrary",)),
        name="moe_dispatch",
    )(pad_end, padded, dest_tiles, h)


def _experts_kernel(be_ref, nused_ref, x_ref, wg_ref, wu_ref, wd_ref, o_ref):
    @pl.when(pl.program_id(0) < nused_ref[0])
    def _():
        xb = x_ref[...].astype(BF16)
        a = _silu(_dot(xb, wg_ref[0, 0])) * _dot(xb, wu_ref[0, 0])
        o_ref[...] = _dot(a.astype(BF16), wd_ref[0, 0])

    @pl.when(pl.program_id(0) >= nused_ref[0])
    def _():
        o_ref[...] = jnp.zeros_like(o_ref)


def _experts(block_e, n_used, xs, wg, wu, wd, layer, block_rows):
    n_blocks = xs.shape[0] // block_rows

    def row_block(i, be, nu):
        return (jnp.minimum(i, nu[0] - 1), 0)

    def expert_block(i, be, nu):
        return (layer, be[jnp.minimum(i, nu[0] - 1)], 0, 0)

    grid_spec = pltpu.PrefetchScalarGridSpec(
        num_scalar_prefetch=2,
        grid=(n_blocks,),
        in_specs=[
            pl.BlockSpec((block_rows, D_MODEL), row_block),
            pl.BlockSpec((1, 1, D_MODEL, D_EXPERT), expert_block),
            pl.BlockSpec((1, 1, D_MODEL, D_EXPERT), expert_block),
            pl.BlockSpec((1, 1, D_EXPERT, D_MODEL), expert_block),
        ],
        out_specs=pl.BlockSpec((block_rows, D_MODEL), lambda i, be, nu: (i, 0)),
    )
    return pl.pallas_call(
        _experts_kernel,
        grid_spec=grid_spec,
        out_shape=jax.ShapeDtypeStruct(xs.shape, F32),
        compiler_params=pltpu.CompilerParams(
            dimension_semantics=("arbitrary",), vmem_limit_bytes=VMEM_LIMIT),
        name="moe_experts",
    )(block_e, n_used, xs, wg, wu, wd)


def _combine_kernel(dst_cur_ref, dst_nxt_ref, x_ref, mod_ref, wts_ref, nrm_ref, y_hbm, o_ref, buf, sem,
                    *, final_norm):
    i = pl.program_id(0)
    n = pl.num_programs(0)
    slot = i % 2

    @pl.when(i == 0)
    def _():
        _start_row_gather(dst_cur_ref, TILE_ASSIGN, y_hbm, buf.at[0], sem.at[0])

    _wait_row_gather(TILE_ASSIGN, y_hbm, buf.at[slot], sem.at[slot])
    u, lt, _ = x_ref.shape
    n_pieces = MOE_TILE // CHUNK
    per_piece = TILE_ASSIGN // n_pieces
    for c in range(n_pieces):
        unit, r0 = divmod(c * CHUNK, lt)
        t0 = c * CHUNK
        y = (wts_ref[t0:t0 + CHUNK, 0:1] * buf[slot, t0:t0 + CHUNK, :]
             + wts_ref[t0:t0 + CHUNK, 1:2] * buf[slot, MOE_TILE + t0:MOE_TILE + t0 + CHUNK, :])
        out = x_ref[unit, r0:r0 + CHUNK, :] + mod_ref[unit, 5:6, :] * y
        if final_norm:
            out = out * lax.rsqrt(jnp.mean(out * out, axis=-1, keepdims=True) + NORM_EPS) * nrm_ref[...]
        o_ref[unit, r0:r0 + CHUNK, :] = out
        for r in range(c * per_piece, (c + 1) * per_piece):
            row = dst_nxt_ref[0, 0, r]
            pltpu.make_async_copy(y_hbm.at[pl.ds(row, 1)], buf.at[1 - slot, pl.ds(r, 1)],
                                  sem.at[1 - slot]).start()

    @pl.when(i == n - 1)
    def _():
        _wait_row_gather(TILE_ASSIGN, y_hbm, buf.at[1 - slot], sem.at[1 - slot])


def _combine(dest_tiles, x_units, mod_units, wts_col, nrm, y_slots, *, final_norm):
    n_units, lt, _ = x_units.shape
    u = MOE_TILE // lt
    n_tiles = n_units // u
    return pl.pallas_call(
        functools.partial(_combine_kernel, final_norm=final_norm),
        grid=(n_tiles,),
        in_specs=[
            pl.BlockSpec((1, 1, TILE_ASSIGN), lambda i: (i, 0, 0), memory_space=pltpu.SMEM),
            pl.BlockSpec((1, 1, TILE_ASSIGN), lambda i: (jnp.minimum(i + 1, n_tiles - 1), 0, 0),
                         memory_space=pltpu.SMEM),
            pl.BlockSpec((u, lt, D_MODEL), lambda i: (i, 0, 0)),
            pl.BlockSpec((u, 6, D_MODEL), lambda i: (i, 0, 0)),
            pl.BlockSpec((MOE_TILE, TOPK), lambda i: (i, 0)),
            pl.BlockSpec((1, D_MODEL), lambda i: (0, 0)),
            pl.BlockSpec(memory_space=pl.ANY),
        ],
        out_specs=pl.BlockSpec((u, lt, D_MODEL), lambda i: (i, 0, 0)),
        out_shape=jax.ShapeDtypeStruct(x_units.shape, F32),
        scratch_shapes=[pltpu.VMEM((2, TILE_ASSIGN, D_MODEL), F32), pltpu.SemaphoreType.DMA((2,))],
        compiler_params=pltpu.CompilerParams(
            dimension_semantics=("arbitrary",), vmem_limit_bytes=VMEM_LIMIT),
        name="moe_combine",
    )(dest_tiles, dest_tiles, x_units, mod_units, wts_col, nrm, y_slots)


def _routing_tables(eid_tiles, rank_tiles, counts, block_rows):
    n_blocks = eid_tiles.size // block_rows + N_EXPERTS
    padded = (counts + block_rows - 1) // block_rows * block_rows
    pad_end = jnp.cumsum(padded).astype(jnp.int32)
    pad_start = pad_end - padded
    block_start = jnp.arange(n_blocks, dtype=jnp.int32)[:, None] * block_rows
    block_e = jnp.minimum(jnp.sum((block_start >= pad_end[None, :]).astype(jnp.int32), axis=1),
                          N_EXPERTS - 1).astype(jnp.int32)
    n_used = pad_end[-1:] // block_rows
    experts = jnp.arange(N_EXPERTS, dtype=jnp.int32)
    first_slot = jnp.sum(jnp.where(eid_tiles[..., None] == experts, pad_start, 0), axis=-1)
    return block_e, n_used, pad_end, padded, first_slot + rank_tiles


def _moe_units(x, mod_l):
    bsz, seq, _ = x.shape
    lt = min(seq, MOE_TILE)
    per = seq // lt
    x_units = x.reshape(bsz * seq // lt, lt, D_MODEL)
    mod_units = jnp.repeat(mod_l, per, axis=0) if per > 1 else mod_l
    return x_units, mod_units


def _moe_experts(x, mod_l, nrm_ffn, wr, br, tri, wg, wu, wd, layer):
    x_units, mod_units = _moe_units(x, mod_l)
    h, eid_tiles, rank_tiles, wts, counts = _router(x_units, mod_units, nrm_ffn, wr, br, tri)
    block_rows = _expert_block_rows(eid_tiles.size)
    block_e, n_used, pad_end, padded, dest_tiles = _routing_tables(
        eid_tiles, rank_tiles, counts[:, 0], block_rows)
    n_slots = block_e.shape[0] * block_rows
    xs = _dispatch(pad_end, padded, dest_tiles, h, n_slots, block_rows)
    return dest_tiles, wts.T, _experts(block_e, n_used, xs, wg, wu, wd, layer, block_rows)


def _dest_per_step(dest_tiles, tb):
    n_tiles = dest_tiles.shape[0]
    per = MOE_TILE // tb
    d = dest_tiles.reshape(n_tiles, TOPK, per, tb).transpose(0, 2, 1, 3)
    return d.reshape(n_tiles * per, 1, TOPK * tb)


def kernel(x_prompt, x_sample, c_prompt, c_sample, state_hgrn, state_gla, w_ada, b_ada, norm_mix,
           norm_ffn, w_in, hg_lb, hg_onorm, w_gk2, b_gk, gla_onorm, w_br_a, w_br_b, w_out, w_rg, b_rg,
           w_re, b_re, w_e_gate, w_e_up, w_e_down, norm_final):
    bp = x_prompt.shape[0]
    bs = x_sample.shape[0]
    mod = _ada_mod(jnp.concatenate([c_prompt, c_sample], axis=0), w_ada, b_ada)
    mod = mod.reshape(DEPTH, bp + bs, 6, D_MODEL)

    glr0 = C_GOG + GLA_W
    win_r = jnp.concatenate(
        [w_in[:, :, :glr0], w_in[:, :, glr0 + GLA_GATE_RANK:], w_in[:, :, glr0:glr0 + GLA_GATE_RANK],
         jnp.zeros((DEPTH, D_MODEL, LANES - GLA_GATE_RANK), F32)], axis=2).astype(BF16)
    wgk2_p = jnp.concatenate(
        [w_gk2, jnp.zeros((DEPTH, LANES - GLA_GATE_RANK, GLA_KW), F32)], axis=1).astype(BF16)
    wa_b = w_br_a.astype(BF16)
    wb_b = w_br_b.astype(BF16)
    wo_b = w_out.astype(BF16)

    def plan_consts(rows):
        r = np.arange(rows)
        chunk_tril = (r[:, None] // CHUNK == r[None, :] // CHUNK) & (r[None, :] <= r[:, None])
        return [jnp.asarray(chunk_tril, BF16), jnp.asarray(_slab_mask(), F32),
                jnp.asarray(SAFE_PLAN.segment_sum_matrix(), BF16), jnp.asarray(SAFE_PLAN.masks(), F32)]

    zpad = jnp.zeros((DEPTH, EXPERT_ROW0 - N_GROUPS, D_MODEL), F32)
    ztail = jnp.zeros((DEPTH, ROUTER_ROWS - EXPERT_ROW0 - N_EXPERTS, D_MODEL), F32)
    wr = jnp.concatenate([jnp.swapaxes(w_rg, 1, 2), zpad, jnp.swapaxes(w_re, 1, 2), ztail], axis=1)
    br = jnp.concatenate([b_rg, jnp.zeros((DEPTH, EXPERT_ROW0 - N_GROUPS), F32), b_re,
                          jnp.zeros((DEPTH, ROUTER_ROWS - EXPERT_ROW0 - N_EXPERTS), F32)], axis=1)[:, :, None]
    wg_b = w_e_gate.astype(BF16)
    wu_b = w_e_up.astype(BF16)
    wd_b = w_e_down.astype(BF16)
    nrm_f = norm_final.reshape(1, D_MODEL)
    assign = np.arange(TILE_ASSIGN)
    tri = jnp.asarray(assign[:, None] < assign[None, :], BF16)

    def run(x, mod_g, shg, sgla, tb, nb):
        bsz = x.shape[0]
        new_hg, new_gla = [], []
        pending = None
        for l in range(DEPTH):
            x, s1, s2 = _mixer(
                x, mod_g[l], norm_mix[l:l + 1], win_r[l], hg_lb, wgk2_p[l], b_gk[l:l + 1],
                hg_onorm[l:l + 1], gla_onorm[l:l + 1], wa_b[l], wb_b[l], wo_b[l],
                shg[l], sgla[l].reshape(bsz, GLA_KW // LANES, LANES, HEAD_DV), plan_consts(nb * tb),
                layer=l, tb=tb, nb=nb, pending_moe=pending)
            new_hg.append(s1)
            new_gla.append(s2.reshape(bsz, GLA_HEADS, GLA_DK, HEAD_DV))
            dest_tiles, wts_col, y_slots = _moe_experts(
                x, mod_g[l], norm_ffn[l:l + 1], wr[l], br[l], tri, wg_b, wu_b, wd_b, l)
            pending = (_dest_per_step(dest_tiles, nb * tb), wts_col, mod_g[l], y_slots)
        x_units, mod_units = _moe_units(x, mod_g[DEPTH - 1])
        y = _combine(dest_tiles, x_units, mod_units, wts_col, nrm_f, y_slots, final_norm=True)
        return y.reshape(x.shape), jnp.stack(new_hg), jnp.stack(new_gla)

    zeros_hg = jnp.zeros((DEPTH, bp, HG_HEADS, HG_DK, HEAD_DV), F32)
    zeros_gla = jnp.zeros((DEPTH, bp, GLA_HEADS, GLA_DK, HEAD_DV), F32)
    y_p, hg_p, gla_p = run(x_prompt, mod[:, :bp], zeros_hg, zeros_gla, 256, 1)
    y_s, hg_s, gla_s = run(x_sample, mod[:, bp:], state_hgrn, state_gla, CHUNK, 4)
    return (y_p, y_s, hg_p, gla_p, hg_s, gla_s)
```

```python
import functools

import numpy as np
import jax
import jax.numpy as jnp
from jax import lax
from jax.experimental import pallas as pl
from jax.experimental.pallas import tpu as pltpu

F32 = jnp.float32
BF16 = jnp.bfloat16

D_MODEL = 1024
DEPTH = 2
CHUNK = 64
NORM_EPS = 1e-6
LOG_FLOOR = 1e-30
HG_HEADS = 4
HG_DK = 128
HEAD_DV = 128
HG_KW = HG_HEADS * HG_DK
HG_W = HG_HEADS * HEAD_DV
GLA_HEADS = 4
GLA_DK = 64
GLA_KW = GLA_HEADS * GLA_DK
GLA_W = GLA_HEADS * HEAD_DV
GLA_GATE_RANK = 16
GLA_GATE_NORM = 16.0
N_GROUPS = 4
EXPERTS_PER_GROUP = 8
N_EXPERTS = N_GROUPS * EXPERTS_PER_GROUP
TOPK = 2
D_EXPERT = 512

LANES = 128
VMEM_LIMIT = 56 * 1024 * 1024

C_HQ = 0
C_HF = C_HQ + HG_KW
C_HI = C_HF + HG_KW
C_HOG = C_HI + HG_W
C_GQ = C_HOG + HG_W
C_GK = C_GQ + GLA_KW
C_GV = C_GK + GLA_KW
C_GOG = C_GV + GLA_W
C_GA = C_GOG + GLA_W
C_GB = C_GA + D_MODEL
C_GLR = C_GB + D_MODEL
IN_COLS_PAD = C_GLR + LANES
MXU_WIDTH = 256
PROJ_TILE = 4 * MXU_WIDTH


class _ScorePlan:
    def __init__(self, levels):
        self.levels = levels
        self.cum_rows = (2 * len(levels) + 1) * CHUNK
        self.n_masks = len(levels) + 2

    def segment_sum_matrix(self):
        t = np.arange(CHUNK)[:, None]
        r = np.arange(CHUNK)[None, :]
        rows = []
        for m in self.levels:
            same = (t // m) == (r // m)
            rows.append(same & (r <= t))
            rows.append(same & (r > t))
        rows.append(r <= t)
        return np.concatenate(rows, axis=0).astype(np.float32)

    def masks(self):
        t = np.arange(CHUNK)[:, None]
        s = np.arange(CHUNK)[None, :]
        masks = [t == s]
        for m in self.levels + (1,):
            masks.append(((t // (2 * m)) == (s // (2 * m))) & ((t // m) % 2 == 1) & ((s // m) % 2 == 0))
        return np.stack(masks).astype(np.float32)


SAFE_PLAN = _ScorePlan((32, 16, 8, 4, 2))


def _dot(a, b):
    return jnp.dot(a, b, preferred_element_type=F32)


def _dot_nt(a, b):
    return lax.dot_general(a, b, (((1,), (1,)), ((), ())), preferred_element_type=F32)


def _sigmoid(x):
    return 1.0 / (1.0 + jnp.exp(-x))


def _silu(x):
    return x * _sigmoid(x)


def _rms_mod(x, gain, scale, shift):
    y = x * lax.rsqrt(jnp.mean(x * x, axis=-1, keepdims=True) + NORM_EPS)
    return y * gain * (1.0 + scale) + shift


ADA_COL_TILE = 512


def _ada_kernel(c_ref, w_ref, b_ref, o_ref):
    c = c_ref[...]
    o_ref[0] = jnp.dot(_silu(c), w_ref[0], preferred_element_type=F32,
                       precision=lax.Precision.HIGHEST) + b_ref[0]


def _ada_mod(c_all, w_ada, b_ada):
    nb = c_all.shape[0]
    tn = ADA_COL_TILE
    return pl.pallas_call(
        _ada_kernel,
        grid=(DEPTH, 6 * D_MODEL // tn),
        in_specs=[
            pl.BlockSpec((nb, D_MODEL), lambda l, j: (0, 0)),
            pl.BlockSpec((1, D_MODEL, tn), lambda l, j: (l, 0, j)),
            pl.BlockSpec((1, 1, tn), lambda l, j: (l, 0, j)),
        ],
        out_specs=pl.BlockSpec((1, nb, tn), lambda l, j: (l, 0, j)),
        out_shape=jax.ShapeDtypeStruct((DEPTH, nb, 6 * D_MODEL), F32),
        name="ada_mod",
    )(c_all, w_ada, b_ada.reshape(DEPTH, 1, 6 * D_MODEL))


def _chunk_attention(q, k, v, g, states, mall_ref, mask_ref, heads_per_tile, plan):
    w = q.shape[1]
    n_tiles = w // LANES
    g_hi = g.astype(BF16)
    r1 = g - g_hi.astype(F32)
    g_mid = r1.astype(BF16)
    g_lo = (r1 - g_mid.astype(F32)).astype(BF16)
    mall = mall_ref[...]
    cums = _dot(mall, g_hi) + _dot(mall, g_mid) + _dot(mall, g_lo)
    b = cums[plan.cum_rows - CHUNK:plan.cum_rows]
    level_q = []
    level_k = []
    for i in range(len(plan.levels)):
        level_q.append(q * jnp.exp(cums[2 * i * CHUNK:(2 * i + 1) * CHUNK]))
        level_k.append(k * jnp.exp(cums[(2 * i + 1) * CHUNK:(2 * i + 2) * CHUNK]))
    qs = [q] + level_q + [q * jnp.exp(g)]
    ks = [k] + level_k + [k]
    b_last = b[CHUNK - 1:CHUNK]
    q_in = q * jnp.exp(b)
    k_out = k * jnp.exp(b_last - b)
    e_last = jnp.exp(b_last)

    dk = LANES // heads_per_tile
    lane = lax.broadcasted_iota(jnp.int32, (CHUNK, LANES), 1)
    row = lax.broadcasted_iota(jnp.int32, (LANES, HEAD_DV), 0)
    outs = []
    new_states = []
    for ti in range(n_tiles):
        sl = slice(ti * LANES, (ti + 1) * LANES)
        ks_t = [kk[:, sl].astype(BF16) for kk in ks]
        k_out_t = k_out[:, sl].T.astype(BF16)
        e_col = jnp.broadcast_to(e_last[:, sl], (LANES, LANES)).T
        s_old = states[ti]
        s_old_b = s_old.astype(BF16)
        upd = None
        for j in range(heads_per_tile):
            head = ti * heads_per_tile + j
            if heads_per_tile == 1:
                sel = lambda a: a
            else:
                in_head = (lane // dk) == j
                sel = lambda a, in_head=in_head: jnp.where(in_head, a, 0.0)
            sc = jnp.zeros((CHUNK, CHUNK), F32)
            for i in range(plan.n_masks):
                sc = sc + _dot_nt(sel(qs[i][:, sl]).astype(BF16), ks_t[i]) * mask_ref[i]
            vh = v[:, head * HEAD_DV:(head + 1) * HEAD_DV].astype(BF16)
            o = _dot(sc.astype(BF16), vh) + _dot(sel(q_in[:, sl]).astype(BF16), s_old_b)
            outs.append(o)
            u = _dot(k_out_t, vh)
            upd = u if upd is None else jnp.where((row // dk) == j, u, upd)
        new_states.append(e_col * s_old + upd)
    return jnp.concatenate(outs, axis=1), new_states


FAST_BLOCK = 16
N_SUB = CHUNK // FAST_BLOCK
SLAB_ROWS = FAST_BLOCK * (N_SUB * (N_SUB - 1) // 2) + CHUNK
FAST_BLOCK_DECAY_LIMIT = 60.0


def _slab_mask():
    t = np.arange(CHUNK)[:, None]
    cols = []
    for i in range(1, N_SUB):
        cols.append(np.broadcast_to(t // FAST_BLOCK == i, (CHUNK, i * FAST_BLOCK)))
    s = np.arange(CHUNK)[None, :]
    cols.append((t // FAST_BLOCK == s // FAST_BLOCK) & (s <= t))
    return np.concatenate(cols, axis=1).astype(np.float32)


def _block_attention_fast(q, k, v, g, states, tril_ref, slab_mask_ref, heads_per_tile, carry=True):
    rows, w = q.shape
    n_chunks = rows // CHUNK
    n_tiles = w // LANES
    dk = LANES // heads_per_tile
    n_heads = n_tiles * heads_per_tile

    g_hi = g.astype(BF16)
    r1 = g - g_hi.astype(F32)
    g_mid = r1.astype(BF16)
    g_lo = (r1 - g_mid.astype(F32)).astype(BF16)
    tril = tril_ref[...]
    b = _dot(tril, g_hi) + _dot(tril, g_mid) + _dot(tril, g_lo)

    def end_row(c, i):
        r = c * CHUNK + (i + 1) * FAST_BLOCK
        return b[r - 1:r]

    def per_block(row_of):
        return jnp.concatenate([jnp.broadcast_to(row_of(c, i), (FAST_BLOCK, w))
                                for c in range(n_chunks) for i in range(N_SUB)], axis=0)

    zero = jnp.zeros((1, w), F32)
    b_start = per_block(lambda c, i: zero if i == 0 else end_row(c, i - 1))
    b_end = per_block(end_row)
    q_blk = q * jnp.exp(b - b_start)
    k_diag = k * jnp.exp(b_start - b)
    k_end = k * jnp.exp(b_end - b)
    q_in = q_blk * jnp.exp(b_start)
    k_out = k_end * jnp.exp(per_block(lambda c, i: end_row(c, N_SUB - 1)) - b_end)

    lane = lax.broadcasted_iota(jnp.int32, (CHUNK, LANES), 1)
    row = lax.broadcasted_iota(jnp.int32, (LANES, HEAD_DV), 0)
    slab_mask = slab_mask_ref[...]

    def sel(a, j):
        return a if heads_per_tile == 1 else jnp.where((lane // dk) == j, a, 0.0)

    v_b = v.astype(BF16)

    scores = {}
    for c in range(n_chunks):
        r0 = c * CHUNK
        slabs = []
        for i in range(1, N_SUB):
            for jb in range(i):
                blk = k_end[r0 + jb * FAST_BLOCK:r0 + (jb + 1) * FAST_BLOCK]
                slabs.append(blk if jb == i - 1 else blk * jnp.exp(end_row(c, i - 1) - end_row(c, jb)))
        slabs.append(k_diag[r0:r0 + CHUNK])
        k_slab = jnp.concatenate(slabs, axis=0).astype(BF16)
        for ti in range(n_tiles):
            sl = slice(ti * LANES, (ti + 1) * LANES)
            for j in range(heads_per_tile):
                qh = sel(q_blk[r0:r0 + CHUNK, sl], j).astype(BF16)
                scores[c, ti * heads_per_tile + j] = (_dot_nt(qh, k_slab[:, sl]) * slab_mask).astype(BF16)

    entering = [list(states)] if carry else [list(st) for st in states]
    leaving = []
    for c in range(n_chunks):
        r0 = c * CHUNK
        nxt = []
        for ti in range(n_tiles):
            sl = slice(ti * LANES, (ti + 1) * LANES)
            k_out_t = k_out[r0:r0 + CHUNK, sl].T.astype(BF16)
            upd = None
            for j in range(heads_per_tile):
                head = ti * heads_per_tile + j
                u = _dot(k_out_t, v_b[r0:r0 + CHUNK, head * HEAD_DV:(head + 1) * HEAD_DV])
                upd = u if upd is None else jnp.where((row // dk) == j, u, upd)
            e_col = jnp.broadcast_to(jnp.exp(end_row(c, N_SUB - 1)[:, sl]), (LANES, LANES)).T
            nxt.append(e_col * entering[c][ti] + upd)
        leaving.append(nxt)
        if carry:
            entering.append(nxt)

    out_rows = []
    for c in range(n_chunks):
        r0 = c * CHUNK
        outs = []
        for head in range(n_heads):
            ti, j = divmod(head, heads_per_tile)
            sl = slice(ti * LANES, (ti + 1) * LANES)
            vh = v_b[r0:r0 + CHUNK, head * HEAD_DV:(head + 1) * HEAD_DV]
            v_slab = jnp.concatenate([vh[:i * FAST_BLOCK] for i in range(1, N_SUB)] + [vh], axis=0)
            outs.append(_dot(scores[c, head], v_slab)
                        + _dot(sel(q_in[r0:r0 + CHUNK, sl], j).astype(BF16), entering[c][ti].astype(BF16)))
        out_rows.append(jnp.concatenate(outs, axis=1))
    return jnp.concatenate(out_rows, axis=0), (leaving[-1] if carry else leaving)


def _head_norm_gate(o, gain, gate):
    outs = []
    for h in range(o.shape[1] // HEAD_DV):
        sl = slice(h * HEAD_DV, (h + 1) * HEAD_DV)
        oh = o[:, sl]
        oh = oh * lax.rsqrt(jnp.mean(oh * oh, axis=-1, keepdims=True) + NORM_EPS) * gain
        outs.append(oh * _silu(gate[:, sl]))
    return jnp.concatenate(outs, axis=1)


def _mixer_kernel(x_ref, mod_ref, nrm_ref, win_ref, lb_ref, wgk2_ref, bgk_ref, hgn_ref, glan_ref,
                  wa_ref, wb_ref, wo_ref, shg0_ref, sgla0_ref,
                  tril_ref, slab_mask_ref, mall_safe_ref, mask_safe_ref, *rest, layer, tb, pending_moe):
    if pending_moe:
        dst_cur_ref, dst_nxt_ref, wts_ref, modp_ref, y_hbm = rest[:5]
        rest = rest[5:]
    xo_ref, shg_o_ref, sgla_o_ref, p_scr, k_scr, lg_scr, shg_scr, sgla_scr = rest[:8]
    j = pl.program_id(1)
    nb = x_ref.shape[0]
    rows_all = nb * tb

    @pl.when(j == 0)
    def _():
        shg_scr[...] = shg0_ref[...].reshape(shg_scr.shape)
        sgla_scr[...] = sgla0_ref[...].reshape(sgla_scr.shape)

    def per_row(ref, i):
        if nb == 1:
            return ref[0, i:i + 1, :]
        return jnp.broadcast_to(ref[:, i:i + 1, :], (nb, tb, D_MODEL)).reshape(rows_all, D_MODEL)

    x = x_ref[...].reshape(rows_all, D_MODEL)
    n_prefetch = 0
    if pending_moe:
        cbuf, csem = rest[8:]
        n_prefetch = TOPK * rows_all
        step = pl.program_id(0) * pl.num_programs(1) + j
        last = pl.num_programs(0) * pl.num_programs(1) - 1
        slot = step % 2

        @pl.when(step == 0)
        def _():
            _start_row_gather(dst_cur_ref, n_prefetch, y_hbm, cbuf.at[0], csem.at[0])

        _wait_row_gather(n_prefetch, y_hbm, cbuf.at[slot], csem.at[slot])
        x = x + per_row(modp_ref, 5) * (wts_ref[:, 0:1] * cbuf[slot, 0:rows_all, :]
                                        + wts_ref[:, 1:2] * cbuf[slot, rows_all:n_prefetch, :])

    def prefetch_rows(r0, r1):
        for r in range(r0, r1):
            row = dst_nxt_ref[0, 0, r]
            pltpu.make_async_copy(y_hbm.at[pl.ds(row, 1)], cbuf.at[1 - slot, pl.ds(r, 1)],
                                  csem.at[1 - slot]).start()

    sh1 = per_row(mod_ref, 0)
    sc1 = per_row(mod_ref, 1)
    g1 = per_row(mod_ref, 2)
    hb = _rms_mod(x, nrm_ref[...], sc1, sh1).astype(BF16)
    col_tiles = list(range(0, IN_COLS_PAD, PROJ_TILE))
    for i, c in enumerate(col_tiles):
        c1 = min(c + PROJ_TILE, IN_COLS_PAD)
        p_scr[:, c:c1] = _dot(hb, win_ref[:, c:c1])
        if pending_moe:
            prefetch_rows(n_prefetch * i // len(col_tiles), n_prefetch * (i + 1) // len(col_tiles))

    lb_all = lb_ref[...]
    lb_max = jnp.max(lb_all, axis=0, keepdims=True)
    lb_exp = jnp.exp(lb_all - lb_max)
    sm = lb_exp / jnp.sum(lb_exp, axis=0, keepdims=True)
    lbl = jnp.clip(jnp.sum(sm[0:layer + 1], axis=0, keepdims=True) - sm[0:1], 0.0, 1.0)

    p_scr[:, C_HQ:C_HQ + HG_KW] = _silu(p_scr[:, C_HQ:C_HQ + HG_KW]) * (HG_DK ** -0.5)
    z = p_scr[:, C_HF:C_HF + HG_KW]
    f = lbl + (1.0 - lbl) * _sigmoid(z)
    p_scr[:, C_HF:C_HF + HG_KW] = jnp.log(jnp.maximum(f, LOG_FLOOR))
    k_scr[...] = (1.0 - lbl) * _sigmoid(-z)
    glr = p_scr[:, C_GLR:C_GLR + LANES].astype(BF16)
    gate = _dot(glr, wgk2_ref[...]) + bgk_ref[...]
    lg_scr[...] = (jnp.minimum(gate, 0.0) - jnp.log1p(jnp.exp(-jnp.abs(gate)))) * (1.0 / GLA_GATE_NORM)
    p_scr[:, C_GQ:C_GQ + GLA_KW] = p_scr[:, C_GQ:C_GQ + GLA_KW] * (GLA_DK ** -0.5)

    n_hg_tiles = HG_KW // LANES
    n_gla_tiles = GLA_KW // LANES

    def one_chunk(rows, states, attend):
        o_hg, st_hg = attend(p_scr[rows, C_HQ:C_HQ + HG_KW], k_scr[rows, :],
                             p_scr[rows, C_HI:C_HI + HG_W], p_scr[rows, C_HF:C_HF + HG_KW],
                             states[:n_hg_tiles], 1)
        p_scr[rows, C_HI:C_HI + HG_W] = o_hg
        o_gla, st_gla = attend(p_scr[rows, C_GQ:C_GQ + GLA_KW], p_scr[rows, C_GK:C_GK + GLA_KW],
                               p_scr[rows, C_GV:C_GV + GLA_W], lg_scr[rows, :],
                               states[n_hg_tiles:], 2)
        p_scr[rows, C_GV:C_GV + GLA_W] = o_gla
        return st_hg + st_gla

    def attend_safe(q, k, v, g, states, heads_per_tile):
        return _chunk_attention(q, k, v, g, states, mall_safe_ref, mask_safe_ref, heads_per_tile, SAFE_PLAN)

    def load_states(seq=0):
        return ([shg_scr[seq * n_hg_tiles + t] for t in range(n_hg_tiles)]
                + [sgla_scr[seq * n_gla_tiles + t] for t in range(n_gla_tiles)])

    def store_states(states, seq=0):
        for t in range(n_hg_tiles):
            shg_scr[seq * n_hg_tiles + t] = states[t]
        for t in range(n_gla_tiles):
            sgla_scr[seq * n_gla_tiles + t] = states[n_hg_tiles + t]

    def run_block_fast():
        carry = nb == 1
        seqs = [load_states(s) for s in range(nb)]
        hg_in = seqs[0][:n_hg_tiles] if carry else [st[:n_hg_tiles] for st in seqs]
        gla_in = seqs[0][n_hg_tiles:] if carry else [st[n_hg_tiles:] for st in seqs]
        o_hg, st_hg = _block_attention_fast(
            p_scr[:, C_HQ:C_HQ + HG_KW], k_scr[...], p_scr[:, C_HI:C_HI + HG_W],
            p_scr[:, C_HF:C_HF + HG_KW], hg_in, tril_ref, slab_mask_ref, 1, carry)
        p_scr[:, C_HI:C_HI + HG_W] = o_hg
        o_gla, st_gla = _block_attention_fast(
            p_scr[:, C_GQ:C_GQ + GLA_KW], p_scr[:, C_GK:C_GK + GLA_KW], p_scr[:, C_GV:C_GV + GLA_W],
            lg_scr[...], gla_in, tril_ref, slab_mask_ref, 2, carry)
        p_scr[:, C_GV:C_GV + GLA_W] = o_gla
        if carry:
            store_states(st_hg + st_gla)
        else:
            for s in range(nb):
                store_states(st_hg[s] + st_gla[s], s)

    def run_chunks_safe():
        def chunk_body(ci, carry):
            rows = pl.ds(pl.multiple_of(ci * CHUNK, CHUNK), CHUNK)
            seq = 0 if nb == 1 else ci
            store_states(one_chunk(rows, load_states(seq), attend_safe), seq)
            return carry

        lax.fori_loop(0, rows_all // CHUNK, chunk_body, 0)

    blk = FAST_BLOCK
    min_hg = jnp.min(jnp.sum(p_scr[:, C_HF:C_HF + HG_KW].reshape(rows_all // blk, blk, HG_KW), axis=1))
    min_gla = jnp.min(jnp.sum(lg_scr[...].reshape(rows_all // blk, blk, GLA_KW), axis=1))
    bounded = jnp.minimum(min_hg, min_gla) >= -FAST_BLOCK_DECAY_LIMIT

    @pl.when(bounded)
    def _():
        run_block_fast()

    @pl.when(jnp.logical_not(bounded))
    def _():
        run_chunks_safe()

    o_hg = _head_norm_gate(p_scr[:, C_HI:C_HI + HG_W], hgn_ref[...], p_scr[:, C_HOG:C_HOG + HG_W])
    o_gla = _head_norm_gate(p_scr[:, C_GV:C_GV + GLA_W], glan_ref[...], p_scr[:, C_GOG:C_GOG + GLA_W])
    ya = _dot(o_hg.astype(BF16), wa_ref[...])
    yb = _dot(o_gla.astype(BF16), wb_ref[...])
    merged = (_sigmoid(p_scr[:, C_GA:C_GA + D_MODEL]) * ya
              + _sigmoid(p_scr[:, C_GB:C_GB + D_MODEL]) * yb)
    m = _dot(merged.astype(BF16), wo_ref[...])
    xo_ref[...] = (x + g1 * m).reshape(nb, tb, D_MODEL)

    @pl.when(j == pl.num_programs(1) - 1)
    def _():
        shg_o_ref[...] = shg_scr[...].reshape(shg_o_ref.shape)
        sgla_o_ref[...] = sgla_scr[...].reshape(sgla_o_ref.shape)

    if pending_moe:
        @pl.when(step == last)
        def _():
            _wait_row_gather(n_prefetch, y_hbm, cbuf.at[1 - slot], csem.at[1 - slot])


def _const_spec(shape):
    nd = len(shape)
    return pl.BlockSpec(shape, lambda b, j, nd=nd: (0,) * nd, pipeline_mode=pl.Buffered(1))


def _mixer(x, mod, nrm, win, hg_lb, wgk2, bgk, hgn, glan, wa, wb, wo, shg0, sgla0, plan_consts,
           *, layer, tb, nb=1, pending_moe=None):
    bsz, seq, _ = x.shape
    assert nb == 1 or seq == tb == CHUNK
    nj = seq // tb
    rows = nb * tb
    kern = functools.partial(_mixer_kernel, layer=layer, tb=tb, pending_moe=pending_moe is not None)
    n_gla_tiles = GLA_KW // LANES
    extra_specs, extra_args, extra_scratch = [], [], []
    if pending_moe is not None:
        dest_steps, wts_col, mod_prev, y_slots = pending_moe
        n_steps = bsz // nb * nj
        extra_specs = [
            pl.BlockSpec((1, 1, TOPK * rows), lambda b, j: (b * nj + j, 0, 0), memory_space=pltpu.SMEM),
            pl.BlockSpec((1, 1, TOPK * rows), lambda b, j: (jnp.minimum(b * nj + j + 1, n_steps - 1), 0, 0),
                         memory_space=pltpu.SMEM),
            pl.BlockSpec((rows, TOPK), lambda b, j: (b * nj + j, 0)),
            pl.BlockSpec((nb, 6, D_MODEL), lambda b, j: (b, 0, 0)),
            pl.BlockSpec(memory_space=pl.ANY),
        ]
        extra_args = [dest_steps, dest_steps, wts_col, mod_prev, y_slots]
        extra_scratch = [pltpu.VMEM((2, TOPK * rows, D_MODEL), F32), pltpu.SemaphoreType.DMA((2,))]
    return pl.pallas_call(
        kern,
        grid=(bsz // nb, nj),
        in_specs=[
            pl.BlockSpec((nb, tb, D_MODEL), lambda b, j: (b, j, 0)),
            pl.BlockSpec((nb, 6, D_MODEL), lambda b, j: (b, 0, 0)),
            _const_spec((1, D_MODEL)),
            _const_spec((D_MODEL, IN_COLS_PAD)),
            _const_spec((DEPTH, HG_KW)),
            _const_spec((LANES, GLA_KW)),
            _const_spec((1, GLA_KW)),
            _const_spec((1, HEAD_DV)),
            _const_spec((1, HEAD_DV)),
            _const_spec((HG_W, D_MODEL)),
            _const_spec((GLA_W, D_MODEL)),
            _const_spec((D_MODEL, D_MODEL)),
            pl.BlockSpec((nb, HG_HEADS, HG_DK, HEAD_DV), lambda b, j: (b, 0, 0, 0)),
            pl.BlockSpec((nb, n_gla_tiles, LANES, HEAD_DV), lambda b, j: (b, 0, 0, 0)),
            _const_spec((rows, rows)),
            _const_spec((CHUNK, SLAB_ROWS)),
            _const_spec((SAFE_PLAN.cum_rows, CHUNK)),
            _const_spec((SAFE_PLAN.n_masks, CHUNK, CHUNK)),
        ] + extra_specs,
        out_specs=[
            pl.BlockSpec((nb, tb, D_MODEL), lambda b, j: (b, j, 0)),
            pl.BlockSpec((nb, HG_HEADS, HG_DK, HEAD_DV), lambda b, j: (b, 0, 0, 0)),
            pl.BlockSpec((nb, n_gla_tiles, LANES, HEAD_DV), lambda b, j: (b, 0, 0, 0)),
        ],
        out_shape=[
            jax.ShapeDtypeStruct((bsz, seq, D_MODEL), F32),
            jax.ShapeDtypeStruct((bsz, HG_HEADS, HG_DK, HEAD_DV), F32),
            jax.ShapeDtypeStruct((bsz, n_gla_tiles, LANES, HEAD_DV), F32),
        ],
        scratch_shapes=[
            pltpu.VMEM((rows, IN_COLS_PAD), F32),
            pltpu.VMEM((rows, HG_KW), F32),
            pltpu.VMEM((rows, GLA_KW), F32),
            pltpu.VMEM((nb * HG_HEADS, HG_DK, HEAD_DV), F32),
            pltpu.VMEM((nb * n_gla_tiles, LANES, HEAD_DV), F32),
        ] + extra_scratch,
        compiler_params=pltpu.CompilerParams(
            dimension_semantics=("arbitrary", "arbitrary"), vmem_limit_bytes=VMEM_LIMIT),
        name=f"mixer_l{layer}",
    )(x, mod, nrm, win, hg_lb, wgk2, bgk, hgn, glan, wa, wb, wo, shg0, sgla0, *plan_consts, *extra_args)


SUBLANES = 8
EXPERT_ROW0 = SUBLANES
ROUTER_ROWS = 48
MOE_TILE = 512
TILE_ASSIGN = TOPK * MOE_TILE
MAX_EXPERT_BLOCK = 1024


def _expert_block_rows(n_assign):
    return max(LANES, min(MAX_EXPERT_BLOCK, n_assign // N_EXPERTS // 2))


def _first_argmax_rows(vals, n):
    ridx = lax.broadcasted_iota(jnp.int32, vals.shape, 0)
    vmax = jnp.max(vals, axis=0, keepdims=True)
    imax = jnp.min(jnp.where(vals == vmax, ridx, n), axis=0, keepdims=True)
    return vmax, imax


def _router_kernel(x_ref, mod_ref, nrm_ref, wr_ref, br_ref, tri_ref,
                   h_ref, eid_ref, rank_ref, wts_ref, cnt_ref, run_scr):
    @pl.when(pl.program_id(0) == 0)
    def _():
        run_scr[...] = jnp.zeros_like(run_scr)

    u, lt, _ = x_ref.shape
    x = x_ref[...]
    sh2 = mod_ref[:, 3:4, :]
    sc2 = mod_ref[:, 4:5, :]
    h = _rms_mod(x, nrm_ref[...].reshape(1, 1, D_MODEL), sc2, sh2).reshape(u * lt, D_MODEL)
    h_ref[...] = h
    h_hi = h.astype(BF16)
    h_lo = (h - h_hi.astype(F32)).astype(BF16)
    w = wr_ref[...]
    w_hi = w.astype(BF16)
    w_lo = (w - w_hi.astype(F32)).astype(BF16)
    p_hi = _dot_nt(jnp.concatenate([w_hi, w_lo], axis=0), h_hi)
    logits = p_hi[:ROUTER_ROWS] + p_hi[ROUTER_ROWS:] + _dot_nt(w_hi, h_lo) + br_ref[...]
    gl = logits[0:N_GROUPS]
    gmax, gi = _first_argmax_rows(gl, N_GROUPS)
    gp = 1.0 / jnp.sum(jnp.exp(gl - gmax), axis=0, keepdims=True)
    le = logits[EXPERT_ROW0:EXPERT_ROW0 + EXPERTS_PER_GROUP]
    for g in range(1, N_GROUPS):
        r0 = EXPERT_ROW0 + g * EXPERTS_PER_GROUP
        le = jnp.where(gi == g, logits[r0:r0 + EXPERTS_PER_GROUP], le)
    pe = jnp.exp(le - jnp.max(le, axis=0, keepdims=True))
    pe = pe / jnp.sum(pe, axis=0, keepdims=True)
    v1, i1 = _first_argmax_rows(pe, EXPERTS_PER_GROUP)
    ridx = lax.broadcasted_iota(jnp.int32, pe.shape, 0)
    v2, i2 = _first_argmax_rows(jnp.where(ridx == i1, -1.0, pe), EXPERTS_PER_GROUP)
    vsum = v1 + v2
    wts_ref[0:1, :] = gp * v1 / vsum
    wts_ref[1:2, :] = gp * v2 / vsum
    eflat = jnp.concatenate([gi * EXPERTS_PER_GROUP + i1, gi * EXPERTS_PER_GROUP + i2], axis=1)
    eid_ref[0] = eflat
    onehot = (eflat == lax.broadcasted_iota(jnp.int32, (N_EXPERTS, TILE_ASSIGN), 0)).astype(F32)
    before = _dot(onehot.astype(BF16), tri_ref[...]) + run_scr[...]
    rank_ref[0] = jnp.sum(onehot * before, axis=0, keepdims=True).astype(jnp.int32)
    run_scr[...] = run_scr[...] + jnp.sum(onehot, axis=1, keepdims=True)
    cnt_ref[...] = run_scr[...].astype(jnp.int32)


def _router(x_units, mod_units, nrm, wr, br, tri):
    n_units, lt, _ = x_units.shape
    u = MOE_TILE // lt
    n_tiles = n_units // u
    return pl.pallas_call(
        _router_kernel,
        grid=(n_tiles,),
        in_specs=[
            pl.BlockSpec((u, lt, D_MODEL), lambda i: (i, 0, 0)),
            pl.BlockSpec((u, 6, D_MODEL), lambda i: (i, 0, 0)),
            pl.BlockSpec((1, D_MODEL), lambda i: (0, 0)),
            pl.BlockSpec((ROUTER_ROWS, D_MODEL), lambda i: (0, 0)),
            pl.BlockSpec((ROUTER_ROWS, 1), lambda i: (0, 0)),
            pl.BlockSpec((TILE_ASSIGN, TILE_ASSIGN), lambda i: (0, 0)),
        ],
        out_specs=[
            pl.BlockSpec((MOE_TILE, D_MODEL), lambda i: (i, 0)),
            pl.BlockSpec((1, 1, TILE_ASSIGN), lambda i: (i, 0, 0)),
            pl.BlockSpec((1, 1, TILE_ASSIGN), lambda i: (i, 0, 0)),
            pl.BlockSpec((TOPK, MOE_TILE), lambda i: (0, i)),
            pl.BlockSpec((N_EXPERTS, 1), lambda i: (0, 0)),
        ],
        out_shape=[
            jax.ShapeDtypeStruct((n_tiles * MOE_TILE, D_MODEL), F32),
            jax.ShapeDtypeStruct((n_tiles, 1, TILE_ASSIGN), jnp.int32),
            jax.ShapeDtypeStruct((n_tiles, 1, TILE_ASSIGN), jnp.int32),
            jax.ShapeDtypeStruct((TOPK, n_tiles * MOE_TILE), F32),
            jax.ShapeDtypeStruct((N_EXPERTS, 1), jnp.int32),
        ],
        scratch_shapes=[pltpu.VMEM((N_EXPERTS, 1), F32)],
        compiler_params=pltpu.CompilerParams(dimension_semantics=("arbitrary",)),
        name="moe_router",
    )(x_units, mod_units, nrm, wr, br, tri)


def _start_row_gather(idx_ref, n_rows, src_hbm, dst, sem):
    def body(r, carry):
        row = idx_ref[0, 0, r]
        pltpu.make_async_copy(src_hbm.at[pl.ds(row, 1)], dst.at[pl.ds(r, 1)], sem).start()
        return carry
    lax.fori_loop(0, n_rows, body, 0, unroll=8)


def _wait_row_gather(n_rows, src_hbm, dst, sem):
    pltpu.make_async_copy(src_hbm.at[pl.ds(0, n_rows)], dst, sem).wait()


def _dispatch_kernel(pend_ref, padded_ref, dest_ref, h_ref, xs_hbm, zbuf, sem):
    block_rows = zbuf.shape[0]
    n_blocks = xs_hbm.shape[0] // block_rows

    def zero_block(first_row):
        return pltpu.make_async_copy(
            zbuf, xs_hbm.at[pl.ds(pl.multiple_of(first_row, block_rows), block_rows)], sem.at[0])

    @pl.when(pl.program_id(0) == 0)
    def _():
        zbuf[...] = jnp.zeros_like(zbuf)
        n_used = pend_ref[N_EXPERTS - 1] // block_rows
        for e in range(N_EXPERTS):
            @pl.when(padded_ref[e] > 0)
            def _():
                zero_block(pend_ref[e] - block_rows).start()
        lax.fori_loop(n_used, n_blocks, lambda b, c: (zero_block(b * block_rows).start(), c)[1], 0)
        for e in range(N_EXPERTS):
            @pl.when(padded_ref[e] > 0)
            def _():
                zero_block(pend_ref[e] - block_rows).wait()
        lax.fori_loop(n_used, n_blocks, lambda b, c: (zero_block(b * block_rows).wait(), c)[1], 0)

    def body(t, carry):
        for k in range(TOPK):
            slot = dest_ref[0, 0, k * MOE_TILE + t]
            pltpu.make_async_copy(h_ref.at[pl.ds(t, 1)], xs_hbm.at[pl.ds(slot, 1)], sem.at[1]).start()
        return carry
    lax.fori_loop(0, MOE_TILE, body, 0, unroll=8)
    for k in range(TOPK):
        pltpu.make_async_copy(h_ref, xs_hbm.at[pl.ds(0, MOE_TILE)], sem.at[1]).wait()


def _dispatch(pad_end, padded, dest_tiles, h, n_slots, block_rows):
    n_tiles = dest_tiles.shape[0]
    grid_spec = pltpu.PrefetchScalarGridSpec(
        num_scalar_prefetch=2,
        grid=(n_tiles,),
        in_specs=[
            pl.BlockSpec((1, 1, TILE_ASSIGN), lambda i, pe, pd: (i, 0, 0), memory_space=pltpu.SMEM),
            pl.BlockSpec((MOE_TILE, D_MODEL), lambda i, pe, pd: (i, 0)),
        ],
        out_specs=pl.BlockSpec(memory_space=pl.ANY),
        scratch_shapes=[pltpu.VMEM((block_rows, D_MODEL), F32), pltpu.SemaphoreType.DMA((2,))],
    )
    return pl.pallas_call(
        _dispatch_kernel,
        grid_spec=grid_spec,
        out_shape=jax.ShapeDtypeStruct((n_slots, D_MODEL), F32),
        compiler_params=pltpu.CompilerParams(dimension_semantics=("arbitrary",)),
        name="moe_dispatch",
    )(pad_end, padded, dest_tiles, h)


def _experts_kernel(be_ref, nused_ref, x_ref, wg_ref, wu_ref, wd_ref, o_ref):
    @pl.when(pl.program_id(0) < nused_ref[0])
    def _():
        xb = x_ref[...].astype(BF16)
        a = _silu(_dot(xb, wg_ref[0, 0])) * _dot(xb, wu_ref[0, 0])
        o_ref[...] = _dot(a.astype(BF16), wd_ref[0, 0])

    @pl.when(pl.program_id(0) >= nused_ref[0])
    def _():
        o_ref[...] = jnp.zeros_like(o_ref)


def _experts(block_e, n_used, xs, wg, wu, wd, layer, block_rows):
    n_blocks = xs.shape[0] // block_rows

    def row_block(i, be, nu):
        return (jnp.minimum(i, nu[0] - 1), 0)

    def expert_block(i, be, nu):
        return (layer, be[jnp.minimum(i, nu[0] - 1)], 0, 0)

    grid_spec = pltpu.PrefetchScalarGridSpec(
        num_scalar_prefetch=2,
        grid=(n_blocks,),
        in_specs=[
            pl.BlockSpec((block_rows, D_MODEL), row_block),
            pl.BlockSpec((1, 1, D_MODEL, D_EXPERT), expert_block),
            pl.BlockSpec((1, 1, D_MODEL, D_EXPERT), expert_block),
            pl.BlockSpec((1, 1, D_EXPERT, D_MODEL), expert_block),
        ],
        out_specs=pl.BlockSpec((block_rows, D_MODEL), lambda i, be, nu: (i, 0)),
    )
    return pl.pallas_call(
        _experts_kernel,
        grid_spec=grid_spec,
        out_shape=jax.ShapeDtypeStruct(xs.shape, F32),
        compiler_params=pltpu.CompilerParams(
            dimension_semantics=("arbitrary",), vmem_limit_bytes=VMEM_LIMIT),
        name="moe_experts",
    )(block_e, n_used, xs, wg, wu, wd)


def _combine_kernel(dst_cur_ref, dst_nxt_ref, x_ref, mod_ref, wts_ref, nrm_ref, y_hbm, o_ref, buf, sem,
                    *, final_norm):
    i = pl.program_id(0)
    n = pl.num_programs(0)
    slot = i % 2

    @pl.when(i == 0)
    def _():
        _start_row_gather(dst_cur_ref, TILE_ASSIGN, y_hbm, buf.at[0], sem.at[0])

    _wait_row_gather(TILE_ASSIGN, y_hbm, buf.at[slot], sem.at[slot])
    u, lt, _ = x_ref.shape
    n_pieces = MOE_TILE // CHUNK
    per_piece = TILE_ASSIGN // n_pieces
    for c in range(n_pieces):
        unit, r0 = divmod(c * CHUNK, lt)
        t0 = c * CHUNK
        y = (wts_ref[t0:t0 + CHUNK, 0:1] * buf[slot, t0:t0 + CHUNK, :]
             + wts_ref[t0:t0 + CHUNK, 1:2] * buf[slot, MOE_TILE + t0:MOE_TILE + t0 + CHUNK, :])
        out = x_ref[unit, r0:r0 + CHUNK, :] + mod_ref[unit, 5:6, :] * y
        if final_norm:
            out = out * lax.rsqrt(jnp.mean(out * out, axis=-1, keepdims=True) + NORM_EPS) * nrm_ref[...]
        o_ref[unit, r0:r0 + CHUNK, :] = out
        for r in range(c * per_piece, (c + 1) * per_piece):
            row = dst_nxt_ref[0, 0, r]
            pltpu.make_async_copy(y_hbm.at[pl.ds(row, 1)], buf.at[1 - slot, pl.ds(r, 1)],
                                  sem.at[1 - slot]).start()

    @pl.when(i == n - 1)
    def _():
        _wait_row_gather(TILE_ASSIGN, y_hbm, buf.at[1 - slot], sem.at[1 - slot])


def _combine(dest_tiles, x_units, mod_units, wts_col, nrm, y_slots, *, final_norm):
    n_units, lt, _ = x_units.shape
    u = MOE_TILE // lt
    n_tiles = n_units // u
    return pl.pallas_call(
        functools.partial(_combine_kernel, final_norm=final_norm),
        grid=(n_tiles,),
        in_specs=[
            pl.BlockSpec((1, 1, TILE_ASSIGN), lambda i: (i, 0, 0), memory_space=pltpu.SMEM),
            pl.BlockSpec((1, 1, TILE_ASSIGN), lambda i: (jnp.minimum(i + 1, n_tiles - 1), 0, 0),
                         memory_space=pltpu.SMEM),
            pl.BlockSpec((u, lt, D_MODEL), lambda i: (i, 0, 0)),
            pl.BlockSpec((u, 6, D_MODEL), lambda i: (i, 0, 0)),
            pl.BlockSpec((MOE_TILE, TOPK), lambda i: (i, 0)),
            pl.BlockSpec((1, D_MODEL), lambda i: (0, 0)),
            pl.BlockSpec(memory_space=pl.ANY),
        ],
        out_specs=pl.BlockSpec((u, lt, D_MODEL), lambda i: (i, 0, 0)),
        out_shape=jax.ShapeDtypeStruct(x_units.shape, F32),
        scratch_shapes=[pltpu.VMEM((2, TILE_ASSIGN, D_MODEL), F32), pltpu.SemaphoreType.DMA((2,))],
        compiler_params=pltpu.CompilerParams(
            dimension_semantics=("arbitrary",), vmem_limit_bytes=VMEM_LIMIT),
        name="moe_combine",
    )(dest_tiles, dest_tiles, x_units, mod_units, wts_col, nrm, y_slots)


def _routing_tables(eid_tiles, rank_tiles, counts, block_rows):
    n_blocks = eid_tiles.size // block_rows + N_EXPERTS
    padded = (counts + block_rows - 1) // block_rows * block_rows
    pad_end = jnp.cumsum(padded).astype(jnp.int32)
    pad_start = pad_end - padded
    block_start = jnp.arange(n_blocks, dtype=jnp.int32)[:, None] * block_rows
    block_e = jnp.minimum(jnp.sum((block_start >= pad_end[None, :]).astype(jnp.int32), axis=1),
                          N_EXPERTS - 1).astype(jnp.int32)
    n_used = pad_end[-1:] // block_rows
    experts = jnp.arange(N_EXPERTS, dtype=jnp.int32)
    first_slot = jnp.sum(jnp.where(eid_tiles[..., None] == experts, pad_start, 0), axis=-1)
    return block_e, n_used, pad_end, padded, first_slot + rank_tiles


def _moe_units(x, mod_l):
    bsz, seq, _ = x.shape
    lt = min(seq, MOE_TILE)
    per = seq // lt
    x_units = x.reshape(bsz * seq // lt, lt, D_MODEL)
    mod_units = jnp.repeat(mod_l, per, axis=0) if per > 1 else mod_l
    return x_units, mod_units


def _moe_experts(x, mod_l, nrm_ffn, wr, br, tri, wg, wu, wd, layer):
    x_units, mod_units = _moe_units(x, mod_l)
    h, eid_tiles, rank_tiles, wts, counts = _router(x_units, mod_units, nrm_ffn, wr, br, tri)
    block_rows = _expert_block_rows(eid_tiles.size)
    block_e, n_used, pad_end, padded, dest_tiles = _routing_tables(
        eid_tiles, rank_tiles, counts[:, 0], block_rows)
    n_slots = block_e.shape[0] * block_rows
    xs = _dispatch(pad_end, padded, dest_tiles, h, n_slots, block_rows)
    return dest_tiles, wts.T, _experts(block_e, n_used, xs, wg, wu, wd, layer, block_rows)


def _dest_per_step(dest_tiles, tb):
    n_tiles = dest_tiles.shape[0]
    per = MOE_TILE // tb
    d = dest_tiles.reshape(n_tiles, TOPK, per, tb).transpose(0, 2, 1, 3)
    return d.reshape(n_tiles * per, 1, TOPK * tb)


def kernel(x_prompt, x_sample, c_prompt, c_sample, state_hgrn, state_gla, w_ada, b_ada, norm_mix,
           norm_ffn, w_in, hg_lb, hg_onorm, w_gk2, b_gk, gla_onorm, w_br_a, w_br_b, w_out, w_rg, b_rg,
           w_re, b_re, w_e_gate, w_e_up, w_e_down, norm_final):
    bp = x_prompt.shape[0]
    bs = x_sample.shape[0]
    mod = _ada_mod(jnp.concatenate([c_prompt, c_sample], axis=0), w_ada, b_ada)
    mod = mod.reshape(DEPTH, bp + bs, 6, D_MODEL)

    glr0 = C_GOG + GLA_W
    win_r = jnp.concatenate(
        [w_in[:, :, :glr0], w_in[:, :, glr0 + GLA_GATE_RANK:], w_in[:, :, glr0:glr0 + GLA_GATE_RANK],
         jnp.zeros((DEPTH, D_MODEL, LANES - GLA_GATE_RANK), F32)], axis=2).astype(BF16)
    wgk2_p = jnp.concatenate(
        [w_gk2, jnp.zeros((DEPTH, LANES - GLA_GATE_RANK, GLA_KW), F32)], axis=1).astype(BF16)
    wa_b = w_br_a.astype(BF16)
    wb_b = w_br_b.astype(BF16)
    wo_b = w_out.astype(BF16)

    def plan_consts(rows):
        r = np.arange(rows)
        chunk_tril = (r[:, None] // CHUNK == r[None, :] // CHUNK) & (r[None, :] <= r[:, None])
        return [jnp.asarray(chunk_tril, BF16), jnp.asarray(_slab_mask(), F32),
                jnp.asarray(SAFE_PLAN.segment_sum_matrix(), BF16), jnp.asarray(SAFE_PLAN.masks(), F32)]

    zpad = jnp.zeros((DEPTH, EXPERT_ROW0 - N_GROUPS, D_MODEL), F32)
    ztail = jnp.zeros((DEPTH, ROUTER_ROWS - EXPERT_ROW0 - N_EXPERTS, D_MODEL), F32)
    wr = jnp.concatenate([jnp.swapaxes(w_rg, 1, 2), zpad, jnp.swapaxes(w_re, 1, 2), ztail], axis=1)
    br = jnp.concatenate([b_rg, jnp.zeros((DEPTH, EXPERT_ROW0 - N_GROUPS), F32), b_re,
                          jnp.zeros((DEPTH, ROUTER_ROWS - EXPERT_ROW0 - N_EXPERTS), F32)], axis=1)[:, :, None]
    wg_b = w_e_gate.astype(BF16)
    wu_b = w_e_up.astype(BF16)
    wd_b = w_e_down.astype(BF16)
    nrm_f = norm_final.reshape(1, D_MODEL)
    assign = np.arange(TILE_ASSIGN)
    tri = jnp.asarray(assign[:, None] < assign[None, :], BF16)

    def run(x, mod_g, shg, sgla, tb, nb):
        bsz = x.shape[0]
        new_hg, new_gla = [], []
        pending = None
        for l in range(DEPTH):
            x, s1, s2 = _mixer(
                x, mod_g[l], norm_mix[l:l + 1], win_r[l], hg_lb, wgk2_p[l], b_gk[l:l + 1],
                hg_onorm[l:l + 1], gla_onorm[l:l + 1], wa_b[l], wb_b[l], wo_b[l],
                shg[l], sgla[l].reshape(bsz, GLA_KW // LANES, LANES, HEAD_DV), plan_consts(nb * tb),
                layer=l, tb=tb, nb=nb, pending_moe=pending)
            new_hg.append(s1)
            new_gla.append(s2.reshape(bsz, GLA_HEADS, GLA_DK, HEAD_DV))
            dest_tiles, wts_col, y_slots = _moe_experts(
                x, mod_g[l], norm_ffn[l:l + 1], wr[l], br[l], tri, wg_b, wu_b, wd_b, l)
            pending = (_dest_per_step(dest_tiles, nb * tb), wts_col, mod_g[l], y_slots)
        x_units, mod_units = _moe_units(x, mod_g[DEPTH - 1])
        y = _combine(dest_tiles, x_units, mod_units, wts_col, nrm_f, y_slots, final_norm=True)
        return y.reshape(x.shape), jnp.stack(new_hg), jnp.stack(new_gla)

    zeros_hg = jnp.zeros((DEPTH, bp, HG_HEADS, HG_DK, HEAD_DV), F32)
    zeros_gla = jnp.zeros((DEPTH, bp, GLA_HEADS, GLA_DK, HEAD_DV), F32)
    y_p, hg_p, gla_p = run(x_prompt, mod[:, :bp], zeros_hg, zeros_gla, 256, 1)
    y_s, hg_s, gla_s = run(x_sample, mod[:, bp:], state_hgrn, state_gla, CHUNK, 4)
    return (y_p, y_s, hg_p, gla_p, hg_s, gla_s)
```

```python
import functools

import numpy as np
import jax
import jax.numpy as jnp
from jax import lax
from jax.experimental import pallas as pl
from jax.experimental.pallas import tpu as pltpu

F32 = jnp.float32
BF16 = jnp.bfloat16

D_MODEL = 1024
DEPTH = 2
CHUNK = 64
NORM_EPS = 1e-6
LOG_FLOOR = 1e-30
HG_HEADS = 4
HG_DK = 128
HEAD_DV = 128
HG_KW = HG_HEADS * HG_DK
HG_W = HG_HEADS * HEAD_DV
GLA_HEADS = 4
GLA_DK = 64
GLA_KW = GLA_HEADS * GLA_DK
GLA_W = GLA_HEADS * HEAD_DV
GLA_GATE_RANK = 16
GLA_GATE_NORM = 16.0
N_GROUPS = 4
EXPERTS_PER_GROUP = 8
N_EXPERTS = N_GROUPS * EXPERTS_PER_GROUP
TOPK = 2
D_EXPERT = 512

LANES = 128
VMEM_LIMIT = 56 * 1024 * 1024

C_HQ = 0
C_HF = C_HQ + HG_KW
C_HI = C_HF + HG_KW
C_HOG = C_HI + HG_W
C_GQ = C_HOG + HG_W
C_GK = C_GQ + GLA_KW
C_GV = C_GK + GLA_KW
C_GOG = C_GV + GLA_W
C_GA = C_GOG + GLA_W
C_GB = C_GA + D_MODEL
C_GLR = C_GB + D_MODEL
IN_COLS_PAD = C_GLR + LANES
MXU_WIDTH = 256
PROJ_TILE = 4 * MXU_WIDTH


class _ScorePlan:
    def __init__(self, levels):
        self.levels = levels
        self.cum_rows = (2 * len(levels) + 1) * CHUNK
        self.n_masks = len(levels) + 2

    def segment_sum_matrix(self):
        t = np.arange(CHUNK)[:, None]
        r = np.arange(CHUNK)[None, :]
        rows = []
        for m in self.levels:
            same = (t // m) == (r // m)
            rows.append(same & (r <= t))
            rows.append(same & (r > t))
        rows.append(r <= t)
        return np.concatenate(rows, axis=0).astype(np.float32)

    def masks(self):
        t = np.arange(CHUNK)[:, None]
        s = np.arange(CHUNK)[None, :]
        masks = [t == s]
        for m in self.levels + (1,):
            masks.append(((t // (2 * m)) == (s // (2 * m))) & ((t // m) % 2 == 1) & ((s // m) % 2 == 0))
        return np.stack(masks).astype(np.float32)


SAFE_PLAN = _ScorePlan((32, 16, 8, 4, 2))


def _dot(a, b):
    return jnp.dot(a, b, preferred_element_type=F32)


def _dot_nt(a, b):
    return lax.dot_general(a, b, (((1,), (1,)), ((), ())), preferred_element_type=F32)


def _sigmoid(x):
    return 1.0 / (1.0 + jnp.exp(-x))


def _silu(x):
    return x * _sigmoid(x)


def _rms_mod(x, gain, scale, shift):
    y = x * lax.rsqrt(jnp.mean(x * x, axis=-1, keepdims=True) + NORM_EPS)
    return y * gain * (1.0 + scale) + shift


ADA_COL_TILE = 512


def _ada_kernel(c_ref, w_ref, b_ref, o_ref):
    c = c_ref[...]
    o_ref[0] = jnp.dot(_silu(c), w_ref[0], preferred_element_type=F32,
                       precision=lax.Precision.HIGHEST) + b_ref[0]


def _ada_mod(c_all, w_ada, b_ada):
    nb = c_all.shape[0]
    tn = ADA_COL_TILE
    return pl.pallas_call(
        _ada_kernel,
        grid=(DEPTH, 6 * D_MODEL // tn),
        in_specs=[
            pl.BlockSpec((nb, D_MODEL), lambda l, j: (0, 0)),
            pl.BlockSpec((1, D_MODEL, tn), lambda l, j: (l, 0, j)),
            pl.BlockSpec((1, 1, tn), lambda l, j: (l, 0, j)),
        ],
        out_specs=pl.BlockSpec((1, nb, tn), lambda l, j: (l, 0, j)),
        out_shape=jax.ShapeDtypeStruct((DEPTH, nb, 6 * D_MODEL), F32),
        name="ada_mod",
    )(c_all, w_ada, b_ada.reshape(DEPTH, 1, 6 * D_MODEL))


def _chunk_attention(q, k, v, g, states, mall_ref, mask_ref, heads_per_tile, plan):
    w = q.shape[1]
    n_tiles = w // LANES
    g_hi = g.astype(BF16)
    r1 = g - g_hi.astype(F32)
    g_mid = r1.astype(BF16)
    g_lo = (r1 - g_mid.astype(F32)).astype(BF16)
    mall = mall_ref[...]
    cums = _dot(mall, g_hi) + _dot(mall, g_mid) + _dot(mall, g_lo)
    b = cums[plan.cum_rows - CHUNK:plan.cum_rows]
    level_q = []
    level_k = []
    for i in range(len(plan.levels)):
        level_q.append(q * jnp.exp(cums[2 * i * CHUNK:(2 * i + 1) * CHUNK]))
        level_k.append(k * jnp.exp(cums[(2 * i + 1) * CHUNK:(2 * i + 2) * CHUNK]))
    qs = [q] + level_q + [q * jnp.exp(g)]
    ks = [k] + level_k + [k]
    b_last = b[CHUNK - 1:CHUNK]
    q_in = q * jnp.exp(b)
    k_out = k * jnp.exp(b_last - b)
    e_last = jnp.exp(b_last)

    dk = LANES // heads_per_tile
    lane = lax.broadcasted_iota(jnp.int32, (CHUNK, LANES), 1)
    row = lax.broadcasted_iota(jnp.int32, (LANES, HEAD_DV), 0)
    outs = []
    new_states = []
    for ti in range(n_tiles):
        sl = slice(ti * LANES, (ti + 1) * LANES)
        ks_t = [kk[:, sl].astype(BF16) for kk in ks]
        k_out_t = k_out[:, sl].T.astype(BF16)
        e_col = jnp.broadcast_to(e_last[:, sl], (LANES, LANES)).T
        s_old = states[ti]
        s_old_b = s_old.astype(BF16)
        upd = None
        for j in range(heads_per_tile):
            head = ti * heads_per_tile + j
            if heads_per_tile == 1:
                sel = lambda a: a
            else:
                in_head = (lane // dk) == j
                sel = lambda a, in_head=in_head: jnp.where(in_head, a, 0.0)
            sc = jnp.zeros((CHUNK, CHUNK), F32)
            for i in range(plan.n_masks):
                sc = sc + _dot_nt(sel(qs[i][:, sl]).astype(BF16), ks_t[i]) * mask_ref[i]
            vh = v[:, head * HEAD_DV:(head + 1) * HEAD_DV].astype(BF16)
            o = _dot(sc.astype(BF16), vh) + _dot(sel(q_in[:, sl]).astype(BF16), s_old_b)
            outs.append(o)
            u = _dot(k_out_t, vh)
            upd = u if upd is None else jnp.where((row // dk) == j, u, upd)
        new_states.append(e_col * s_old + upd)
    return jnp.concatenate(outs, axis=1), new_states


FAST_BLOCK = 16
N_SUB = CHUNK // FAST_BLOCK
SLAB_ROWS = FAST_BLOCK * (N_SUB * (N_SUB - 1) // 2) + CHUNK
FAST_BLOCK_DECAY_LIMIT = 60.0


def _slab_mask():
    t = np.arange(CHUNK)[:, None]
    cols = []
    for i in range(1, N_SUB):
        cols.append(np.broadcast_to(t // FAST_BLOCK == i, (CHUNK, i * FAST_BLOCK)))
    s = np.arange(CHUNK)[None, :]
    cols.append((t // FAST_BLOCK == s // FAST_BLOCK) & (s <= t))
    return np.concatenate(cols, axis=1).astype(np.float32)


def _block_attention_fast(q, k, v, g, states, tril_ref, slab_mask_ref, heads_per_tile, carry=True):
    rows, w = q.shape
    n_chunks = rows // CHUNK
    n_tiles = w // LANES
    dk = LANES // heads_per_tile
    n_heads = n_tiles * heads_per_tile

    g_hi = g.astype(BF16)
    r1 = g - g_hi.astype(F32)
    g_mid = r1.astype(BF16)
    g_lo = (r1 - g_mid.astype(F32)).astype(BF16)
    tril = tril_ref[...]
    b = _dot(tril, g_hi) + _dot(tril, g_mid) + _dot(tril, g_lo)

    def end_row(c, i):
        r = c * CHUNK + (i + 1) * FAST_BLOCK
        return b[r - 1:r]

    def per_block(row_of):
        return jnp.concatenate([jnp.broadcast_to(row_of(c, i), (FAST_BLOCK, w))
                                for c in range(n_chunks) for i in range(N_SUB)], axis=0)

    zero = jnp.zeros((1, w), F32)
    b_start = per_block(lambda c, i: zero if i == 0 else end_row(c, i - 1))
    b_end = per_block(end_row)
    q_blk = q * jnp.exp(b - b_start)
    k_diag = k * jnp.exp(b_start - b)
    k_end = k * jnp.exp(b_end - b)
    q_in = q_blk * jnp.exp(b_start)
    k_out = k_end * jnp.exp(per_block(lambda c, i: end_row(c, N_SUB - 1)) - b_end)

    lane = lax.broadcasted_iota(jnp.int32, (CHUNK, LANES), 1)
    row = lax.broadcasted_iota(jnp.int32, (LANES, HEAD_DV), 0)
    slab_mask = slab_mask_ref[...]

    def sel(a, j):
        return a if heads_per_tile == 1 else jnp.where((lane // dk) == j, a, 0.0)

    v_b = v.astype(BF16)

    scores = {}
    for c in range(n_chunks):
        r0 = c * CHUNK
        slabs = []
        for i in range(1, N_SUB):
            for jb in range(i):
                blk = k_end[r0 + jb * FAST_BLOCK:r0 + (jb + 1) * FAST_BLOCK]
                slabs.append(blk if jb == i - 1 else blk * jnp.exp(end_row(c, i - 1) - end_row(c, jb)))
        slabs.append(k_diag[r0:r0 + CHUNK])
        k_slab = jnp.concatenate(slabs, axis=0).astype(BF16)
        for ti in range(n_tiles):
            sl = slice(ti * LANES, (ti + 1) * LANES)
            for j in range(heads_per_tile):
                qh = sel(q_blk[r0:r0 + CHUNK, sl], j).astype(BF16)
                scores[c, ti * heads_per_tile + j] = (_dot_nt(qh, k_slab[:, sl]) * slab_mask).astype(BF16)

    entering = [list(states)] if carry else [list(st) for st in states]
    leaving = []
    for c in range(n_chunks):
        r0 = c * CHUNK
        nxt = []
        for ti in range(n_tiles):
            sl = slice(ti * LANES, (ti + 1) * LANES)
            k_out_t = k_out[r0:r0 + CHUNK, sl].T.astype(BF16)
            upd = None
            for j in range(heads_per_tile):
                head = ti * heads_per_tile + j
                u = _dot(k_out_t, v_b[r0:r0 + CHUNK, head * HEAD_DV:(head + 1) * HEAD_DV])
                upd = u if upd is None else jnp.where((row // dk) == j, u, upd)
            e_col = jnp.broadcast_to(jnp.exp(end_row(c, N_SUB - 1)[:, sl]), (LANES, LANES)).T
            nxt.append(e_col * entering[c][ti] + upd)
        leaving.append(nxt)
        if carry:
            entering.append(nxt)

    out_rows = []
    for c in range(n_chunks):
        r0 = c * CHUNK
        outs = []
        for head in range(n_heads):
            ti, j = divmod(head, heads_per_tile)
            sl = slice(ti * LANES, (ti + 1) * LANES)
            vh = v_b[r0:r0 + CHUNK, head * HEAD_DV:(head + 1) * HEAD_DV]
            v_slab = jnp.concatenate([vh[:i * FAST_BLOCK] for i in range(1, N_SUB)] + [vh], axis=0)
            outs.append(_dot(scores[c, head], v_slab)
                        + _dot(sel(q_in[r0:r0 + CHUNK, sl], j).astype(BF16), entering[c][ti].astype(BF16)))
        out_rows.append(jnp.concatenate(outs, axis=1))
    return jnp.concatenate(out_rows, axis=0), (leaving[-1] if carry else leaving)


def _head_norm_gate(o, gain, gate):
    outs = []
    for h in range(o.shape[1] // HEAD_DV):
        sl = slice(h * HEAD_DV, (h + 1) * HEAD_DV)
        oh = o[:, sl]
        oh = oh * lax.rsqrt(jnp.mean(oh * oh, axis=-1, keepdims=True) + NORM_EPS) * gain
        outs.append(oh * _silu(gate[:, sl]))
    return jnp.concatenate(outs, axis=1)


def _mixer_kernel(x_ref, mod_ref, nrm_ref, win_ref, lb_ref, wgk2_ref, bgk_ref, hgn_ref, glan_ref,
                  wa_ref, wb_ref, wo_ref, shg0_ref, sgla0_ref,
                  tril_ref, slab_mask_ref, mall_safe_ref, mask_safe_ref, *rest, layer, tb, pending_moe):
    if pending_moe:
        dst_cur_ref, dst_nxt_ref, wts_ref, modp_ref, y_hbm = rest[:5]
        rest = rest[5:]
    xo_ref, shg_o_ref, sgla_o_ref, p_scr, k_scr, lg_scr, shg_scr, sgla_scr = rest[:8]
    j = pl.program_id(1)
    nb = x_ref.shape[0]
    rows_all = nb * tb

    @pl.when(j == 0)
    def _():
        shg_scr[...] = shg0_ref[...].reshape(shg_scr.shape)
        sgla_scr[...] = sgla0_ref[...].reshape(sgla_scr.shape)

    def per_row(ref, i):
        if nb == 1:
            return ref[0, i:i + 1, :]
        return jnp.broadcast_to(ref[:, i:i + 1, :], (nb, tb, D_MODEL)).reshape(rows_all, D_MODEL)

    x = x_ref[...].reshape(rows_all, D_MODEL)
    n_prefetch = 0
    if pending_moe:
        cbuf, csem = rest[8:]
        n_prefetch = TOPK * rows_all
        step = pl.program_id(0) * pl.num_programs(1) + j
        last = pl.num_programs(0) * pl.num_programs(1) - 1
        slot = step % 2

        @pl.when(step == 0)
        def _():
            _start_row_gather(dst_cur_ref, n_prefetch, y_hbm, cbuf.at[0], csem.at[0])

        _wait_row_gather(n_prefetch, y_hbm, cbuf.at[slot], csem.at[slot])
        x = x + per_row(modp_ref, 5) * (wts_ref[:, 0:1] * cbuf[slot, 0:rows_all, :]
                                        + wts_ref[:, 1:2] * cbuf[slot, rows_all:n_prefetch, :])

    def prefetch_rows(r0, r1):
        for r in range(r0, r1):
            row = dst_nxt_ref[0, 0, r]
            pltpu.make_async_copy(y_hbm.at[pl.ds(row, 1)], cbuf.at[1 - slot, pl.ds(r, 1)],
                                  csem.at[1 - slot]).start()

    sh1 = per_row(mod_ref, 0)
    sc1 = per_row(mod_ref, 1)
    g1 = per_row(mod_ref, 2)
    hb = _rms_mod(x, nrm_ref[...], sc1, sh1).astype(BF16)
    col_tiles = list(range(0, IN_COLS_PAD, PROJ_TILE))
    for i, c in enumerate(col_tiles):
        c1 = min(c + PROJ_TILE, IN_COLS_PAD)
        p_scr[:, c:c1] = _dot(hb, win_ref[:, c:c1])
        if pending_moe:
            prefetch_rows(n_prefetch * i // len(col_tiles), n_prefetch * (i + 1) // len(col_tiles))

    lb_all = lb_ref[...]
    lb_max = jnp.max(lb_all, axis=0, keepdims=True)
    lb_exp = jnp.exp(lb_all - lb_max)
    sm = lb_exp / jnp.sum(lb_exp, axis=0, keepdims=True)
    lbl = jnp.clip(jnp.sum(sm[0:layer + 1], axis=0, keepdims=True) - sm[0:1], 0.0, 1.0)

    p_scr[:, C_HQ:C_HQ + HG_KW] = _silu(p_scr[:, C_HQ:C_HQ + HG_KW]) * (HG_DK ** -0.5)
    z = p_scr[:, C_HF:C_HF + HG_KW]
    f = lbl + (1.0 - lbl) * _sigmoid(z)
    p_scr[:, C_HF:C_HF + HG_KW] = jnp.log(jnp.maximum(f, LOG_FLOOR))
    k_scr[...] = (1.0 - lbl) * _sigmoid(-z)
    glr = p_scr[:, C_GLR:C_GLR + LANES].astype(BF16)
    gate = _dot(glr, wgk2_ref[...]) + bgk_ref[...]
    lg_scr[...] = (jnp.minimum(gate, 0.0) - jnp.log1p(jnp.exp(-jnp.abs(gate)))) * (1.0 / GLA_GATE_NORM)
    p_scr[:, C_GQ:C_GQ + GLA_KW] = p_scr[:, C_GQ:C_GQ + GLA_KW] * (GLA_DK ** -0.5)

    n_hg_tiles = HG_KW // LANES
    n_gla_tiles = GLA_KW // LANES

    def one_chunk(rows, states, attend):
        o_hg, st_hg = attend(p_scr[rows, C_HQ:C_HQ + HG_KW], k_scr[rows, :],
                             p_scr[rows, C_HI:C_HI + HG_W], p_scr[rows, C_HF:C_HF + HG_KW],
                             states[:n_hg_tiles], 1)
        p_scr[rows, C_HI:C_HI + HG_W] = o_hg
        o_gla, st_gla = attend(p_scr[rows, C_GQ:C_GQ + GLA_KW], p_scr[rows, C_GK:C_GK + GLA_KW],
                               p_scr[rows, C_GV:C_GV + GLA_W], lg_scr[rows, :],
                               states[n_hg_tiles:], 2)
        p_scr[rows, C_GV:C_GV + GLA_W] = o_gla
        return st_hg + st_gla

    def attend_safe(q, k, v, g, states, heads_per_tile):
        return _chunk_attention(q, k, v, g, states, mall_safe_ref, mask_safe_ref, heads_per_tile, SAFE_PLAN)

    def load_states(seq=0):
        return ([shg_scr[seq * n_hg_tiles + t] for t in range(n_hg_tiles)]
                + [sgla_scr[seq * n_gla_tiles + t] for t in range(n_gla_tiles)])

    def store_states(states, seq=0):
        for t in range(n_hg_tiles):
            shg_scr[seq * n_hg_tiles + t] = states[t]
        for t in range(n_gla_tiles):
            sgla_scr[seq * n_gla_tiles + t] = states[n_hg_tiles + t]

    def run_block_fast():
        carry = nb == 1
        seqs = [load_states(s) for s in range(nb)]
        hg_in = seqs[0][:n_hg_tiles] if carry else [st[:n_hg_tiles] for st in seqs]
        gla_in = seqs[0][n_hg_tiles:] if carry else [st[n_hg_tiles:] for st in seqs]
        o_hg, st_hg = _block_attention_fast(
            p_scr[:, C_HQ:C_HQ + HG_KW], k_scr[...], p_scr[:, C_HI:C_HI + HG_W],
            p_scr[:, C_HF:C_HF + HG_KW], hg_in, tril_ref, slab_mask_ref, 1, carry)
        p_scr[:, C_HI:C_HI + HG_W] = o_hg
        o_gla, st_gla = _block_attention_fast(
            p_scr[:, C_GQ:C_GQ + GLA_KW], p_scr[:, C_GK:C_GK + GLA_KW], p_scr[:, C_GV:C_GV + GLA_W],
            lg_scr[...], gla_in, tril_ref, slab_mask_ref, 2, carry)
        p_scr[:, C_GV:C_GV + GLA_W] = o_gla
        if carry:
            store_states(st_hg + st_gla)
        else:
            for s in range(nb):
                store_states(st_hg[s] + st_gla[s], s)

    def run_chunks_safe():
        def chunk_body(ci, carry):
            rows = pl.ds(pl.multiple_of(ci * CHUNK, CHUNK), CHUNK)
            seq = 0 if nb == 1 else ci
            store_states(one_chunk(rows, load_states(seq), attend_safe), seq)
            return carry

        lax.fori_loop(0, rows_all // CHUNK, chunk_body, 0)

    blk = FAST_BLOCK
    min_hg = jnp.min(jnp.sum(p_scr[:, C_HF:C_HF + HG_KW].reshape(rows_all // blk, blk, HG_KW), axis=1))
    min_gla = jnp.min(jnp.sum(lg_scr[...].reshape(rows_all // blk, blk, GLA_KW), axis=1))
    bounded = jnp.minimum(min_hg, min_gla) >= -FAST_BLOCK_DECAY_LIMIT

    @pl.when(bounded)
    def _():
        run_block_fast()

    @pl.when(jnp.logical_not(bounded))
    def _():
        run_chunks_safe()

    o_hg = _head_norm_gate(p_scr[:, C_HI:C_HI + HG_W], hgn_ref[...], p_scr[:, C_HOG:C_HOG + HG_W])
    o_gla = _head_norm_gate(p_scr[:, C_GV:C_GV + GLA_W], glan_ref[...], p_scr[:, C_GOG:C_GOG + GLA_W])
    ya = _dot(o_hg.astype(BF16), wa_ref[...])
    yb = _dot(o_gla.astype(BF16), wb_ref[...])
    merged = (_sigmoid(p_scr[:, C_GA:C_GA + D_MODEL]) * ya
              + _sigmoid(p_scr[:, C_GB:C_GB + D_MODEL]) * yb)
    m = _dot(merged.astype(BF16), wo_ref[...])
    xo_ref[...] = (x + g1 * m).reshape(nb, tb, D_MODEL)

    @pl.when(j == pl.num_programs(1) - 1)
    def _():
        shg_o_ref[...] = shg_scr[...].reshape(shg_o_ref.shape)
        sgla_o_ref[...] = sgla_scr[...].reshape(sgla_o_ref.shape)

    if pending_moe:
        @pl.when(step == last)
        def _():
            _wait_row_gather(n_prefetch, y_hbm, cbuf.at[1 - slot], csem.at[1 - slot])


def _const_spec(shape):
    nd = len(shape)
    return pl.BlockSpec(shape, lambda b, j, nd=nd: (0,) * nd, pipeline_mode=pl.Buffered(1))


def _mixer(x, mod, nrm, win, hg_lb, wgk2, bgk, hgn, glan, wa, wb, wo, shg0, sgla0, plan_consts,
           *, layer, tb, nb=1, pending_moe=None):
    bsz, seq, _ = x.shape
    assert nb == 1 or seq == tb == CHUNK
    nj = seq // tb
    rows = nb * tb
    kern = functools.partial(_mixer_kernel, layer=layer, tb=tb, pending_moe=pending_moe is not None)
    n_gla_tiles = GLA_KW // LANES
    extra_specs, extra_args, extra_scratch = [], [], []
    if pending_moe is not None:
        dest_steps, wts_col, mod_prev, y_slots = pending_moe
        n_steps = bsz // nb * nj
        extra_specs = [
            pl.BlockSpec((1, 1, TOPK * rows), lambda b, j: (b * nj + j, 0, 0), memory_space=pltpu.SMEM),
            pl.BlockSpec((1, 1, TOPK * rows), lambda b, j: (jnp.minimum(b * nj + j + 1, n_steps - 1), 0, 0),
                         memory_space=pltpu.SMEM),
            pl.BlockSpec((rows, TOPK), lambda b, j: (b * nj + j, 0)),
            pl.BlockSpec((nb, 6, D_MODEL), lambda b, j: (b, 0, 0)),
            pl.BlockSpec(memory_space=pl.ANY),
        ]
        extra_args = [dest_steps, dest_steps, wts_col, mod_prev, y_slots]
        extra_scratch = [pltpu.VMEM((2, TOPK * rows, D_MODEL), F32), pltpu.SemaphoreType.DMA((2,))]
    return pl.pallas_call(
        kern,
        grid=(bsz // nb, nj),
        in_specs=[
            pl.BlockSpec((nb, tb, D_MODEL), lambda b, j: (b, j, 0)),
            pl.BlockSpec((nb, 6, D_MODEL), lambda b, j: (b, 0, 0)),
            _const_spec((1, D_MODEL)),
            _const_spec((D_MODEL, IN_COLS_PAD)),
            _const_spec((DEPTH, HG_KW)),
            _const_spec((LANES, GLA_KW)),
            _const_spec((1, GLA_KW)),
            _const_spec((1, HEAD_DV)),
            _const_spec((1, HEAD_DV)),
            _const_spec((HG_W, D_MODEL)),
            _const_spec((GLA_W, D_MODEL)),
            _const_spec((D_MODEL, D_MODEL)),
            pl.BlockSpec((nb, HG_HEADS, HG_DK, HEAD_DV), lambda b, j: (b, 0, 0, 0)),
            pl.BlockSpec((nb, n_gla_tiles, LANES, HEAD_DV), lambda b, j: (b, 0, 0, 0)),
            _const_spec((rows, rows)),
            _const_spec((CHUNK, SLAB_ROWS)),
            _const_spec((SAFE_PLAN.cum_rows, CHUNK)),
            _const_spec((SAFE_PLAN.n_masks, CHUNK, CHUNK)),
        ] + extra_specs,
        out_specs=[
            pl.BlockSpec((nb, tb, D_MODEL), lambda b, j: (b, j, 0)),
            pl.BlockSpec((nb, HG_HEADS, HG_DK, HEAD_DV), lambda b, j: (b, 0, 0, 0)),
            pl.BlockSpec((nb, n_gla_tiles, LANES, HEAD_DV), lambda b, j: (b, 0, 0, 0)),
        ],
        out_shape=[
            jax.ShapeDtypeStruct((bsz, seq, D_MODEL), F32),
            jax.ShapeDtypeStruct((bsz, HG_HEADS, HG_DK, HEAD_DV), F32),
            jax.ShapeDtypeStruct((bsz, n_gla_tiles, LANES, HEAD_DV), F32),
        ],
        scratch_shapes=[
            pltpu.VMEM((rows, IN_COLS_PAD), F32),
            pltpu.VMEM((rows, HG_KW), F32),
            pltpu.VMEM((rows, GLA_KW), F32),
            pltpu.VMEM((nb * HG_HEADS, HG_DK, HEAD_DV), F32),
            pltpu.VMEM((nb * n_gla_tiles, LANES, HEAD_DV), F32),
        ] + extra_scratch,
        compiler_params=pltpu.CompilerParams(
            dimension_semantics=("arbitrary", "arbitrary"), vmem_limit_bytes=VMEM_LIMIT),
        name=f"mixer_l{layer}",
    )(x, mod, nrm, win, hg_lb, wgk2, bgk, hgn, glan, wa, wb, wo, shg0, sgla0, *plan_consts, *extra_args)


SUBLANES = 8
EXPERT_ROW0 = SUBLANES
ROUTER_ROWS = 48
MOE_TILE = 512
TILE_ASSIGN = TOPK * MOE_TILE
MAX_EXPERT_BLOCK = 1024


def _expert_block_rows(n_assign):
    return max(LANES, min(MAX_EXPERT_BLOCK, n_assign // N_EXPERTS // 2))


def _first_argmax_rows(vals, n):
    ridx = lax.broadcasted_iota(jnp.int32, vals.shape, 0)
    vmax = jnp.max(vals, axis=0, keepdims=True)
    imax = jnp.min(jnp.where(vals == vmax, ridx, n), axis=0, keepdims=True)
    return vmax, imax


def _router_kernel(x_ref, mod_ref, nrm_ref, wr_ref, br_ref, tri_ref,
                   h_ref, eid_ref, rank_ref, wts_ref, cnt_ref, run_scr):
    @pl.when(pl.program_id(0) == 0)
    def _():
        run_scr[...] = jnp.zeros_like(run_scr)

    u, lt, _ = x_ref.shape
    x = x_ref[...]
    sh2 = mod_ref[:, 3:4, :]
    sc2 = mod_ref[:, 4:5, :]
    h = _rms_mod(x, nrm_ref[...].reshape(1, 1, D_MODEL), sc2, sh2).reshape(u * lt, D_MODEL)
    h_ref[...] = h
    h_hi = h.astype(BF16)
    h_lo = (h - h_hi.astype(F32)).astype(BF16)
    w = wr_ref[...]
    w_hi = w.astype(BF16)
    w_lo = (w - w_hi.astype(F32)).astype(BF16)
    p_hi = _dot_nt(jnp.concatenate([w_hi, w_lo], axis=0), h_hi)
    logits = p_hi[:ROUTER_ROWS] + p_hi[ROUTER_ROWS:] + _dot_nt(w_hi, h_lo) + br_ref[...]
    gl = logits[0:N_GROUPS]
    gmax, gi = _first_argmax_rows(gl, N_GROUPS)
    gp = 1.0 / jnp.sum(jnp.exp(gl - gmax), axis=0, keepdims=True)
    le = logits[EXPERT_ROW0:EXPERT_ROW0 + EXPERTS_PER_GROUP]
    for g in range(1, N_GROUPS):
        r0 = EXPERT_ROW0 + g * EXPERTS_PER_GROUP
        le = jnp.where(gi == g, logits[r0:r0 + EXPERTS_PER_GROUP], le)
    pe = jnp.exp(le - jnp.max(le, axis=0, keepdims=True))
    pe = pe / jnp.sum(pe, axis=0, keepdims=True)
    v1, i1 = _first_argmax_rows(pe, EXPERTS_PER_GROUP)
    ridx = lax.broadcasted_iota(jnp.int32, pe.shape, 0)
    v2, i2 = _first_argmax_rows(jnp.where(ridx == i1, -1.0, pe), EXPERTS_PER_GROUP)
    vsum = v1 + v2
    wts_ref[0:1, :] = gp * v1 / vsum
    wts_ref[1:2, :] = gp * v2 / vsum
    eflat = jnp.concatenate([gi * EXPERTS_PER_GROUP + i1, gi * EXPERTS_PER_GROUP + i2], axis=1)
    eid_ref[0] = eflat
    onehot = (eflat == lax.broadcasted_iota(jnp.int32, (N_EXPERTS, TILE_ASSIGN), 0)).astype(F32)
    onehot_b = onehot.astype(BF16)
    tri = tri_ref[...]
    seen = run_scr[...]
    before = []
    for c in range(0, TILE_ASSIGN, LANES):
        before.append(_dot(onehot_b[:, c:c + LANES], tri) + seen)
        seen = seen + jnp.sum(onehot[:, c:c + LANES], axis=1, keepdims=True)
    before = jnp.concatenate(before, axis=1)
    rank_ref[0] = jnp.sum(onehot * before, axis=0, keepdims=True).astype(jnp.int32)
    run_scr[...] = seen
    cnt_ref[...] = run_scr[...].astype(jnp.int32)


def _router(x_units, mod_units, nrm, wr, br, tri):
    n_units, lt, _ = x_units.shape
    u = MOE_TILE // lt
    n_tiles = n_units // u
    return pl.pallas_call(
        _router_kernel,
        grid=(n_tiles,),
        in_specs=[
            pl.BlockSpec((u, lt, D_MODEL), lambda i: (i, 0, 0)),
            pl.BlockSpec((u, 6, D_MODEL), lambda i: (i, 0, 0)),
            pl.BlockSpec((1, D_MODEL), lambda i: (0, 0)),
            pl.BlockSpec((ROUTER_ROWS, D_MODEL), lambda i: (0, 0)),
            pl.BlockSpec((ROUTER_ROWS, 1), lambda i: (0, 0)),
            pl.BlockSpec((LANES, LANES), lambda i: (0, 0)),
        ],
        out_specs=[
            pl.BlockSpec((MOE_TILE, D_MODEL), lambda i: (i, 0)),
            pl.BlockSpec((1, 1, TILE_ASSIGN), lambda i: (i, 0, 0)),
            pl.BlockSpec((1, 1, TILE_ASSIGN), lambda i: (i, 0, 0)),
            pl.BlockSpec((TOPK, MOE_TILE), lambda i: (0, i)),
            pl.BlockSpec((N_EXPERTS, 1), lambda i: (0, 0)),
        ],
        out_shape=[
            jax.ShapeDtypeStruct((n_tiles * MOE_TILE, D_MODEL), F32),
            jax.ShapeDtypeStruct((n_tiles, 1, TILE_ASSIGN), jnp.int32),
            jax.ShapeDtypeStruct((n_tiles, 1, TILE_ASSIGN), jnp.int32),
            jax.ShapeDtypeStruct((TOPK, n_tiles * MOE_TILE), F32),
            jax.ShapeDtypeStruct((N_EXPERTS, 1), jnp.int32),
        ],
        scratch_shapes=[pltpu.VMEM((N_EXPERTS, 1), F32)],
        compiler_params=pltpu.CompilerParams(dimension_semantics=("arbitrary",)),
        name="moe_router",
    )(x_units, mod_units, nrm, wr, br, tri)


def _start_row_gather(idx_ref, n_rows, src_hbm, dst, sem):
    def body(r, carry):
        row = idx_ref[0, 0, r]
        pltpu.make_async_copy(src_hbm.at[pl.ds(row, 1)], dst.at[pl.ds(r, 1)], sem).start()
        return carry
    lax.fori_loop(0, n_rows, body, 0, unroll=8)


def _wait_row_gather(n_rows, src_hbm, dst, sem):
    pltpu.make_async_copy(src_hbm.at[pl.ds(0, n_rows)], dst, sem).wait()


def _dispatch_kernel(pend_ref, padded_ref, dest_ref, h_ref, xs_hbm, zbuf, sem):
    block_rows = zbuf.shape[0]
    n_blocks = xs_hbm.shape[0] // block_rows

    def zero_block(first_row):
        return pltpu.make_async_copy(
            zbuf, xs_hbm.at[pl.ds(pl.multiple_of(first_row, block_rows), block_rows)], sem.at[0])

    @pl.when(pl.program_id(0) == 0)
    def _():
        zbuf[...] = jnp.zeros_like(zbuf)
        n_used = pend_ref[N_EXPERTS - 1] // block_rows
        for e in range(N_EXPERTS):
            @pl.when(padded_ref[e] > 0)
            def _():
                zero_block(pend_ref[e] - block_rows).start()
        lax.fori_loop(n_used, n_blocks, lambda b, c: (zero_block(b * block_rows).start(), c)[1], 0)
        for e in range(N_EXPERTS):
            @pl.when(padded_ref[e] > 0)
            def _():
                zero_block(pend_ref[e] - block_rows).wait()
        lax.fori_loop(n_used, n_blocks, lambda b, c: (zero_block(b * block_rows).wait(), c)[1], 0)

    def body(t, carry):
        for k in range(TOPK):
            slot = dest_ref[0, 0, k * MOE_TILE + t]
            pltpu.make_async_copy(h_ref.at[pl.ds(t, 1)], xs_hbm.at[pl.ds(slot, 1)], sem.at[1]).start()
        return carry
    lax.fori_loop(0, MOE_TILE, body, 0, unroll=8)
    for k in range(TOPK):
        pltpu.make_async_copy(h_ref, xs_hbm.at[pl.ds(0, MOE_TILE)], sem.at[1]).wait()


def _dispatch(pad_end, padded, dest_tiles, h, n_slots, block_rows):
    n_tiles = dest_tiles.shape[0]
    grid_spec = pltpu.PrefetchScalarGridSpec(
        num_scalar_prefetch=2,
        grid=(n_tiles,),
        in_specs=[
            pl.BlockSpec((1, 1, TILE_ASSIGN), lambda i, pe, pd: (i, 0, 0), memory_space=pltpu.SMEM),
            pl.BlockSpec((MOE_TILE, D_MODEL), lambda i, pe, pd: (i, 0)),
        ],
        out_specs=pl.BlockSpec(memory_space=pl.ANY),
        scratch_shapes=[pltpu.VMEM((block_rows, D_MODEL), F32), pltpu.SemaphoreType.DMA((2,))],
    )
    return pl.pallas_call(
        _dispatch_kernel,
        grid_spec=grid_spec,
        out_shape=jax.ShapeDtypeStruct((n_slots, D_MODEL), F32),
        compiler_params=pltpu.CompilerParams(dimension_semantics=("arbitrary",)),
        name="moe_dispatch",
    )(pad_end, padded, dest_tiles, h)


def _experts_kernel(be_ref, nused_ref, x_ref, wg_ref, wu_ref, wd_ref, o_ref):
    @pl.when(pl.program_id(0) < nused_ref[0])
    def _():
        xb = x_ref[...].astype(BF16)
        a = _silu(_dot(xb, wg_ref[0, 0])) * _dot(xb, wu_ref[0, 0])
        o_ref[...] = _dot(a.astype(BF16), wd_ref[0, 0])

    @pl.when(pl.program_id(0) >= nused_ref[0])
    def _():
        o_ref[...] = jnp.zeros_like(o_ref)


def _experts(block_e, n_used, xs, wg, wu, wd, layer, block_rows):
    n_blocks = xs.shape[0] // block_rows

    def row_block(i, be, nu):
        return (jnp.minimum(i, nu[0] - 1), 0)

    def expert_block(i, be, nu):
        return (layer, be[jnp.minimum(i, nu[0] - 1)], 0, 0)

    grid_spec = pltpu.PrefetchScalarGridSpec(
        num_scalar_prefetch=2,
        grid=(n_blocks,),
        in_specs=[
            pl.BlockSpec((block_rows, D_MODEL), row_block),
            pl.BlockSpec((1, 1, D_MODEL, D_EXPERT), expert_block),
            pl.BlockSpec((1, 1, D_MODEL, D_EXPERT), expert_block),
            pl.BlockSpec((1, 1, D_EXPERT, D_MODEL), expert_block),
        ],
        out_specs=pl.BlockSpec((block_rows, D_MODEL), lambda i, be, nu: (i, 0)),
    )
    return pl.pallas_call(
        _experts_kernel,
        grid_spec=grid_spec,
        out_shape=jax.ShapeDtypeStruct(xs.shape, F32),
        compiler_params=pltpu.CompilerParams(
            dimension_semantics=("arbitrary",), vmem_limit_bytes=VMEM_LIMIT),
        name="moe_experts",
    )(block_e, n_used, xs, wg, wu, wd)


def _combine_kernel(dst_cur_ref, dst_nxt_ref, x_ref, mod_ref, wts_ref, nrm_ref, y_hbm, o_ref, buf, sem,
                    *, final_norm):
    i = pl.program_id(0)
    n = pl.num_programs(0)
    slot = i % 2

    @pl.when(i == 0)
    def _():
        _start_row_gather(dst_cur_ref, TILE_ASSIGN, y_hbm, buf.at[0], sem.at[0])

    _wait_row_gather(TILE_ASSIGN, y_hbm, buf.at[slot], sem.at[slot])
    u, lt, _ = x_ref.shape
    n_pieces = MOE_TILE // CHUNK
    per_piece = TILE_ASSIGN // n_pieces
    for c in range(n_pieces):
        unit, r0 = divmod(c * CHUNK, lt)
        t0 = c * CHUNK
        y = (wts_ref[t0:t0 + CHUNK, 0:1] * buf[slot, t0:t0 + CHUNK, :]
             + wts_ref[t0:t0 + CHUNK, 1:2] * buf[slot, MOE_TILE + t0:MOE_TILE + t0 + CHUNK, :])
        out = x_ref[unit, r0:r0 + CHUNK, :] + mod_ref[unit, 5:6, :] * y
        if final_norm:
            out = out * lax.rsqrt(jnp.mean(out * out, axis=-1, keepdims=True) + NORM_EPS) * nrm_ref[...]
        o_ref[unit, r0:r0 + CHUNK, :] = out
        for r in range(c * per_piece, (c + 1) * per_piece):
            row = dst_nxt_ref[0, 0, r]
            pltpu.make_async_copy(y_hbm.at[pl.ds(row, 1)], buf.at[1 - slot, pl.ds(r, 1)],
                                  sem.at[1 - slot]).start()

    @pl.when(i == n - 1)
    def _():
        _wait_row_gather(TILE_ASSIGN, y_hbm, buf.at[1 - slot], sem.at[1 - slot])


def _combine(dest_tiles, x_units, mod_units, wts_col, nrm, y_slots, *, final_norm):
    n_units, lt, _ = x_units.shape
    u = MOE_TILE // lt
    n_tiles = n_units // u
    return pl.pallas_call(
        functools.partial(_combine_kernel, final_norm=final_norm),
        grid=(n_tiles,),
        in_specs=[
            pl.BlockSpec((1, 1, TILE_ASSIGN), lambda i: (i, 0, 0), memory_space=pltpu.SMEM),
            pl.BlockSpec((1, 1, TILE_ASSIGN), lambda i: (jnp.minimum(i + 1, n_tiles - 1), 0, 0),
                         memory_space=pltpu.SMEM),
            pl.BlockSpec((u, lt, D_MODEL), lambda i: (i, 0, 0)),
            pl.BlockSpec((u, 6, D_MODEL), lambda i: (i, 0, 0)),
            pl.BlockSpec((MOE_TILE, TOPK), lambda i: (i, 0)),
            pl.BlockSpec((1, D_MODEL), lambda i: (0, 0)),
            pl.BlockSpec(memory_space=pl.ANY),
        ],
        out_specs=pl.BlockSpec((u, lt, D_MODEL), lambda i: (i, 0, 0)),
        out_shape=jax.ShapeDtypeStruct(x_units.shape, F32),
        scratch_shapes=[pltpu.VMEM((2, TILE_ASSIGN, D_MODEL), F32), pltpu.SemaphoreType.DMA((2,))],
        compiler_params=pltpu.CompilerParams(
            dimension_semantics=("arbitrary",), vmem_limit_bytes=VMEM_LIMIT),
        name="moe_combine",
    )(dest_tiles, dest_tiles, x_units, mod_units, wts_col, nrm, y_slots)


def _routing_tables(eid_tiles, rank_tiles, counts, block_rows):
    n_blocks = eid_tiles.size // block_rows + N_EXPERTS
    padded = (counts + block_rows - 1) // block_rows * block_rows
    pad_end = jnp.cumsum(padded).astype(jnp.int32)
    pad_start = pad_end - padded
    block_start = jnp.arange(n_blocks, dtype=jnp.int32)[:, None] * block_rows
    block_e = jnp.minimum(jnp.sum((block_start >= pad_end[None, :]).astype(jnp.int32), axis=1),
                          N_EXPERTS - 1).astype(jnp.int32)
    n_used = pad_end[-1:] // block_rows
    experts = jnp.arange(N_EXPERTS, dtype=jnp.int32)
    first_slot = jnp.sum(jnp.where(eid_tiles[..., None] == experts, pad_start, 0), axis=-1)
    return block_e, n_used, pad_end, padded, first_slot + rank_tiles


def _moe_units(x, mod_l):
    bsz, seq, _ = x.shape
    lt = min(seq, MOE_TILE)
    per = seq // lt
    x_units = x.reshape(bsz * seq // lt, lt, D_MODEL)
    mod_units = jnp.repeat(mod_l, per, axis=0) if per > 1 else mod_l
    return x_units, mod_units


def _moe_experts(x, mod_l, nrm_ffn, wr, br, tri, wg, wu, wd, layer):
    x_units, mod_units = _moe_units(x, mod_l)
    h, eid_tiles, rank_tiles, wts, counts = _router(x_units, mod_units, nrm_ffn, wr, br, tri)
    block_rows = _expert_block_rows(eid_tiles.size)
    block_e, n_used, pad_end, padded, dest_tiles = _routing_tables(
        eid_tiles, rank_tiles, counts[:, 0], block_rows)
    n_slots = block_e.shape[0] * block_rows
    xs = _dispatch(pad_end, padded, dest_tiles, h, n_slots, block_rows)
    return dest_tiles, wts.T, _experts(block_e, n_used, xs, wg, wu, wd, layer, block_rows)


def _dest_per_step(dest_tiles, tb):
    n_tiles = dest_tiles.shape[0]
    per = MOE_TILE // tb
    d = dest_tiles.reshape(n_tiles, TOPK, per, tb).transpose(0, 2, 1, 3)
    return d.reshape(n_tiles * per, 1, TOPK * tb)


def kernel(x_prompt, x_sample, c_prompt, c_sample, state_hgrn, state_gla, w_ada, b_ada, norm_mix,
           norm_ffn, w_in, hg_lb, hg_onorm, w_gk2, b_gk, gla_onorm, w_br_a, w_br_b, w_out, w_rg, b_rg,
           w_re, b_re, w_e_gate, w_e_up, w_e_down, norm_final):
    bp = x_prompt.shape[0]
    bs = x_sample.shape[0]
    mod = _ada_mod(jnp.concatenate([c_prompt, c_sample], axis=0), w_ada, b_ada)
    mod = mod.reshape(DEPTH, bp + bs, 6, D_MODEL)

    glr0 = C_GOG + GLA_W
    win_r = jnp.concatenate(
        [w_in[:, :, :glr0], w_in[:, :, glr0 + GLA_GATE_RANK:], w_in[:, :, glr0:glr0 + GLA_GATE_RANK],
         jnp.zeros((DEPTH, D_MODEL, LANES - GLA_GATE_RANK), F32)], axis=2).astype(BF16)
    wgk2_p = jnp.concatenate(
        [w_gk2, jnp.zeros((DEPTH, LANES - GLA_GATE_RANK, GLA_KW), F32)], axis=1).astype(BF16)
    wa_b = w_br_a.astype(BF16)
    wb_b = w_br_b.astype(BF16)
    wo_b = w_out.astype(BF16)

    def plan_consts(rows):
        r = np.arange(rows)
        chunk_tril = (r[:, None] // CHUNK == r[None, :] // CHUNK) & (r[None, :] <= r[:, None])
        return [jnp.asarray(chunk_tril, BF16), jnp.asarray(_slab_mask(), F32),
                jnp.asarray(SAFE_PLAN.segment_sum_matrix(), BF16), jnp.asarray(SAFE_PLAN.masks(), F32)]

    zpad = jnp.zeros((DEPTH, EXPERT_ROW0 - N_GROUPS, D_MODEL), F32)
    ztail = jnp.zeros((DEPTH, ROUTER_ROWS - EXPERT_ROW0 - N_EXPERTS, D_MODEL), F32)
    wr = jnp.concatenate([jnp.swapaxes(w_rg, 1, 2), zpad, jnp.swapaxes(w_re, 1, 2), ztail], axis=1)
    br = jnp.concatenate([b_rg, jnp.zeros((DEPTH, EXPERT_ROW0 - N_GROUPS), F32), b_re,
                          jnp.zeros((DEPTH, ROUTER_ROWS - EXPERT_ROW0 - N_EXPERTS), F32)], axis=1)[:, :, None]
    wg_b = w_e_gate.astype(BF16)
    wu_b = w_e_up.astype(BF16)
    wd_b = w_e_down.astype(BF16)
    nrm_f = norm_final.reshape(1, D_MODEL)
    lane = np.arange(LANES)
    tri = jnp.asarray(lane[:, None] < lane[None, :], BF16)

    def run(x, mod_g, shg, sgla, tb, nb):
        bsz = x.shape[0]
        new_hg, new_gla = [], []
        pending = None
        for l in range(DEPTH):
            x, s1, s2 = _mixer(
                x, mod_g[l], norm_mix[l:l + 1], win_r[l], hg_lb, wgk2_p[l], b_gk[l:l + 1],
                hg_onorm[l:l + 1], gla_onorm[l:l + 1], wa_b[l], wb_b[l], wo_b[l],
                shg[l], sgla[l].reshape(bsz, GLA_KW // LANES, LANES, HEAD_DV), plan_consts(nb * tb),
                layer=l, tb=tb, nb=nb, pending_moe=pending)
            new_hg.append(s1)
            new_gla.append(s2.reshape(bsz, GLA_HEADS, GLA_DK, HEAD_DV))
            dest_tiles, wts_col, y_slots = _moe_experts(
                x, mod_g[l], norm_ffn[l:l + 1], wr[l], br[l], tri, wg_b, wu_b, wd_b, l)
            pending = (_dest_per_step(dest_tiles, nb * tb), wts_col, mod_g[l], y_slots)
        x_units, mod_units = _moe_units(x, mod_g[DEPTH - 1])
        y = _combine(dest_tiles, x_units, mod_units, wts_col, nrm_f, y_slots, final_norm=True)
        return y.reshape(x.shape), jnp.stack(new_hg), jnp.stack(new_gla)

    zeros_hg = jnp.zeros((DEPTH, bp, HG_HEADS, HG_DK, HEAD_DV), F32)
    zeros_gla = jnp.zeros((DEPTH, bp, GLA_HEADS, GLA_DK, HEAD_DV), F32)
    y_p, hg_p, gla_p = run(x_prompt, mod[:, :bp], zeros_hg, zeros_gla, 256, 1)
    y_s, hg_s, gla_s = run(x_sample, mod[:, bp:], state_hgrn, state_gla, CHUNK, 4)
    return (y_p, y_s, hg_p, gla_p, hg_s, gla_s)
```
